```python
import math
import jax
import jax.numpy as jnp
from jax import lax
import numpy as np

D_MODEL = 2048
BATCH = 4
SEQ = 4096
DEPTH = 1

SSD_INNER = D_MODEL
SSD_HEADDIM = 64
SSD_HEADS = SSD_INNER // SSD_HEADDIM
SSD_GROUPS = 4
SSD_STATE = 128
SSD_CONV = 4
SSD_CHUNK = 128
SSD_XBC = SSD_INNER + 2 * SSD_GROUPS * SSD_STATE
ATTN_HEADS = D_MODEL // 128
ATTN_DK = 64
ATTN_DV = 2 * ATTN_DK
ATTN_WIDTH = ATTN_HEADS * ATTN_DV
ROT_DIM = ATTN_DK // 4
ROPE_THETA = 500000.0
Q_BLOCK = 128
N_EXPERTS = 32
TOP_K = 4
D_EXPERT = D_MODEL
SWIGLU_LIMIT = 7.0
SWIGLU_ALPHA = 1.702
MOE_BLOCK = 256
NORM_EPS = 1e-6
SUB_EPS = 1e-5
IN_SIZES = (SSD_INNER, SSD_XBC, SSD_HEADS, 2 * ATTN_HEADS * ATTN_DK, 2 * ATTN_HEADS * ATTN_DK, ATTN_WIDTH, D_MODEL, D_MODEL)
D_IN = sum(IN_SIZES)
IN_SPLITS = tuple(sum(IN_SIZES[:i + 1]) for i in range(len(IN_SIZES) - 1))

kernel_name = 'hybrid_ssd_diffattn_moe_adaln_block'


def rmsnorm(x, w, eps=NORM_EPS):
    xf = x.astype(jnp.float32)
    y = xf * lax.rsqrt(jnp.mean(xf * xf, axis=-1, keepdims=True) + eps)
    return (y * w.astype(jnp.float32)).astype(x.dtype)


def causal_depthwise_conv(u, w, b):
    y = lax.conv_general_dilated(u, w[:, None, :].astype(u.dtype), window_strides=(1,), padding=[(SSD_CONV - 1, 0)], dimension_numbers=('NWC', 'WIO', 'NWC'), feature_group_count=u.shape[-1])
    return y + b


def ssd_chunked(xs, dt, A, Bm, Cm):
    b, s, h, p = xs.shape
    g, n = Bm.shape[-2:]
    e = h // g
    nc = s // SSD_CHUNK
    L = SSD_CHUNK
    a = (dt * A).reshape(b, nc, L, g, e)
    X = (xs * dt[..., None]).reshape(b, nc, L, g, e, p)
    Bc = Bm.reshape(b, nc, L, g, n)
    Cc = Cm.reshape(b, nc, L, g, n)
    a_cs = jnp.cumsum(a, axis=2)
    causal = jnp.tril(jnp.ones((L, L), dtype=bool))[None, None, :, :, None, None]
    seg = a_cs[:, :, :, None] - a_cs[:, :, None, :]
    decay_in = jnp.exp(jnp.where(causal, seg, -jnp.inf))
    scores = jnp.einsum('bclgn,bcsgn->bclsg', Cc, Bc)
    y_diag = jnp.einsum('bclsge,bcsgep->bclgep', scores[..., None] * decay_in, X)
    decay_to_end = jnp.exp(a_cs[:, :, -1:] - a_cs)
    chunk_states = jnp.einsum('bclgn,bclgep->bcgepn', Bc, X * decay_to_end[..., None])
    chunk_decay = jnp.exp(a_cs[:, :, -1])

    def carry_state(state, inp):
        st, dec = inp
        return state * dec[..., None, None] + st, state

    init = jnp.zeros((b, g, e, p, n), jnp.float32)
    _, prev = lax.scan(carry_state, init, (jnp.moveaxis(chunk_states, 1, 0), jnp.moveaxis(chunk_decay, 1, 0)))
    prev = jnp.moveaxis(prev, 0, 1)
    y_off = jnp.einsum('bclgn,bcgepn->bclgep', Cc, prev) * jnp.exp(a_cs)[..., None]
    return (y_diag + y_off).reshape(b, s, h, p)


def gated_group_rmsnorm(y, z, w):
    b, s, d = y.shape
    u = (y * jax.nn.silu(z.astype(jnp.float32))).reshape(b, s, SSD_GROUPS, d // SSD_GROUPS)
    u = u * lax.rsqrt(jnp.mean(u * u, axis=-1, keepdims=True) + SUB_EPS)
    return u.reshape(b, s, d) * w.astype(jnp.float32)


def rope_tables(s):
    inv = ROPE_THETA ** (-jnp.arange(0, ROT_DIM, 2, dtype=jnp.float32) / ROT_DIM)
    ang = jnp.arange(s, dtype=jnp.float32)[:, None] * inv[None, :]
    return jnp.cos(ang), jnp.sin(ang)


def partial_rope(x, cos, sin):
    xf = x.astype(jnp.float32)
    x1 = xf[..., :ROT_DIM // 2]
    x2 = xf[..., ROT_DIM // 2:ROT_DIM]
    c = cos[None, :, None, :]
    s_ = sin[None, :, None, :]
    rot = jnp.concatenate([x1 * c - x2 * s_, x2 * c + x1 * s_], axis=-1).astype(x.dtype)
    return jnp.concatenate([rot, x[..., ROT_DIM:]], axis=-1)


def diff_attention(q, k, v, lam, lam_init, subln_w):
    b, s = q.shape[:2]
    nq = s // Q_BLOCK
    scale = ATTN_DK ** -0.5
    qb = q.reshape(b, nq, Q_BLOCK, 2 * ATTN_HEADS, ATTN_DK).transpose(1, 0, 3, 2, 4)
    kt = k.transpose(0, 2, 1, 3)
    vt = v.transpose(0, 2, 1, 3)
    kpos = jnp.arange(s)

    def block(args):
        qi, i = args
        qpos = i * Q_BLOCK + jnp.arange(Q_BLOCK)
        sc = jnp.einsum('bhqd,bhkd->bhqk', qi, kt).astype(jnp.float32) * scale
        sc = jnp.where(kpos[None, :] <= qpos[:, None], sc, -jnp.inf)
        pr = jax.nn.softmax(sc, axis=-1).reshape(b, ATTN_HEADS, 2, Q_BLOCK, s)
        att = pr[:, :, 0] - lam * pr[:, :, 1]
        return jnp.einsum('bhqk,bhkd->bhqd', att.astype(vt.dtype), vt)

    o = lax.map(block, (qb, jnp.arange(nq)))
    o = o.transpose(1, 0, 3, 2, 4).reshape(b, s, ATTN_HEADS, ATTN_DV)
    o = rmsnorm(o, subln_w, SUB_EPS) * (1.0 - lam_init)
    return o.reshape(b, s, ATTN_WIDTH)


def moe_ffn(h, w_router, b_router, w_gate_up, b_gate_up, w_down, b_down):
    bsz, s, d = h.shape
    t = h.reshape(-1, d)
    T = t.shape[0]
    TK = T * TOP_K
    logits = (t @ w_router + b_router).astype(jnp.float32)
    top_val, top_idx = lax.top_k(logits, TOP_K)
    top_w = jax.nn.softmax(top_val, axis=-1)
    flat_e = top_idx.reshape(-1)
    flat_tok = jnp.arange(TK, dtype=jnp.int32) // TOP_K
    flat_w = top_w.reshape(-1)
    order = jnp.argsort(flat_e)
    sorted_e = flat_e[order]
    counts = jnp.bincount(flat_e, length=N_EXPERTS)
    group_start = jnp.cumsum(counts) - counts
    padded = ((counts + MOE_BLOCK - 1) // MOE_BLOCK) * MOE_BLOCK
    padded_end = jnp.cumsum(padded)
    padded_start = padded_end - padded
    dest = padded_start[sorted_e] + (jnp.arange(TK) - group_start[sorted_e])
    n_slots = ((TK + MOE_BLOCK - 1) // MOE_BLOCK) * MOE_BLOCK + N_EXPERTS * MOE_BLOCK
    n_blocks = n_slots // MOE_BLOCK
    slot_tok = jnp.zeros((n_slots,), jnp.int32).at[dest].set(flat_tok[order])
    slot_w = jnp.zeros((n_slots,), jnp.float32).at[dest].set(flat_w[order])
    block_e = jnp.minimum(jnp.searchsorted(padded_end, jnp.arange(n_blocks) * MOE_BLOCK, side='right'), N_EXPERTS - 1)

    def expert_block(args):
        tok, wt, e = args
        xb = t[tok]
        gu = xb @ w_gate_up[e] + b_gate_up[e]
        gate = jnp.minimum(gu[:, :D_EXPERT], SWIGLU_LIMIT)
        up = jnp.clip(gu[:, D_EXPERT:], -SWIGLU_LIMIT, SWIGLU_LIMIT)
        act = (up + 1.0) * gate * jax.nn.sigmoid(SWIGLU_ALPHA * gate)
        y = act @ w_down[e] + b_down[e]
        return (y * wt[:, None]).astype(t.dtype)

    ys = lax.map(expert_block, (slot_tok.reshape(n_blocks, MOE_BLOCK), slot_w.reshape(n_blocks, MOE_BLOCK), block_e))
    out = jnp.zeros_like(t).at[slot_tok].add(ys.reshape(n_slots, d))
    return out.reshape(bsz, s, d)


def setup_inputs(seed: int = 0) -> dict:
    key = jax.random.key(seed)
    ks = jax.random.split(key, 32)
    f32 = jnp.float32
    nrm = lambda k, shape, sc: jax.random.normal(k, shape, f32) * sc
    dt = jnp.exp(jax.random.uniform(ks[9], (DEPTH, SSD_HEADS), f32) * (math.log(0.1) - math.log(0.001)) + math.log(0.001))
    return {
        'x': nrm(ks[0], (BATCH, SEQ, D_MODEL), 1.0),
        'c': nrm(ks[1], (BATCH, D_MODEL), 1.0),
        'w_ada': nrm(ks[2], (DEPTH, D_MODEL, 6 * D_MODEL), D_MODEL ** -0.5),
        'b_ada': nrm(ks[3], (DEPTH, 6 * D_MODEL), 0.02),
        'g_pre_mix': 1.0 + nrm(ks[4], (DEPTH, D_MODEL), 0.1),
        'g_post_mix': 1.0 + nrm(ks[5], (DEPTH, D_MODEL), 0.1),
        'g_pre_ffn': 1.0 + nrm(ks[6], (DEPTH, D_MODEL), 0.1),
        'g_post_ffn': 1.0 + nrm(ks[7], (DEPTH, D_MODEL), 0.1),
        'w_in': nrm(ks[8], (DEPTH, D_MODEL, D_IN), D_MODEL ** -0.5),
        'conv_w': nrm(ks[10], (DEPTH, SSD_CONV, SSD_XBC), SSD_CONV ** -0.5),
        'conv_b': nrm(ks[11], (DEPTH, SSD_XBC), 0.02),
        'dt_bias': dt + jnp.log(-jnp.expm1(-dt)),
        'a_log': jnp.log(jax.random.uniform(ks[12], (DEPTH, SSD_HEADS), f32, 1.0, 16.0)),
        'd_skip': 1.0 + nrm(ks[13], (DEPTH, SSD_HEADS), 0.1),
        'ssd_norm_w': 1.0 + nrm(ks[14], (DEPTH, SSD_INNER), 0.1),
        'lambda_q1': nrm(ks[15], (DEPTH, ATTN_DK), 0.1),
        'lambda_k1': nrm(ks[16], (DEPTH, ATTN_DK), 0.1),
        'lambda_q2': nrm(ks[17], (DEPTH, ATTN_DK), 0.1),
        'lambda_k2': nrm(ks[18], (DEPTH, ATTN_DK), 0.1),
        'subln_w': 1.0 + nrm(ks[19], (DEPTH, ATTN_DV), 0.1),
        'w_br_ssd': nrm(ks[20], (DEPTH, SSD_INNER, D_MODEL), SSD_INNER ** -0.5),
        'w_br_attn': nrm(ks[21], (DEPTH, ATTN_WIDTH, D_MODEL), ATTN_WIDTH ** -0.5),
        'w_out': nrm(ks[22], (DEPTH, D_MODEL, D_MODEL), D_MODEL ** -0.5),
        'w_router': nrm(ks[23], (DEPTH, D_MODEL, N_EXPERTS), D_MODEL ** -0.5),
        'b_router': nrm(ks[24], (DEPTH, N_EXPERTS), 0.01),
        'w_gate_up': nrm(ks[25], (DEPTH, N_EXPERTS, D_MODEL, 2 * D_EXPERT), D_MODEL ** -0.5),
        'b_gate_up': nrm(ks[26], (DEPTH, N_EXPERTS, 2 * D_EXPERT), 0.02),
        'w_down': nrm(ks[27], (DEPTH, N_EXPERTS, D_EXPERT, D_MODEL), D_EXPERT ** -0.5),
        'b_down': nrm(ks[28], (DEPTH, N_EXPERTS, D_MODEL), 0.02),
    }


def reference(x, c, w_ada, b_ada, g_pre_mix, g_post_mix, g_pre_ffn, g_post_ffn, w_in, conv_w, conv_b, dt_bias, a_log, d_skip, ssd_norm_w, lambda_q1, lambda_k1, lambda_q2, lambda_k2, subln_w, w_br_ssd, w_br_attn, w_out, w_router, b_router, w_gate_up, b_gate_up, w_down, b_down):
    bsz, s, _ = x.shape
    cos, sin = rope_tables(s)
    for l in range(DEPTH):
        mod = jax.nn.silu(c) @ w_ada[l] + b_ada[l]
        sh_m, sc_m, gt_m, sh_f, sc_f, gt_f = [m[:, None, :] for m in jnp.split(mod, 6, axis=-1)]

        h = rmsnorm(x, g_pre_mix[l]) * (1.0 + sc_m) + sh_m
        proj = h @ w_in[l]
        z, xbc, dt_raw, q, k, v, gate_s, gate_a = jnp.split(proj, IN_SPLITS, axis=-1)

        xbc = jax.nn.silu(causal_depthwise_conv(xbc, conv_w[l], conv_b[l]))
        xs, bm, cm = jnp.split(xbc, [SSD_INNER, SSD_INNER + SSD_GROUPS * SSD_STATE], axis=-1)
        xs = xs.reshape(bsz, s, SSD_HEADS, SSD_HEADDIM).astype(jnp.float32)
        bm = bm.reshape(bsz, s, SSD_GROUPS, SSD_STATE).astype(jnp.float32)
        cm = cm.reshape(bsz, s, SSD_GROUPS, SSD_STATE).astype(jnp.float32)
        dt = jax.nn.softplus(dt_raw.astype(jnp.float32) + dt_bias[l].astype(jnp.float32))
        A = -jnp.exp(a_log[l].astype(jnp.float32))
        y = ssd_chunked(xs, dt, A, bm, cm) + d_skip[l].astype(jnp.float32)[:, None] * xs
        y = gated_group_rmsnorm(y.reshape(bsz, s, SSD_INNER), z, ssd_norm_w[l]).astype(x.dtype)
        branch_ssd = y @ w_br_ssd[l]

        q = partial_rope(q.reshape(bsz, s, 2 * ATTN_HEADS, ATTN_DK), cos, sin)
        k = partial_rope(k.reshape(bsz, s, 2 * ATTN_HEADS, ATTN_DK), cos, sin)
        v = v.reshape(bsz, s, ATTN_HEADS, ATTN_DV)
        lam_init = 0.8 - 0.6 * math.exp(-0.3 * l)
        lam = (jnp.exp(jnp.sum(lambda_q1[l].astype(jnp.float32) * lambda_k1[l].astype(jnp.float32)))
               - jnp.exp(jnp.sum(lambda_q2[l].astype(jnp.float32) * lambda_k2[l].astype(jnp.float32))) + lam_init)
        o = diff_attention(q, k, v, lam, lam_init, subln_w[l])
        branch_attn = o @ w_br_attn[l]

        merged = jax.nn.sigmoid(gate_s) * branch_ssd + jax.nn.sigmoid(gate_a) * branch_attn
        mix = merged @ w_out[l]
        x = x + gt_m * rmsnorm(mix, g_post_mix[l])

        h2 = rmsnorm(x, g_pre_ffn[l]) * (1.0 + sc_f) + sh_f
        f = moe_ffn(h2, w_router[l], b_router[l], w_gate_up[l], b_gate_up[l], w_down[l], b_down[l])
        x = x + gt_f * rmsnorm(f, g_post_ffn[l])
    return x
```

```python
import functools
import math

import jax
import jax.numpy as jnp
from jax import lax
from jax.experimental import pallas as pl
from jax.experimental.pallas import tpu as pltpu

F32 = jnp.float32
BF16 = jnp.bfloat16
HIGHEST = lax.Precision.HIGHEST

SSD_HEADDIM = 64
SSD_GROUPS = 4
SSD_STATE = 128
SSD_CONV = 4
SSD_CHUNK = 128
ATTN_DK = 64
ATTN_DV = 128
ROT_DIM = ATTN_DK // 4
ROPE_THETA = 500000.0
TOP_K = 4
SWIGLU_LIMIT = 7.0
SWIGLU_ALPHA = 1.702
NORM_EPS = 1e-6
SUB_EPS = 1e-5
LANES = 128
NEG_BIG = -1e30

VMEM_LIMIT = 56 * 1024 * 1024


def _cparams(sem, vmem=VMEM_LIMIT):
    return pltpu.CompilerParams(dimension_semantics=sem, vmem_limit_bytes=vmem)


def _sigmoid(x):
    return 1.0 / (1.0 + jnp.exp(-x))


def _ada_kernel(c_ref, w_ref, b_ref, o_ref):
    c = c_ref[...]
    sc = c * _sigmoid(c)
    o_ref[...] = jnp.dot(sc, w_ref[...], preferred_element_type=F32, precision=HIGHEST) + b_ref[...]


def _ada(c_pad, w, b, tn):
    rows, d = c_pad.shape
    n = w.shape[1]
    return pl.pallas_call(
        _ada_kernel,
        grid=(n // tn,),
        in_specs=[pl.BlockSpec((rows, d), lambda j: (0, 0)),
                  pl.BlockSpec((d, tn), lambda j: (0, j)),
                  pl.BlockSpec((1, tn), lambda j: (0, j))],
        out_specs=pl.BlockSpec((rows, tn), lambda j: (0, j)),
        out_shape=jax.ShapeDtypeStruct((rows, n), F32),
        compiler_params=_cparams(("arbitrary",)),
        name="ada",
    )(c_pad, w, b)


def _prenorm_kernel(x_ref, g_ref, sc_ref, sh_ref, o_ref):
    x = x_ref[0]
    y = x * lax.rsqrt(jnp.mean(x * x, axis=-1, keepdims=True) + NORM_EPS) * g_ref[...]
    o_ref[0] = (y * (1.0 + sc_ref[0]) + sh_ref[0]).astype(o_ref.dtype)


def _prenorm(x, g, mod3, sc_blk, sh_blk, ts):
    b, s, d = x.shape
    return pl.pallas_call(
        _prenorm_kernel,
        grid=(b, s // ts),
        in_specs=[pl.BlockSpec((1, ts, d), lambda bi, si: (bi, si, 0)),
                  pl.BlockSpec((1, d), lambda bi, si: (0, 0)),
                  pl.BlockSpec((1, 1, d), lambda bi, si: (bi, 0, sc_blk)),
                  pl.BlockSpec((1, 1, d), lambda bi, si: (bi, 0, sh_blk))],
        out_specs=pl.BlockSpec((1, ts, d), lambda bi, si: (bi, si, 0)),
        out_shape=jax.ShapeDtypeStruct((b, s, d), BF16),
        compiler_params=_cparams(("arbitrary", "arbitrary")),
        name="prenorm",
    )(x, g, mod3, mod3)


def _mm_kernel(a_ref, w_ref, o_ref):
    o_ref[...] = jnp.dot(a_ref[...], w_ref[...], preferred_element_type=F32).astype(o_ref.dtype)


def _mm_sigmoid_kernel(a_ref, w_ref, o_ref):
    acc = jnp.dot(a_ref[...], w_ref[...], preferred_element_type=F32)
    o_ref[...] = _sigmoid(acc).astype(o_ref.dtype)


def _mm_rope_kernel(a_ref, w_ref, c_ref, s1_ref, s2_ref, o_ref):
    acc = jnp.dot(a_ref[...], w_ref[...], preferred_element_type=F32)
    c = c_ref[...]
    s1 = s1_ref[...]
    s2 = s2_ref[...]
    for g in range(acc.shape[1] // LANES):
        blk = acc[:, g * LANES:(g + 1) * LANES]
        fwd = pltpu.roll(blk, LANES - ROT_DIM // 2, 1)
        bwd = pltpu.roll(blk, ROT_DIM // 2, 1)
        o_ref[:, g * LANES:(g + 1) * LANES] = (blk * c + fwd * s1 + bwd * s2).astype(o_ref.dtype)


def _mm(a, w, out_dtype, tm, tn, epilogue="none", rope_tabs=None, name="mm"):
    m, k = a.shape
    n = w.shape[1]
    in_specs = [pl.BlockSpec((tm, k), lambda i, j: (i, 0)),
                pl.BlockSpec((k, tn), lambda i, j: (0, j))]
    args = [a, w]
    if epilogue == "rope":
        seq_blocks = rope_tabs[0].shape[0] // tm
        for t in rope_tabs:
            in_specs.append(pl.BlockSpec((tm, LANES), lambda i, j: (i % seq_blocks, 0)))
            args.append(t)
        kern = _mm_rope_kernel
    elif epilogue == "sigmoid":
        kern = _mm_sigmoid_kernel
    else:
        kern = _mm_kernel
    return pl.pallas_call(
        kern,
        grid=(m // tm, n // tn),
        in_specs=in_specs,
        out_specs=pl.BlockSpec((tm, tn), lambda i, j: (i, j)),
        out_shape=jax.ShapeDtypeStruct((m, n), out_dtype),
        compiler_params=_cparams(("arbitrary", "arbitrary")),
        name=name,
    )(*args)


def _ssd_kernel(z_ref, xs_ref, bm_ref, cm_ref, dt_ref, cwx_ref, cwb_ref, cwc_ref, cbx_ref, cbb_ref,
                cbc_ref, dtb_ref, alog_ref, dsk_ref, nw_ref, o_ref,
                px_sc, pb_sc, pc_sc, st_sc, y_sc, *, n_heads):
    c = pl.program_id(1)
    L = SSD_CHUNK
    N = SSD_STATE
    hpg = n_heads // SSD_GROUPS
    gw = hpg * SSD_HEADDIM

    @pl.when(c == 0)
    def _():
        px_sc[...] = jnp.zeros_like(px_sc)
        pb_sc[...] = jnp.zeros_like(pb_sc)
        pc_sc[...] = jnp.zeros_like(pc_sc)
        st_sc[...] = jnp.zeros_like(st_sc)

    def conv_silu(u, prev_sc, w_ref, b_ref):
        prev = prev_sc[...]
        row = lax.broadcasted_iota(jnp.int32, u.shape, 0)
        acc = u * w_ref[SSD_CONV - 1:SSD_CONV, :] + b_ref[...]
        for j in range(1, SSD_CONV):
            sh = jnp.where(row < j, pltpu.roll(prev, j, 0), pltpu.roll(u, j, 0))
            acc = acc + sh * w_ref[SSD_CONV - 1 - j:SSD_CONV - j, :]
        prev_sc[...] = u
        return acc * _sigmoid(acc)

    xs = conv_silu(xs_ref[0].astype(F32), px_sc, cwx_ref, cbx_ref)
    bm = conv_silu(bm_ref[0].astype(F32), pb_sc, cwb_ref, cbb_ref)
    cm = conv_silu(cm_ref[0].astype(F32), pc_sc, cwc_ref, cbc_ref)

    raw = dt_ref[0] + dtb_ref[...]
    dt = jnp.maximum(raw, 0.0) + jnp.log1p(jnp.exp(-jnp.abs(raw)))
    a = dt * (-jnp.exp(alog_ref[...]))
    r_i = lax.broadcasted_iota(jnp.int32, (L, L), 0)
    c_i = lax.broadcasted_iota(jnp.int32, (L, L), 1)
    causal = r_i >= c_i
    tri = jnp.where(causal, 1.0, 0.0).astype(F32)
    a_cs = jnp.dot(tri, a, preferred_element_type=F32, precision=HIGHEST)
    a_cs_t = a_cs.T

    lane = lax.broadcasted_iota(jnp.int32, (L, LANES), 1)
    first = lane < SSD_HEADDIM
    acs_tiles = []
    dt_tiles = []
    for j in range(n_heads // 2):
        h0, h1 = 2 * j, 2 * j + 1
        acs_tiles.append(jnp.where(first, a_cs[:, h0:h0 + 1], a_cs[:, h1:h1 + 1]))
        dt_tiles.append(jnp.where(first, dt[:, h0:h0 + 1], dt[:, h1:h1 + 1]))
    acs_e = jnp.concatenate(acs_tiles, axis=1)
    dt_e = jnp.concatenate(dt_tiles, axis=1)
    xdt = xs * dt_e
    ea = jnp.exp(acs_e)
    alast = acs_e[L - 1:L, :]
    xdec = (xdt * jnp.exp(alast - acs_e)).astype(BF16)
    ealast = jnp.exp(alast)

    for g in range(SSD_GROUPS):
        bg = bm[:, g * N:(g + 1) * N]
        cg = cm[:, g * N:(g + 1) * N].astype(BF16)
        scores = lax.dot_general(cg, bg.astype(BF16), (((1,), (1,)), ((), ())),
                                 preferred_element_type=F32)
        st_old = st_sc[:, g * gw:(g + 1) * gw]
        y_off = jnp.dot(cg, st_old.astype(BF16), preferred_element_type=F32) * ea[:, g * gw:(g + 1) * gw]
        st_sc[:, g * gw:(g + 1) * gw] = st_old * ealast[:, g * gw:(g + 1) * gw] + jnp.dot(
            bg.T.astype(BF16), xdec[:, g * gw:(g + 1) * gw], preferred_element_type=F32)
        for jj in range(hpg // 2):
            j = g * (hpg // 2) + jj
            h0, h1 = 2 * j, 2 * j + 1
            la = jnp.exp(jnp.where(causal, a_cs[:, h0:h0 + 1] - a_cs_t[h0:h0 + 1, :], NEG_BIG))
            lb = jnp.exp(jnp.where(causal, a_cs[:, h1:h1 + 1] - a_cs_t[h1:h1 + 1, :], NEG_BIG))
            mcat = jnp.concatenate([(scores * la).astype(BF16), (scores * lb).astype(BF16)], axis=1)
            xp = xdt[:, j * LANES:(j + 1) * LANES]
            xcat = jnp.concatenate([jnp.where(first, xp, 0.0).astype(BF16),
                                    jnp.where(first, 0.0, xp).astype(BF16)], axis=0)
            y_diag = jnp.dot(mcat, xcat, preferred_element_type=F32)
            lo = jj * LANES
            y_sc[:, j * LANES:(j + 1) * LANES] = (
                y_diag + y_off[:, lo:lo + LANES]
                + dsk_ref[:, j * LANES:(j + 1) * LANES] * xs[:, j * LANES:(j + 1) * LANES])

    z = z_ref[0].astype(F32)
    u = y_sc[...] * (z * _sigmoid(z))
    for g in range(SSD_GROUPS):
        ug = u[:, g * gw:(g + 1) * gw]
        ms = jnp.mean(ug * ug, axis=-1, keepdims=True)
        o_ref[0, :, g * gw:(g + 1) * gw] = (ug * lax.rsqrt(ms + SUB_EPS)
                                            * nw_ref[:, g * gw:(g + 1) * gw]).astype(o_ref.dtype)


def _ssd(plain3, dt3, conv_w, conv_b, dtb, alog, dsk_e, norm_w, dh, n_heads):
    b, s, _ = plain3.shape
    L = SSD_CHUNK
    gn = SSD_GROUPS * SSD_STATE
    nc = s // L
    xblk = 1
    bblk = (2 * dh) // gn
    cw_x, cw_b, cw_c = conv_w[:, :dh], conv_w[:, dh:dh + gn], conv_w[:, dh + gn:]
    cb_x, cb_b, cb_c = conv_b[:, :dh], conv_b[:, dh:dh + gn], conv_b[:, dh + gn:]
    full = lambda shape: pl.BlockSpec(shape, lambda bi, ci: (0, 0))
    return pl.pallas_call(
        functools.partial(_ssd_kernel, n_heads=n_heads),
        grid=(b, nc),
        in_specs=[pl.BlockSpec((1, L, dh), lambda bi, ci: (bi, ci, 0)),
                  pl.BlockSpec((1, L, dh), lambda bi, ci: (bi, ci, xblk)),
                  pl.BlockSpec((1, L, gn), lambda bi, ci: (bi, ci, bblk)),
                  pl.BlockSpec((1, L, gn), lambda bi, ci: (bi, ci, bblk + 1)),
                  pl.BlockSpec((1, L, LANES), lambda bi, ci: (bi, ci, 0)),
                  full((SSD_CONV, dh)), full((SSD_CONV, gn)), full((SSD_CONV, gn)),
                  full((1, dh)), full((1, gn)), full((1, gn)),
                  full((1, LANES)), full((1, LANES)), full((1, dh)), full((1, dh))],
        out_specs=pl.BlockSpec((1, L, dh), lambda bi, ci: (bi, ci, 0)),
        out_shape=jax.ShapeDtypeStruct((b, s, dh), BF16),
        scratch_shapes=[pltpu.VMEM((L, dh), F32), pltpu.VMEM((L, gn), F32), pltpu.VMEM((L, gn), F32),
                        pltpu.VMEM((SSD_STATE, dh), F32), pltpu.VMEM((L, dh), F32)],
        compiler_params=_cparams(("arbitrary", "arbitrary")),
        name="ssd",
    )(plain3, plain3, plain3, plain3, dt3, cw_x, cw_b, cw_c, cb_x, cb_b, cb_c, dtb, alog, dsk_e, norm_w)


def _attn_kernel(q_ref, k_ref, v_ref, lam_ref, sw_ref, o_ref, m_sc, l_sc, acc_sc, *, tq, lam_init):
    qi = pl.program_id(2)
    q = q_ref[0]
    lane = lax.broadcasted_iota(jnp.int32, q.shape, 1)
    zero = jnp.zeros_like(q)
    qm = (jnp.where(lane < ATTN_DK, q, zero), jnp.where(lane < ATTN_DK, zero, q))
    m_sc[...] = jnp.full_like(m_sc, NEG_BIG)
    l_sc[...] = jnp.zeros_like(l_sc)
    acc_sc[...] = jnp.zeros_like(acc_sc)

    def step(kb, masked):
        start = pl.multiple_of(kb * tq, tq)
        k = k_ref[0, pl.ds(start, tq), :]
        v = v_ref[0, pl.ds(start, tq), :]
        for m in range(2):
            s = lax.dot_general(qm[m], k, (((1,), (1,)), ((), ())), preferred_element_type=F32)
            if masked:
                r_i = lax.broadcasted_iota(jnp.int32, s.shape, 0)
                c_i = lax.broadcasted_iota(jnp.int32, s.shape, 1)
                s = jnp.where(r_i >= c_i, s, NEG_BIG)
            m_prev = m_sc[m]
            m_new = jnp.maximum(m_prev, jnp.max(s, axis=-1, keepdims=True))
            alpha = jnp.exp(m_prev - m_new)
            p = jnp.exp(s - m_new)
            l_sc[m] = alpha * l_sc[m] + jnp.sum(p, axis=-1, keepdims=True)
            acc_sc[m] = alpha * acc_sc[m] + jnp.dot(p.astype(BF16), v, preferred_element_type=F32)
            m_sc[m] = m_new

    def body(kb, carry):
        step(kb, False)
        return carry

    lax.fori_loop(0, qi, body, 0)
    step(qi, True)

    lv = lam_ref[...]
    lam = (jnp.exp(jnp.sum(lv[0:1] * lv[1:2], axis=-1, keepdims=True))
           - jnp.exp(jnp.sum(lv[2:3] * lv[3:4], axis=-1, keepdims=True)) + lam_init)
    o = acc_sc[0] / l_sc[0] - lam * (acc_sc[1] / l_sc[1])
    o = o * lax.rsqrt(jnp.mean(o * o, axis=-1, keepdims=True) + SUB_EPS) * sw_ref[...]
    o_ref[0] = (o * (1.0 - lam_init)).astype(o_ref.dtype)


def _attention(qk3, plain3, lam_rows, subln_w, n_heads, v_blk0, tq, lam_init):
    b, s, _ = qk3.shape
    return pl.pallas_call(
        functools.partial(_attn_kernel, tq=tq, lam_init=lam_init),
        grid=(b, n_heads, s // tq),
        in_specs=[pl.BlockSpec((1, tq, LANES), lambda bi, hi, qi: (bi, qi, hi)),
                  pl.BlockSpec((1, s, LANES), lambda bi, hi, qi: (bi, 0, n_heads + hi)),
                  pl.BlockSpec((1, s, LANES), lambda bi, hi, qi: (bi, 0, v_blk0 + hi)),
                  pl.BlockSpec((8, LANES), lambda bi, hi, qi: (0, 0)),
                  pl.BlockSpec((1, LANES), lambda bi, hi, qi: (0, 0))],
        out_specs=pl.BlockSpec((1, tq, LANES), lambda bi, hi, qi: (bi, qi, hi)),
        out_shape=jax.ShapeDtypeStruct((b, s, n_heads * ATTN_DV), BF16),
        scratch_shapes=[pltpu.VMEM((2, tq, 1), F32), pltpu.VMEM((2, tq, 1), F32),
                        pltpu.VMEM((2, tq, ATTN_DV), F32)],
        compiler_params=_cparams(("arbitrary", "arbitrary", "arbitrary")),
        name="diff_attn",
    )(qk3, qk3, plain3, lam_rows, subln_w)


def _merge_kernel(y_ref, o_ref, ws_ref, wa_ref, gs_ref, ga_ref, out_ref):
    bs = jnp.dot(y_ref[...], ws_ref[...], preferred_element_type=F32)
    ba = jnp.dot(o_ref[...], wa_ref[...], preferred_element_type=F32)
    out_ref[...] = (gs_ref[...].astype(F32) * bs + ga_ref[...].astype(F32) * ba).astype(out_ref.dtype)


def _merge(y, o, ws, wa, gates, tm, tn):
    m, k = y.shape
    n = ws.shape[1]
    nj = n // tn
    return pl.pallas_call(
        _merge_kernel,
        grid=(m // tm, nj),
        in_specs=[pl.BlockSpec((tm, k), lambda i, j: (i, 0)),
                  pl.BlockSpec((tm, o.shape[1]), lambda i, j: (i, 0)),
                  pl.BlockSpec((k, tn), lambda i, j: (0, j)),
                  pl.BlockSpec((o.shape[1], tn), lambda i, j: (0, j)),
                  pl.BlockSpec((tm, tn), lambda i, j: (i, j)),
                  pl.BlockSpec((tm, tn), lambda i, j: (i, nj + j))],
        out_specs=pl.BlockSpec((tm, tn), lambda i, j: (i, j)),
        out_shape=jax.ShapeDtypeStruct((m, n), BF16),
        compiler_params=_cparams(("arbitrary", "arbitrary")),
        name="merge",
    )(y, o, ws, wa, gates, gates)


def _outproj_kernel(mg_ref, wo_ref, x_ref, gpost_ref, gt_ref, gpre_ref, sc_ref, sh_ref, wr_hi_ref,
                    wr_lo_ref, br_ref, x1_ref, hp_ref, lg_ref):
    mix = jnp.dot(mg_ref[...], wo_ref[...], preferred_element_type=F32)
    nm = mix * lax.rsqrt(jnp.mean(mix * mix, axis=-1, keepdims=True) + NORM_EPS) * gpost_ref[...]
    x1 = x_ref[...] + gt_ref[0] * nm
    x1_ref[...] = x1
    h2 = (x1 * lax.rsqrt(jnp.mean(x1 * x1, axis=-1, keepdims=True) + NORM_EPS) * gpre_ref[...]
          * (1.0 + sc_ref[0]) + sh_ref[0])
    h_hi = h2.astype(BF16)
    h_lo = (h2 - h_hi.astype(F32)).astype(BF16)
    wr_hi = wr_hi_ref[...]
    lg_ref[...] = (jnp.dot(h_hi, wr_hi, preferred_element_type=F32)
                   + jnp.dot(h_lo, wr_hi, preferred_element_type=F32)
                   + jnp.dot(h_hi, wr_lo_ref[...], preferred_element_type=F32) + br_ref[...])
    bits = lax.bitcast_convert_type(h_hi.astype(F32), jnp.uint32)
    half = bits.shape[1] // 2
    hp_ref[...] = (bits[:, :half] >> 16) | bits[:, half:]


def _outproj(merged, wo, x2, gpost, mod3, gpre, wr_hi, wr_lo, br, seq, tm, gt_blk, sc_blk, sh_blk):
    m, d = x2.shape
    per_b = seq // tm
    row = lambda i: (i, 0)
    const = lambda i: (0, 0)
    return pl.pallas_call(
        _outproj_kernel,
        grid=(m // tm,),
        in_specs=[pl.BlockSpec((tm, d), row),
                  pl.BlockSpec((d, d), const),
                  pl.BlockSpec((tm, d), row),
                  pl.BlockSpec((1, d), const),
                  pl.BlockSpec((1, 1, d), lambda i: (i // per_b, 0, gt_blk)),
                  pl.BlockSpec((1, d), const),
                  pl.BlockSpec((1, 1, d), lambda i: (i // per_b, 0, sc_blk)),
                  pl.BlockSpec((1, 1, d), lambda i: (i // per_b, 0, sh_blk)),
                  pl.BlockSpec((d, LANES), const),
                  pl.BlockSpec((d, LANES), const),
                  pl.BlockSpec((1, LANES), const)],
        out_specs=[pl.BlockSpec((tm, d), row),
                   pl.BlockSpec((tm, d // 2), row),
                   pl.BlockSpec((tm, LANES), row)],
        out_shape=[jax.ShapeDtypeStruct((m, d), F32),
                   jax.ShapeDtypeStruct((m, d // 2), jnp.uint32),
                   jax.ShapeDtypeStruct((m, LANES), F32)],
        compiler_params=_cparams(("arbitrary",)),
        name="outproj",
    )(merged, wo, x2, gpost, mod3, gpre, mod3, mod3, wr_hi, wr_lo, br)


def _route_kernel(lg_ref, dest_ref, w_ref, cnt_ref, cnt_sc, pst_sc, run_sc, *, n_experts, blk):
    ph = pl.program_id(0)
    t = pl.program_id(1)
    tk = lg_ref.shape[0]
    lane = lax.broadcasted_iota(jnp.int32, (tk, LANES), 1)
    lg = jnp.where(lane < n_experts, lg_ref[...], -jnp.inf)
    vals = []
    hots = []
    for _ in range(TOP_K):
        mx = jnp.max(lg, axis=-1, keepdims=True)
        ix = jnp.min(jnp.where(lg == mx, lane, LANES), axis=-1, keepdims=True)
        hot = lane == ix
        lg = jnp.where(hot, -jnp.inf, lg)
        vals.append(mx)
        hots.append(hot)
    multi = jnp.zeros((tk, LANES), F32)
    for hot in hots:
        multi = multi + jnp.where(hot, 1.0, 0.0)
    colsum = jnp.sum(multi, axis=0, keepdims=True)

    @pl.when((ph == 0) & (t == 0))
    def _():
        cnt_sc[...] = jnp.zeros_like(cnt_sc)

    @pl.when(ph == 0)
    def _():
        cnt_sc[...] += colsum

    @pl.when((ph == 1) & (t == 0))
    def _():
        cnt = cnt_sc[...].astype(jnp.int32)
        padded = (((cnt + (blk - 1)) // blk) * blk).astype(F32)
        r_i = lax.broadcasted_iota(jnp.int32, (LANES, LANES), 0)
        c_i = lax.broadcasted_iota(jnp.int32, (LANES, LANES), 1)
        upper = jnp.where(r_i < c_i, 1.0, 0.0).astype(F32)
        pst_sc[...] = jnp.dot(jnp.broadcast_to(padded, (8, LANES)), upper,
                              preferred_element_type=F32, precision=HIGHEST)[0:1]
        run_sc[...] = jnp.zeros_like(run_sc)

    @pl.when(ph == 1)
    def _():
        r_i = lax.broadcasted_iota(jnp.int32, (tk, tk), 0)
        c_i = lax.broadcasted_iota(jnp.int32, (tk, tk), 1)
        strict = jnp.where(r_i > c_i, 1.0, 0.0).astype(BF16)
        before = jnp.dot(strict, multi.astype(BF16), preferred_element_type=F32)
        base = before + run_sc[...] + pst_sc[...]
        esum = jnp.zeros((tk, 1), F32)
        evals = []
        for r in range(TOP_K):
            e = jnp.exp(vals[r] - vals[0])
            evals.append(e)
            esum = esum + e
        dest = jnp.zeros((tk, LANES), jnp.int32)
        wts = jnp.zeros((tk, LANES), F32)
        for r in range(TOP_K):
            d_r = jnp.sum(jnp.where(hots[r], base, 0.0), axis=-1, keepdims=True).astype(jnp.int32)
            dest = jnp.where(lane == r, d_r, dest)
            wts = jnp.where(lane == r, evals[r] / esum, wts)
        dest_ref[...] = dest
        w_ref[...] = wts
        run_sc[...] += colsum
        cnt_ref[...] = jnp.broadcast_to(cnt_sc[...], cnt_ref.shape)


def _route(logits, n_experts, blk, tk):
    t = logits.shape[0]
    return pl.pallas_call(
        functools.partial(_route_kernel, n_experts=n_experts, blk=blk),
        grid=(2, t // tk),
        in_specs=[pl.BlockSpec((tk, LANES), lambda ph, ti: (ti, 0))],
        out_specs=[pl.BlockSpec((tk, LANES), lambda ph, ti: (ti * ph, 0)),
                   pl.BlockSpec((tk, LANES), lambda ph, ti: (ti * ph, 0)),
                   pl.BlockSpec((8, LANES), lambda ph, ti: (0, 0))],
        out_shape=[jax.ShapeDtypeStruct((t, LANES), jnp.int32),
                   jax.ShapeDtypeStruct((t, LANES), F32),
                   jax.ShapeDtypeStruct((8, LANES), F32)],
        scratch_shapes=[pltpu.VMEM((1, LANES), F32), pltpu.VMEM((1, LANES), F32), pltpu.VMEM((1, LANES), F32)],
        compiler_params=_cparams(("arbitrary", "arbitrary")),
        name="route",
    )(logits)


def _pad_fill_copies(b, nv_sm, zero_sc, xs_hbm, sem, blk):
    nv = nv_sm[b]
    head = (-nv) & 7
    out = []
    for r in range(7):
        out.append((r < head, pltpu.make_async_copy(
            zero_sc.at[pl.ds(0, 1), :], xs_hbm.at[pl.ds(b * blk + nv + r, 1), :], sem)))
    off = b * blk + nv + head
    rest = blk - nv - head
    p = blk
    while p >= 8:
        cond = (rest & p) != 0
        out.append((cond, pltpu.make_async_copy(
            zero_sc.at[pl.ds(0, p), :], xs_hbm.at[pl.ds(pl.multiple_of(off, 8), p), :], sem)))
        off = off + jnp.where(cond, p, 0)
        p //= 2
    return out


def _dispatch_kernel(dest_sm, nv_sm, h_ref, xs_hbm, zero_sc, sem, fill_sem, *, tt, blk, nb):
    base = pl.program_id(0) * tt

    @pl.when(pl.program_id(0) == 0)
    def _():
        zero_sc[...] = jnp.zeros_like(zero_sc)

        def fill(b, carry):
            for cond, cp in _pad_fill_copies(b, nv_sm, zero_sc, xs_hbm, fill_sem, blk):
                pl.when(cond)(cp.start)
            return carry

        def fill_wait(b, carry):
            for cond, cp in _pad_fill_copies(b, nv_sm, zero_sc, xs_hbm, fill_sem, blk):
                pl.when(cond)(cp.wait)
            return carry

        lax.fori_loop(0, nb, fill, 0)
        lax.fori_loop(0, nb, fill_wait, 0)

    def body(t, carry):
        for k in range(TOP_K):
            d = dest_sm[(base + t) * TOP_K + k]
            pltpu.make_async_copy(h_ref.at[pl.ds(t, 1), :], xs_hbm.at[pl.ds(d, 1), :], sem).start()
        return carry

    lax.fori_loop(0, tt, body, 0)

    def wait_body(t, carry):
        for k in range(TOP_K):
            pltpu.make_async_copy(h_ref.at[pl.ds(t, 1), :], xs_hbm.at[pl.ds(0, 1), :], sem).wait()
        return carry

    lax.fori_loop(0, tt, wait_body, 0)


def _dispatch(dest_flat, nvalid, h_packed, n_slots, tt, blk):
    t, dp = h_packed.shape
    nb = n_slots // blk
    return pl.pallas_call(
        functools.partial(_dispatch_kernel, tt=tt, blk=blk, nb=nb),
        grid_spec=pltpu.PrefetchScalarGridSpec(
            num_scalar_prefetch=2,
            grid=(t // tt,),
            in_specs=[pl.BlockSpec((tt, dp), lambda i, d, nv: (i, 0))],
            out_specs=pl.BlockSpec(memory_space=pl.ANY),
            scratch_shapes=[pltpu.VMEM((blk, dp), jnp.uint32), pltpu.SemaphoreType.DMA(()),
                            pltpu.SemaphoreType.DMA(())],
        ),
        out_shape=jax.ShapeDtypeStruct((n_slots, dp), jnp.uint32),
        compiler_params=_cparams(("arbitrary",)),
        name="dispatch",
    )(dest_flat, nvalid, h_packed)


def _expert_changed(be, i, last):
    ii = jnp.minimum(i, last)
    prev = jnp.maximum(ii - 1, 0)
    return (i == 0) | (be[ii] != be[prev])


def _gateup_kernel(be, nv, nu, x_ref, wg_ref, wu_ref, bg_ref, bu_ref, o_ref, wg_sc, wu_sc):
    i = pl.program_id(1)
    active = i < nu[0]

    @pl.when(active & _expert_changed(be, i, nu[0] - 1))
    def _():
        wg_sc[...] = wg_ref[0].astype(BF16)
        wu_sc[...] = wu_ref[0].astype(BF16)

    @pl.when(active)
    def _():
        u = x_ref[...]
        xa = lax.bitcast_convert_type(u << 16, F32).astype(BF16)
        xb = lax.bitcast_convert_type(u & jnp.uint32(0xFFFF0000), F32).astype(BF16)
        x = jnp.concatenate([xa, xb], axis=1)
        g = jnp.dot(x, wg_sc[...], preferred_element_type=F32) + bg_ref[0]
        up = jnp.dot(x, wu_sc[...], preferred_element_type=F32) + bu_ref[0]
        gate = jnp.minimum(g, SWIGLU_LIMIT)
        up = jnp.clip(up, -SWIGLU_LIMIT, SWIGLU_LIMIT)
        o_ref[...] = ((up + 1.0) * gate * _sigmoid(SWIGLU_ALPHA * gate)).astype(o_ref.dtype)

    @pl.when(jnp.logical_not(active))
    def _():
        o_ref[...] = jnp.zeros_like(o_ref)


def _gateup(block_e, nvalid, nused, xs, w_gu, b_gu3, tm, th):
    n_slots, dp = xs.shape
    d = 2 * dp
    dff = w_gu.shape[2] // 2
    nj = dff // th
    nb = n_slots // tm

    def blk(i, nu):
        return jnp.minimum(i, nu[0] - 1)

    return pl.pallas_call(
        _gateup_kernel,
        grid_spec=pltpu.PrefetchScalarGridSpec(
            num_scalar_prefetch=3,
            grid=(nj, nb),
            in_specs=[pl.BlockSpec((tm, dp), lambda j, i, be, nv, nu: (blk(i, nu), 0)),
                      pl.BlockSpec((1, d, th), lambda j, i, be, nv, nu: (be[blk(i, nu)], 0, j)),
                      pl.BlockSpec((1, d, th), lambda j, i, be, nv, nu: (be[blk(i, nu)], 0, nj + j)),
                      pl.BlockSpec((1, 1, th), lambda j, i, be, nv, nu: (be[blk(i, nu)], 0, j)),
                      pl.BlockSpec((1, 1, th), lambda j, i, be, nv, nu: (be[blk(i, nu)], 0, nj + j))],
            out_specs=pl.BlockSpec((tm, th), lambda j, i, be, nv, nu: (i, j)),
            scratch_shapes=[pltpu.VMEM((d, th), BF16), pltpu.VMEM((d, th), BF16)],
        ),
        out_shape=jax.ShapeDtypeStruct((n_slots, dff), BF16),
        compiler_params=_cparams(("arbitrary", "arbitrary")),
        name="expert_gate_up",
    )(block_e, nvalid, nused, xs, w_gu, w_gu, b_gu3, b_gu3)


def _down_kernel(be, nv, nu, a_ref, wd_ref, bd_ref, o_ref, wd_sc):
    i = pl.program_id(1)
    active = i < nu[0]

    @pl.when(active & _expert_changed(be, i, nu[0] - 1))
    def _():
        wd_sc[...] = wd_ref[0].astype(BF16)

    @pl.when(active)
    def _():
        o_ref[...] = jnp.dot(a_ref[...], wd_sc[...], preferred_element_type=F32) + bd_ref[0]

    @pl.when(jnp.logical_not(active))
    def _():
        o_ref[...] = jnp.zeros_like(o_ref)


def _down(block_e, nvalid, nused, act, w_d, b_d3, tm, tn):
    n_slots, dff = act.shape
    d = w_d.shape[2]
    nj = d // tn
    nb = n_slots // tm

    def blk(i, nu):
        return jnp.minimum(i, nu[0] - 1)

    return pl.pallas_call(
        _down_kernel,
        grid_spec=pltpu.PrefetchScalarGridSpec(
            num_scalar_prefetch=3,
            grid=(nj, nb),
            in_specs=[pl.BlockSpec((tm, dff), lambda j, i, be, nv, nu: (blk(i, nu), 0)),
                      pl.BlockSpec((1, dff, tn), lambda j, i, be, nv, nu: (be[blk(i, nu)], 0, j)),
                      pl.BlockSpec((1, 1, tn), lambda j, i, be, nv, nu: (be[blk(i, nu)], 0, j))],
            out_specs=pl.BlockSpec((tm, tn), lambda j, i, be, nv, nu: (i, j)),
            scratch_shapes=[pltpu.VMEM((dff, tn), BF16)],
        ),
        out_shape=jax.ShapeDtypeStruct((n_slots, d), F32),
        compiler_params=_cparams(("arbitrary", "arbitrary")),
        name="expert_down",
    )(block_e, nvalid, nused, act, w_d, b_d3)


def _combine_kernel(dest_sm, y_hbm, w_ref, x1_ref, gt_ref, g_ref, o_ref, buf, sem, *, tt):
    base = pl.program_id(0) * tt

    def body(t, carry):
        for k in range(TOP_K):
            d = dest_sm[(base + t) * TOP_K + k]
            pltpu.make_async_copy(y_hbm.at[pl.ds(d, 1), :], buf.at[pl.ds(k * tt + t, 1), :], sem).start()
        return carry

    lax.fori_loop(0, tt, body, 0)

    def wait_body(t, carry):
        for k in range(TOP_K):
            pltpu.make_async_copy(y_hbm.at[pl.ds(0, 1), :], buf.at[pl.ds(k * tt + t, 1), :], sem).wait()
        return carry

    lax.fori_loop(0, tt, wait_body, 0)

    w = w_ref[...]
    f = buf[0:tt, :] * w[:, 0:1]
    for k in range(1, TOP_K):
        f = f + buf[k * tt:(k + 1) * tt, :] * w[:, k:k + 1]
    nf = f * lax.rsqrt(jnp.mean(f * f, axis=-1, keepdims=True) + NORM_EPS) * g_ref[...]
    o_ref[...] = x1_ref[...] + gt_ref[0] * nf


def _combine(dest_flat, y_sorted, wts, x1, mod3, gpost, seq, tt, gt_blk):
    t, d = x1.shape
    per_b = seq // tt
    return pl.pallas_call(
        functools.partial(_combine_kernel, tt=tt),
        grid_spec=pltpu.PrefetchScalarGridSpec(
            num_scalar_prefetch=1,
            grid=(t // tt,),
            in_specs=[pl.BlockSpec(memory_space=pl.ANY),
                      pl.BlockSpec((tt, LANES), lambda i, ds: (i, 0)),
                      pl.BlockSpec((tt, d), lambda i, ds: (i, 0)),
                      pl.BlockSpec((1, 1, d), lambda i, ds: (i // per_b, 0, gt_blk)),
                      pl.BlockSpec((1, d), lambda i, ds: (0, 0))],
            out_specs=pl.BlockSpec((tt, d), lambda i, ds: (i, 0)),
            scratch_shapes=[pltpu.VMEM((TOP_K * tt, d), F32), pltpu.SemaphoreType.DMA(())],
        ),
        out_shape=jax.ShapeDtypeStruct((t, d), F32),
        compiler_params=_cparams(("arbitrary",)),
        name="combine",
    )(dest_flat, y_sorted, wts, x1, mod3, gpost)


def _tile(n, pref):
    t = min(n, pref)
    while n % t:
        t //= 2
    return t


def _rope_tables(seq):
    half = ROT_DIM // 2
    inv = ROPE_THETA ** (-jnp.arange(0, ROT_DIM, 2, dtype=F32) / ROT_DIM)
    ang = jnp.arange(seq, dtype=F32)[:, None] * inv[None, :]
    cos, sin = jnp.cos(ang), jnp.sin(ang)
    ones = jnp.ones((seq, ATTN_DK - ROT_DIM), F32)
    zeros = jnp.zeros((seq, ATTN_DK - ROT_DIM), F32)
    zh = jnp.zeros((seq, half), F32)
    c64 = jnp.concatenate([cos, cos, ones], axis=1)
    s1_64 = jnp.concatenate([-sin, zh, zeros], axis=1)
    s2_64 = jnp.concatenate([zh, sin, zeros], axis=1)
    rep = LANES // ATTN_DK
    return tuple(jnp.tile(t, (1, rep)) for t in (c64, s1_64, s2_64))


def _layer(x, c_pad, l, p, moe_blk):
    bsz, seq, d = x.shape
    t = bsz * seq
    dh = d
    n_sheads = dh // SSD_HEADDIM
    gn = SSD_GROUPS * SSD_STATE
    n_aheads = d // ATTN_DV
    aw = n_aheads * ATTN_DV
    qkw = 2 * n_aheads * ATTN_DK
    n_experts = p["w_router"].shape[-1]

    mod = _ada(c_pad, p["w_ada"][l], p["b_ada"][l][None, :], _tile(6 * d, 1024))
    mod3 = mod[:bsz].reshape(bsz, 1, 6 * d)

    w_in = p["w_in"][l]
    o = 0
    segs = {}
    for name, size in (("z", dh), ("xbc", dh + 2 * gn), ("dt", n_sheads), ("q", qkw), ("k", qkw),
                       ("v", aw), ("gs", d), ("ga", d)):
        segs[name] = w_in[:, o:o + size]
        o += size
    scale = ATTN_DK ** -0.5
    w_plain = jnp.concatenate([segs["z"], segs["xbc"], segs["v"]], axis=1).astype(BF16)
    w_rope = jnp.concatenate([segs["q"] * scale, segs["k"]], axis=1).astype(BF16)
    w_gate = jnp.concatenate([segs["gs"], segs["ga"]], axis=1).astype(BF16)
    w_dt = jnp.pad(segs["dt"], ((0, 0), (0, LANES - n_sheads))).astype(BF16)

    h = _prenorm(x, p["g_pre_mix"][l][None, :], mod3, 1, 0, _tile(seq, 512)).reshape(t, d)
    tm = _tile(seq, 1024)
    plain = _mm(h, w_plain, BF16, tm, _tile(w_plain.shape[1], 512), name="proj_plain")
    qk = _mm(h, w_rope, BF16, tm, _tile(w_rope.shape[1], 512), "rope", _rope_tables(seq), name="proj_rope")
    gates = _mm(h, w_gate, BF16, tm, _tile(w_gate.shape[1], 512), "sigmoid", name="proj_gate")
    dt_raw = _mm(h, w_dt, F32, tm, LANES, name="proj_dt")

    pad_h = lambda v: jnp.pad(v, (0, LANES - n_sheads))[None, :]
    plain3 = plain.reshape(bsz, seq, plain.shape[1])
    y_ssd = _ssd(plain3, dt_raw.reshape(bsz, seq, LANES), p["conv_w"][l], p["conv_b"][l][None, :],
                 pad_h(p["dt_bias"][l]), pad_h(p["a_log"][l]),
                 jnp.repeat(p["d_skip"][l], SSD_HEADDIM)[None, :], p["ssd_norm_w"][l][None, :], dh, n_sheads)

    lam_init = 0.8 - 0.6 * math.exp(-0.3 * l)
    lam_rows = jnp.zeros((8, LANES), F32)
    for r, nm in enumerate(("lambda_q1", "lambda_k1", "lambda_q2", "lambda_k2")):
        lam_rows = lam_rows.at[r, :ATTN_DK].set(p[nm][l])
    v_blk0 = (dh + dh + 2 * gn) // LANES
    o_attn = _attention(qk.reshape(bsz, seq, 2 * qkw), plain3, lam_rows, p["subln_w"][l][None, :],
                        n_aheads, v_blk0, _tile(seq, 512), lam_init)

    tm2 = _tile(seq, 512)
    merged = _merge(y_ssd.reshape(t, dh), o_attn.reshape(t, aw), p["w_br_ssd"][l].astype(BF16),
                    p["w_br_attn"][l].astype(BF16), gates, tm2, _tile(d, 512))
    wr = jnp.pad(p["w_router"][l], ((0, 0), (0, LANES - n_experts)))
    wr_hi = wr.astype(BF16)
    wr_lo = (wr - wr_hi.astype(F32)).astype(BF16)
    br = jnp.pad(p["b_router"][l], (0, LANES - n_experts))[None, :]
    x1, h_packed, logits = _outproj(merged, p["w_out"][l].astype(BF16), x.reshape(t, d),
                                    p["g_post_mix"][l][None, :], mod3, p["g_pre_ffn"][l][None, :],
                                    wr_hi, wr_lo, br, seq, tm2, 2, 4, 3)

    dest, wts, cnt = _route(logits, n_experts, moe_blk, _tile(t, 512))
    counts = cnt[0, :n_experts].astype(jnp.int32)
    n_slots = t * TOP_K + n_experts * moe_blk
    nb = n_slots // moe_blk
    pblocks = (counts + moe_blk - 1) // moe_blk
    pend = jnp.cumsum(pblocks)
    nused = jnp.maximum(pend[-1], 1).astype(jnp.int32)
    bidx = jnp.arange(nb, dtype=jnp.int32)
    block_e = jnp.minimum(jnp.searchsorted(pend, bidx, side="right"), n_experts - 1).astype(jnp.int32)
    pstart = pend - pblocks
    nvalid = jnp.clip(counts[block_e] - (bidx - pstart[block_e]) * moe_blk, 0, moe_blk).astype(jnp.int32)
    dest_flat = dest[:, :TOP_K].reshape(-1)
    nused1 = nused.reshape(1)

    xs = _dispatch(dest_flat, nvalid, h_packed, n_slots, _tile(t, 256), moe_blk)
    dff = p["w_down"].shape[2]
    act = _gateup(block_e, nvalid, nused1, xs, p["w_gate_up"][l], p["b_gate_up"][l][:, None, :],
                  moe_blk, _tile(dff, 512))
    y_sorted = _down(block_e, nvalid, nused1, act, p["w_down"][l], p["b_down"][l][:, None, :],
                     moe_blk, _tile(d, 1024))
    out = _combine(dest_flat, y_sorted, wts, x1, mod3, p["g_post_ffn"][l][None, :], seq, _tile(seq, 128), 5)
    return out.reshape(bsz, seq, d)


MOE_ROW_BLOCK = 512


def kernel(x, c, w_ada, b_ada, g_pre_mix, g_post_mix, g_pre_ffn, g_post_ffn, w_in, conv_w, conv_b, dt_bias, a_log, d_skip, ssd_norm_w, lambda_q1, lambda_k1, lambda_q2, lambda_k2, subln_w, w_br_ssd, w_br_attn, w_out, w_router, b_router, w_gate_up, b_gate_up, w_down, b_down):
    p = dict(w_ada=w_ada, b_ada=b_ada, g_pre_mix=g_pre_mix, g_post_mix=g_post_mix, g_pre_ffn=g_pre_ffn,
             g_post_ffn=g_post_ffn, w_in=w_in, conv_w=conv_w, conv_b=conv_b, dt_bias=dt_bias, a_log=a_log,
             d_skip=d_skip, ssd_norm_w=ssd_norm_w, lambda_q1=lambda_q1, lambda_k1=lambda_k1,
             lambda_q2=lambda_q2, lambda_k2=lambda_k2, subln_w=subln_w, w_br_ssd=w_br_ssd,
             w_br_attn=w_br_attn, w_out=w_out, w_router=w_router, b_router=b_router, w_gate_up=w_gate_up,
             b_gate_up=b_gate_up, w_down=w_down, b_down=b_down)
    bsz = x.shape[0]
    c_pad = jnp.pad(c, ((0, (-bsz) % 8), (0, 0)))
    for l in range(w_ada.shape[0]):
        x = _layer(x, c_pad, l, p, min(MOE_ROW_BLOCK, x.shape[0] * x.shape[1]))
    return x
```

```python
import functools
import math

import jax
import jax.numpy as jnp
from jax import lax
from jax.experimental import pallas as pl
from jax.experimental.pallas import tpu as pltpu

F32 = jnp.float32
BF16 = jnp.bfloat16
HIGHEST = lax.Precision.HIGHEST

SSD_HEADDIM = 64
SSD_GROUPS = 4
SSD_STATE = 128
SSD_CONV = 4
SSD_CHUNK = 128
ATTN_DK = 64
ATTN_DV = 128
ATTN_KV_UNIT = 512
ATTN_Q_TILE = 512
ROT_DIM = ATTN_DK // 4
ROPE_THETA = 500000.0
TOP_K = 4
SWIGLU_LIMIT = 7.0
SWIGLU_ALPHA = 1.702
NORM_EPS = 1e-6
SUB_EPS = 1e-5
LANES = 128
NEG_BIG = -1e30

VMEM_LIMIT = 56 * 1024 * 1024


def _cparams(sem, vmem=VMEM_LIMIT):
    return pltpu.CompilerParams(dimension_semantics=sem, vmem_limit_bytes=vmem)


def _sigmoid(x):
    return 1.0 / (1.0 + jnp.exp(-x))


def _ada_kernel(c_ref, w_ref, b_ref, o_ref):
    c = c_ref[...]
    sc = c * _sigmoid(c)
    o_ref[...] = jnp.dot(sc, w_ref[...], preferred_element_type=F32, precision=HIGHEST) + b_ref[...]


def _ada(c_pad, w, b, tn):
    rows, d = c_pad.shape
    n = w.shape[1]
    return pl.pallas_call(
        _ada_kernel,
        grid=(n // tn,),
        in_specs=[pl.BlockSpec((rows, d), lambda j: (0, 0)),
                  pl.BlockSpec((d, tn), lambda j: (0, j)),
                  pl.BlockSpec((1, tn), lambda j: (0, j))],
        out_specs=pl.BlockSpec((rows, tn), lambda j: (0, j)),
        out_shape=jax.ShapeDtypeStruct((rows, n), F32),
        compiler_params=_cparams(("arbitrary",)),
        name="ada",
    )(c_pad, w, b)


def _prenorm_kernel(x_ref, g_ref, sc_ref, sh_ref, o_ref):
    x = x_ref[0]
    y = x * lax.rsqrt(jnp.mean(x * x, axis=-1, keepdims=True) + NORM_EPS) * g_ref[...]
    o_ref[0] = (y * (1.0 + sc_ref[0]) + sh_ref[0]).astype(o_ref.dtype)


def _prenorm(x, g, mod3, sc_blk, sh_blk, ts):
    b, s, d = x.shape
    return pl.pallas_call(
        _prenorm_kernel,
        grid=(b, s // ts),
        in_specs=[pl.BlockSpec((1, ts, d), lambda bi, si: (bi, si, 0)),
                  pl.BlockSpec((1, d), lambda bi, si: (0, 0)),
                  pl.BlockSpec((1, 1, d), lambda bi, si: (bi, 0, sc_blk)),
                  pl.BlockSpec((1, 1, d), lambda bi, si: (bi, 0, sh_blk))],
        out_specs=pl.BlockSpec((1, ts, d), lambda bi, si: (bi, si, 0)),
        out_shape=jax.ShapeDtypeStruct((b, s, d), BF16),
        compiler_params=_cparams(("arbitrary", "arbitrary")),
        name="prenorm",
    )(x, g, mod3, mod3)


def _mm_kernel(a_ref, w_ref, o_ref):
    o_ref[...] = jnp.dot(a_ref[...], w_ref[...], preferred_element_type=F32).astype(o_ref.dtype)


def _mm_sigmoid_kernel(a_ref, w_ref, o_ref):
    acc = jnp.dot(a_ref[...], w_ref[...], preferred_element_type=F32)
    o_ref[...] = _sigmoid(acc).astype(o_ref.dtype)


def _mm_rope_kernel(a_ref, w_ref, c_ref, s1_ref, s2_ref, o_ref):
    acc = jnp.dot(a_ref[...], w_ref[...], preferred_element_type=F32)
    c = c_ref[...]
    s1 = s1_ref[...]
    s2 = s2_ref[...]
    for g in range(acc.shape[1] // LANES):
        blk = acc[:, g * LANES:(g + 1) * LANES]
        fwd = pltpu.roll(blk, LANES - ROT_DIM // 2, 1)
        bwd = pltpu.roll(blk, ROT_DIM // 2, 1)
        o_ref[:, g * LANES:(g + 1) * LANES] = (blk * c + fwd * s1 + bwd * s2).astype(o_ref.dtype)


def _mm(a, w, out_dtype, tm, tn, epilogue="none", rope_tabs=None, name="mm"):
    m, k = a.shape
    n = w.shape[1]
    in_specs = [pl.BlockSpec((tm, k), lambda i, j: (i, 0)),
                pl.BlockSpec((k, tn), lambda i, j: (0, j))]
    args = [a, w]
    if epilogue == "rope":
        seq_blocks = rope_tabs[0].shape[0] // tm
        for t in rope_tabs:
            in_specs.append(pl.BlockSpec((tm, LANES), lambda i, j: (i % seq_blocks, 0)))
            args.append(t)
        kern = _mm_rope_kernel
    elif epilogue == "sigmoid":
        kern = _mm_sigmoid_kernel
    else:
        kern = _mm_kernel
    return pl.pallas_call(
        kern,
        grid=(m // tm, n // tn),
        in_specs=in_specs,
        out_specs=pl.BlockSpec((tm, tn), lambda i, j: (i, j)),
        out_shape=jax.ShapeDtypeStruct((m, n), out_dtype),
        compiler_params=_cparams(("arbitrary", "arbitrary")),
        name=name,
    )(*args)


def _ssd_kernel(z_ref, xs_ref, bm_ref, cm_ref, dt_ref, cwx_ref, cwb_ref, cwc_ref, cbx_ref, cbb_ref,
                cbc_ref, dtb_ref, alog_ref, dsk_ref, nw_ref, o_ref,
                px_sc, pb_sc, pc_sc, st_sc, y_sc, *, n_heads):
    c = pl.program_id(1)
    L = SSD_CHUNK
    N = SSD_STATE
    hpg = n_heads // SSD_GROUPS
    gw = hpg * SSD_HEADDIM

    @pl.when(c == 0)
    def _():
        px_sc[...] = jnp.zeros_like(px_sc)
        pb_sc[...] = jnp.zeros_like(pb_sc)
        pc_sc[...] = jnp.zeros_like(pc_sc)
        st_sc[...] = jnp.zeros_like(st_sc)

    def conv_silu(u, prev_sc, w_ref, b_ref):
        prev = prev_sc[...]
        row = lax.broadcasted_iota(jnp.int32, u.shape, 0)
        acc = u * w_ref[SSD_CONV - 1:SSD_CONV, :] + b_ref[...]
        for j in range(1, SSD_CONV):
            sh = jnp.where(row < j, pltpu.roll(prev, j, 0), pltpu.roll(u, j, 0))
            acc = acc + sh * w_ref[SSD_CONV - 1 - j:SSD_CONV - j, :]
        prev_sc[...] = u
        return acc * _sigmoid(acc)

    xs = conv_silu(xs_ref[0].astype(F32), px_sc, cwx_ref, cbx_ref)
    bm = conv_silu(bm_ref[0].astype(F32), pb_sc, cwb_ref, cbb_ref)
    cm = conv_silu(cm_ref[0].astype(F32), pc_sc, cwc_ref, cbc_ref)

    raw = dt_ref[0] + dtb_ref[...]
    dt = jnp.maximum(raw, 0.0) + jnp.log1p(jnp.exp(-jnp.abs(raw)))
    a = dt * (-jnp.exp(alog_ref[...]))
    r_i = lax.broadcasted_iota(jnp.int32, (L, L), 0)
    c_i = lax.broadcasted_iota(jnp.int32, (L, L), 1)
    causal = r_i >= c_i
    tri = jnp.where(causal, 1.0, 0.0).astype(F32)
    a_cs = jnp.dot(tri, a, preferred_element_type=F32, precision=HIGHEST)
    a_cs_t = a_cs.T

    lane = lax.broadcasted_iota(jnp.int32, (L, LANES), 1)
    first = lane < SSD_HEADDIM
    acs_tiles = []
    dt_tiles = []
    for j in range(n_heads // 2):
        h0, h1 = 2 * j, 2 * j + 1
        acs_tiles.append(jnp.where(first, a_cs[:, h0:h0 + 1], a_cs[:, h1:h1 + 1]))
        dt_tiles.append(jnp.where(first, dt[:, h0:h0 + 1], dt[:, h1:h1 + 1]))
    acs_e = jnp.concatenate(acs_tiles, axis=1)
    dt_e = jnp.concatenate(dt_tiles, axis=1)
    xdt = xs * dt_e
    ea = jnp.exp(acs_e)
    alast = acs_e[L - 1:L, :]
    xdec = (xdt * jnp.exp(alast - acs_e)).astype(BF16)
    ealast = jnp.exp(alast)

    for g in range(SSD_GROUPS):
        bg = bm[:, g * N:(g + 1) * N]
        cg = cm[:, g * N:(g + 1) * N].astype(BF16)
        scores = lax.dot_general(cg, bg.astype(BF16), (((1,), (1,)), ((), ())),
                                 preferred_element_type=F32)
        st_old = st_sc[:, g * gw:(g + 1) * gw]
        y_off = jnp.dot(cg, st_old.astype(BF16), preferred_element_type=F32) * ea[:, g * gw:(g + 1) * gw]
        st_sc[:, g * gw:(g + 1) * gw] = st_old * ealast[:, g * gw:(g + 1) * gw] + jnp.dot(
            bg.T.astype(BF16), xdec[:, g * gw:(g + 1) * gw], preferred_element_type=F32)
        for jj in range(hpg // 2):
            j = g * (hpg // 2) + jj
            h0, h1 = 2 * j, 2 * j + 1
            la = jnp.exp(jnp.where(causal, a_cs[:, h0:h0 + 1] - a_cs_t[h0:h0 + 1, :], NEG_BIG))
            lb = jnp.exp(jnp.where(causal, a_cs[:, h1:h1 + 1] - a_cs_t[h1:h1 + 1, :], NEG_BIG))
            mcat = jnp.concatenate([(scores * la).astype(BF16), (scores * lb).astype(BF16)], axis=1)
            xp = xdt[:, j * LANES:(j + 1) * LANES]
            xcat = jnp.concatenate([jnp.where(first, xp, 0.0).astype(BF16),
                                    jnp.where(first, 0.0, xp).astype(BF16)], axis=0)
            y_diag = jnp.dot(mcat, xcat, preferred_element_type=F32)
            lo = jj * LANES
            y_sc[:, j * LANES:(j + 1) * LANES] = (
                y_diag + y_off[:, lo:lo + LANES]
                + dsk_ref[:, j * LANES:(j + 1) * LANES] * xs[:, j * LANES:(j + 1) * LANES])

    z = z_ref[0].astype(F32)
    u = y_sc[...] * (z * _sigmoid(z))
    for g in range(SSD_GROUPS):
        ug = u[:, g * gw:(g + 1) * gw]
        ms = jnp.mean(ug * ug, axis=-1, keepdims=True)
        o_ref[0, :, g * gw:(g + 1) * gw] = (ug * lax.rsqrt(ms + SUB_EPS)
                                            * nw_ref[:, g * gw:(g + 1) * gw]).astype(o_ref.dtype)


def _ssd(plain3, dt3, conv_w, conv_b, dtb, alog, dsk_e, norm_w, dh, n_heads):
    b, s, _ = plain3.shape
    L = SSD_CHUNK
    gn = SSD_GROUPS * SSD_STATE
    nc = s // L
    xblk = 1
    bblk = (2 * dh) // gn
    cw_x, cw_b, cw_c = conv_w[:, :dh], conv_w[:, dh:dh + gn], conv_w[:, dh + gn:]
    cb_x, cb_b, cb_c = conv_b[:, :dh], conv_b[:, dh:dh + gn], conv_b[:, dh + gn:]
    full = lambda shape: pl.BlockSpec(shape, lambda bi, ci: (0, 0))
    return pl.pallas_call(
        functools.partial(_ssd_kernel, n_heads=n_heads),
        grid=(b, nc),
        in_specs=[pl.BlockSpec((1, L, dh), lambda bi, ci: (bi, ci, 0)),
                  pl.BlockSpec((1, L, dh), lambda bi, ci: (bi, ci, xblk)),
                  pl.BlockSpec((1, L, gn), lambda bi, ci: (bi, ci, bblk)),
                  pl.BlockSpec((1, L, gn), lambda bi, ci: (bi, ci, bblk + 1)),
                  pl.BlockSpec((1, L, LANES), lambda bi, ci: (bi, ci, 0)),
                  full((SSD_CONV, dh)), full((SSD_CONV, gn)), full((SSD_CONV, gn)),
                  full((1, dh)), full((1, gn)), full((1, gn)),
                  full((1, LANES)), full((1, LANES)), full((1, dh)), full((1, dh))],
        out_specs=pl.BlockSpec((1, L, dh), lambda bi, ci: (bi, ci, 0)),
        out_shape=jax.ShapeDtypeStruct((b, s, dh), BF16),
        scratch_shapes=[pltpu.VMEM((L, dh), F32), pltpu.VMEM((L, gn), F32), pltpu.VMEM((L, gn), F32),
                        pltpu.VMEM((SSD_STATE, dh), F32), pltpu.VMEM((L, dh), F32)],
        compiler_params=_cparams(("arbitrary", "arbitrary")),
        name="ssd",
    )(plain3, plain3, plain3, plain3, dt3, cw_x, cw_b, cw_c, cb_x, cb_b, cb_c, dtb, alog, dsk_e, norm_w)


def _attn_kernel(q_ref, k_ref, v_ref, lam_ref, sw_ref, o_ref, vt_sc, st_a, st_b, m_sc, l_sc, acc_sc, *,
                 tq, lam_init):
    qi = pl.program_id(2)
    tu = ATTN_KV_UNIT
    n_all = k_ref.shape[1] // tu

    @pl.when(qi == 0)
    def _():
        def transpose_block(c, carry):
            start = pl.multiple_of(c * tu, tu)
            vt_sc[c] = v_ref[0, pl.ds(start, tu), :].astype(F32).T.astype(BF16)
            return carry

        lax.fori_loop(0, n_all, transpose_block, 0)

    qt = q_ref[0].astype(F32).T
    row = lax.broadcasted_iota(jnp.int32, qt.shape, 0)
    qts = (jnp.where(row < ATTN_DK, qt, 0.0).astype(BF16), jnp.where(row < ATTN_DK, 0.0, qt).astype(BF16))
    m_sc[...] = jnp.full_like(m_sc, NEG_BIG)
    l_sc[...] = jnp.zeros_like(l_sc)
    acc_sc[...] = jnp.zeros_like(acc_sc)

    def scores(u, st_ref):
        k = k_ref[0, pl.ds(pl.multiple_of(u * tu, tu), tu), :]
        for m in range(2):
            st_ref[m] = jnp.dot(k, qts[m], preferred_element_type=F32)

    def update(u, st_ref, masked):
        vt = vt_sc[u]
        for m in range(2):
            st = st_ref[m]
            if masked:
                kpos = u * tu + lax.broadcasted_iota(jnp.int32, st.shape, 0)
                qpos = qi * tq + lax.broadcasted_iota(jnp.int32, st.shape, 1)
                st = jnp.where(kpos <= qpos, st, NEG_BIG)
            m_prev = m_sc[m]
            m_new = jnp.maximum(m_prev, jnp.max(st, axis=0, keepdims=True))
            alpha = jnp.exp2(m_prev - m_new)
            pt = jnp.exp2(st - m_new)
            l_sc[m] = alpha * l_sc[m] + jnp.sum(pt, axis=0, keepdims=True)
            acc_sc[m] = alpha * acc_sc[m] + jnp.dot(vt, pt.astype(BF16), preferred_element_type=F32)
            m_sc[m] = m_new

    n_units = (qi * tq) // tu + 1
    n_loop = (n_units - 1) // 2
    scores(0, st_a)

    def two_units(j, carry):
        u = 2 * j
        scores(u + 1, st_b)
        update(u, st_a, False)
        scores(u + 2, st_a)
        update(u + 1, st_b, False)
        return carry

    lax.fori_loop(0, n_loop, two_units, 0)
    last = n_units - 1

    @pl.when(last == 2 * n_loop)
    def _():
        update(last, st_a, True)

    @pl.when(last != 2 * n_loop)
    def _():
        scores(last, st_b)
        update(last - 1, st_a, False)
        update(last, st_b, True)

    lv = lam_ref[...]
    lam = (jnp.exp(jnp.sum(lv[0:1] * lv[1:2], axis=-1, keepdims=True))
           - jnp.exp(jnp.sum(lv[2:3] * lv[3:4], axis=-1, keepdims=True)) + lam_init)
    ot = acc_sc[0] * (1.0 / l_sc[0]) - lam * (acc_sc[1] * (1.0 / l_sc[1]))
    ot = ot * lax.rsqrt(jnp.mean(ot * ot, axis=0, keepdims=True) + SUB_EPS)
    o_ref[0] = (ot.T * sw_ref[...] * (1.0 - lam_init)).astype(o_ref.dtype)


def _attention(qk3, plain3, lam_rows, subln_w, n_heads, v_blk0, tq, lam_init):
    b, s, _ = qk3.shape
    tu = ATTN_KV_UNIT
    return pl.pallas_call(
        functools.partial(_attn_kernel, tq=tq, lam_init=lam_init),
        grid=(b, n_heads, s // tq),
        in_specs=[pl.BlockSpec((1, tq, LANES), lambda bi, hi, qi: (bi, qi, hi)),
                  pl.BlockSpec((1, s, LANES), lambda bi, hi, qi: (bi, 0, n_heads + hi)),
                  pl.BlockSpec((1, s, LANES), lambda bi, hi, qi: (bi, 0, v_blk0 + hi)),
                  pl.BlockSpec((8, LANES), lambda bi, hi, qi: (0, 0)),
                  pl.BlockSpec((1, LANES), lambda bi, hi, qi: (0, 0))],
        out_specs=pl.BlockSpec((1, tq, LANES), lambda bi, hi, qi: (bi, qi, hi)),
        out_shape=jax.ShapeDtypeStruct((b, s, n_heads * ATTN_DV), BF16),
        scratch_shapes=[pltpu.VMEM((s // tu, ATTN_DV, tu), BF16),
                        pltpu.VMEM((2, tu, tq), F32), pltpu.VMEM((2, tu, tq), F32),
                        pltpu.VMEM((2, 1, tq), F32), pltpu.VMEM((2, 1, tq), F32),
                        pltpu.VMEM((2, ATTN_DV, tq), F32)],
        compiler_params=_cparams(("arbitrary", "arbitrary", "arbitrary")),
        name="diff_attn",
    )(qk3, qk3, plain3, lam_rows, subln_w)


def _merge_kernel(y_ref, o_ref, ws_ref, wa_ref, gs_ref, ga_ref, out_ref):
    bs = jnp.dot(y_ref[...], ws_ref[...], preferred_element_type=F32)
    ba = jnp.dot(o_ref[...], wa_ref[...], preferred_element_type=F32)
    out_ref[...] = (gs_ref[...].astype(F32) * bs + ga_ref[...].astype(F32) * ba).astype(out_ref.dtype)


def _merge(y, o, ws, wa, gates, tm, tn):
    m, k = y.shape
    n = ws.shape[1]
    nj = n // tn
    return pl.pallas_call(
        _merge_kernel,
        grid=(m // tm, nj),
        in_specs=[pl.BlockSpec((tm, k), lambda i, j: (i, 0)),
                  pl.BlockSpec((tm, o.shape[1]), lambda i, j: (i, 0)),
                  pl.BlockSpec((k, tn), lambda i, j: (0, j)),
                  pl.BlockSpec((o.shape[1], tn), lambda i, j: (0, j)),
                  pl.BlockSpec((tm, tn), lambda i, j: (i, j)),
                  pl.BlockSpec((tm, tn), lambda i, j: (i, nj + j))],
        out_specs=pl.BlockSpec((tm, tn), lambda i, j: (i, j)),
        out_shape=jax.ShapeDtypeStruct((m, n), BF16),
        compiler_params=_cparams(("arbitrary", "arbitrary")),
        name="merge",
    )(y, o, ws, wa, gates, gates)


def _outproj_kernel(mg_ref, wo_ref, x_ref, gpost_ref, gt_ref, gpre_ref, sc_ref, sh_ref, wr_hi_ref,
                    wr_lo_ref, br_ref, x1_ref, hp_ref, lg_ref):
    mix = jnp.dot(mg_ref[...], wo_ref[...], preferred_element_type=F32)
    nm = mix * lax.rsqrt(jnp.mean(mix * mix, axis=-1, keepdims=True) + NORM_EPS) * gpost_ref[...]
    x1 = x_ref[...] + gt_ref[0] * nm
    x1_ref[...] = x1
    h2 = (x1 * lax.rsqrt(jnp.mean(x1 * x1, axis=-1, keepdims=True) + NORM_EPS) * gpre_ref[...]
          * (1.0 + sc_ref[0]) + sh_ref[0])
    h_hi = h2.astype(BF16)
    h_lo = (h2 - h_hi.astype(F32)).astype(BF16)
    wr_hi = wr_hi_ref[...]
    lg_ref[...] = (jnp.dot(h_hi, wr_hi, preferred_element_type=F32)
                   + jnp.dot(h_lo, wr_hi, preferred_element_type=F32)
                   + jnp.dot(h_hi, wr_lo_ref[...], preferred_element_type=F32) + br_ref[...])
    bits = lax.bitcast_convert_type(h_hi.astype(F32), jnp.uint32)
    half = bits.shape[1] // 2
    hp_ref[...] = (bits[:, :half] >> 16) | bits[:, half:]


def _outproj(merged, wo, x2, gpost, mod3, gpre, wr_hi, wr_lo, br, seq, tm, gt_blk, sc_blk, sh_blk):
    m, d = x2.shape
    per_b = seq // tm
    row = lambda i: (i, 0)
    const = lambda i: (0, 0)
    return pl.pallas_call(
        _outproj_kernel,
        grid=(m // tm,),
        in_specs=[pl.BlockSpec((tm, d), row),
                  pl.BlockSpec((d, d), const),
                  pl.BlockSpec((tm, d), row),
                  pl.BlockSpec((1, d), const),
                  pl.BlockSpec((1, 1, d), lambda i: (i // per_b, 0, gt_blk)),
                  pl.BlockSpec((1, d), const),
                  pl.BlockSpec((1, 1, d), lambda i: (i // per_b, 0, sc_blk)),
                  pl.BlockSpec((1, 1, d), lambda i: (i // per_b, 0, sh_blk)),
                  pl.BlockSpec((d, LANES), const),
                  pl.BlockSpec((d, LANES), const),
                  pl.BlockSpec((1, LANES), const)],
        out_specs=[pl.BlockSpec((tm, d), row),
                   pl.BlockSpec((tm, d // 2), row),
                   pl.BlockSpec((tm, LANES), row)],
        out_shape=[jax.ShapeDtypeStruct((m, d), F32),
                   jax.ShapeDtypeStruct((m, d // 2), jnp.uint32),
                   jax.ShapeDtypeStruct((m, LANES), F32)],
        compiler_params=_cparams(("arbitrary",)),
        name="outproj",
    )(merged, wo, x2, gpost, mod3, gpre, mod3, mod3, wr_hi, wr_lo, br)


def _route_kernel(lg_ref, dest_ref, w_ref, cnt_ref, cnt_sc, pst_sc, run_sc, *, n_experts, blk):
    ph = pl.program_id(0)
    t = pl.program_id(1)
    tk = lg_ref.shape[0]
    lane = lax.broadcasted_iota(jnp.int32, (tk, LANES), 1)
    lg = jnp.where(lane < n_experts, lg_ref[...], -jnp.inf)
    vals = []
    hots = []
    for _ in range(TOP_K):
        mx = jnp.max(lg, axis=-1, keepdims=True)
        ix = jnp.min(jnp.where(lg == mx, lane, LANES), axis=-1, keepdims=True)
        hot = lane == ix
        lg = jnp.where(hot, -jnp.inf, lg)
        vals.append(mx)
        hots.append(hot)
    multi = jnp.zeros((tk, LANES), F32)
    for hot in hots:
        multi = multi + jnp.where(hot, 1.0, 0.0)
    colsum = jnp.sum(multi, axis=0, keepdims=True)

    @pl.when((ph == 0) & (t == 0))
    def _():
        cnt_sc[...] = jnp.zeros_like(cnt_sc)

    @pl.when(ph == 0)
    def _():
        cnt_sc[...] += colsum

    @pl.when((ph == 1) & (t == 0))
    def _():
        cnt = cnt_sc[...].astype(jnp.int32)
        padded = (((cnt + (blk - 1)) // blk) * blk).astype(F32)
        r_i = lax.broadcasted_iota(jnp.int32, (LANES, LANES), 0)
        c_i = lax.broadcasted_iota(jnp.int32, (LANES, LANES), 1)
        upper = jnp.where(r_i < c_i, 1.0, 0.0).astype(F32)
        pst_sc[...] = jnp.dot(jnp.broadcast_to(padded, (8, LANES)), upper,
                              preferred_element_type=F32, precision=HIGHEST)[0:1]
        run_sc[...] = jnp.zeros_like(run_sc)

    @pl.when(ph == 1)
    def _():
        r_i = lax.broadcasted_iota(jnp.int32, (tk, tk), 0)
        c_i = lax.broadcasted_iota(jnp.int32, (tk, tk), 1)
        strict = jnp.where(r_i > c_i, 1.0, 0.0).astype(BF16)
        before = jnp.dot(strict, multi.astype(BF16), preferred_element_type=F32)
        base = before + run_sc[...] + pst_sc[...]
        esum = jnp.zeros((tk, 1), F32)
        evals = []
        for r in range(TOP_K):
            e = jnp.exp(vals[r] - vals[0])
            evals.append(e)
            esum = esum + e
        dest = jnp.zeros((tk, LANES), jnp.int32)
        wts = jnp.zeros((tk, LANES), F32)
        for r in range(TOP_K):
            d_r = jnp.sum(jnp.where(hots[r], base, 0.0), axis=-1, keepdims=True).astype(jnp.int32)
            dest = jnp.where(lane == r, d_r, dest)
            wts = jnp.where(lane == r, evals[r] / esum, wts)
        dest_ref[...] = dest
        w_ref[...] = wts
        run_sc[...] += colsum
        cnt_ref[...] = jnp.broadcast_to(cnt_sc[...], cnt_ref.shape)


def _route(logits, n_experts, blk, tk):
    t = logits.shape[0]
    return pl.pallas_call(
        functools.partial(_route_kernel, n_experts=n_experts, blk=blk),
        grid=(2, t // tk),
        in_specs=[pl.BlockSpec((tk, LANES), lambda ph, ti: (ti, 0))],
        out_specs=[pl.BlockSpec((tk, LANES), lambda ph, ti: (ti * ph, 0)),
                   pl.BlockSpec((tk, LANES), lambda ph, ti: (ti * ph, 0)),
                   pl.BlockSpec((8, LANES), lambda ph, ti: (0, 0))],
        out_shape=[jax.ShapeDtypeStruct((t, LANES), jnp.int32),
                   jax.ShapeDtypeStruct((t, LANES), F32),
                   jax.ShapeDtypeStruct((8, LANES), F32)],
        scratch_shapes=[pltpu.VMEM((1, LANES), F32), pltpu.VMEM((1, LANES), F32), pltpu.VMEM((1, LANES), F32)],
        compiler_params=_cparams(("arbitrary", "arbitrary")),
        name="route",
    )(logits)


def _pad_fill_copies(b, nv_sm, zero_sc, xs_hbm, sem, blk):
    nv = nv_sm[b]
    head = (-nv) & 7
    out = []
    for r in range(7):
        out.append((r < head, pltpu.make_async_copy(
            zero_sc.at[pl.ds(0, 1), :], xs_hbm.at[pl.ds(b * blk + nv + r, 1), :], sem)))
    off = b * blk + nv + head
    rest = blk - nv - head
    p = blk
    while p >= 8:
        cond = (rest & p) != 0
        out.append((cond, pltpu.make_async_copy(
            zero_sc.at[pl.ds(0, p), :], xs_hbm.at[pl.ds(pl.multiple_of(off, 8), p), :], sem)))
        off = off + jnp.where(cond, p, 0)
        p //= 2
    return out


def _dispatch_kernel(dest_sm, nv_sm, h_ref, xs_hbm, zero_sc, sem, fill_sem, *, tt, blk, nb):
    base = pl.program_id(0) * tt

    @pl.when(pl.program_id(0) == 0)
    def _():
        zero_sc[...] = jnp.zeros_like(zero_sc)

        def fill(b, carry):
            for cond, cp in _pad_fill_copies(b, nv_sm, zero_sc, xs_hbm, fill_sem, blk):
                pl.when(cond)(cp.start)
            return carry

        def fill_wait(b, carry):
            for cond, cp in _pad_fill_copies(b, nv_sm, zero_sc, xs_hbm, fill_sem, blk):
                pl.when(cond)(cp.wait)
            return carry

        lax.fori_loop(0, nb, fill, 0)
        lax.fori_loop(0, nb, fill_wait, 0)

    def body(t, carry):
        for k in range(TOP_K):
            d = dest_sm[(base + t) * TOP_K + k]
            pltpu.make_async_copy(h_ref.at[pl.ds(t, 1), :], xs_hbm.at[pl.ds(d, 1), :], sem).start()
        return carry

    lax.fori_loop(0, tt, body, 0)

    def wait_body(t, carry):
        for k in range(TOP_K):
            pltpu.make_async_copy(h_ref.at[pl.ds(t, 1), :], xs_hbm.at[pl.ds(0, 1), :], sem).wait()
        return carry

    lax.fori_loop(0, tt, wait_body, 0)


def _dispatch(dest_flat, nvalid, h_packed, n_slots, tt, blk):
    t, dp = h_packed.shape
    nb = n_slots // blk
    return pl.pallas_call(
        functools.partial(_dispatch_kernel, tt=tt, blk=blk, nb=nb),
        grid_spec=pltpu.PrefetchScalarGridSpec(
            num_scalar_prefetch=2,
            grid=(t // tt,),
            in_specs=[pl.BlockSpec((tt, dp), lambda i, d, nv: (i, 0))],
            out_specs=pl.BlockSpec(memory_space=pl.ANY),
            scratch_shapes=[pltpu.VMEM((blk, dp), jnp.uint32), pltpu.SemaphoreType.DMA(()),
                            pltpu.SemaphoreType.DMA(())],
        ),
        out_shape=jax.ShapeDtypeStruct((n_slots, dp), jnp.uint32),
        compiler_params=_cparams(("arbitrary",)),
        name="dispatch",
    )(dest_flat, nvalid, h_packed)


def _expert_changed(be, i, last):
    ii = jnp.minimum(i, last)
    prev = jnp.maximum(ii - 1, 0)
    return (i == 0) | (be[ii] != be[prev])


def _gateup_kernel(be, nv, nu, x_ref, wg_ref, wu_ref, bg_ref, bu_ref, o_ref, wg_sc, wu_sc):
    i = pl.program_id(1)
    active = i < nu[0]

    @pl.when(active & _expert_changed(be, i, nu[0] - 1))
    def _():
        wg_sc[...] = wg_ref[0].astype(BF16)
        wu_sc[...] = wu_ref[0].astype(BF16)

    sub = o_ref.shape[0] // MOE_SUB_BLOCKS
    for r in range(MOE_SUB_BLOCKS):
        rows = slice(r * sub, (r + 1) * sub)
        live = active & (nv[i] > r * sub)

        @pl.when(live)
        def _(rows=rows):
            u = x_ref[rows, :]
            xa = lax.bitcast_convert_type(u << 16, F32).astype(BF16)
            xb = lax.bitcast_convert_type(u & jnp.uint32(0xFFFF0000), F32).astype(BF16)
            x = jnp.concatenate([xa, xb], axis=1)
            g = jnp.dot(x, wg_sc[...], preferred_element_type=F32) + bg_ref[0]
            up = jnp.dot(x, wu_sc[...], preferred_element_type=F32) + bu_ref[0]
            gate = jnp.minimum(g, SWIGLU_LIMIT)
            up = jnp.clip(up, -SWIGLU_LIMIT, SWIGLU_LIMIT)
            o_ref[rows, :] = ((up + 1.0) * gate * _sigmoid(SWIGLU_ALPHA * gate)).astype(o_ref.dtype)

        @pl.when(jnp.logical_not(live))
        def _(rows=rows):
            o_ref[rows, :] = jnp.zeros((sub, o_ref.shape[1]), o_ref.dtype)


def _gateup(block_e, nvalid, nused, xs, w_gu, b_gu3, tm, th):
    n_slots, dp = xs.shape
    d = 2 * dp
    dff = w_gu.shape[2] // 2
    nj = dff // th
    nb = n_slots // tm

    def blk(i, nu):
        return jnp.minimum(i, nu[0] - 1)

    return pl.pallas_call(
        _gateup_kernel,
        grid_spec=pltpu.PrefetchScalarGridSpec(
            num_scalar_prefetch=3,
            grid=(nj, nb),
            in_specs=[pl.BlockSpec((tm, dp), lambda j, i, be, nv, nu: (blk(i, nu), 0)),
                      pl.BlockSpec((1, d, th), lambda j, i, be, nv, nu: (be[blk(i, nu)], 0, j)),
                      pl.BlockSpec((1, d, th), lambda j, i, be, nv, nu: (be[blk(i, nu)], 0, nj + j)),
                      pl.BlockSpec((1, 1, th), lambda j, i, be, nv, nu: (be[blk(i, nu)], 0, j)),
                      pl.BlockSpec((1, 1, th), lambda j, i, be, nv, nu: (be[blk(i, nu)], 0, nj + j))],
            out_specs=pl.BlockSpec((tm, th), lambda j, i, be, nv, nu: (i, j)),
            scratch_shapes=[pltpu.VMEM((d, th), BF16), pltpu.VMEM((d, th), BF16)],
        ),
        out_shape=jax.ShapeDtypeStruct((n_slots, dff), BF16),
        compiler_params=_cparams(("arbitrary", "arbitrary")),
        name="expert_gate_up",
    )(block_e, nvalid, nused, xs, w_gu, w_gu, b_gu3, b_gu3)


def _down_kernel(be, nv, nu, a_ref, wd_ref, bd_ref, o_ref, wd_sc):
    i = pl.program_id(1)
    active = i < nu[0]

    @pl.when(active & _expert_changed(be, i, nu[0] - 1))
    def _():
        wd_sc[...] = wd_ref[0].astype(BF16)

    sub = o_ref.shape[0] // MOE_SUB_BLOCKS
    for r in range(MOE_SUB_BLOCKS):
        rows = slice(r * sub, (r + 1) * sub)
        live = active & (nv[i] > r * sub)

        @pl.when(live)
        def _(rows=rows):
            o_ref[rows, :] = jnp.dot(a_ref[rows, :], wd_sc[...], preferred_element_type=F32) + bd_ref[0]

        @pl.when(jnp.logical_not(live))
        def _(rows=rows):
            o_ref[rows, :] = jnp.zeros((sub, o_ref.shape[1]), o_ref.dtype)


def _down(block_e, nvalid, nused, act, w_d, b_d3, tm, tn):
    n_slots, dff = act.shape
    d = w_d.shape[2]
    nj = d // tn
    nb = n_slots // tm

    def blk(i, nu):
        return jnp.minimum(i, nu[0] - 1)

    return pl.pallas_call(
        _down_kernel,
        grid_spec=pltpu.PrefetchScalarGridSpec(
            num_scalar_prefetch=3,
            grid=(nj, nb),
            in_specs=[pl.BlockSpec((tm, dff), lambda j, i, be, nv, nu: (blk(i, nu), 0)),
                      pl.BlockSpec((1, dff, tn), lambda j, i, be, nv, nu: (be[blk(i, nu)], 0, j)),
                      pl.BlockSpec((1, 1, tn), lambda j, i, be, nv, nu: (be[blk(i, nu)], 0, j))],
            out_specs=pl.BlockSpec((tm, tn), lambda j, i, be, nv, nu: (i, j)),
            scratch_shapes=[pltpu.VMEM((dff, tn), BF16)],
        ),
        out_shape=jax.ShapeDtypeStruct((n_slots, d), F32),
        compiler_params=_cparams(("arbitrary", "arbitrary")),
        name="expert_down",
    )(block_e, nvalid, nused, act, w_d, b_d3)


def _combine_kernel(dest_sm, y_hbm, w_ref, x1_ref, gt_ref, g_ref, o_ref, buf, sem, *, tt):
    base = pl.program_id(0) * tt

    def body(t, carry):
        for k in range(TOP_K):
            d = dest_sm[(base + t) * TOP_K + k]
            pltpu.make_async_copy(y_hbm.at[pl.ds(d, 1), :], buf.at[pl.ds(k * tt + t, 1), :], sem).start()
        return carry

    lax.fori_loop(0, tt, body, 0)

    def wait_body(t, carry):
        for k in range(TOP_K):
            pltpu.make_async_copy(y_hbm.at[pl.ds(0, 1), :], buf.at[pl.ds(k * tt + t, 1), :], sem).wait()
        return carry

    lax.fori_loop(0, tt, wait_body, 0)

    w = w_ref[...]
    f = buf[0:tt, :] * w[:, 0:1]
    for k in range(1, TOP_K):
        f = f + buf[k * tt:(k + 1) * tt, :] * w[:, k:k + 1]
    nf = f * lax.rsqrt(jnp.mean(f * f, axis=-1, keepdims=True) + NORM_EPS) * g_ref[...]
    o_ref[...] = x1_ref[...] + gt_ref[0] * nf


def _combine(dest_flat, y_sorted, wts, x1, mod3, gpost, seq, tt, gt_blk):
    t, d = x1.shape
    per_b = seq // tt
    return pl.pallas_call(
        functools.partial(_combine_kernel, tt=tt),
        grid_spec=pltpu.PrefetchScalarGridSpec(
            num_scalar_prefetch=1,
            grid=(t // tt,),
            in_specs=[pl.BlockSpec(memory_space=pl.ANY),
                      pl.BlockSpec((tt, LANES), lambda i, ds: (i, 0)),
                      pl.BlockSpec((tt, d), lambda i, ds: (i, 0)),
                      pl.BlockSpec((1, 1, d), lambda i, ds: (i // per_b, 0, gt_blk)),
                      pl.BlockSpec((1, d), lambda i, ds: (0, 0))],
            out_specs=pl.BlockSpec((tt, d), lambda i, ds: (i, 0)),
            scratch_shapes=[pltpu.VMEM((TOP_K * tt, d), F32), pltpu.SemaphoreType.DMA(())],
        ),
        out_shape=jax.ShapeDtypeStruct((t, d), F32),
        compiler_params=_cparams(("arbitrary",)),
        name="combine",
    )(dest_flat, y_sorted, wts, x1, mod3, gpost)


def _tile(n, pref):
    t = min(n, pref)
    while n % t:
        t //= 2
    return t


def _rope_tables(seq):
    half = ROT_DIM // 2
    inv = ROPE_THETA ** (-jnp.arange(0, ROT_DIM, 2, dtype=F32) / ROT_DIM)
    ang = jnp.arange(seq, dtype=F32)[:, None] * inv[None, :]
    cos, sin = jnp.cos(ang), jnp.sin(ang)
    ones = jnp.ones((seq, ATTN_DK - ROT_DIM), F32)
    zeros = jnp.zeros((seq, ATTN_DK - ROT_DIM), F32)
    zh = jnp.zeros((seq, half), F32)
    c64 = jnp.concatenate([cos, cos, ones], axis=1)
    s1_64 = jnp.concatenate([-sin, zh, zeros], axis=1)
    s2_64 = jnp.concatenate([zh, sin, zeros], axis=1)
    rep = LANES // ATTN_DK
    return tuple(jnp.tile(t, (1, rep)) for t in (c64, s1_64, s2_64))


def _layer(x, c_pad, l, p, moe_blk):
    bsz, seq, d = x.shape
    t = bsz * seq
    dh = d
    n_sheads = dh // SSD_HEADDIM
    gn = SSD_GROUPS * SSD_STATE
    n_aheads = d // ATTN_DV
    aw = n_aheads * ATTN_DV
    qkw = 2 * n_aheads * ATTN_DK
    n_experts = p["w_router"].shape[-1]

    mod = _ada(c_pad, p["w_ada"][l], p["b_ada"][l][None, :], _tile(6 * d, 1024))
    mod3 = mod[:bsz].reshape(bsz, 1, 6 * d)

    w_in = p["w_in"][l]
    o = 0
    segs = {}
    for name, size in (("z", dh), ("xbc", dh + 2 * gn), ("dt", n_sheads), ("q", qkw), ("k", qkw),
                       ("v", aw), ("gs", d), ("ga", d)):
        segs[name] = w_in[:, o:o + size]
        o += size
    scale = ATTN_DK ** -0.5 * math.log2(math.e)
    w_plain = jnp.concatenate([segs["z"], segs["xbc"], segs["v"]], axis=1).astype(BF16)
    w_rope = jnp.concatenate([segs["q"] * scale, segs["k"]], axis=1).astype(BF16)
    w_gate = jnp.concatenate([segs["gs"], segs["ga"]], axis=1).astype(BF16)
    w_dt = jnp.pad(segs["dt"], ((0, 0), (0, LANES - n_sheads))).astype(BF16)

    h = _prenorm(x, p["g_pre_mix"][l][None, :], mod3, 1, 0, _tile(seq, 512)).reshape(t, d)
    tm = _tile(seq, 1024)
    plain = _mm(h, w_plain, BF16, tm, _tile(w_plain.shape[1], 512), name="proj_plain")
    qk = _mm(h, w_rope, BF16, tm, _tile(w_rope.shape[1], 512), "rope", _rope_tables(seq), name="proj_rope")
    gates = _mm(h, w_gate, BF16, tm, _tile(w_gate.shape[1], 512), "sigmoid", name="proj_gate")
    dt_raw = _mm(h, w_dt, F32, tm, LANES, name="proj_dt")

    pad_h = lambda v: jnp.pad(v, (0, LANES - n_sheads))[None, :]
    plain3 = plain.reshape(bsz, seq, plain.shape[1])
    y_ssd = _ssd(plain3, dt_raw.reshape(bsz, seq, LANES), p["conv_w"][l], p["conv_b"][l][None, :],
                 pad_h(p["dt_bias"][l]), pad_h(p["a_log"][l]),
                 jnp.repeat(p["d_skip"][l], SSD_HEADDIM)[None, :], p["ssd_norm_w"][l][None, :], dh, n_sheads)

    lam_init = 0.8 - 0.6 * math.exp(-0.3 * l)
    lam_rows = jnp.zeros((8, LANES), F32)
    for r, nm in enumerate(("lambda_q1", "lambda_k1", "lambda_q2", "lambda_k2")):
        lam_rows = lam_rows.at[r, :ATTN_DK].set(p[nm][l])
    v_blk0 = (dh + dh + 2 * gn) // LANES
    o_attn = _attention(qk.reshape(bsz, seq, 2 * qkw), plain3, lam_rows, p["subln_w"][l][None, :],
                        n_aheads, v_blk0, _tile(seq, ATTN_Q_TILE), lam_init)

    tm2 = _tile(seq, 512)
    merged = _merge(y_ssd.reshape(t, dh), o_attn.reshape(t, aw), p["w_br_ssd"][l].astype(BF16),
                    p["w_br_attn"][l].astype(BF16), gates, tm2, _tile(d, 512))
    wr = jnp.pad(p["w_router"][l], ((0, 0), (0, LANES - n_experts)))
    wr_hi = wr.astype(BF16)
    wr_lo = (wr - wr_hi.astype(F32)).astype(BF16)
    br = jnp.pad(p["b_router"][l], (0, LANES - n_experts))[None, :]
    x1, h_packed, logits = _outproj(merged, p["w_out"][l].astype(BF16), x.reshape(t, d),
                                    p["g_post_mix"][l][None, :], mod3, p["g_pre_ffn"][l][None, :],
                                    wr_hi, wr_lo, br, seq, tm2, 2, 4, 3)

    dest, wts, cnt = _route(logits, n_experts, moe_blk, _tile(t, 512))
    counts = cnt[0, :n_experts].astype(jnp.int32)
    n_slots = t * TOP_K + n_experts * moe_blk
    nb = n_slots // moe_blk
    pblocks = (counts + moe_blk - 1) // moe_blk
    pend = jnp.cumsum(pblocks)
    nused = jnp.maximum(pend[-1], 1).astype(jnp.int32)
    bidx = jnp.arange(nb, dtype=jnp.int32)
    block_e = jnp.minimum(jnp.sum((pend[None, :] <= bidx[:, None]).astype(jnp.int32), axis=1), n_experts - 1)
    pstart = pend - pblocks
    nvalid = jnp.clip(counts[block_e] - (bidx - pstart[block_e]) * moe_blk, 0, moe_blk).astype(jnp.int32)
    dest_flat = dest[:, :TOP_K].reshape(-1)
    nused1 = nused.reshape(1)

    xs = _dispatch(dest_flat, nvalid, h_packed, n_slots, _tile(t, 256), moe_blk)
    dff = p["w_down"].shape[2]
    act = _gateup(block_e, nvalid, nused1, xs, p["w_gate_up"][l], p["b_gate_up"][l][:, None, :],
                  moe_blk, _tile(dff, 512))
    y_sorted = _down(block_e, nvalid, nused1, act, p["w_down"][l], p["b_down"][l][:, None, :],
                     moe_blk, _tile(d, 1024))
    out = _combine(dest_flat, y_sorted, wts, x1, mod3, p["g_post_ffn"][l][None, :], seq, _tile(seq, 128), 5)
    return out.reshape(bsz, seq, d)


MOE_ROW_BLOCK = 512
MOE_SUB_BLOCKS = 2


def kernel(x, c, w_ada, b_ada, g_pre_mix, g_post_mix, g_pre_ffn, g_post_ffn, w_in, conv_w, conv_b, dt_bias, a_log, d_skip, ssd_norm_w, lambda_q1, lambda_k1, lambda_q2, lambda_k2, subln_w, w_br_ssd, w_br_attn, w_out, w_router, b_router, w_gate_up, b_gate_up, w_down, b_down):
    p = dict(w_ada=w_ada, b_ada=b_ada, g_pre_mix=g_pre_mix, g_post_mix=g_post_mix, g_pre_ffn=g_pre_ffn,
             g_post_ffn=g_post_ffn, w_in=w_in, conv_w=conv_w, conv_b=conv_b, dt_bias=dt_bias, a_log=a_log,
             d_skip=d_skip, ssd_norm_w=ssd_norm_w, lambda_q1=lambda_q1, lambda_k1=lambda_k1,
             lambda_q2=lambda_q2, lambda_k2=lambda_k2, subln_w=subln_w, w_br_ssd=w_br_ssd,
             w_br_attn=w_br_attn, w_out=w_out, w_router=w_router, b_router=b_router, w_gate_up=w_gate_up,
             b_gate_up=b_gate_up, w_down=w_down, b_down=b_down)
    bsz = x.shape[0]
    c_pad = jnp.pad(c, ((0, (-bsz) % 8), (0, 0)))
    for l in range(w_ada.shape[0]):
        x = _layer(x, c_pad, l, p, min(MOE_ROW_BLOCK, x.shape[0] * x.shape[1]))
    return x
```

```python
import functools
import math

import jax
import jax.numpy as jnp
from jax import lax
from jax.experimental import pallas as pl
from jax.experimental.pallas import tpu as pltpu

F32 = jnp.float32
BF16 = jnp.bfloat16
HIGHEST = lax.Precision.HIGHEST

SSD_HEADDIM = 64
SSD_GROUPS = 4
SSD_STATE = 128
SSD_CONV = 4
SSD_CHUNK = 128
ATTN_DK = 64
ATTN_DV = 128
ATTN_KV_UNIT = 512
ATTN_Q_TILE = 512
ATTN_HEADS_PER_STEP = 2
ROT_DIM = ATTN_DK // 4
ROPE_THETA = 500000.0
TOP_K = 4
SWIGLU_LIMIT = 7.0
SWIGLU_ALPHA = 1.702
NORM_EPS = 1e-6
SUB_EPS = 1e-5
LANES = 128
NEG_BIG = -1e30

VMEM_LIMIT = 56 * 1024 * 1024


def _cparams(sem, vmem=VMEM_LIMIT):
    return pltpu.CompilerParams(dimension_semantics=sem, vmem_limit_bytes=vmem)


def _sigmoid(x):
    return 1.0 / (1.0 + jnp.exp(-x))


def _store_token_rows(ref, first_tok, val):
    n_tok, width = val.shape
    ns = width // LANES
    for s_ in range(ns):
        ref[pl.ds(first_tok * ns + s_, n_tok, stride=ns), :] = val[:, s_ * LANES:(s_ + 1) * LANES]


def _load_token_rows(ref, first_tok, n_tok, ns):
    return jnp.concatenate([ref[pl.ds(first_tok * ns + s_, n_tok, stride=ns), :] for s_ in range(ns)], axis=1)


def _ada_kernel(c_ref, w_ref, b_ref, o_ref):
    c = c_ref[...]
    sc = c * _sigmoid(c)
    o_ref[...] = jnp.dot(sc, w_ref[...], preferred_element_type=F32, precision=HIGHEST) + b_ref[...]


def _ada(c_pad, w, b, tn):
    rows, d = c_pad.shape
    n = w.shape[1]
    return pl.pallas_call(
        _ada_kernel,
        grid=(n // tn,),
        in_specs=[pl.BlockSpec((rows, d), lambda j: (0, 0)),
                  pl.BlockSpec((d, tn), lambda j: (0, j)),
                  pl.BlockSpec((1, tn), lambda j: (0, j))],
        out_specs=pl.BlockSpec((rows, tn), lambda j: (0, j)),
        out_shape=jax.ShapeDtypeStruct((rows, n), F32),
        compiler_params=_cparams(("arbitrary",)),
        name="ada",
    )(c_pad, w, b)


def _prenorm_kernel(x_ref, g_ref, sc_ref, sh_ref, o_ref):
    x = x_ref[0]
    y = x * lax.rsqrt(jnp.mean(x * x, axis=-1, keepdims=True) + NORM_EPS) * g_ref[...]
    o_ref[0] = (y * (1.0 + sc_ref[0]) + sh_ref[0]).astype(o_ref.dtype)


def _prenorm(x, g, mod3, sc_blk, sh_blk, ts):
    b, s, d = x.shape
    return pl.pallas_call(
        _prenorm_kernel,
        grid=(b, s // ts),
        in_specs=[pl.BlockSpec((1, ts, d), lambda bi, si: (bi, si, 0)),
                  pl.BlockSpec((1, d), lambda bi, si: (0, 0)),
                  pl.BlockSpec((1, 1, d), lambda bi, si: (bi, 0, sc_blk)),
                  pl.BlockSpec((1, 1, d), lambda bi, si: (bi, 0, sh_blk))],
        out_specs=pl.BlockSpec((1, ts, d), lambda bi, si: (bi, si, 0)),
        out_shape=jax.ShapeDtypeStruct((b, s, d), BF16),
        compiler_params=_cparams(("arbitrary", "arbitrary")),
        name="prenorm",
    )(x, g, mod3, mod3)


def _regroup_kernel(w_ref, plain_ref, rope_ref, gate_ref, dt_ref, *, segs, q_scale):
    def seg(name):
        lo, hi = segs[name]
        return w_ref[:, lo:hi]

    o = 0
    for name in ("z", "xbc", "v"):
        blk = seg(name)
        plain_ref[:, o:o + blk.shape[1]] = blk.astype(BF16)
        o += blk.shape[1]
    q = seg("q")
    rope_ref[:, :q.shape[1]] = (q * q_scale).astype(BF16)
    rope_ref[:, q.shape[1]:] = seg("k").astype(BF16)
    gs = seg("gs")
    gate_ref[:, :gs.shape[1]] = gs.astype(BF16)
    gate_ref[:, gs.shape[1]:] = seg("ga").astype(BF16)
    dt = seg("dt")
    dt_ref[...] = jnp.zeros_like(dt_ref)
    dt_ref[:, :dt.shape[1]] = dt.astype(BF16)


def _regroup(w_in, segs, q_scale, tk):
    k, n = w_in.shape
    width = lambda *names: sum(segs[nm][1] - segs[nm][0] for nm in names)
    widths = (width("z", "xbc", "v"), width("q", "k"), width("gs", "ga"), LANES)
    return pl.pallas_call(
        functools.partial(_regroup_kernel, segs=segs, q_scale=q_scale),
        grid=(k // tk,),
        in_specs=[pl.BlockSpec((tk, n), lambda i: (i, 0))],
        out_specs=[pl.BlockSpec((tk, wd), lambda i: (i, 0)) for wd in widths],
        out_shape=[jax.ShapeDtypeStruct((k, wd), BF16) for wd in widths],
        compiler_params=_cparams(("arbitrary",)),
        name="regroup_w_in",
    )(w_in)


def _mm_kernel(a_ref, w_ref, o_ref):
    o_ref[...] = jnp.dot(a_ref[...], w_ref[...], preferred_element_type=F32).astype(o_ref.dtype)


def _row_chunks(n_rows):
    step = n_rows // MM_ROW_CHUNKS if n_rows % MM_ROW_CHUNKS == 0 and n_rows >= 64 * MM_ROW_CHUNKS else n_rows
    return [slice(r, r + step) for r in range(0, n_rows, step)]


def _mm_sigmoid_kernel(a_ref, w_ref, o_ref):
    for rows in _row_chunks(a_ref.shape[0]):
        acc = jnp.dot(a_ref[rows, :], w_ref[...], preferred_element_type=F32)
        o_ref[rows, :] = _sigmoid(acc).astype(o_ref.dtype)


def _mm_rope_kernel(a_ref, w_ref, c_ref, s1_ref, s2_ref, o_ref):
    for rows in _row_chunks(a_ref.shape[0]):
        acc = jnp.dot(a_ref[rows, :], w_ref[...], preferred_element_type=F32)
        c = c_ref[rows, :]
        s1 = s1_ref[rows, :]
        s2 = s2_ref[rows, :]
        for g in range(acc.shape[1] // LANES):
            blk = acc[:, g * LANES:(g + 1) * LANES]
            fwd = pltpu.roll(blk, LANES - ROT_DIM // 2, 1)
            bwd = pltpu.roll(blk, ROT_DIM // 2, 1)
            o_ref[rows, g * LANES:(g + 1) * LANES] = (blk * c + fwd * s1 + bwd * s2).astype(o_ref.dtype)


def _mm(a, w, out_dtype, tm, tn, epilogue="none", rope_tabs=None, name="mm"):
    m, k = a.shape
    n = w.shape[1]
    in_specs = [pl.BlockSpec((tm, k), lambda i, j: (i, 0)),
                pl.BlockSpec((k, tn), lambda i, j: (0, j))]
    args = [a, w]
    if epilogue == "rope":
        seq_blocks = rope_tabs[0].shape[0] // tm
        for t in rope_tabs:
            in_specs.append(pl.BlockSpec((tm, LANES), lambda i, j: (i % seq_blocks, 0)))
            args.append(t)
        kern = _mm_rope_kernel
    elif epilogue == "sigmoid":
        kern = _mm_sigmoid_kernel
    else:
        kern = _mm_kernel
    return pl.pallas_call(
        kern,
        grid=(m // tm, n // tn),
        in_specs=in_specs,
        out_specs=pl.BlockSpec((tm, tn), lambda i, j: (i, j)),
        out_shape=jax.ShapeDtypeStruct((m, n), out_dtype),
        compiler_params=_cparams(("arbitrary", "arbitrary")),
        name=name,
    )(*args)


def _ssd_kernel(z_ref, xs_ref, bm_ref, cm_ref, dt_ref, cwx_ref, cwb_ref, cwc_ref, cbx_ref, cbb_ref,
                cbc_ref, dtb_ref, alog_ref, dsk_ref, nw_ref, o_ref,
                px_sc, pb_sc, pc_sc, st_sc, y_sc, *, n_heads):
    c = pl.program_id(1)
    L = SSD_CHUNK
    N = SSD_STATE
    hpg = n_heads // SSD_GROUPS
    gw = hpg * SSD_HEADDIM

    @pl.when(c == 0)
    def _():
        px_sc[...] = jnp.zeros_like(px_sc)
        pb_sc[...] = jnp.zeros_like(pb_sc)
        pc_sc[...] = jnp.zeros_like(pc_sc)
        st_sc[...] = jnp.zeros_like(st_sc)

    def conv_silu(u, prev_sc, w_ref, b_ref):
        prev = prev_sc[...]
        row = lax.broadcasted_iota(jnp.int32, u.shape, 0)
        acc = u * w_ref[SSD_CONV - 1:SSD_CONV, :] + b_ref[...]
        for j in range(1, SSD_CONV):
            sh = jnp.where(row < j, pltpu.roll(prev, j, 0), pltpu.roll(u, j, 0))
            acc = acc + sh * w_ref[SSD_CONV - 1 - j:SSD_CONV - j, :]
        prev_sc[...] = u
        return acc * _sigmoid(acc)

    xs = conv_silu(xs_ref[0].astype(F32), px_sc, cwx_ref, cbx_ref)
    bm = conv_silu(bm_ref[0].astype(F32), pb_sc, cwb_ref, cbb_ref)
    cm = conv_silu(cm_ref[0].astype(F32), pc_sc, cwc_ref, cbc_ref)

    raw = dt_ref[0] + dtb_ref[...]
    dt = jnp.maximum(raw, 0.0) + jnp.log1p(jnp.exp(-jnp.abs(raw)))
    a = dt * (-jnp.exp(alog_ref[...]))
    r_i = lax.broadcasted_iota(jnp.int32, (L, L), 0)
    c_i = lax.broadcasted_iota(jnp.int32, (L, L), 1)
    causal = r_i >= c_i
    tri = jnp.where(causal, 1.0, 0.0).astype(F32)
    a_cs = jnp.dot(tri, a, preferred_element_type=F32, precision=HIGHEST)
    a_cs_t = a_cs.T

    lane = lax.broadcasted_iota(jnp.int32, (L, LANES), 1)
    first = lane < SSD_HEADDIM
    acs_tiles = []
    dt_tiles = []
    for j in range(n_heads // 2):
        h0, h1 = 2 * j, 2 * j + 1
        acs_tiles.append(jnp.where(first, a_cs[:, h0:h0 + 1], a_cs[:, h1:h1 + 1]))
        dt_tiles.append(jnp.where(first, dt[:, h0:h0 + 1], dt[:, h1:h1 + 1]))
    acs_e = jnp.concatenate(acs_tiles, axis=1)
    dt_e = jnp.concatenate(dt_tiles, axis=1)
    xdt = xs * dt_e
    ea = jnp.exp(acs_e)
    alast = acs_e[L - 1:L, :]
    xdec = (xdt * jnp.exp(alast - acs_e)).astype(BF16)
    ealast = jnp.exp(alast)

    for g in range(SSD_GROUPS):
        bg = bm[:, g * N:(g + 1) * N]
        cg = cm[:, g * N:(g + 1) * N].astype(BF16)
        scores = lax.dot_general(cg, bg.astype(BF16), (((1,), (1,)), ((), ())),
                                 preferred_element_type=F32)
        st_old = st_sc[:, g * gw:(g + 1) * gw]
        y_off = jnp.dot(cg, st_old.astype(BF16), preferred_element_type=F32) * ea[:, g * gw:(g + 1) * gw]
        st_sc[:, g * gw:(g + 1) * gw] = st_old * ealast[:, g * gw:(g + 1) * gw] + jnp.dot(
            bg.T.astype(BF16), xdec[:, g * gw:(g + 1) * gw], preferred_element_type=F32)
        for jj in range(hpg // 2):
            j = g * (hpg // 2) + jj
            h0, h1 = 2 * j, 2 * j + 1
            la = jnp.exp(jnp.where(causal, a_cs[:, h0:h0 + 1] - a_cs_t[h0:h0 + 1, :], NEG_BIG))
            lb = jnp.exp(jnp.where(causal, a_cs[:, h1:h1 + 1] - a_cs_t[h1:h1 + 1, :], NEG_BIG))
            mcat = jnp.concatenate([(scores * la).astype(BF16), (scores * lb).astype(BF16)], axis=1)
            xp = xdt[:, j * LANES:(j + 1) * LANES]
            xcat = jnp.concatenate([jnp.where(first, xp, 0.0).astype(BF16),
                                    jnp.where(first, 0.0, xp).astype(BF16)], axis=0)
            y_diag = jnp.dot(mcat, xcat, preferred_element_type=F32)
            lo = jj * LANES
            y_sc[:, j * LANES:(j + 1) * LANES] = (
                y_diag + y_off[:, lo:lo + LANES]
                + dsk_ref[:, j * LANES:(j + 1) * LANES] * xs[:, j * LANES:(j + 1) * LANES])

    z = z_ref[0].astype(F32)
    u = y_sc[...] * (z * _sigmoid(z))
    for g in range(SSD_GROUPS):
        ug = u[:, g * gw:(g + 1) * gw]
        ms = jnp.mean(ug * ug, axis=-1, keepdims=True)
        o_ref[0, :, g * gw:(g + 1) * gw] = (ug * lax.rsqrt(ms + SUB_EPS)
                                            * nw_ref[:, g * gw:(g + 1) * gw]).astype(o_ref.dtype)


def _ssd(plain3, dt3, conv_w, conv_b, dtb, alog, dsk_e, norm_w, dh, n_heads):
    b, s, _ = plain3.shape
    L = SSD_CHUNK
    gn = SSD_GROUPS * SSD_STATE
    nc = s // L
    xblk = 1
    bblk = (2 * dh) // gn
    cw_x, cw_b, cw_c = conv_w[:, :dh], conv_w[:, dh:dh + gn], conv_w[:, dh + gn:]
    cb_x, cb_b, cb_c = conv_b[:, :dh], conv_b[:, dh:dh + gn], conv_b[:, dh + gn:]
    full = lambda shape: pl.BlockSpec(shape, lambda bi, ci: (0, 0))
    return pl.pallas_call(
        functools.partial(_ssd_kernel, n_heads=n_heads),
        grid=(b, nc),
        in_specs=[pl.BlockSpec((1, L, dh), lambda bi, ci: (bi, ci, 0)),
                  pl.BlockSpec((1, L, dh), lambda bi, ci: (bi, ci, xblk)),
                  pl.BlockSpec((1, L, gn), lambda bi, ci: (bi, ci, bblk)),
                  pl.BlockSpec((1, L, gn), lambda bi, ci: (bi, ci, bblk + 1)),
                  pl.BlockSpec((1, L, LANES), lambda bi, ci: (bi, ci, 0)),
                  full((SSD_CONV, dh)), full((SSD_CONV, gn)), full((SSD_CONV, gn)),
                  full((1, dh)), full((1, gn)), full((1, gn)),
                  full((1, LANES)), full((1, LANES)), full((1, dh)), full((1, dh))],
        out_specs=pl.BlockSpec((1, L, dh), lambda bi, ci: (bi, ci, 0)),
        out_shape=jax.ShapeDtypeStruct((b, s, dh), BF16),
        scratch_shapes=[pltpu.VMEM((L, dh), F32), pltpu.VMEM((L, gn), F32), pltpu.VMEM((L, gn), F32),
                        pltpu.VMEM((SSD_STATE, dh), F32), pltpu.VMEM((L, dh), F32)],
        compiler_params=_cparams(("arbitrary", "arbitrary")),
        name="ssd",
    )(plain3, plain3, plain3, plain3, dt3, cw_x, cw_b, cw_c, cb_x, cb_b, cb_c, dtb, alog, dsk_e, norm_w)


def _attn_kernel(q_ref, k_ref, v_ref, bias_ref, lam_ref, sw_ref, o_ref, vt_sc, st_a, st_b, m_sc, l_sc, acc_sc, *,
                 tq, lam_init):
    qi = pl.program_id(2)
    tu = ATTN_KV_UNIT
    nh = ATTN_HEADS_PER_STEP
    n_all = k_ref.shape[1] // tu
    heads = [slice(hh * LANES, (hh + 1) * LANES) for hh in range(nh)]

    @pl.when(qi == 0)
    def _():
        def transpose_block(c, carry):
            start = pl.multiple_of(c * tu, tu)
            for hh in range(nh):
                vt_sc[hh, c] = v_ref[0, pl.ds(start, tu), heads[hh]].astype(F32).T.astype(BF16)
            return carry

        lax.fori_loop(0, n_all, transpose_block, 0)

    qts = []
    for hh in range(nh):
        qt = q_ref[0, :, heads[hh]].astype(F32).T
        row = lax.broadcasted_iota(jnp.int32, qt.shape, 0)
        qts.append((jnp.where(row < ATTN_DK, qt, 0.0).astype(BF16),
                    jnp.where(row < ATTN_DK, 0.0, qt).astype(BF16)))
    m_sc[...] = jnp.full_like(m_sc, NEG_BIG)
    l_sc[...] = jnp.zeros_like(l_sc)
    acc_sc[...] = jnp.zeros_like(acc_sc)

    def scores(u, st_ref):
        start = pl.multiple_of(u * tu, tu)
        for hh in range(nh):
            k = k_ref[0, pl.ds(start, tu), heads[hh]]
            for m in range(2):
                st_ref[2 * hh + m] = jnp.dot(k, qts[hh][m], preferred_element_type=F32)

    def update(u, st_ref, masked):
        for hh in range(nh):
            vt = vt_sc[hh, u]
            for m in range(2):
                c = 2 * hh + m
                st = st_ref[c]
                if masked:
                    st = st + bias_ref[...]
                m_prev = m_sc[c]
                m_new = jnp.maximum(m_prev, jnp.max(st, axis=0, keepdims=True))
                alpha = jnp.exp2(m_prev - m_new)
                pt = jnp.exp2(st - m_new)
                l_sc[c] = alpha * l_sc[c] + jnp.sum(pt, axis=0, keepdims=True)
                acc_sc[c] = alpha * acc_sc[c] + jnp.dot(vt, pt.astype(BF16), preferred_element_type=F32)
                m_sc[c] = m_new

    n_units = (qi * tq) // tu + 1
    n_loop = (n_units - 1) // 2
    scores(0, st_a)

    def two_units(j, carry):
        u = 2 * j
        scores(u + 1, st_b)
        update(u, st_a, False)
        scores(u + 2, st_a)
        update(u + 1, st_b, False)
        return carry

    lax.fori_loop(0, n_loop, two_units, 0)
    last = n_units - 1

    @pl.when(last == 2 * n_loop)
    def _():
        update(last, st_a, True)

    @pl.when(last != 2 * n_loop)
    def _():
        scores(last, st_b)
        update(last - 1, st_a, False)
        update(last, st_b, True)

    lv = lam_ref[...]
    lam = (jnp.exp(jnp.sum(lv[0:1] * lv[1:2], axis=-1, keepdims=True))
           - jnp.exp(jnp.sum(lv[2:3] * lv[3:4], axis=-1, keepdims=True)) + lam_init)
    for hh in range(nh):
        c0, c1 = 2 * hh, 2 * hh + 1
        ot = acc_sc[c0] * (1.0 / l_sc[c0]) - lam * (acc_sc[c1] * (1.0 / l_sc[c1]))
        ot = ot * lax.rsqrt(jnp.mean(ot * ot, axis=0, keepdims=True) + SUB_EPS)
        o_ref[0, :, heads[hh]] = (ot.T * sw_ref[...] * (1.0 - lam_init)).astype(o_ref.dtype)


def _attention(qk3, plain3, lam_rows, subln_w, n_heads, v_blk0, tq, lam_init):
    b, s, _ = qk3.shape
    tu = ATTN_KV_UNIT
    nh = ATTN_HEADS_PER_STEP
    hw = nh * LANES
    assert tq == tu, "the diagonal unit must coincide with the query tile"
    kpos = lax.broadcasted_iota(jnp.int32, (tu, tq), 0)
    qpos = lax.broadcasted_iota(jnp.int32, (tu, tq), 1)
    diag_bias = jnp.where(kpos <= qpos, 0.0, NEG_BIG).astype(F32)
    return pl.pallas_call(
        functools.partial(_attn_kernel, tq=tq, lam_init=lam_init),
        grid=(b, n_heads // nh, s // tq),
        in_specs=[pl.BlockSpec((1, tq, hw), lambda bi, hi, qi: (bi, qi, hi)),
                  pl.BlockSpec((1, s, hw), lambda bi, hi, qi: (bi, 0, n_heads // nh + hi)),
                  pl.BlockSpec((1, s, hw), lambda bi, hi, qi: (bi, 0, v_blk0 // nh + hi)),
                  pl.BlockSpec((tu, tq), lambda bi, hi, qi: (0, 0)),
                  pl.BlockSpec((8, LANES), lambda bi, hi, qi: (0, 0)),
                  pl.BlockSpec((1, LANES), lambda bi, hi, qi: (0, 0))],
        out_specs=pl.BlockSpec((1, tq, hw), lambda bi, hi, qi: (bi, qi, hi)),
        out_shape=jax.ShapeDtypeStruct((b, s, n_heads * ATTN_DV), BF16),
        scratch_shapes=[pltpu.VMEM((nh, s // tu, ATTN_DV, tu), BF16),
                        pltpu.VMEM((2 * nh, tu, tq), F32), pltpu.VMEM((2 * nh, tu, tq), F32),
                        pltpu.VMEM((2 * nh, 1, tq), F32), pltpu.VMEM((2 * nh, 1, tq), F32),
                        pltpu.VMEM((2 * nh, ATTN_DV, tq), F32)],
        compiler_params=_cparams(("arbitrary", "arbitrary", "arbitrary")),
        name="diff_attn",
    )(qk3, qk3, plain3, diag_bias, lam_rows, subln_w)


def _merge_kernel(y_ref, o_ref, ws_ref, wa_ref, gs_ref, ga_ref, out_ref):
    for rows in _row_chunks(y_ref.shape[0]):
        bs = jnp.dot(y_ref[rows, :], ws_ref[...], preferred_element_type=F32)
        ba = jnp.dot(o_ref[rows, :], wa_ref[...], preferred_element_type=F32)
        out_ref[rows, :] = (gs_ref[rows, :].astype(F32) * bs
                            + ga_ref[rows, :].astype(F32) * ba).astype(out_ref.dtype)


def _merge(y, o, ws, wa, gates, tm, tn):
    m, k = y.shape
    n = ws.shape[1]
    nj = n // tn
    return pl.pallas_call(
        _merge_kernel,
        grid=(m // tm, nj),
        in_specs=[pl.BlockSpec((tm, k), lambda i, j: (i, 0)),
                  pl.BlockSpec((tm, o.shape[1]), lambda i, j: (i, 0)),
                  pl.BlockSpec((k, tn), lambda i, j: (0, j)),
                  pl.BlockSpec((o.shape[1], tn), lambda i, j: (0, j)),
                  pl.BlockSpec((tm, tn), lambda i, j: (i, j)),
                  pl.BlockSpec((tm, tn), lambda i, j: (i, nj + j))],
        out_specs=pl.BlockSpec((tm, tn), lambda i, j: (i, j)),
        out_shape=jax.ShapeDtypeStruct((m, n), BF16),
        compiler_params=_cparams(("arbitrary", "arbitrary")),
        name="merge",
    )(y, o, ws, wa, gates, gates)


def _outproj_kernel(mg_ref, wo_ref, x_ref, gpost_ref, gt_ref, gpre_ref, sc_ref, sh_ref, wr_hi_ref,
                    wr_lo_ref, br_ref, x1_ref, hp_ref, lg_ref):
    mix = jnp.dot(mg_ref[...], wo_ref[...], preferred_element_type=F32)
    nm = mix * lax.rsqrt(jnp.mean(mix * mix, axis=-1, keepdims=True) + NORM_EPS) * gpost_ref[...]
    x1 = x_ref[...] + gt_ref[0] * nm
    x1_ref[...] = x1
    h2 = (x1 * lax.rsqrt(jnp.mean(x1 * x1, axis=-1, keepdims=True) + NORM_EPS) * gpre_ref[...]
          * (1.0 + sc_ref[0]) + sh_ref[0])
    h_hi = h2.astype(BF16)
    h_lo = (h2 - h_hi.astype(F32)).astype(BF16)
    wr_hi = wr_hi_ref[...]
    lg_ref[...] = (jnp.dot(h_hi, wr_hi, preferred_element_type=F32)
                   + jnp.dot(h_lo, wr_hi, preferred_element_type=F32)
                   + jnp.dot(h_hi, wr_lo_ref[...], preferred_element_type=F32) + br_ref[...])
    _store_token_rows(hp_ref, 0, h2)


def _outproj(merged, wo, x2, gpost, mod3, gpre, wr_hi, wr_lo, br, seq, tm, gt_blk, sc_blk, sh_blk):
    m, d = x2.shape
    per_b = seq // tm
    row = lambda i: (i, 0)
    const = lambda i: (0, 0)
    return pl.pallas_call(
        _outproj_kernel,
        grid=(m // tm,),
        in_specs=[pl.BlockSpec((tm, d), row),
                  pl.BlockSpec((d, d), const),
                  pl.BlockSpec((tm, d), row),
                  pl.BlockSpec((1, d), const),
                  pl.BlockSpec((1, 1, d), lambda i: (i // per_b, 0, gt_blk)),
                  pl.BlockSpec((1, d), const),
                  pl.BlockSpec((1, 1, d), lambda i: (i // per_b, 0, sc_blk)),
                  pl.BlockSpec((1, 1, d), lambda i: (i // per_b, 0, sh_blk)),
                  pl.BlockSpec((d, LANES), const),
                  pl.BlockSpec((d, LANES), const),
                  pl.BlockSpec((1, LANES), const)],
        out_specs=[pl.BlockSpec((tm, d), row),
                   pl.BlockSpec((tm * (d // LANES), LANES), row),
                   pl.BlockSpec((tm, LANES), row)],
        out_shape=[jax.ShapeDtypeStruct((m, d), F32),
                   jax.ShapeDtypeStruct((m * (d // LANES), LANES), F32),
                   jax.ShapeDtypeStruct((m, LANES), F32)],
        compiler_params=_cparams(("arbitrary",)),
        name="outproj",
    )(merged, wo, x2, gpost, mod3, gpre, mod3, mod3, wr_hi, wr_lo, br)


def _route_kernel(lg_ref, dest_ref, w_ref, cnt_ref, cnt_sc, pst_sc, run_sc, *, n_experts, blk):
    ph = pl.program_id(0)
    t = pl.program_id(1)
    tk = lg_ref.shape[0]
    lane = lax.broadcasted_iota(jnp.int32, (tk, LANES), 1)
    lg = jnp.where(lane < n_experts, lg_ref[...], -jnp.inf)
    vals = []
    hots = []
    for _ in range(TOP_K):
        mx = jnp.max(lg, axis=-1, keepdims=True)
        ix = jnp.min(jnp.where(lg == mx, lane, LANES), axis=-1, keepdims=True)
        hot = lane == ix
        lg = jnp.where(hot, -jnp.inf, lg)
        vals.append(mx)
        hots.append(hot)
    multi = jnp.zeros((tk, LANES), F32)
    for hot in hots:
        multi = multi + jnp.where(hot, 1.0, 0.0)
    colsum = jnp.sum(multi, axis=0, keepdims=True)

    @pl.when((ph == 0) & (t == 0))
    def _():
        cnt_sc[...] = jnp.zeros_like(cnt_sc)

    @pl.when(ph == 0)
    def _():
        cnt_sc[...] += colsum

    @pl.when((ph == 1) & (t == 0))
    def _():
        cnt = cnt_sc[...].astype(jnp.int32)
        padded = (((cnt + (blk - 1)) // blk) * blk).astype(F32)
        r_i = lax.broadcasted_iota(jnp.int32, (LANES, LANES), 0)
        c_i = lax.broadcasted_iota(jnp.int32, (LANES, LANES), 1)
        upper = jnp.where(r_i < c_i, 1.0, 0.0).astype(F32)
        pst_sc[...] = jnp.dot(jnp.broadcast_to(padded, (8, LANES)), upper,
                              preferred_element_type=F32, precision=HIGHEST)[0:1]
        run_sc[...] = jnp.zeros_like(run_sc)

    @pl.when(ph == 1)
    def _():
        r_i = lax.broadcasted_iota(jnp.int32, (tk, tk), 0)
        c_i = lax.broadcasted_iota(jnp.int32, (tk, tk), 1)
        strict = jnp.where(r_i > c_i, 1.0, 0.0).astype(BF16)
        before = jnp.dot(strict, multi.astype(BF16), preferred_element_type=F32)
        base = before + run_sc[...] + pst_sc[...]
        esum = jnp.zeros((tk, 1), F32)
        evals = []
        for r in range(TOP_K):
            e = jnp.exp(vals[r] - vals[0])
            evals.append(e)
            esum = esum + e
        dest = jnp.zeros((tk, LANES), jnp.int32)
        wts = jnp.zeros((tk, LANES), F32)
        for r in range(TOP_K):
            d_r = jnp.sum(jnp.where(hots[r], base, 0.0), axis=-1, keepdims=True).astype(jnp.int32)
            dest = jnp.where(lane == r, d_r, dest)
            wts = jnp.where(lane == r, evals[r] / esum, wts)
        dest_ref[...] = dest
        w_ref[...] = wts
        run_sc[...] += colsum
        cnt_ref[...] = jnp.broadcast_to(cnt_sc[...], cnt_ref.shape)


def _route(logits, n_experts, blk, tk):
    t = logits.shape[0]
    return pl.pallas_call(
        functools.partial(_route_kernel, n_experts=n_experts, blk=blk),
        grid=(2, t // tk),
        in_specs=[pl.BlockSpec((tk, LANES), lambda ph, ti: (ti, 0))],
        out_specs=[pl.BlockSpec((tk, LANES), lambda ph, ti: (ti * ph, 0)),
                   pl.BlockSpec((tk, LANES), lambda ph, ti: (ti * ph, 0)),
                   pl.BlockSpec((8, LANES), lambda ph, ti: (0, 0))],
        out_shape=[jax.ShapeDtypeStruct((t, LANES), jnp.int32),
                   jax.ShapeDtypeStruct((t, LANES), F32),
                   jax.ShapeDtypeStruct((8, LANES), F32)],
        scratch_shapes=[pltpu.VMEM((1, LANES), F32), pltpu.VMEM((1, LANES), F32), pltpu.VMEM((1, LANES), F32)],
        compiler_params=_cparams(("arbitrary", "arbitrary")),
        name="route",
    )(logits)


def _pad_fill_copies(b, nv_sm, zero_sc, xs_hbm, sem, blk, ns):
    nv = nv_sm[b]
    out = []
    off = b * blk + nv
    rest = blk - nv
    p = blk
    while p >= 1:
        cond = (rest & p) != 0
        out.append((cond, pltpu.make_async_copy(
            zero_sc.at[pl.ds(0, p * ns), :], xs_hbm.at[pl.ds(pl.multiple_of(off * ns, ns), p * ns), :], sem)))
        off = off + jnp.where(cond, p, 0)
        p //= 2
    return out


def _dispatch_kernel(dest_sm, nv_sm, h_ref, xs_hbm, zero_sc, sem, fill_sem, *, tt, blk, nb, ns):
    base = pl.program_id(0) * tt

    @pl.when(pl.program_id(0) == 0)
    def _():
        zero_sc[...] = jnp.zeros_like(zero_sc)

        def fill(b, carry):
            for cond, cp in _pad_fill_copies(b, nv_sm, zero_sc, xs_hbm, fill_sem, blk, ns):
                pl.when(cond)(cp.start)
            return carry

        def fill_wait(b, carry):
            for cond, cp in _pad_fill_copies(b, nv_sm, zero_sc, xs_hbm, fill_sem, blk, ns):
                pl.when(cond)(cp.wait)
            return carry

        lax.fori_loop(0, nb, fill, 0)
        lax.fori_loop(0, nb, fill_wait, 0)

    def body(t, carry):
        src = h_ref.at[pl.ds(pl.multiple_of(t * ns, ns), ns), :]
        for k in range(TOP_K):
            d = dest_sm[(base + t) * TOP_K + k]
            pltpu.make_async_copy(src, xs_hbm.at[pl.ds(pl.multiple_of(d * ns, ns), ns), :], sem).start()
        return carry

    lax.fori_loop(0, tt, body, 0)
    for _ in range(TOP_K):
        pltpu.make_async_copy(h_ref, xs_hbm.at[pl.ds(0, tt * ns), :], sem).wait()


def _dispatch(dest_flat, nvalid, h_rows, n_slots, tt, blk, ns):
    nb = n_slots // blk
    t = h_rows.shape[0] // ns
    return pl.pallas_call(
        functools.partial(_dispatch_kernel, tt=tt, blk=blk, nb=nb, ns=ns),
        grid_spec=pltpu.PrefetchScalarGridSpec(
            num_scalar_prefetch=2,
            grid=(t // tt,),
            in_specs=[pl.BlockSpec((tt * ns, LANES), lambda i, d, nv: (i, 0))],
            out_specs=pl.BlockSpec(memory_space=pl.ANY),
            scratch_shapes=[pltpu.VMEM((blk * ns, LANES), F32), pltpu.SemaphoreType.DMA(()),
                            pltpu.SemaphoreType.DMA(())],
        ),
        out_shape=jax.ShapeDtypeStruct((n_slots * ns, LANES), F32),
        compiler_params=_cparams(("arbitrary",)),
        name="dispatch",
    )(dest_flat, nvalid, h_rows)


def _expert_changed(be, i, last):
    ii = jnp.minimum(i, last)
    prev = jnp.maximum(ii - 1, 0)
    return (i == 0) | (be[ii] != be[prev])


def _for_live_sub_blocks(active, n_valid, sub, o_ref, compute, rows_per_slot=1):
    n_live = jnp.where(active, (n_valid + sub - 1) // sub, 0)
    for count in range(MOE_SUB_BLOCKS + 1):
        @pl.when(n_live == count)
        def _(count=count):
            for r in range(count):
                compute(slice(r * sub, (r + 1) * sub))
            if count < MOE_SUB_BLOCKS:
                first = count * sub * rows_per_slot
                o_ref[first:, :] = jnp.zeros((o_ref.shape[0] - first, o_ref.shape[1]), o_ref.dtype)


def _stream_expert_weights(changed, first, prefetch, wait_cur, cast, start_next):
    @pl.when(changed)
    def _():
        pl.when(first)(lambda: start_next(True))
        wait_cur()
        cast()
        pl.when(prefetch)(lambda: start_next(False))


def _gateup_kernel(be, nv, nu, nxt, x_ref, w_hbm, bg_ref, bu_ref, o_ref, wbuf, wg_sc, wu_sc, sem):
    j = pl.program_id(0)
    i = pl.program_id(1)
    nj = pl.num_programs(0)
    th = wg_sc.shape[1]
    dff = w_hbm.shape[2] // 2
    last = nu[0] - 1
    active = i < nu[0]
    ii = jnp.minimum(i, last)

    def copies(e, jj):
        col = pl.multiple_of(jj * th, th)
        return (pltpu.make_async_copy(w_hbm.at[e, :, pl.ds(col, th)], wbuf.at[0], sem.at[0]),
                pltpu.make_async_copy(w_hbm.at[e, :, pl.ds(dff + col, th)], wbuf.at[1], sem.at[1]))

    nx = nxt[ii]
    same_pass = nx >= 0
    e_next = jnp.where(same_pass, be[jnp.maximum(nx, 0)], be[0])
    j_next = jnp.where(same_pass, j, j + 1)

    def start_next(current):
        for cp in (copies(be[ii], j) if current else copies(e_next, j_next)):
            cp.start()

    def wait_cur():
        for cp in copies(be[ii], j):
            cp.wait()

    def cast():
        def chunk(r, carry):
            rows = pl.ds(pl.multiple_of(r * CAST_ROWS, CAST_ROWS), CAST_ROWS)
            wg_sc[rows, :] = wbuf[0, rows, :].astype(BF16)
            wu_sc[rows, :] = wbuf[1, rows, :].astype(BF16)
            return carry

        lax.fori_loop(0, wg_sc.shape[0] // CAST_ROWS, chunk, 0)

    _stream_expert_weights(active & _expert_changed(be, i, last), (j == 0) & (i == 0),
                           same_pass | (j + 1 < nj), wait_cur, cast, start_next)

    sub = o_ref.shape[0] // MOE_SUB_BLOCKS

    def compute(rows):
        x = _load_token_rows(x_ref, rows.start, sub, wg_sc.shape[0] // LANES).astype(BF16)
        g = jnp.dot(x, wg_sc[...], preferred_element_type=F32) + bg_ref[0]
        up = jnp.dot(x, wu_sc[...], preferred_element_type=F32) + bu_ref[0]
        gate = jnp.minimum(g, SWIGLU_LIMIT)
        up = jnp.clip(up, -SWIGLU_LIMIT, SWIGLU_LIMIT)
        o_ref[rows, :] = ((up + 1.0) * gate * _sigmoid(SWIGLU_ALPHA * gate)).astype(o_ref.dtype)

    _for_live_sub_blocks(active, nv[i], sub, o_ref, compute)


def _gateup(block_e, nvalid, nused, nxt, xs, w_gu, b_gu3, tm, th):
    d = w_gu.shape[1]
    ns = d // LANES
    n_slots = xs.shape[0] // ns
    dff = w_gu.shape[2] // 2
    nj = dff // th
    nb = n_slots // tm

    def blk(i, nu):
        return jnp.minimum(i, nu[0] - 1)

    return pl.pallas_call(
        _gateup_kernel,
        grid_spec=pltpu.PrefetchScalarGridSpec(
            num_scalar_prefetch=4,
            grid=(nj, nb),
            in_specs=[pl.BlockSpec((tm * ns, LANES), lambda j, i, be, nv, nu, nx: (blk(i, nu), 0)),
                      pl.BlockSpec(memory_space=pl.ANY),
                      pl.BlockSpec((1, 1, th), lambda j, i, be, nv, nu, nx: (be[blk(i, nu)], 0, j)),
                      pl.BlockSpec((1, 1, th), lambda j, i, be, nv, nu, nx: (be[blk(i, nu)], 0, nj + j))],
            out_specs=pl.BlockSpec((tm, th), lambda j, i, be, nv, nu, nx: (i, j)),
            scratch_shapes=[pltpu.VMEM((2, d, th), F32), pltpu.VMEM((d, th), BF16), pltpu.VMEM((d, th), BF16),
                            pltpu.SemaphoreType.DMA((2,))],
        ),
        out_shape=jax.ShapeDtypeStruct((n_slots, dff), BF16),
        compiler_params=_cparams(("arbitrary", "arbitrary")),
        name="expert_gate_up",
    )(block_e, nvalid, nused, nxt, xs, w_gu, b_gu3, b_gu3)


def _down_kernel(be, nv, nu, nxt, a_ref, w_hbm, bd_ref, o_ref, wbuf, wd_sc, sem):
    i = pl.program_id(1)
    last = nu[0] - 1
    active = i < nu[0]
    ii = jnp.minimum(i, last)
    nx = nxt[ii]

    def copy(e):
        return pltpu.make_async_copy(w_hbm.at[e], wbuf, sem)

    def start_next(current):
        copy(be[ii] if current else be[jnp.maximum(nx, 0)]).start()

    def cast():
        def chunk(r, carry):
            rows = pl.ds(pl.multiple_of(r * CAST_ROWS, CAST_ROWS), CAST_ROWS)
            wd_sc[rows, :] = wbuf[rows, :].astype(BF16)
            return carry

        lax.fori_loop(0, wd_sc.shape[0] // CAST_ROWS, chunk, 0)

    _stream_expert_weights(active & _expert_changed(be, i, last), i == 0, nx >= 0,
                           lambda: copy(be[ii]).wait(), cast, start_next)

    sub = a_ref.shape[0] // MOE_SUB_BLOCKS
    ns = o_ref.shape[0] // a_ref.shape[0]

    def compute(rows):
        y = jnp.dot(a_ref[rows, :], wd_sc[...], preferred_element_type=F32) + bd_ref[0]
        _store_token_rows(o_ref, rows.start, y)

    _for_live_sub_blocks(active, nv[i], sub, o_ref, compute, rows_per_slot=ns)


def _down(block_e, nvalid, nused, nxt, act, w_d, b_d3, tm):
    n_slots, dff = act.shape
    d = w_d.shape[2]
    ns = d // LANES
    nb = n_slots // tm

    def blk(i, nu):
        return jnp.minimum(i, nu[0] - 1)

    return pl.pallas_call(
        _down_kernel,
        grid_spec=pltpu.PrefetchScalarGridSpec(
            num_scalar_prefetch=4,
            grid=(1, nb),
            in_specs=[pl.BlockSpec((tm, dff), lambda j, i, be, nv, nu, nx: (blk(i, nu), 0)),
                      pl.BlockSpec(memory_space=pl.ANY),
                      pl.BlockSpec((1, 1, d), lambda j, i, be, nv, nu, nx: (be[blk(i, nu)], 0, 0))],
            out_specs=pl.BlockSpec((tm * ns, LANES), lambda j, i, be, nv, nu, nx: (i, 0)),
            scratch_shapes=[pltpu.VMEM((dff, d), F32), pltpu.VMEM((dff, d), BF16), pltpu.SemaphoreType.DMA(())],
        ),
        out_shape=jax.ShapeDtypeStruct((n_slots * ns, LANES), F32),
        compiler_params=_cparams(("arbitrary", "arbitrary")),
        name="expert_down",
    )(block_e, nvalid, nused, nxt, act, w_d, b_d3)


def _combine_kernel(dest_sm, y_hbm, w_ref, x1_ref, gt_ref, g_ref, o_ref, buf, sem, *, tt, ns):
    i = pl.program_id(0)
    n = pl.num_programs(0)

    def gather(tile, slot):
        def body(t, carry):
            for k in range(TOP_K):
                d = dest_sm[(tile * tt + t) * TOP_K + k]
                pltpu.make_async_copy(y_hbm.at[pl.ds(pl.multiple_of(d * ns, ns), ns), :],
                                      buf.at[slot, pl.ds(pl.multiple_of((k * tt + t) * ns, ns), ns), :],
                                      sem.at[slot]).start()
            return carry

        lax.fori_loop(0, tt, body, 0)

    slot = i % 2

    @pl.when(i == 0)
    def _():
        gather(0, 0)

    @pl.when(i + 1 < n)
    def _():
        gather(i + 1, 1 - slot)

    pltpu.make_async_copy(y_hbm.at[pl.ds(0, TOP_K * tt * ns), :], buf.at[slot], sem.at[slot]).wait()
    w = w_ref[...]
    rows = buf.at[slot]
    f = None
    for k in range(TOP_K):
        yk = _load_token_rows(rows, k * tt, tt, ns) * w[:, k:k + 1]
        f = yk if f is None else f + yk
    nf = f * lax.rsqrt(jnp.mean(f * f, axis=-1, keepdims=True) + NORM_EPS) * g_ref[...]
    o_ref[...] = x1_ref[...] + gt_ref[0] * nf


def _combine(dest_flat, y_rows, wts, x1, mod3, gpost, seq, tt, gt_blk):
    t, d = x1.shape
    ns = d // LANES
    per_b = seq // tt
    return pl.pallas_call(
        functools.partial(_combine_kernel, tt=tt, ns=ns),
        grid_spec=pltpu.PrefetchScalarGridSpec(
            num_scalar_prefetch=1,
            grid=(t // tt,),
            in_specs=[pl.BlockSpec(memory_space=pl.ANY),
                      pl.BlockSpec((tt, LANES), lambda i, ds: (i, 0)),
                      pl.BlockSpec((tt, d), lambda i, ds: (i, 0)),
                      pl.BlockSpec((1, 1, d), lambda i, ds: (i // per_b, 0, gt_blk)),
                      pl.BlockSpec((1, d), lambda i, ds: (0, 0))],
            out_specs=pl.BlockSpec((tt, d), lambda i, ds: (i, 0)),
            scratch_shapes=[pltpu.VMEM((2, TOP_K * tt * ns, LANES), F32), pltpu.SemaphoreType.DMA((2,))],
        ),
        out_shape=jax.ShapeDtypeStruct((t, d), F32),
        compiler_params=_cparams(("arbitrary",)),
        name="combine",
    )(dest_flat, y_rows, wts, x1, mod3, gpost)


def _tile(n, pref):
    t = min(n, pref)
    while n % t:
        t //= 2
    return t


def _rope_tables(seq):
    half = ROT_DIM // 2
    inv = ROPE_THETA ** (-jnp.arange(0, ROT_DIM, 2, dtype=F32) / ROT_DIM)
    ang = jnp.arange(seq, dtype=F32)[:, None] * inv[None, :]
    cos, sin = jnp.cos(ang), jnp.sin(ang)
    ones = jnp.ones((seq, ATTN_DK - ROT_DIM), F32)
    zeros = jnp.zeros((seq, ATTN_DK - ROT_DIM), F32)
    zh = jnp.zeros((seq, half), F32)
    c64 = jnp.concatenate([cos, cos, ones], axis=1)
    s1_64 = jnp.concatenate([-sin, zh, zeros], axis=1)
    s2_64 = jnp.concatenate([zh, sin, zeros], axis=1)
    rep = LANES // ATTN_DK
    return tuple(jnp.tile(t, (1, rep)) for t in (c64, s1_64, s2_64))


def _layer(x, c_pad, l, p, moe_blk):
    bsz, seq, d = x.shape
    t = bsz * seq
    dh = d
    n_sheads = dh // SSD_HEADDIM
    gn = SSD_GROUPS * SSD_STATE
    n_aheads = d // ATTN_DV
    aw = n_aheads * ATTN_DV
    qkw = 2 * n_aheads * ATTN_DK
    n_experts = p["w_router"].shape[-1]

    mod = _ada(c_pad, p["w_ada"][l], p["b_ada"][l][None, :], _tile(6 * d, 1024))
    mod3 = mod[:bsz].reshape(bsz, 1, 6 * d)

    o = 0
    segs = {}
    for name, size in (("z", dh), ("xbc", dh + 2 * gn), ("dt", n_sheads), ("q", qkw), ("k", qkw),
                       ("v", aw), ("gs", d), ("ga", d)):
        segs[name] = (o, o + size)
        o += size
    scale = ATTN_DK ** -0.5 * math.log2(math.e)
    w_plain, w_rope, w_gate, w_dt = _regroup(p["w_in"][l], segs, scale, _tile(d, 256))

    h = _prenorm(x, p["g_pre_mix"][l][None, :], mod3, 1, 0, _tile(seq, 512)).reshape(t, d)
    tm = _tile(seq, 1024)
    plain = _mm(h, w_plain, BF16, tm, _tile(w_plain.shape[1], 1024), name="proj_plain")
    qk = _mm(h, w_rope, BF16, tm, _tile(w_rope.shape[1], 1024), "rope", _rope_tables(seq), name="proj_rope")
    gates = _mm(h, w_gate, BF16, tm, _tile(w_gate.shape[1], 1024), "sigmoid", name="proj_gate")
    dt_raw = _mm(h, w_dt, F32, tm, LANES, name="proj_dt")

    pad_h = lambda v: jnp.pad(v, (0, LANES - n_sheads))[None, :]
    plain3 = plain.reshape(bsz, seq, plain.shape[1])
    y_ssd = _ssd(plain3, dt_raw.reshape(bsz, seq, LANES), p["conv_w"][l], p["conv_b"][l][None, :],
                 pad_h(p["dt_bias"][l]), pad_h(p["a_log"][l]),
                 jnp.repeat(p["d_skip"][l], SSD_HEADDIM)[None, :], p["ssd_norm_w"][l][None, :], dh, n_sheads)

    lam_init = 0.8 - 0.6 * math.exp(-0.3 * l)
    lam_rows = jnp.zeros((8, LANES), F32)
    for r, nm in enumerate(("lambda_q1", "lambda_k1", "lambda_q2", "lambda_k2")):
        lam_rows = lam_rows.at[r, :ATTN_DK].set(p[nm][l])
    v_blk0 = (dh + dh + 2 * gn) // LANES
    o_attn = _attention(qk.reshape(bsz, seq, 2 * qkw), plain3, lam_rows, p["subln_w"][l][None, :],
                        n_aheads, v_blk0, _tile(seq, ATTN_Q_TILE), lam_init)

    tm2 = _tile(seq, 512)
    merged = _merge(y_ssd.reshape(t, dh), o_attn.reshape(t, aw), p["w_br_ssd"][l].astype(BF16),
                    p["w_br_attn"][l].astype(BF16), gates, tm2, _tile(d, 1024))
    wr = jnp.pad(p["w_router"][l], ((0, 0), (0, LANES - n_experts)))
    wr_hi = wr.astype(BF16)
    wr_lo = (wr - wr_hi.astype(F32)).astype(BF16)
    br = jnp.pad(p["b_router"][l], (0, LANES - n_experts))[None, :]
    x1, h_rows, logits = _outproj(merged, p["w_out"][l].astype(BF16), x.reshape(t, d),
                                    p["g_post_mix"][l][None, :], mod3, p["g_pre_ffn"][l][None, :],
                                    wr_hi, wr_lo, br, seq, tm2, 2, 4, 3)

    dest, wts, cnt = _route(logits, n_experts, moe_blk, _tile(t, 512))
    counts = cnt[0, :n_experts].astype(jnp.int32)
    n_slots = t * TOP_K + n_experts * moe_blk
    nb = n_slots // moe_blk
    pblocks = (counts + moe_blk - 1) // moe_blk
    pend = jnp.cumsum(pblocks)
    nused = jnp.maximum(pend[-1], 1).astype(jnp.int32)
    bidx = jnp.arange(nb, dtype=jnp.int32)
    block_e = jnp.minimum(jnp.sum((pend[None, :] <= bidx[:, None]).astype(jnp.int32), axis=1), n_experts - 1)
    pstart = pend - pblocks
    nvalid = jnp.clip(counts[block_e] - (bidx - pstart[block_e]) * moe_blk, 0, moe_blk).astype(jnp.int32)
    dest_flat = dest[:, :TOP_K].reshape(-1)
    nused1 = nused.reshape(1)
    after = pend[block_e].astype(jnp.int32)
    nxt = jnp.where(after < nused, after, -1).astype(jnp.int32)

    xs = _dispatch(dest_flat, nvalid, h_rows, n_slots, _tile(t, 256), moe_blk, d // LANES)
    dff = p["w_down"].shape[2]
    act = _gateup(block_e, nvalid, nused1, nxt, xs, p["w_gate_up"][l], p["b_gate_up"][l][:, None, :],
                  moe_blk, _tile(dff, 1024))
    y_sorted = _down(block_e, nvalid, nused1, nxt, act, p["w_down"][l], p["b_down"][l][:, None, :], moe_blk)
    out = _combine(dest_flat, y_sorted, wts, x1, mod3, p["g_post_ffn"][l][None, :], seq, _tile(seq, 128), 5)
    return out.reshape(bsz, seq, d)


MOE_ROW_BLOCK = 512
MM_ROW_CHUNKS = 4
CAST_ROWS = 128
MOE_SUB_BLOCKS = 2


def kernel(x, c, w_ada, b_ada, g_pre_mix, g_post_mix, g_pre_ffn, g_post_ffn, w_in, conv_w, conv_b, dt_bias, a_log, d_skip, ssd_norm_w, lambda_q1, lambda_k1, lambda_q2, lambda_k2, subln_w, w_br_ssd, w_br_attn, w_out, w_router, b_router, w_gate_up, b_gate_up, w_down, b_down):
    p = dict(w_ada=w_ada, b_ada=b_ada, g_pre_mix=g_pre_mix, g_post_mix=g_post_mix, g_pre_ffn=g_pre_ffn,
             g_post_ffn=g_post_ffn, w_in=w_in, conv_w=conv_w, conv_b=conv_b, dt_bias=dt_bias, a_log=a_log,
             d_skip=d_skip, ssd_norm_w=ssd_norm_w, lambda_q1=lambda_q1, lambda_k1=lambda_k1,
             lambda_q2=lambda_q2, lambda_k2=lambda_k2, subln_w=subln_w, w_br_ssd=w_br_ssd,
             w_br_attn=w_br_attn, w_out=w_out, w_router=w_router, b_router=b_router, w_gate_up=w_gate_up,
             b_gate_up=b_gate_up, w_down=w_down, b_down=b_down)
    bsz = x.shape[0]
    c_pad = jnp.pad(c, ((0, (-bsz) % 8), (0, 0)))
    for l in range(w_ada.shape[0]):
        x = _layer(x, c_pad, l, p, min(MOE_ROW_BLOCK, x.shape[0] * x.shape[1]))
    return x
```

```python
import functools
import math

import jax
import jax.numpy as jnp
from jax import lax
from jax.experimental import pallas as pl
from jax.experimental.pallas import tpu as pltpu

F32 = jnp.float32
BF16 = jnp.bfloat16
HIGHEST = lax.Precision.HIGHEST

SSD_HEADDIM = 64
SSD_GROUPS = 4
SSD_STATE = 128
SSD_CONV = 4
SSD_CHUNK = 128
ATTN_DK = 64
ATTN_DV = 128
ATTN_KV_UNIT = 512
ATTN_Q_TILE = 512
ATTN_HEADS_PER_STEP = 2
ROT_DIM = ATTN_DK // 4
ROPE_THETA = 500000.0
TOP_K = 4
SWIGLU_LIMIT = 7.0
SWIGLU_ALPHA = 1.702
NORM_EPS = 1e-6
SUB_EPS = 1e-5
LANES = 128
NEG_BIG = -1e30

VMEM_LIMIT = 56 * 1024 * 1024


def _cparams(sem, vmem=VMEM_LIMIT):
    return pltpu.CompilerParams(dimension_semantics=sem, vmem_limit_bytes=vmem)


def _sigmoid(x):
    return 1.0 / (1.0 + jnp.exp(-x))


def _store_token_rows(ref, first_tok, val):
    n_tok, width = val.shape
    ns = width // LANES
    for s_ in range(ns):
        ref[pl.ds(first_tok * ns + s_, n_tok, stride=ns), :] = val[:, s_ * LANES:(s_ + 1) * LANES]


def _load_token_rows(ref, first_tok, n_tok, ns):
    return jnp.concatenate([ref[pl.ds(first_tok * ns + s_, n_tok, stride=ns), :] for s_ in range(ns)], axis=1)


def _ada_kernel(c_ref, w_ref, b_ref, o_ref):
    c = c_ref[...]
    sc = c * _sigmoid(c)
    o_ref[...] = jnp.dot(sc, w_ref[...], preferred_element_type=F32, precision=HIGHEST) + b_ref[...]


def _ada(c_pad, w, b, tn):
    rows, d = c_pad.shape
    n = w.shape[1]
    return pl.pallas_call(
        _ada_kernel,
        grid=(n // tn,),
        in_specs=[pl.BlockSpec((rows, d), lambda j: (0, 0)),
                  pl.BlockSpec((d, tn), lambda j: (0, j)),
                  pl.BlockSpec((1, tn), lambda j: (0, j))],
        out_specs=pl.BlockSpec((rows, tn), lambda j: (0, j)),
        out_shape=jax.ShapeDtypeStruct((rows, n), F32),
        compiler_params=_cparams(("arbitrary",)),
        name="ada",
    )(c_pad, w, b)


def _prenorm_kernel(x_ref, g_ref, sc_ref, sh_ref, o_ref):
    x = x_ref[0]
    y = x * lax.rsqrt(jnp.mean(x * x, axis=-1, keepdims=True) + NORM_EPS) * g_ref[...]
    o_ref[0] = (y * (1.0 + sc_ref[0]) + sh_ref[0]).astype(o_ref.dtype)


def _prenorm(x, g, mod3, sc_blk, sh_blk, ts):
    b, s, d = x.shape
    return pl.pallas_call(
        _prenorm_kernel,
        grid=(b, s // ts),
        in_specs=[pl.BlockSpec((1, ts, d), lambda bi, si: (bi, si, 0)),
                  pl.BlockSpec((1, d), lambda bi, si: (0, 0)),
                  pl.BlockSpec((1, 1, d), lambda bi, si: (bi, 0, sc_blk)),
                  pl.BlockSpec((1, 1, d), lambda bi, si: (bi, 0, sh_blk))],
        out_specs=pl.BlockSpec((1, ts, d), lambda bi, si: (bi, si, 0)),
        out_shape=jax.ShapeDtypeStruct((b, s, d), BF16),
        compiler_params=_cparams(("arbitrary", "arbitrary")),
        name="prenorm",
    )(x, g, mod3, mod3)


REGROUP_TILE = 512


def _regroup_kernel(src_sm, wt_hbm, plain_ref, rope_ref, gate_ref, dt_ref, buf, dt_buf, sem, dt_sem, *,
                    tiles, dt_rows, q_scale):
    t = pl.program_id(0)
    n = pl.num_programs(0)
    tr = REGROUP_TILE
    n_plain, n_rope, n_gate, n_q = tiles

    def load(step):
        row = pl.multiple_of(src_sm[step], 8)
        return pltpu.make_async_copy(wt_hbm.at[pl.ds(row, tr), :], buf.at[step % 2], sem.at[step % 2])

    dt_copy = pltpu.make_async_copy(wt_hbm.at[pl.ds(dt_rows[0], dt_rows[1]), :], dt_buf, dt_sem)

    @pl.when(t == 0)
    def _():
        load(0).start()
        dt_copy.start()

    pl.when(t + 1 < n)(lambda: load(t + 1).start())
    load(t).wait()
    cols = buf[t % 2].T

    @pl.when(t < n_plain)
    def _():
        plain_ref[...] = cols.astype(BF16)

    @pl.when((t >= n_plain) & (t < n_plain + n_rope))
    def _():
        rope_ref[...] = jnp.where(t < n_plain + n_q, cols * q_scale, cols).astype(BF16)

    @pl.when(t >= n_plain + n_rope)
    def _():
        gate_ref[...] = cols.astype(BF16)

    @pl.when(t == n - 1)
    def _():
        dt_copy.wait()
        dt_ref[...] = jnp.zeros_like(dt_ref)
        dt_ref[:, :dt_rows[1]] = dt_buf[...].T.astype(BF16)


def _regroup(w_in_t, segs, q_scale):
    n_in, d = w_in_t.shape
    tr = REGROUP_TILE
    starts = []
    counts = []
    for group in (("z", "xbc", "v"), ("q", "k"), ("gs", "ga")):
        c = 0
        for nm in group:
            lo, hi = segs[nm]
            assert (hi - lo) % tr == 0 and lo % 8 == 0
            starts += list(range(lo, hi, tr))
            c += (hi - lo) // tr
        counts.append(c)
    n_plain, n_rope, n_gate = counts
    n_q = (segs["q"][1] - segs["q"][0]) // tr
    dt_lo, dt_hi = segs["dt"]
    clamp = lambda t, first, cnt: jnp.clip(t - first, 0, cnt - 1)
    return pl.pallas_call(
        functools.partial(_regroup_kernel, tiles=(n_plain, n_rope, n_gate, n_q), dt_rows=(dt_lo, dt_hi - dt_lo),
                          q_scale=q_scale),
        grid_spec=pltpu.PrefetchScalarGridSpec(
            num_scalar_prefetch=1,
            grid=(len(starts),),
            in_specs=[pl.BlockSpec(memory_space=pl.ANY)],
            out_specs=[pl.BlockSpec((d, tr), lambda t, src: (0, clamp(t, 0, n_plain))),
                       pl.BlockSpec((d, tr), lambda t, src: (0, clamp(t, n_plain, n_rope))),
                       pl.BlockSpec((d, tr), lambda t, src: (0, clamp(t, n_plain + n_rope, n_gate))),
                       pl.BlockSpec((d, LANES), lambda t, src: (0, 0))],
            scratch_shapes=[pltpu.VMEM((2, tr, d), F32), pltpu.VMEM((dt_hi - dt_lo, d), F32),
                            pltpu.SemaphoreType.DMA((2,)), pltpu.SemaphoreType.DMA(())],
        ),
        out_shape=[jax.ShapeDtypeStruct((d, n_plain * tr), BF16), jax.ShapeDtypeStruct((d, n_rope * tr), BF16),
                   jax.ShapeDtypeStruct((d, n_gate * tr), BF16), jax.ShapeDtypeStruct((d, LANES), BF16)],
        compiler_params=_cparams(("arbitrary",)),
        name="regroup_w_in",
    )(jnp.asarray(starts, jnp.int32), w_in_t)


def _mm_kernel(a_ref, w_ref, o_ref):
    o_ref[...] = jnp.dot(a_ref[...], w_ref[...], preferred_element_type=F32).astype(o_ref.dtype)


def _row_chunks(n_rows):
    step = n_rows // MM_ROW_CHUNKS if n_rows % MM_ROW_CHUNKS == 0 and n_rows >= 64 * MM_ROW_CHUNKS else n_rows
    return [slice(r, r + step) for r in range(0, n_rows, step)]


def _mm_sigmoid_kernel(a_ref, w_ref, o_ref):
    for rows in _row_chunks(a_ref.shape[0]):
        acc = jnp.dot(a_ref[rows, :], w_ref[...], preferred_element_type=F32)
        o_ref[rows, :] = _sigmoid(acc).astype(o_ref.dtype)


def _mm_rope_kernel(a_ref, w_ref, c_ref, s1_ref, s2_ref, o_ref):
    for rows in _row_chunks(a_ref.shape[0]):
        acc = jnp.dot(a_ref[rows, :], w_ref[...], preferred_element_type=F32)
        c = c_ref[rows, :]
        s1 = s1_ref[rows, :]
        s2 = s2_ref[rows, :]
        for g in range(acc.shape[1] // LANES):
            blk = acc[:, g * LANES:(g + 1) * LANES]
            fwd = pltpu.roll(blk, LANES - ROT_DIM // 2, 1)
            bwd = pltpu.roll(blk, ROT_DIM // 2, 1)
            o_ref[rows, g * LANES:(g + 1) * LANES] = (blk * c + fwd * s1 + bwd * s2).astype(o_ref.dtype)


def _mm(a, w, out_dtype, tm, tn, epilogue="none", rope_tabs=None, name="mm"):
    m, k = a.shape
    n = w.shape[1]
    in_specs = [pl.BlockSpec((tm, k), lambda i, j: (i, 0)),
                pl.BlockSpec((k, tn), lambda i, j: (0, j))]
    args = [a, w]
    if epilogue == "rope":
        seq_blocks = rope_tabs[0].shape[0] // tm
        for t in rope_tabs:
            in_specs.append(pl.BlockSpec((tm, LANES), lambda i, j: (i % seq_blocks, 0)))
            args.append(t)
        kern = _mm_rope_kernel
    elif epilogue == "sigmoid":
        kern = _mm_sigmoid_kernel
    else:
        kern = _mm_kernel
    return pl.pallas_call(
        kern,
        grid=(m // tm, n // tn),
        in_specs=in_specs,
        out_specs=pl.BlockSpec((tm, tn), lambda i, j: (i, j)),
        out_shape=jax.ShapeDtypeStruct((m, n), out_dtype),
        compiler_params=_cparams(("arbitrary", "arbitrary")),
        name=name,
    )(*args)


def _ssd_kernel(z_ref, xs_ref, bm_ref, cm_ref, dt_ref, cwx_ref, cwb_ref, cwc_ref, cbx_ref, cbb_ref,
                cbc_ref, dtb_ref, alog_ref, dsk_ref, nw_ref, o_ref,
                px_sc, pb_sc, pc_sc, st_sc, y_sc, *, n_heads):
    c = pl.program_id(1)
    L = SSD_CHUNK
    N = SSD_STATE
    hpg = n_heads // SSD_GROUPS
    gw = hpg * SSD_HEADDIM

    @pl.when(c == 0)
    def _():
        px_sc[...] = jnp.zeros_like(px_sc)
        pb_sc[...] = jnp.zeros_like(pb_sc)
        pc_sc[...] = jnp.zeros_like(pc_sc)
        st_sc[...] = jnp.zeros_like(st_sc)

    def conv_silu(u, prev_sc, w_ref, b_ref):
        prev = prev_sc[...]
        row = lax.broadcasted_iota(jnp.int32, u.shape, 0)
        acc = u * w_ref[SSD_CONV - 1:SSD_CONV, :] + b_ref[...]
        for j in range(1, SSD_CONV):
            sh = jnp.where(row < j, pltpu.roll(prev, j, 0), pltpu.roll(u, j, 0))
            acc = acc + sh * w_ref[SSD_CONV - 1 - j:SSD_CONV - j, :]
        prev_sc[...] = u
        return acc * _sigmoid(acc)

    xs = conv_silu(xs_ref[0].astype(F32), px_sc, cwx_ref, cbx_ref)
    bm = conv_silu(bm_ref[0].astype(F32), pb_sc, cwb_ref, cbb_ref)
    cm = conv_silu(cm_ref[0].astype(F32), pc_sc, cwc_ref, cbc_ref)

    raw = dt_ref[0] + dtb_ref[...]
    dt = jnp.maximum(raw, 0.0) + jnp.log1p(jnp.exp(-jnp.abs(raw)))
    a = dt * (-jnp.exp(alog_ref[...]))
    r_i = lax.broadcasted_iota(jnp.int32, (L, L), 0)
    c_i = lax.broadcasted_iota(jnp.int32, (L, L), 1)
    causal = r_i >= c_i
    tri = jnp.where(causal, 1.0, 0.0).astype(F32)
    a_cs = jnp.dot(tri, a, preferred_element_type=F32, precision=HIGHEST)
    a_cs_t = a_cs.T

    lane = lax.broadcasted_iota(jnp.int32, (L, LANES), 1)
    first = lane < SSD_HEADDIM
    acs_tiles = []
    dt_tiles = []
    for j in range(n_heads // 2):
        h0, h1 = 2 * j, 2 * j + 1
        acs_tiles.append(jnp.where(first, a_cs[:, h0:h0 + 1], a_cs[:, h1:h1 + 1]))
        dt_tiles.append(jnp.where(first, dt[:, h0:h0 + 1], dt[:, h1:h1 + 1]))
    acs_e = jnp.concatenate(acs_tiles, axis=1)
    dt_e = jnp.concatenate(dt_tiles, axis=1)
    xdt = xs * dt_e
    ea = jnp.exp(acs_e)
    alast = acs_e[L - 1:L, :]
    xdec = (xdt * jnp.exp(alast - acs_e)).astype(BF16)
    ealast = jnp.exp(alast)

    for g in range(SSD_GROUPS):
        bg = bm[:, g * N:(g + 1) * N]
        cg = cm[:, g * N:(g + 1) * N].astype(BF16)
        scores = lax.dot_general(cg, bg.astype(BF16), (((1,), (1,)), ((), ())),
                                 preferred_element_type=F32)
        st_old = st_sc[:, g * gw:(g + 1) * gw]
        y_off = jnp.dot(cg, st_old.astype(BF16), preferred_element_type=F32) * ea[:, g * gw:(g + 1) * gw]
        st_sc[:, g * gw:(g + 1) * gw] = st_old * ealast[:, g * gw:(g + 1) * gw] + jnp.dot(
            bg.T.astype(BF16), xdec[:, g * gw:(g + 1) * gw], preferred_element_type=F32)
        for jj in range(hpg // 2):
            j = g * (hpg // 2) + jj
            h0, h1 = 2 * j, 2 * j + 1
            la = jnp.exp(jnp.where(causal, a_cs[:, h0:h0 + 1] - a_cs_t[h0:h0 + 1, :], NEG_BIG))
            lb = jnp.exp(jnp.where(causal, a_cs[:, h1:h1 + 1] - a_cs_t[h1:h1 + 1, :], NEG_BIG))
            mcat = jnp.concatenate([(scores * la).astype(BF16), (scores * lb).astype(BF16)], axis=1)
            xp = xdt[:, j * LANES:(j + 1) * LANES]
            xcat = jnp.concatenate([jnp.where(first, xp, 0.0).astype(BF16),
                                    jnp.where(first, 0.0, xp).astype(BF16)], axis=0)
            y_diag = jnp.dot(mcat, xcat, preferred_element_type=F32)
            lo = jj * LANES
            y_sc[:, j * LANES:(j + 1) * LANES] = (
                y_diag + y_off[:, lo:lo + LANES]
                + dsk_ref[:, j * LANES:(j + 1) * LANES] * xs[:, j * LANES:(j + 1) * LANES])

    z = z_ref[0].astype(F32)
    u = y_sc[...] * (z * _sigmoid(z))
    for g in range(SSD_GROUPS):
        ug = u[:, g * gw:(g + 1) * gw]
        ms = jnp.mean(ug * ug, axis=-1, keepdims=True)
        o_ref[0, :, g * gw:(g + 1) * gw] = (ug * lax.rsqrt(ms + SUB_EPS)
                                            * nw_ref[:, g * gw:(g + 1) * gw]).astype(o_ref.dtype)


def _ssd(plain3, dt3, conv_w, conv_b, dtb, alog, dsk_e, norm_w, dh, n_heads):
    b, s, _ = plain3.shape
    L = SSD_CHUNK
    gn = SSD_GROUPS * SSD_STATE
    nc = s // L
    xblk = 1
    bblk = (2 * dh) // gn
    cw_x, cw_b, cw_c = conv_w[:, :dh], conv_w[:, dh:dh + gn], conv_w[:, dh + gn:]
    cb_x, cb_b, cb_c = conv_b[:, :dh], conv_b[:, dh:dh + gn], conv_b[:, dh + gn:]
    full = lambda shape: pl.BlockSpec(shape, lambda bi, ci: (0, 0))
    return pl.pallas_call(
        functools.partial(_ssd_kernel, n_heads=n_heads),
        grid=(b, nc),
        in_specs=[pl.BlockSpec((1, L, dh), lambda bi, ci: (bi, ci, 0)),
                  pl.BlockSpec((1, L, dh), lambda bi, ci: (bi, ci, xblk)),
                  pl.BlockSpec((1, L, gn), lambda bi, ci: (bi, ci, bblk)),
                  pl.BlockSpec((1, L, gn), lambda bi, ci: (bi, ci, bblk + 1)),
                  pl.BlockSpec((1, L, LANES), lambda bi, ci: (bi, ci, 0)),
                  full((SSD_CONV, dh)), full((SSD_CONV, gn)), full((SSD_CONV, gn)),
                  full((1, dh)), full((1, gn)), full((1, gn)),
                  full((1, LANES)), full((1, LANES)), full((1, dh)), full((1, dh))],
        out_specs=pl.BlockSpec((1, L, dh), lambda bi, ci: (bi, ci, 0)),
        out_shape=jax.ShapeDtypeStruct((b, s, dh), BF16),
        scratch_shapes=[pltpu.VMEM((L, dh), F32), pltpu.VMEM((L, gn), F32), pltpu.VMEM((L, gn), F32),
                        pltpu.VMEM((SSD_STATE, dh), F32), pltpu.VMEM((L, dh), F32)],
        compiler_params=_cparams(("arbitrary", "arbitrary")),
        name="ssd",
    )(plain3, plain3, plain3, plain3, dt3, cw_x, cw_b, cw_c, cb_x, cb_b, cb_c, dtb, alog, dsk_e, norm_w)


def _attn_kernel(q_ref, k_ref, v_ref, bias_ref, lam_ref, sw_ref, o_ref, vt_sc, st_a, st_b, m_sc, l_sc, acc_sc, *,
                 tq, lam_init):
    qi = pl.program_id(2)
    tu = ATTN_KV_UNIT
    nh = ATTN_HEADS_PER_STEP
    n_all = k_ref.shape[1] // tu
    heads = [slice(hh * LANES, (hh + 1) * LANES) for hh in range(nh)]

    @pl.when(qi == 0)
    def _():
        def transpose_block(c, carry):
            start = pl.multiple_of(c * tu, tu)
            for hh in range(nh):
                vt_sc[hh, c] = v_ref[0, pl.ds(start, tu), heads[hh]].astype(F32).T.astype(BF16)
            return carry

        lax.fori_loop(0, n_all, transpose_block, 0)

    qts = []
    for hh in range(nh):
        qt = q_ref[0, :, heads[hh]].astype(F32).T
        row = lax.broadcasted_iota(jnp.int32, qt.shape, 0)
        qts.append((jnp.where(row < ATTN_DK, qt, 0.0).astype(BF16),
                    jnp.where(row < ATTN_DK, 0.0, qt).astype(BF16)))
    m_sc[...] = jnp.full_like(m_sc, NEG_BIG)
    l_sc[...] = jnp.zeros_like(l_sc)
    acc_sc[...] = jnp.zeros_like(acc_sc)

    def scores(u, st_ref):
        start = pl.multiple_of(u * tu, tu)
        for hh in range(nh):
            k = k_ref[0, pl.ds(start, tu), heads[hh]]
            for m in range(2):
                st_ref[2 * hh + m] = jnp.dot(k, qts[hh][m], preferred_element_type=F32)

    def update(u, st_ref, masked):
        for hh in range(nh):
            vt = vt_sc[hh, u]
            for m in range(2):
                c = 2 * hh + m
                st = st_ref[c]
                if masked:
                    st = st + bias_ref[...]
                m_prev = m_sc[c]
                m_new = jnp.maximum(m_prev, jnp.max(st, axis=0, keepdims=True))
                alpha = jnp.exp2(m_prev - m_new)
                pt = jnp.exp2(st - m_new)
                l_sc[c] = alpha * l_sc[c] + jnp.sum(pt, axis=0, keepdims=True)
                acc_sc[c] = alpha * acc_sc[c] + jnp.dot(vt, pt.astype(BF16), preferred_element_type=F32)
                m_sc[c] = m_new

    n_units = (qi * tq) // tu + 1
    n_loop = (n_units - 1) // 2
    scores(0, st_a)

    def two_units(j, carry):
        u = 2 * j
        scores(u + 1, st_b)
        update(u, st_a, False)
        scores(u + 2, st_a)
        update(u + 1, st_b, False)
        return carry

    lax.fori_loop(0, n_loop, two_units, 0)
    last = n_units - 1

    @pl.when(last == 2 * n_loop)
    def _():
        update(last, st_a, True)

    @pl.when(last != 2 * n_loop)
    def _():
        scores(last, st_b)
        update(last - 1, st_a, False)
        update(last, st_b, True)

    lv = lam_ref[...]
    lam = (jnp.exp(jnp.sum(lv[0:1] * lv[1:2], axis=-1, keepdims=True))
           - jnp.exp(jnp.sum(lv[2:3] * lv[3:4], axis=-1, keepdims=True)) + lam_init)
    for hh in range(nh):
        c0, c1 = 2 * hh, 2 * hh + 1
        ot = acc_sc[c0] * (1.0 / l_sc[c0]) - lam * (acc_sc[c1] * (1.0 / l_sc[c1]))
        ot = ot * lax.rsqrt(jnp.mean(ot * ot, axis=0, keepdims=True) + SUB_EPS)
        o_ref[0, :, heads[hh]] = (ot.T * sw_ref[...] * (1.0 - lam_init)).astype(o_ref.dtype)


def _attention(qk3, plain3, lam_rows, subln_w, n_heads, v_blk0, tq, lam_init):
    b, s, _ = qk3.shape
    tu = ATTN_KV_UNIT
    nh = ATTN_HEADS_PER_STEP
    hw = nh * LANES
    assert tq == tu, "the diagonal unit must coincide with the query tile"
    kpos = lax.broadcasted_iota(jnp.int32, (tu, tq), 0)
    qpos = lax.broadcasted_iota(jnp.int32, (tu, tq), 1)
    diag_bias = jnp.where(kpos <= qpos, 0.0, NEG_BIG).astype(F32)
    return pl.pallas_call(
        functools.partial(_attn_kernel, tq=tq, lam_init=lam_init),
        grid=(b, n_heads // nh, s // tq),
        in_specs=[pl.BlockSpec((1, tq, hw), lambda bi, hi, qi: (bi, qi, hi)),
                  pl.BlockSpec((1, s, hw), lambda bi, hi, qi: (bi, 0, n_heads // nh + hi)),
                  pl.BlockSpec((1, s, hw), lambda bi, hi, qi: (bi, 0, v_blk0 // nh + hi)),
                  pl.BlockSpec((tu, tq), lambda bi, hi, qi: (0, 0)),
                  pl.BlockSpec((8, LANES), lambda bi, hi, qi: (0, 0)),
                  pl.BlockSpec((1, LANES), lambda bi, hi, qi: (0, 0))],
        out_specs=pl.BlockSpec((1, tq, hw), lambda bi, hi, qi: (bi, qi, hi)),
        out_shape=jax.ShapeDtypeStruct((b, s, n_heads * ATTN_DV), BF16),
        scratch_shapes=[pltpu.VMEM((nh, s // tu, ATTN_DV, tu), BF16),
                        pltpu.VMEM((2 * nh, tu, tq), F32), pltpu.VMEM((2 * nh, tu, tq), F32),
                        pltpu.VMEM((2 * nh, 1, tq), F32), pltpu.VMEM((2 * nh, 1, tq), F32),
                        pltpu.VMEM((2 * nh, ATTN_DV, tq), F32)],
        compiler_params=_cparams(("arbitrary", "arbitrary", "arbitrary")),
        name="diff_attn",
    )(qk3, qk3, plain3, diag_bias, lam_rows, subln_w)


def _merge_kernel(y_ref, o_ref, ws_ref, wa_ref, gs_ref, ga_ref, out_ref):
    for rows in _row_chunks(y_ref.shape[0]):
        bs = jnp.dot(y_ref[rows, :], ws_ref[...], preferred_element_type=F32)
        ba = jnp.dot(o_ref[rows, :], wa_ref[...], preferred_element_type=F32)
        out_ref[rows, :] = (gs_ref[rows, :].astype(F32) * bs
                            + ga_ref[rows, :].astype(F32) * ba).astype(out_ref.dtype)


def _merge(y, o, ws, wa, gates, tm, tn):
    m, k = y.shape
    n = ws.shape[1]
    nj = n // tn
    return pl.pallas_call(
        _merge_kernel,
        grid=(m // tm, nj),
        in_specs=[pl.BlockSpec((tm, k), lambda i, j: (i, 0)),
                  pl.BlockSpec((tm, o.shape[1]), lambda i, j: (i, 0)),
                  pl.BlockSpec((k, tn), lambda i, j: (0, j)),
                  pl.BlockSpec((o.shape[1], tn), lambda i, j: (0, j)),
                  pl.BlockSpec((tm, tn), lambda i, j: (i, j)),
                  pl.BlockSpec((tm, tn), lambda i, j: (i, nj + j))],
        out_specs=pl.BlockSpec((tm, tn), lambda i, j: (i, j)),
        out_shape=jax.ShapeDtypeStruct((m, n), BF16),
        compiler_params=_cparams(("arbitrary", "arbitrary")),
        name="merge",
    )(y, o, ws, wa, gates, gates)


def _outproj_kernel(mg_ref, wo_ref, x_ref, gpost_ref, gt_ref, gpre_ref, sc_ref, sh_ref, wr_hi_ref,
                    wr_lo_ref, br_ref, x1_ref, hp_ref, lg_ref):
    mix = jnp.dot(mg_ref[...], wo_ref[...], preferred_element_type=F32)
    nm = mix * lax.rsqrt(jnp.mean(mix * mix, axis=-1, keepdims=True) + NORM_EPS) * gpost_ref[...]
    x1 = x_ref[...] + gt_ref[0] * nm
    x1_ref[...] = x1
    h2 = (x1 * lax.rsqrt(jnp.mean(x1 * x1, axis=-1, keepdims=True) + NORM_EPS) * gpre_ref[...]
          * (1.0 + sc_ref[0]) + sh_ref[0])
    h_hi = h2.astype(BF16)
    h_lo = (h2 - h_hi.astype(F32)).astype(BF16)
    wr_hi = wr_hi_ref[...]
    lg_ref[...] = (jnp.dot(h_hi, wr_hi, preferred_element_type=F32)
                   + jnp.dot(h_lo, wr_hi, preferred_element_type=F32)
                   + jnp.dot(h_hi, wr_lo_ref[...], preferred_element_type=F32) + br_ref[...])
    _store_token_rows(hp_ref, 0, h2)


def _outproj(merged, wo, x2, gpost, mod3, gpre, wr_hi, wr_lo, br, seq, tm, gt_blk, sc_blk, sh_blk):
    m, d = x2.shape
    per_b = seq // tm
    row = lambda i: (i, 0)
    const = lambda i: (0, 0)
    return pl.pallas_call(
        _outproj_kernel,
        grid=(m // tm,),
        in_specs=[pl.BlockSpec((tm, d), row),
                  pl.BlockSpec((d, d), const),
                  pl.BlockSpec((tm, d), row),
                  pl.BlockSpec((1, d), const),
                  pl.BlockSpec((1, 1, d), lambda i: (i // per_b, 0, gt_blk)),
                  pl.BlockSpec((1, d), const),
                  pl.BlockSpec((1, 1, d), lambda i: (i // per_b, 0, sc_blk)),
                  pl.BlockSpec((1, 1, d), lambda i: (i // per_b, 0, sh_blk)),
                  pl.BlockSpec((d, LANES), const),
                  pl.BlockSpec((d, LANES), const),
                  pl.BlockSpec((1, LANES), const)],
        out_specs=[pl.BlockSpec((tm, d), row),
                   pl.BlockSpec((tm * (d // LANES), LANES), row),
                   pl.BlockSpec((tm, LANES), row)],
        out_shape=[jax.ShapeDtypeStruct((m, d), F32),
                   jax.ShapeDtypeStruct((m * (d // LANES), LANES), F32),
                   jax.ShapeDtypeStruct((m, LANES), F32)],
        compiler_params=_cparams(("arbitrary",)),
        name="outproj",
    )(merged, wo, x2, gpost, mod3, gpre, mod3, mod3, wr_hi, wr_lo, br)


def _route_kernel(lg_ref, dest_ref, w_ref, cnt_ref, cnt_sc, pst_sc, run_sc, *, n_experts, blk):
    ph = pl.program_id(0)
    t = pl.program_id(1)
    tk = lg_ref.shape[0]
    lane = lax.broadcasted_iota(jnp.int32, (tk, LANES), 1)
    lg = jnp.where(lane < n_experts, lg_ref[...], -jnp.inf)
    vals = []
    hots = []
    for _ in range(TOP_K):
        mx = jnp.max(lg, axis=-1, keepdims=True)
        ix = jnp.min(jnp.where(lg == mx, lane, LANES), axis=-1, keepdims=True)
        hot = lane == ix
        lg = jnp.where(hot, -jnp.inf, lg)
        vals.append(mx)
        hots.append(hot)
    multi = jnp.zeros((tk, LANES), F32)
    for hot in hots:
        multi = multi + jnp.where(hot, 1.0, 0.0)
    colsum = jnp.sum(multi, axis=0, keepdims=True)

    @pl.when((ph == 0) & (t == 0))
    def _():
        cnt_sc[...] = jnp.zeros_like(cnt_sc)

    @pl.when(ph == 0)
    def _():
        cnt_sc[...] += colsum

    @pl.when((ph == 1) & (t == 0))
    def _():
        cnt = cnt_sc[...].astype(jnp.int32)
        padded = (((cnt + (blk - 1)) // blk) * blk).astype(F32)
        r_i = lax.broadcasted_iota(jnp.int32, (LANES, LANES), 0)
        c_i = lax.broadcasted_iota(jnp.int32, (LANES, LANES), 1)
        upper = jnp.where(r_i < c_i, 1.0, 0.0).astype(F32)
        pst_sc[...] = jnp.dot(jnp.broadcast_to(padded, (8, LANES)), upper,
                              preferred_element_type=F32, precision=HIGHEST)[0:1]
        run_sc[...] = jnp.zeros_like(run_sc)

    @pl.when(ph == 1)
    def _():
        r_i = lax.broadcasted_iota(jnp.int32, (tk, tk), 0)
        c_i = lax.broadcasted_iota(jnp.int32, (tk, tk), 1)
        strict = jnp.where(r_i > c_i, 1.0, 0.0).astype(BF16)
        before = jnp.dot(strict, multi.astype(BF16), preferred_element_type=F32)
        base = before + run_sc[...] + pst_sc[...]
        esum = jnp.zeros((tk, 1), F32)
        evals = []
        for r in range(TOP_K):
            e = jnp.exp(vals[r] - vals[0])
            evals.append(e)
            esum = esum + e
        dest = jnp.zeros((tk, LANES), jnp.int32)
        wts = jnp.zeros((tk, LANES), F32)
        for r in range(TOP_K):
            d_r = jnp.sum(jnp.where(hots[r], base, 0.0), axis=-1, keepdims=True).astype(jnp.int32)
            dest = jnp.where(lane == r, d_r, dest)
            wts = jnp.where(lane == r, evals[r] / esum, wts)
        dest_ref[...] = dest
        w_ref[...] = wts
        run_sc[...] += colsum
        cnt_ref[...] = jnp.broadcast_to(cnt_sc[...], cnt_ref.shape)


def _route(logits, n_experts, blk, tk):
    t = logits.shape[0]
    return pl.pallas_call(
        functools.partial(_route_kernel, n_experts=n_experts, blk=blk),
        grid=(2, t // tk),
        in_specs=[pl.BlockSpec((tk, LANES), lambda ph, ti: (ti, 0))],
        out_specs=[pl.BlockSpec((tk, LANES), lambda ph, ti: (ti * ph, 0)),
                   pl.BlockSpec((tk, LANES), lambda ph, ti: (ti * ph, 0)),
                   pl.BlockSpec((8, LANES), lambda ph, ti: (0, 0))],
        out_shape=[jax.ShapeDtypeStruct((t, LANES), jnp.int32),
                   jax.ShapeDtypeStruct((t, LANES), F32),
                   jax.ShapeDtypeStruct((8, LANES), F32)],
        scratch_shapes=[pltpu.VMEM((1, LANES), F32), pltpu.VMEM((1, LANES), F32), pltpu.VMEM((1, LANES), F32)],
        compiler_params=_cparams(("arbitrary", "arbitrary")),
        name="route",
    )(logits)


def _pad_fill_copies(b, nv_sm, zero_sc, xs_hbm, sem, blk, ns):
    nv = nv_sm[b]
    out = []
    off = b * blk + nv
    rest = blk - nv
    p = blk
    while p >= 1:
        cond = (rest & p) != 0
        out.append((cond, pltpu.make_async_copy(
            zero_sc.at[pl.ds(0, p * ns), :], xs_hbm.at[pl.ds(pl.multiple_of(off * ns, ns), p * ns), :], sem)))
        off = off + jnp.where(cond, p, 0)
        p //= 2
    return out


def _dispatch_kernel(dest_sm, nv_sm, h_hbm, xs_hbm, zero_sc, hbuf, in_sem, sem, fill_sem, *, tt, blk, nb, ns):
    i = pl.program_id(0)
    n = pl.num_programs(0)
    base = i * tt
    rows = tt * ns

    def fill(b, carry):
        for cond, cp in _pad_fill_copies(b, nv_sm, zero_sc, xs_hbm, fill_sem, blk, ns):
            pl.when(cond)(cp.start)
        return carry

    def fill_wait(b, carry):
        for cond, cp in _pad_fill_copies(b, nv_sm, zero_sc, xs_hbm, fill_sem, blk, ns):
            pl.when(cond)(cp.wait)
        return carry

    def load(step):
        slot = step % 3
        return pltpu.make_async_copy(h_hbm.at[pl.ds(pl.multiple_of(step * rows, rows), rows), :],
                                     hbuf.at[slot], in_sem.at[slot])

    @pl.when(i == 0)
    def _():
        zero_sc[...] = jnp.zeros_like(zero_sc)
        lax.fori_loop(0, nb, fill, 0)
        load(0).start()
        pl.when(n > 1)(lambda: load(1).start())

    def wait_step(step):
        for _ in range(TOP_K):
            pltpu.make_async_copy(hbuf.at[0], xs_hbm.at[pl.ds(0, rows), :], sem.at[step % 2]).wait()

    load(i).wait()
    src_tile = hbuf.at[i % 3]

    def body(t, carry):
        src = src_tile.at[pl.ds(pl.multiple_of(t * ns, ns), ns), :]
        for k in range(TOP_K):
            d = dest_sm[(base + t) * TOP_K + k]
            pltpu.make_async_copy(src, xs_hbm.at[pl.ds(pl.multiple_of(d * ns, ns), ns), :],
                                  sem.at[i % 2]).start()
        return carry

    lax.fori_loop(0, tt, body, 0)
    pl.when(i > 0)(lambda: wait_step(i - 1))
    pl.when(i + 2 < n)(lambda: load(i + 2).start())

    @pl.when(i == n - 1)
    def _():
        wait_step(i)
        lax.fori_loop(0, nb, fill_wait, 0)


def _dispatch(dest_flat, nvalid, h_rows, n_slots, tt, blk, ns):
    nb = n_slots // blk
    t = h_rows.shape[0] // ns
    return pl.pallas_call(
        functools.partial(_dispatch_kernel, tt=tt, blk=blk, nb=nb, ns=ns),
        grid_spec=pltpu.PrefetchScalarGridSpec(
            num_scalar_prefetch=2,
            grid=(t // tt,),
            in_specs=[pl.BlockSpec(memory_space=pl.ANY)],
            out_specs=pl.BlockSpec(memory_space=pl.ANY),
            scratch_shapes=[pltpu.VMEM((blk * ns, LANES), F32), pltpu.VMEM((3, tt * ns, LANES), F32),
                            pltpu.SemaphoreType.DMA((3,)), pltpu.SemaphoreType.DMA((2,)),
                            pltpu.SemaphoreType.DMA(())],
        ),
        out_shape=jax.ShapeDtypeStruct((n_slots * ns, LANES), F32),
        compiler_params=_cparams(("arbitrary",)),
        name="dispatch",
    )(dest_flat, nvalid, h_rows)


def _expert_changed(be, i, last):
    ii = jnp.minimum(i, last)
    prev = jnp.maximum(ii - 1, 0)
    return (i == 0) | (be[ii] != be[prev])


def _for_live_sub_blocks(active, n_valid, sub, o_ref, compute, rows_per_slot=1):
    n_live = jnp.where(active, (n_valid + sub - 1) // sub, 0)
    for count in range(MOE_SUB_BLOCKS + 1):
        @pl.when(n_live == count)
        def _(count=count):
            for r in range(count):
                compute(slice(r * sub, (r + 1) * sub))
            if count < MOE_SUB_BLOCKS:
                first = count * sub * rows_per_slot
                o_ref[first:, :] = jnp.zeros((o_ref.shape[0] - first, o_ref.shape[1]), o_ref.dtype)


def _stream_expert_weights(changed, first, prefetch, wait_cur, cast, start_next):
    @pl.when(changed)
    def _():
        pl.when(first)(lambda: start_next(True))
        wait_cur()
        cast()
        pl.when(prefetch)(lambda: start_next(False))


def _gateup_kernel(be, nv, nu, nxt, x_ref, w_hbm, bg_ref, bu_ref, o_ref, wbuf, wg_sc, wu_sc, sem):
    j = pl.program_id(0)
    i = pl.program_id(1)
    nj = pl.num_programs(0)
    th = wg_sc.shape[1]
    dff = w_hbm.shape[2] // 2
    last = nu[0] - 1
    active = i < nu[0]
    ii = jnp.minimum(i, last)

    def copies(e, jj):
        col = pl.multiple_of(jj * th, th)
        return (pltpu.make_async_copy(w_hbm.at[e, :, pl.ds(col, th)], wbuf.at[0], sem.at[0]),
                pltpu.make_async_copy(w_hbm.at[e, :, pl.ds(dff + col, th)], wbuf.at[1], sem.at[1]))

    nx = nxt[ii]
    same_pass = nx >= 0
    e_next = jnp.where(same_pass, be[jnp.maximum(nx, 0)], be[0])
    j_next = jnp.where(same_pass, j, j + 1)

    def start_next(current):
        for cp in (copies(be[ii], j) if current else copies(e_next, j_next)):
            cp.start()

    def wait_cur():
        for cp in copies(be[ii], j):
            cp.wait()

    def cast():
        def chunk(r, carry):
            rows = pl.ds(pl.multiple_of(r * CAST_ROWS, CAST_ROWS), CAST_ROWS)
            wg_sc[rows, :] = wbuf[0, rows, :].astype(BF16)
            wu_sc[rows, :] = wbuf[1, rows, :].astype(BF16)
            return carry

        lax.fori_loop(0, wg_sc.shape[0] // CAST_ROWS, chunk, 0)

    _stream_expert_weights(active & _expert_changed(be, i, last), (j == 0) & (i == 0),
                           same_pass | (j + 1 < nj), wait_cur, cast, start_next)

    sub = o_ref.shape[0] // MOE_SUB_BLOCKS

    def compute(rows):
        x = _load_token_rows(x_ref, rows.start, sub, wg_sc.shape[0] // LANES).astype(BF16)
        g = jnp.dot(x, wg_sc[...], preferred_element_type=F32) + bg_ref[0]
        up = jnp.dot(x, wu_sc[...], preferred_element_type=F32) + bu_ref[0]
        gate = jnp.minimum(g, SWIGLU_LIMIT)
        up = jnp.clip(up, -SWIGLU_LIMIT, SWIGLU_LIMIT)
        o_ref[rows, :] = ((up + 1.0) * gate * _sigmoid(SWIGLU_ALPHA * gate)).astype(o_ref.dtype)

    _for_live_sub_blocks(active, nv[i], sub, o_ref, compute)


def _gateup(block_e, nvalid, nused, nxt, xs, w_gu, b_gu3, tm, th):
    d = w_gu.shape[1]
    ns = d // LANES
    n_slots = xs.shape[0] // ns
    dff = w_gu.shape[2] // 2
    nj = dff // th
    nb = n_slots // tm

    def blk(i, nu):
        return jnp.minimum(i, nu[0] - 1)

    return pl.pallas_call(
        _gateup_kernel,
        grid_spec=pltpu.PrefetchScalarGridSpec(
            num_scalar_prefetch=4,
            grid=(nj, nb),
            in_specs=[pl.BlockSpec((tm * ns, LANES), lambda j, i, be, nv, nu, nx: (blk(i, nu), 0)),
                      pl.BlockSpec(memory_space=pl.ANY),
                      pl.BlockSpec((1, 1, th), lambda j, i, be, nv, nu, nx: (be[blk(i, nu)], 0, j)),
                      pl.BlockSpec((1, 1, th), lambda j, i, be, nv, nu, nx: (be[blk(i, nu)], 0, nj + j))],
            out_specs=pl.BlockSpec((tm, th), lambda j, i, be, nv, nu, nx: (i, j)),
            scratch_shapes=[pltpu.VMEM((2, d, th), F32), pltpu.VMEM((d, th), BF16), pltpu.VMEM((d, th), BF16),
                            pltpu.SemaphoreType.DMA((2,))],
        ),
        out_shape=jax.ShapeDtypeStruct((n_slots, dff), BF16),
        compiler_params=_cparams(("arbitrary", "arbitrary")),
        name="expert_gate_up",
    )(block_e, nvalid, nused, nxt, xs, w_gu, b_gu3, b_gu3)


def _down_kernel(be, nv, nu, nxt, a_ref, w_hbm, bd_ref, o_ref, wbuf, wd_sc, sem):
    i = pl.program_id(1)
    last = nu[0] - 1
    active = i < nu[0]
    ii = jnp.minimum(i, last)
    nx = nxt[ii]

    def copy(e):
        return pltpu.make_async_copy(w_hbm.at[e], wbuf, sem)

    def start_next(current):
        copy(be[ii] if current else be[jnp.maximum(nx, 0)]).start()

    def cast():
        def chunk(r, carry):
            rows = pl.ds(pl.multiple_of(r * CAST_ROWS, CAST_ROWS), CAST_ROWS)
            wd_sc[rows, :] = wbuf[rows, :].astype(BF16)
            return carry

        lax.fori_loop(0, wd_sc.shape[0] // CAST_ROWS, chunk, 0)

    _stream_expert_weights(active & _expert_changed(be, i, last), i == 0, nx >= 0,
                           lambda: copy(be[ii]).wait(), cast, start_next)

    sub = a_ref.shape[0] // MOE_SUB_BLOCKS
    ns = o_ref.shape[0] // a_ref.shape[0]

    def compute(rows):
        y = jnp.dot(a_ref[rows, :], wd_sc[...], preferred_element_type=F32) + bd_ref[0]
        _store_token_rows(o_ref, rows.start, y)

    _for_live_sub_blocks(active, nv[i], sub, o_ref, compute, rows_per_slot=ns)


def _down(block_e, nvalid, nused, nxt, act, w_d, b_d3, tm):
    n_slots, dff = act.shape
    d = w_d.shape[2]
    ns = d // LANES
    nb = n_slots // tm

    def blk(i, nu):
        return jnp.minimum(i, nu[0] - 1)

    return pl.pallas_call(
        _down_kernel,
        grid_spec=pltpu.PrefetchScalarGridSpec(
            num_scalar_prefetch=4,
            grid=(1, nb),
            in_specs=[pl.BlockSpec((tm, dff), lambda j, i, be, nv, nu, nx: (blk(i, nu), 0)),
                      pl.BlockSpec(memory_space=pl.ANY),
                      pl.BlockSpec((1, 1, d), lambda j, i, be, nv, nu, nx: (be[blk(i, nu)], 0, 0))],
            out_specs=pl.BlockSpec((tm * ns, LANES), lambda j, i, be, nv, nu, nx: (i, 0)),
            scratch_shapes=[pltpu.VMEM((dff, d), F32), pltpu.VMEM((dff, d), BF16), pltpu.SemaphoreType.DMA(())],
        ),
        out_shape=jax.ShapeDtypeStruct((n_slots * ns, LANES), F32),
        compiler_params=_cparams(("arbitrary", "arbitrary")),
        name="expert_down",
    )(block_e, nvalid, nused, nxt, act, w_d, b_d3)


def _combine_kernel(dest_sm, y_hbm, w_ref, x1_ref, gt_ref, g_ref, o_ref, buf, sem, *, tt, ns):
    i = pl.program_id(0)
    n = pl.num_programs(0)

    def gather(tile, slot):
        def body(t, carry):
            for k in range(TOP_K):
                d = dest_sm[(tile * tt + t) * TOP_K + k]
                pltpu.make_async_copy(y_hbm.at[pl.ds(pl.multiple_of(d * ns, ns), ns), :],
                                      buf.at[slot, pl.ds(pl.multiple_of((k * tt + t) * ns, ns), ns), :],
                                      sem.at[slot]).start()
            return carry

        lax.fori_loop(0, tt, body, 0)

    slot = i % 2

    @pl.when(i == 0)
    def _():
        gather(0, 0)

    @pl.when(i + 1 < n)
    def _():
        gather(i + 1, 1 - slot)

    pltpu.make_async_copy(y_hbm.at[pl.ds(0, TOP_K * tt * ns), :], buf.at[slot], sem.at[slot]).wait()
    w = w_ref[...]
    rows = buf.at[slot]
    f = None
    for k in range(TOP_K):
        yk = _load_token_rows(rows, k * tt, tt, ns) * w[:, k:k + 1]
        f = yk if f is None else f + yk
    nf = f * lax.rsqrt(jnp.mean(f * f, axis=-1, keepdims=True) + NORM_EPS) * g_ref[...]
    o_ref[...] = x1_ref[...] + gt_ref[0] * nf


def _combine(dest_flat, y_rows, wts, x1, mod3, gpost, seq, tt, gt_blk):
    t, d = x1.shape
    ns = d // LANES
    per_b = seq // tt
    return pl.pallas_call(
        functools.partial(_combine_kernel, tt=tt, ns=ns),
        grid_spec=pltpu.PrefetchScalarGridSpec(
            num_scalar_prefetch=1,
            grid=(t // tt,),
            in_specs=[pl.BlockSpec(memory_space=pl.ANY),
                      pl.BlockSpec((tt, LANES), lambda i, ds: (i, 0)),
                      pl.BlockSpec((tt, d), lambda i, ds: (i, 0)),
                      pl.BlockSpec((1, 1, d), lambda i, ds: (i // per_b, 0, gt_blk)),
                      pl.BlockSpec((1, d), lambda i, ds: (0, 0))],
            out_specs=pl.BlockSpec((tt, d), lambda i, ds: (i, 0)),
            scratch_shapes=[pltpu.VMEM((2, TOP_K * tt * ns, LANES), F32), pltpu.SemaphoreType.DMA((2,))],
        ),
        out_shape=jax.ShapeDtypeStruct((t, d), F32),
        compiler_params=_cparams(("arbitrary",)),
        name="combine",
    )(dest_flat, y_rows, wts, x1, mod3, gpost)


def _tile(n, pref):
    t = min(n, pref)
    while n % t:
        t //= 2
    return t


def _rope_tables(seq):
    half = ROT_DIM // 2
    inv = ROPE_THETA ** (-jnp.arange(0, ROT_DIM, 2, dtype=F32) / ROT_DIM)
    ang = jnp.arange(seq, dtype=F32)[:, None] * inv[None, :]
    cos, sin = jnp.cos(ang), jnp.sin(ang)
    ones = jnp.ones((seq, ATTN_DK - ROT_DIM), F32)
    zeros = jnp.zeros((seq, ATTN_DK - ROT_DIM), F32)
    zh = jnp.zeros((seq, half), F32)
    c64 = jnp.concatenate([cos, cos, ones], axis=1)
    s1_64 = jnp.concatenate([-sin, zh, zeros], axis=1)
    s2_64 = jnp.concatenate([zh, sin, zeros], axis=1)
    rep = LANES // ATTN_DK
    return tuple(jnp.tile(t, (1, rep)) for t in (c64, s1_64, s2_64))


def _layer(x, c_pad, l, p, moe_blk):
    bsz, seq, d = x.shape
    t = bsz * seq
    dh = d
    n_sheads = dh // SSD_HEADDIM
    gn = SSD_GROUPS * SSD_STATE
    n_aheads = d // ATTN_DV
    aw = n_aheads * ATTN_DV
    qkw = 2 * n_aheads * ATTN_DK
    n_experts = p["w_router"].shape[-1]

    mod = _ada(c_pad, p["w_ada"][l], p["b_ada"][l][None, :], _tile(6 * d, 1024))
    mod3 = mod[:bsz].reshape(bsz, 1, 6 * d)

    o = 0
    segs = {}
    for name, size in (("z", dh), ("xbc", dh + 2 * gn), ("dt", n_sheads), ("q", qkw), ("k", qkw),
                       ("v", aw), ("gs", d), ("ga", d)):
        segs[name] = (o, o + size)
        o += size
    scale = ATTN_DK ** -0.5 * math.log2(math.e)
    w_plain, w_rope, w_gate, w_dt = _regroup(jnp.swapaxes(p["w_in"][l], 0, 1), segs, scale)

    h = _prenorm(x, p["g_pre_mix"][l][None, :], mod3, 1, 0, _tile(seq, 512)).reshape(t, d)
    tm = _tile(seq, 1024)
    plain = _mm(h, w_plain, BF16, tm, _tile(w_plain.shape[1], 1024), name="proj_plain")
    qk = _mm(h, w_rope, BF16, tm, _tile(w_rope.shape[1], 1024), "rope", _rope_tables(seq), name="proj_rope")
    gates = _mm(h, w_gate, BF16, tm, _tile(w_gate.shape[1], 1024), "sigmoid", name="proj_gate")
    dt_raw = _mm(h, w_dt, F32, tm, LANES, name="proj_dt")

    pad_h = lambda v: jnp.pad(v, (0, LANES - n_sheads))[None, :]
    plain3 = plain.reshape(bsz, seq, plain.shape[1])
    y_ssd = _ssd(plain3, dt_raw.reshape(bsz, seq, LANES), p["conv_w"][l], p["conv_b"][l][None, :],
                 pad_h(p["dt_bias"][l]), pad_h(p["a_log"][l]),
                 jnp.repeat(p["d_skip"][l], SSD_HEADDIM)[None, :], p["ssd_norm_w"][l][None, :], dh, n_sheads)

    lam_init = 0.8 - 0.6 * math.exp(-0.3 * l)
    lam_rows = jnp.zeros((8, LANES), F32)
    for r, nm in enumerate(("lambda_q1", "lambda_k1", "lambda_q2", "lambda_k2")):
        lam_rows = lam_rows.at[r, :ATTN_DK].set(p[nm][l])
    v_blk0 = (dh + dh + 2 * gn) // LANES
    o_attn = _attention(qk.reshape(bsz, seq, 2 * qkw), plain3, lam_rows, p["subln_w"][l][None, :],
                        n_aheads, v_blk0, _tile(seq, ATTN_Q_TILE), lam_init)

    tm2 = _tile(seq, 512)
    merged = _merge(y_ssd.reshape(t, dh), o_attn.reshape(t, aw), p["w_br_ssd"][l].astype(BF16),
                    p["w_br_attn"][l].astype(BF16), gates, tm2, _tile(d, 1024))
    wr = jnp.pad(p["w_router"][l], ((0, 0), (0, LANES - n_experts)))
    wr_hi = wr.astype(BF16)
    wr_lo = (wr - wr_hi.astype(F32)).astype(BF16)
    br = jnp.pad(p["b_router"][l], (0, LANES - n_experts))[None, :]
    x1, h_rows, logits = _outproj(merged, p["w_out"][l].astype(BF16), x.reshape(t, d),
                                    p["g_post_mix"][l][None, :], mod3, p["g_pre_ffn"][l][None, :],
                                    wr_hi, wr_lo, br, seq, tm2, 2, 4, 3)

    dest, wts, cnt = _route(logits, n_experts, moe_blk, _tile(t, 512))
    counts = cnt[0, :n_experts].astype(jnp.int32)
    n_slots = t * TOP_K + n_experts * moe_blk
    nb = n_slots // moe_blk
    pblocks = (counts + moe_blk - 1) // moe_blk
    pend = jnp.cumsum(pblocks)
    nused = jnp.maximum(pend[-1], 1).astype(jnp.int32)
    bidx = jnp.arange(nb, dtype=jnp.int32)
    block_e = jnp.minimum(jnp.sum((pend[None, :] <= bidx[:, None]).astype(jnp.int32), axis=1), n_experts - 1)
    pstart = pend - pblocks
    nvalid = jnp.clip(counts[block_e] - (bidx - pstart[block_e]) * moe_blk, 0, moe_blk).astype(jnp.int32)
    dest_flat = dest[:, :TOP_K].reshape(-1)
    nused1 = nused.reshape(1)
    after = pend[block_e].astype(jnp.int32)
    nxt = jnp.where(after < nused, after, -1).astype(jnp.int32)

    xs = _dispatch(dest_flat, nvalid, h_rows, n_slots, _tile(t, 256), moe_blk, d // LANES)
    dff = p["w_down"].shape[2]
    act = _gateup(block_e, nvalid, nused1, nxt, xs, p["w_gate_up"][l], p["b_gate_up"][l][:, None, :],
                  moe_blk, _tile(dff, 1024))
    y_sorted = _down(block_e, nvalid, nused1, nxt, act, p["w_down"][l], p["b_down"][l][:, None, :], moe_blk)
    out = _combine(dest_flat, y_sorted, wts, x1, mod3, p["g_post_ffn"][l][None, :], seq, _tile(seq, 128), 5)
    return out.reshape(bsz, seq, d)


MOE_ROW_BLOCK = 512
MM_ROW_CHUNKS = 4
CAST_ROWS = 128
MOE_SUB_BLOCKS = 2


def kernel(x, c, w_ada, b_ada, g_pre_mix, g_post_mix, g_pre_ffn, g_post_ffn, w_in, conv_w, conv_b, dt_bias, a_log, d_skip, ssd_norm_w, lambda_q1, lambda_k1, lambda_q2, lambda_k2, subln_w, w_br_ssd, w_br_attn, w_out, w_router, b_router, w_gate_up, b_gate_up, w_down, b_down):
    p = dict(w_ada=w_ada, b_ada=b_ada, g_pre_mix=g_pre_mix, g_post_mix=g_post_mix, g_pre_ffn=g_pre_ffn,
             g_post_ffn=g_post_ffn, w_in=w_in, conv_w=conv_w, conv_b=conv_b, dt_bias=dt_bias, a_log=a_log,
             d_skip=d_skip, ssd_norm_w=ssd_norm_w, lambda_q1=lambda_q1, lambda_k1=lambda_k1,
             lambda_q2=lambda_q2, lambda_k2=lambda_k2, subln_w=subln_w, w_br_ssd=w_br_ssd,
             w_br_attn=w_br_attn, w_out=w_out, w_router=w_router, b_router=b_router, w_gate_up=w_gate_up,
             b_gate_up=b_gate_up, w_down=w_down, b_down=b_down)
    bsz = x.shape[0]
    c_pad = jnp.pad(c, ((0, (-bsz) % 8), (0, 0)))
    for l in range(w_ada.shape[0]):
        x = _layer(x, c_pad, l, p, min(MOE_ROW_BLOCK, x.shape[0] * x.shape[1]))
    return x
```

```python
import functools
import math

import jax
import jax.numpy as jnp
from jax import lax
from jax.experimental import pallas as pl
from jax.experimental.pallas import tpu as pltpu

F32 = jnp.float32
BF16 = jnp.bfloat16
HIGHEST = lax.Precision.HIGHEST

SSD_HEADDIM = 64
SSD_GROUPS = 4
SSD_STATE = 128
SSD_CONV = 4
SSD_CHUNK = 128
ATTN_DK = 64
ATTN_DV = 128
ATTN_KV_UNIT = 512
ATTN_Q_TILE = 512
ATTN_HEADS_PER_STEP = 2
ROT_DIM = ATTN_DK // 4
ROPE_THETA = 500000.0
TOP_K = 4
SWIGLU_LIMIT = 7.0
SWIGLU_ALPHA = 1.702
NORM_EPS = 1e-6
SUB_EPS = 1e-5
LANES = 128
NEG_BIG = -1e30

VMEM_LIMIT = 56 * 1024 * 1024


def _cparams(sem, vmem=VMEM_LIMIT):
    return pltpu.CompilerParams(dimension_semantics=sem, vmem_limit_bytes=vmem)


def _sigmoid(x):
    return 1.0 / (1.0 + jnp.exp(-x))


def _store_token_rows(ref, first_tok, val):
    n_tok, width = val.shape
    ns = width // LANES
    for s_ in range(ns):
        ref[pl.ds(first_tok * ns + s_, n_tok, stride=ns), :] = val[:, s_ * LANES:(s_ + 1) * LANES]


def _load_token_rows(ref, first_tok, n_tok, ns):
    return jnp.concatenate([ref[pl.ds(first_tok * ns + s_, n_tok, stride=ns), :] for s_ in range(ns)], axis=1)


def _ada_kernel(c_ref, w_ref, b_ref, o_ref):
    c = c_ref[...]
    sc = c * _sigmoid(c)
    o_ref[...] = jnp.dot(sc, w_ref[...], preferred_element_type=F32, precision=HIGHEST) + b_ref[...]


def _ada(c_pad, w, b, tn):
    rows, d = c_pad.shape
    n = w.shape[1]
    return pl.pallas_call(
        _ada_kernel,
        grid=(n // tn,),
        in_specs=[pl.BlockSpec((rows, d), lambda j: (0, 0)),
                  pl.BlockSpec((d, tn), lambda j: (0, j)),
                  pl.BlockSpec((1, tn), lambda j: (0, j))],
        out_specs=pl.BlockSpec((rows, tn), lambda j: (0, j)),
        out_shape=jax.ShapeDtypeStruct((rows, n), F32),
        compiler_params=_cparams(("arbitrary",)),
        name="ada",
    )(c_pad, w, b)


def _prenorm_kernel(x_ref, g_ref, sc_ref, sh_ref, o_ref):
    x = x_ref[0]
    y = x * lax.rsqrt(jnp.mean(x * x, axis=-1, keepdims=True) + NORM_EPS) * g_ref[...]
    o_ref[0] = (y * (1.0 + sc_ref[0]) + sh_ref[0]).astype(o_ref.dtype)


def _prenorm(x, g, mod3, sc_blk, sh_blk, ts):
    b, s, d = x.shape
    return pl.pallas_call(
        _prenorm_kernel,
        grid=(b, s // ts),
        in_specs=[pl.BlockSpec((1, ts, d), lambda bi, si: (bi, si, 0)),
                  pl.BlockSpec((1, d), lambda bi, si: (0, 0)),
                  pl.BlockSpec((1, 1, d), lambda bi, si: (bi, 0, sc_blk)),
                  pl.BlockSpec((1, 1, d), lambda bi, si: (bi, 0, sh_blk))],
        out_specs=pl.BlockSpec((1, ts, d), lambda bi, si: (bi, si, 0)),
        out_shape=jax.ShapeDtypeStruct((b, s, d), BF16),
        compiler_params=_cparams(("arbitrary", "arbitrary")),
        name="prenorm",
    )(x, g, mod3, mod3)


REGROUP_TILE = 512


def _regroup_kernel(src_sm, wt_hbm, plain_ref, rope_ref, gate_ref, dt_ref, buf, dt_buf, sem, dt_sem, *,
                    tiles, dt_rows, q_scale):
    t = pl.program_id(0)
    n = pl.num_programs(0)
    tr = REGROUP_TILE
    n_plain, n_rope, n_gate, n_q = tiles

    def load(step):
        row = pl.multiple_of(src_sm[step], 8)
        return pltpu.make_async_copy(wt_hbm.at[pl.ds(row, tr), :], buf.at[step % 2], sem.at[step % 2])

    dt_copy = pltpu.make_async_copy(wt_hbm.at[pl.ds(dt_rows[0], dt_rows[1]), :], dt_buf, dt_sem)

    @pl.when(t == 0)
    def _():
        load(0).start()
        dt_copy.start()

    pl.when(t + 1 < n)(lambda: load(t + 1).start())
    load(t).wait()
    cols = buf[t % 2].T

    @pl.when(t < n_plain)
    def _():
        plain_ref[...] = cols.astype(BF16)

    @pl.when((t >= n_plain) & (t < n_plain + n_rope))
    def _():
        rope_ref[...] = jnp.where(t < n_plain + n_q, cols * q_scale, cols).astype(BF16)

    @pl.when(t >= n_plain + n_rope)
    def _():
        gate_ref[...] = cols.astype(BF16)

    @pl.when(t == n - 1)
    def _():
        dt_copy.wait()
        dt_ref[...] = jnp.zeros_like(dt_ref)
        dt_ref[:, :dt_rows[1]] = dt_buf[...].T.astype(BF16)


def _regroup(w_in_t, segs, q_scale):
    n_in, d = w_in_t.shape
    tr = REGROUP_TILE
    starts = []
    counts = []
    for group in (("z", "xbc", "v"), ("q", "k"), ("gs", "ga")):
        c = 0
        for nm in group:
            lo, hi = segs[nm]
            assert (hi - lo) % tr == 0 and lo % 8 == 0
            starts += list(range(lo, hi, tr))
            c += (hi - lo) // tr
        counts.append(c)
    n_plain, n_rope, n_gate = counts
    n_q = (segs["q"][1] - segs["q"][0]) // tr
    dt_lo, dt_hi = segs["dt"]
    clamp = lambda t, first, cnt: jnp.clip(t - first, 0, cnt - 1)
    return pl.pallas_call(
        functools.partial(_regroup_kernel, tiles=(n_plain, n_rope, n_gate, n_q), dt_rows=(dt_lo, dt_hi - dt_lo),
                          q_scale=q_scale),
        grid_spec=pltpu.PrefetchScalarGridSpec(
            num_scalar_prefetch=1,
            grid=(len(starts),),
            in_specs=[pl.BlockSpec(memory_space=pl.ANY)],
            out_specs=[pl.BlockSpec((d, tr), lambda t, src: (0, clamp(t, 0, n_plain))),
                       pl.BlockSpec((d, tr), lambda t, src: (0, clamp(t, n_plain, n_rope))),
                       pl.BlockSpec((d, tr), lambda t, src: (0, clamp(t, n_plain + n_rope, n_gate))),
                       pl.BlockSpec((d, LANES), lambda t, src: (0, 0))],
            scratch_shapes=[pltpu.VMEM((2, tr, d), F32), pltpu.VMEM((dt_hi - dt_lo, d), F32),
                            pltpu.SemaphoreType.DMA((2,)), pltpu.SemaphoreType.DMA(())],
        ),
        out_shape=[jax.ShapeDtypeStruct((d, n_plain * tr), BF16), jax.ShapeDtypeStruct((d, n_rope * tr), BF16),
                   jax.ShapeDtypeStruct((d, n_gate * tr), BF16), jax.ShapeDtypeStruct((d, LANES), BF16)],
        compiler_params=_cparams(("arbitrary",)),
        name="regroup_w_in",
    )(jnp.asarray(starts, jnp.int32), w_in_t)


def _mm_kernel(a_ref, w_ref, o_ref):
    o_ref[...] = jnp.dot(a_ref[...], w_ref[...], preferred_element_type=F32).astype(o_ref.dtype)


def _row_chunks(n_rows):
    step = n_rows // MM_ROW_CHUNKS if n_rows % MM_ROW_CHUNKS == 0 and n_rows >= 64 * MM_ROW_CHUNKS else n_rows
    return [slice(r, r + step) for r in range(0, n_rows, step)]


def _mm_sigmoid_kernel(a_ref, w_ref, o_ref):
    for rows in _row_chunks(a_ref.shape[0]):
        acc = jnp.dot(a_ref[rows, :], w_ref[...], preferred_element_type=F32)
        o_ref[rows, :] = _sigmoid(acc).astype(o_ref.dtype)


def _mm_rope_kernel(a_ref, w_ref, c_ref, s1_ref, s2_ref, o_ref):
    for rows in _row_chunks(a_ref.shape[0]):
        acc = jnp.dot(a_ref[rows, :], w_ref[...], preferred_element_type=F32)
        c = c_ref[rows, :]
        s1 = s1_ref[rows, :]
        s2 = s2_ref[rows, :]
        for g in range(acc.shape[1] // LANES):
            blk = acc[:, g * LANES:(g + 1) * LANES]
            fwd = pltpu.roll(blk, LANES - ROT_DIM // 2, 1)
            bwd = pltpu.roll(blk, ROT_DIM // 2, 1)
            o_ref[rows, g * LANES:(g + 1) * LANES] = (blk * c + fwd * s1 + bwd * s2).astype(o_ref.dtype)


def _mm(a, w, out_dtype, tm, tn, epilogue="none", rope_tabs=None, name="mm"):
    m, k = a.shape
    n = w.shape[1]
    in_specs = [pl.BlockSpec((tm, k), lambda i, j: (i, 0)),
                pl.BlockSpec((k, tn), lambda i, j: (0, j))]
    args = [a, w]
    if epilogue == "rope":
        seq_blocks = rope_tabs[0].shape[0] // tm
        for t in rope_tabs:
            in_specs.append(pl.BlockSpec((tm, LANES), lambda i, j: (i % seq_blocks, 0)))
            args.append(t)
        kern = _mm_rope_kernel
    elif epilogue == "sigmoid":
        kern = _mm_sigmoid_kernel
    else:
        kern = _mm_kernel
    return pl.pallas_call(
        kern,
        grid=(m // tm, n // tn),
        in_specs=in_specs,
        out_specs=pl.BlockSpec((tm, tn), lambda i, j: (i, j)),
        out_shape=jax.ShapeDtypeStruct((m, n), out_dtype),
        compiler_params=_cparams(("arbitrary", "arbitrary")),
        name=name,
    )(*args)


def _ssd_kernel(z_ref, xs_ref, bm_ref, cm_ref, dt_ref, cwx_ref, cwb_ref, cwc_ref, cbx_ref, cbb_ref,
                cbc_ref, dtb_ref, alog_ref, dsk_ref, nw_ref, o_ref,
                px_sc, pb_sc, pc_sc, st_sc, y_sc, *, n_heads):
    c = pl.program_id(1)
    L = SSD_CHUNK
    N = SSD_STATE
    hpg = n_heads // SSD_GROUPS
    gw = hpg * SSD_HEADDIM

    @pl.when(c == 0)
    def _():
        px_sc[...] = jnp.zeros_like(px_sc)
        pb_sc[...] = jnp.zeros_like(pb_sc)
        pc_sc[...] = jnp.zeros_like(pc_sc)
        st_sc[...] = jnp.zeros_like(st_sc)

    def conv_silu(u, prev_sc, w_ref, b_ref):
        prev = prev_sc[...]
        row = lax.broadcasted_iota(jnp.int32, u.shape, 0)
        acc = u * w_ref[SSD_CONV - 1:SSD_CONV, :] + b_ref[...]
        for j in range(1, SSD_CONV):
            sh = jnp.where(row < j, pltpu.roll(prev, j, 0), pltpu.roll(u, j, 0))
            acc = acc + sh * w_ref[SSD_CONV - 1 - j:SSD_CONV - j, :]
        prev_sc[...] = u
        return acc * _sigmoid(acc)

    xs = conv_silu(xs_ref[0].astype(F32), px_sc, cwx_ref, cbx_ref)
    bm = conv_silu(bm_ref[0].astype(F32), pb_sc, cwb_ref, cbb_ref)
    cm = conv_silu(cm_ref[0].astype(F32), pc_sc, cwc_ref, cbc_ref)

    raw = dt_ref[0] + dtb_ref[...]
    dt = jnp.maximum(raw, 0.0) + jnp.log1p(jnp.exp(-jnp.abs(raw)))
    a = dt * (-jnp.exp(alog_ref[...]))
    r_i = lax.broadcasted_iota(jnp.int32, (L, L), 0)
    c_i = lax.broadcasted_iota(jnp.int32, (L, L), 1)
    causal = r_i >= c_i
    tri = jnp.where(causal, 1.0, 0.0).astype(F32)
    a_cs = jnp.dot(tri, a, preferred_element_type=F32, precision=HIGHEST)
    a_cs_t = a_cs.T

    lane = lax.broadcasted_iota(jnp.int32, (L, LANES), 1)
    first = lane < SSD_HEADDIM
    acs_tiles = []
    dt_tiles = []
    for j in range(n_heads // 2):
        h0, h1 = 2 * j, 2 * j + 1
        acs_tiles.append(jnp.where(first, a_cs[:, h0:h0 + 1], a_cs[:, h1:h1 + 1]))
        dt_tiles.append(jnp.where(first, dt[:, h0:h0 + 1], dt[:, h1:h1 + 1]))
    acs_e = jnp.concatenate(acs_tiles, axis=1)
    dt_e = jnp.concatenate(dt_tiles, axis=1)
    xdt = xs * dt_e
    ea = jnp.exp(acs_e)
    alast = acs_e[L - 1:L, :]
    xdec = (xdt * jnp.exp(alast - acs_e)).astype(BF16)
    ealast = jnp.exp(alast)

    for g in range(SSD_GROUPS):
        bg = bm[:, g * N:(g + 1) * N]
        cg = cm[:, g * N:(g + 1) * N].astype(BF16)
        scores = lax.dot_general(cg, bg.astype(BF16), (((1,), (1,)), ((), ())),
                                 preferred_element_type=F32)
        st_old = st_sc[:, g * gw:(g + 1) * gw]
        y_off = jnp.dot(cg, st_old.astype(BF16), preferred_element_type=F32) * ea[:, g * gw:(g + 1) * gw]
        st_sc[:, g * gw:(g + 1) * gw] = st_old * ealast[:, g * gw:(g + 1) * gw] + jnp.dot(
            bg.T.astype(BF16), xdec[:, g * gw:(g + 1) * gw], preferred_element_type=F32)
        for jj in range(hpg // 2):
            j = g * (hpg // 2) + jj
            h0, h1 = 2 * j, 2 * j + 1
            la = jnp.exp(jnp.where(causal, a_cs[:, h0:h0 + 1] - a_cs_t[h0:h0 + 1, :], NEG_BIG))
            lb = jnp.exp(jnp.where(causal, a_cs[:, h1:h1 + 1] - a_cs_t[h1:h1 + 1, :], NEG_BIG))
            mcat = jnp.concatenate([(scores * la).astype(BF16), (scores * lb).astype(BF16)], axis=1)
            xp = xdt[:, j * LANES:(j + 1) * LANES]
            xcat = jnp.concatenate([jnp.where(first, xp, 0.0).astype(BF16),
                                    jnp.where(first, 0.0, xp).astype(BF16)], axis=0)
            y_diag = jnp.dot(mcat, xcat, preferred_element_type=F32)
            lo = jj * LANES
            y_sc[:, j * LANES:(j + 1) * LANES] = (
                y_diag + y_off[:, lo:lo + LANES]
                + dsk_ref[:, j * LANES:(j + 1) * LANES] * xs[:, j * LANES:(j + 1) * LANES])

    z = z_ref[0].astype(F32)
    u = y_sc[...] * (z * _sigmoid(z))
    for g in range(SSD_GROUPS):
        ug = u[:, g * gw:(g + 1) * gw]
        ms = jnp.mean(ug * ug, axis=-1, keepdims=True)
        o_ref[0, :, g * gw:(g + 1) * gw] = (ug * lax.rsqrt(ms + SUB_EPS)
                                            * nw_ref[:, g * gw:(g + 1) * gw]).astype(o_ref.dtype)


def _ssd(plain3, dt3, conv_w, conv_b, dtb, alog, dsk_e, norm_w, dh, n_heads):
    b, s, _ = plain3.shape
    L = SSD_CHUNK
    gn = SSD_GROUPS * SSD_STATE
    nc = s // L
    xblk = 1
    bblk = (2 * dh) // gn
    cw_x, cw_b, cw_c = conv_w[:, :dh], conv_w[:, dh:dh + gn], conv_w[:, dh + gn:]
    cb_x, cb_b, cb_c = conv_b[:, :dh], conv_b[:, dh:dh + gn], conv_b[:, dh + gn:]
    full = lambda shape: pl.BlockSpec(shape, lambda bi, ci: (0, 0))
    return pl.pallas_call(
        functools.partial(_ssd_kernel, n_heads=n_heads),
        grid=(b, nc),
        in_specs=[pl.BlockSpec((1, L, dh), lambda bi, ci: (bi, ci, 0)),
                  pl.BlockSpec((1, L, dh), lambda bi, ci: (bi, ci, xblk)),
                  pl.BlockSpec((1, L, gn), lambda bi, ci: (bi, ci, bblk)),
                  pl.BlockSpec((1, L, gn), lambda bi, ci: (bi, ci, bblk + 1)),
                  pl.BlockSpec((1, L, LANES), lambda bi, ci: (bi, ci, 0)),
                  full((SSD_CONV, dh)), full((SSD_CONV, gn)), full((SSD_CONV, gn)),
                  full((1, dh)), full((1, gn)), full((1, gn)),
                  full((1, LANES)), full((1, LANES)), full((1, dh)), full((1, dh))],
        out_specs=pl.BlockSpec((1, L, dh), lambda bi, ci: (bi, ci, 0)),
        out_shape=jax.ShapeDtypeStruct((b, s, dh), BF16),
        scratch_shapes=[pltpu.VMEM((L, dh), F32), pltpu.VMEM((L, gn), F32), pltpu.VMEM((L, gn), F32),
                        pltpu.VMEM((SSD_STATE, dh), F32), pltpu.VMEM((L, dh), F32)],
        compiler_params=_cparams(("arbitrary", "arbitrary")),
        name="ssd",
    )(plain3, plain3, plain3, plain3, dt3, cw_x, cw_b, cw_c, cb_x, cb_b, cb_c, dtb, alog, dsk_e, norm_w)


def _attn_kernel(q_ref, k_ref, v_ref, bias_ref, lam_ref, sw_ref, o_ref, vt_sc, st_a, st_b, m_sc, l_sc, acc_sc, *,
                 tq, lam_init):
    qi = pl.program_id(2)
    tu = ATTN_KV_UNIT
    nh = ATTN_HEADS_PER_STEP
    n_all = k_ref.shape[1] // tu
    heads = [slice(hh * LANES, (hh + 1) * LANES) for hh in range(nh)]

    @pl.when(qi == 0)
    def _():
        def transpose_block(c, carry):
            start = pl.multiple_of(c * tu, tu)
            for hh in range(nh):
                vt_sc[hh, c] = v_ref[0, pl.ds(start, tu), heads[hh]].astype(F32).T.astype(BF16)
            return carry

        lax.fori_loop(0, n_all, transpose_block, 0)

    qts = []
    for hh in range(nh):
        qt = q_ref[0, :, heads[hh]].astype(F32).T
        row = lax.broadcasted_iota(jnp.int32, qt.shape, 0)
        qts.append((jnp.where(row < ATTN_DK, qt, 0.0).astype(BF16),
                    jnp.where(row < ATTN_DK, 0.0, qt).astype(BF16)))
    m_sc[...] = jnp.full_like(m_sc, NEG_BIG)
    l_sc[...] = jnp.zeros_like(l_sc)
    acc_sc[...] = jnp.zeros_like(acc_sc)

    def scores(u, st_ref):
        start = pl.multiple_of(u * tu, tu)
        for hh in range(nh):
            k = k_ref[0, pl.ds(start, tu), heads[hh]]
            for m in range(2):
                st_ref[2 * hh + m] = jnp.dot(k, qts[hh][m], preferred_element_type=F32)

    def update(u, st_ref, masked):
        for hh in range(nh):
            vt = vt_sc[hh, u]
            for m in range(2):
                c = 2 * hh + m
                st = st_ref[c]
                if masked:
                    st = st + bias_ref[...]
                m_prev = m_sc[c]
                m_new = jnp.maximum(m_prev, jnp.max(st, axis=0, keepdims=True))
                alpha = jnp.exp2(m_prev - m_new)
                pt = jnp.exp2(st - m_new)
                l_sc[c] = alpha * l_sc[c] + jnp.sum(pt, axis=0, keepdims=True)
                acc_sc[c] = alpha * acc_sc[c] + jnp.dot(vt, pt.astype(BF16), preferred_element_type=F32)
                m_sc[c] = m_new

    n_units = (qi * tq) // tu + 1
    n_loop = (n_units - 1) // 2
    scores(0, st_a)

    def two_units(j, carry):
        u = 2 * j
        scores(u + 1, st_b)
        update(u, st_a, False)
        scores(u + 2, st_a)
        update(u + 1, st_b, False)
        return carry

    lax.fori_loop(0, n_loop, two_units, 0)
    last = n_units - 1

    @pl.when(last == 2 * n_loop)
    def _():
        update(last, st_a, True)

    @pl.when(last != 2 * n_loop)
    def _():
        scores(last, st_b)
        update(last - 1, st_a, False)
        update(last, st_b, True)

    lv = lam_ref[...]
    lam = (jnp.exp(jnp.sum(lv[0:1] * lv[1:2], axis=-1, keepdims=True))
           - jnp.exp(jnp.sum(lv[2:3] * lv[3:4], axis=-1, keepdims=True)) + lam_init)
    for hh in range(nh):
        c0, c1 = 2 * hh, 2 * hh + 1
        ot = acc_sc[c0] * (1.0 / l_sc[c0]) - lam * (acc_sc[c1] * (1.0 / l_sc[c1]))
        ot = ot * lax.rsqrt(jnp.mean(ot * ot, axis=0, keepdims=True) + SUB_EPS)
        o_ref[0, :, heads[hh]] = (ot.T * sw_ref[...] * (1.0 - lam_init)).astype(o_ref.dtype)


def _attention(qk3, plain3, lam_rows, subln_w, n_heads, v_blk0, tq, lam_init):
    b, s, _ = qk3.shape
    tu = ATTN_KV_UNIT
    nh = ATTN_HEADS_PER_STEP
    hw = nh * LANES
    assert tq == tu, "the diagonal unit must coincide with the query tile"
    kpos = lax.broadcasted_iota(jnp.int32, (tu, tq), 0)
    qpos = lax.broadcasted_iota(jnp.int32, (tu, tq), 1)
    diag_bias = jnp.where(kpos <= qpos, 0.0, NEG_BIG).astype(F32)
    return pl.pallas_call(
        functools.partial(_attn_kernel, tq=tq, lam_init=lam_init),
        grid=(b, n_heads // nh, s // tq),
        in_specs=[pl.BlockSpec((1, tq, hw), lambda bi, hi, qi: (bi, qi, hi)),
                  pl.BlockSpec((1, s, hw), lambda bi, hi, qi: (bi, 0, n_heads // nh + hi)),
                  pl.BlockSpec((1, s, hw), lambda bi, hi, qi: (bi, 0, v_blk0 // nh + hi)),
                  pl.BlockSpec((tu, tq), lambda bi, hi, qi: (0, 0)),
                  pl.BlockSpec((8, LANES), lambda bi, hi, qi: (0, 0)),
                  pl.BlockSpec((1, LANES), lambda bi, hi, qi: (0, 0))],
        out_specs=pl.BlockSpec((1, tq, hw), lambda bi, hi, qi: (bi, qi, hi)),
        out_shape=jax.ShapeDtypeStruct((b, s, n_heads * ATTN_DV), BF16),
        scratch_shapes=[pltpu.VMEM((nh, s // tu, ATTN_DV, tu), BF16),
                        pltpu.VMEM((2 * nh, tu, tq), F32), pltpu.VMEM((2 * nh, tu, tq), F32),
                        pltpu.VMEM((2 * nh, 1, tq), F32), pltpu.VMEM((2 * nh, 1, tq), F32),
                        pltpu.VMEM((2 * nh, ATTN_DV, tq), F32)],
        compiler_params=_cparams(("arbitrary", "arbitrary", "arbitrary")),
        name="diff_attn",
    )(qk3, qk3, plain3, diag_bias, lam_rows, subln_w)


def _merge_kernel(y_ref, o_ref, ws_ref, wa_ref, gs_ref, ga_ref, out_ref):
    for rows in _row_chunks(y_ref.shape[0]):
        bs = jnp.dot(y_ref[rows, :], ws_ref[...], preferred_element_type=F32)
        ba = jnp.dot(o_ref[rows, :], wa_ref[...], preferred_element_type=F32)
        out_ref[rows, :] = (gs_ref[rows, :].astype(F32) * bs
                            + ga_ref[rows, :].astype(F32) * ba).astype(out_ref.dtype)


def _merge(y, o, ws, wa, gates, tm, tn):
    m, k = y.shape
    n = ws.shape[1]
    nj = n // tn
    return pl.pallas_call(
        _merge_kernel,
        grid=(m // tm, nj),
        in_specs=[pl.BlockSpec((tm, k), lambda i, j: (i, 0)),
                  pl.BlockSpec((tm, o.shape[1]), lambda i, j: (i, 0)),
                  pl.BlockSpec((k, tn), lambda i, j: (0, j)),
                  pl.BlockSpec((o.shape[1], tn), lambda i, j: (0, j)),
                  pl.BlockSpec((tm, tn), lambda i, j: (i, j)),
                  pl.BlockSpec((tm, tn), lambda i, j: (i, nj + j))],
        out_specs=pl.BlockSpec((tm, tn), lambda i, j: (i, j)),
        out_shape=jax.ShapeDtypeStruct((m, n), BF16),
        compiler_params=_cparams(("arbitrary", "arbitrary")),
        name="merge",
    )(y, o, ws, wa, gates, gates)


def _outproj_kernel(mg_ref, wo_ref, x_ref, gpost_ref, gt_ref, gpre_ref, sc_ref, sh_ref, wr_hi_ref,
                    wr_lo_ref, br_ref, x1_ref, hp_ref, lg_ref):
    mix = jnp.dot(mg_ref[...], wo_ref[...], preferred_element_type=F32)
    nm = mix * lax.rsqrt(jnp.mean(mix * mix, axis=-1, keepdims=True) + NORM_EPS) * gpost_ref[...]
    x1 = x_ref[...] + gt_ref[0] * nm
    x1_ref[...] = x1
    h2 = (x1 * lax.rsqrt(jnp.mean(x1 * x1, axis=-1, keepdims=True) + NORM_EPS) * gpre_ref[...]
          * (1.0 + sc_ref[0]) + sh_ref[0])
    h_hi = h2.astype(BF16)
    h_lo = (h2 - h_hi.astype(F32)).astype(BF16)
    wr_hi = wr_hi_ref[...]
    lg_ref[...] = (jnp.dot(h_hi, wr_hi, preferred_element_type=F32)
                   + jnp.dot(h_lo, wr_hi, preferred_element_type=F32)
                   + jnp.dot(h_hi, wr_lo_ref[...], preferred_element_type=F32) + br_ref[...])
    _store_token_rows(hp_ref, 0, h2)


def _outproj(merged, wo, x2, gpost, mod3, gpre, wr_hi, wr_lo, br, seq, tm, gt_blk, sc_blk, sh_blk):
    m, d = x2.shape
    per_b = seq // tm
    row = lambda i: (i, 0)
    const = lambda i: (0, 0)
    return pl.pallas_call(
        _outproj_kernel,
        grid=(m // tm,),
        in_specs=[pl.BlockSpec((tm, d), row),
                  pl.BlockSpec((d, d), const),
                  pl.BlockSpec((tm, d), row),
                  pl.BlockSpec((1, d), const),
                  pl.BlockSpec((1, 1, d), lambda i: (i // per_b, 0, gt_blk)),
                  pl.BlockSpec((1, d), const),
                  pl.BlockSpec((1, 1, d), lambda i: (i // per_b, 0, sc_blk)),
                  pl.BlockSpec((1, 1, d), lambda i: (i // per_b, 0, sh_blk)),
                  pl.BlockSpec((d, LANES), const),
                  pl.BlockSpec((d, LANES), const),
                  pl.BlockSpec((1, LANES), const)],
        out_specs=[pl.BlockSpec((tm, d), row),
                   pl.BlockSpec((tm * (d // LANES), LANES), row),
                   pl.BlockSpec((tm, LANES), row)],
        out_shape=[jax.ShapeDtypeStruct((m, d), F32),
                   jax.ShapeDtypeStruct((m * (d // LANES), LANES), F32),
                   jax.ShapeDtypeStruct((m, LANES), F32)],
        compiler_params=_cparams(("arbitrary",)),
        name="outproj",
    )(merged, wo, x2, gpost, mod3, gpre, mod3, mod3, wr_hi, wr_lo, br)


def _route_kernel(lg_ref, dest_ref, w_ref, cnt_ref, cnt_sc, pst_sc, run_sc, *, n_experts, blk):
    ph = pl.program_id(0)
    t = pl.program_id(1)
    tk = lg_ref.shape[0]
    lane = lax.broadcasted_iota(jnp.int32, (tk, LANES), 1)
    lg = jnp.where(lane < n_experts, lg_ref[...], -jnp.inf)
    vals = []
    hots = []
    for _ in range(TOP_K):
        mx = jnp.max(lg, axis=-1, keepdims=True)
        ix = jnp.min(jnp.where(lg == mx, lane, LANES), axis=-1, keepdims=True)
        hot = lane == ix
        lg = jnp.where(hot, -jnp.inf, lg)
        vals.append(mx)
        hots.append(hot)
    multi = jnp.zeros((tk, LANES), F32)
    for hot in hots:
        multi = multi + jnp.where(hot, 1.0, 0.0)
    colsum = jnp.sum(multi, axis=0, keepdims=True)

    @pl.when((ph == 0) & (t == 0))
    def _():
        cnt_sc[...] = jnp.zeros_like(cnt_sc)

    @pl.when(ph == 0)
    def _():
        cnt_sc[...] += colsum

    @pl.when((ph == 1) & (t == 0))
    def _():
        cnt = cnt_sc[...].astype(jnp.int32)
        padded = (((cnt + (blk - 1)) // blk) * blk).astype(F32)
        r_i = lax.broadcasted_iota(jnp.int32, (LANES, LANES), 0)
        c_i = lax.broadcasted_iota(jnp.int32, (LANES, LANES), 1)
        upper = jnp.where(r_i < c_i, 1.0, 0.0).astype(F32)
        pst_sc[...] = jnp.dot(jnp.broadcast_to(padded, (8, LANES)), upper,
                              preferred_element_type=F32, precision=HIGHEST)[0:1]
        run_sc[...] = jnp.zeros_like(run_sc)

    @pl.when(ph == 1)
    def _():
        r_i = lax.broadcasted_iota(jnp.int32, (tk, tk), 0)
        c_i = lax.broadcasted_iota(jnp.int32, (tk, tk), 1)
        strict = jnp.where(r_i > c_i, 1.0, 0.0).astype(BF16)
        before = jnp.dot(strict, multi.astype(BF16), preferred_element_type=F32)
        base = before + run_sc[...] + pst_sc[...]
        esum = jnp.zeros((tk, 1), F32)
        evals = []
        for r in range(TOP_K):
            e = jnp.exp(vals[r] - vals[0])
            evals.append(e)
            esum = esum + e
        dest = jnp.zeros((tk, LANES), jnp.int32)
        wts = jnp.zeros((tk, LANES), F32)
        for r in range(TOP_K):
            d_r = jnp.sum(jnp.where(hots[r], base, 0.0), axis=-1, keepdims=True).astype(jnp.int32)
            dest = jnp.where(lane == r, d_r, dest)
            wts = jnp.where(lane == r, evals[r] / esum, wts)
        dest_ref[...] = dest
        w_ref[...] = wts
        run_sc[...] += colsum
        cnt_ref[...] = jnp.broadcast_to(cnt_sc[...], cnt_ref.shape)


def _route(logits, n_experts, blk, tk):
    t = logits.shape[0]
    return pl.pallas_call(
        functools.partial(_route_kernel, n_experts=n_experts, blk=blk),
        grid=(2, t // tk),
        in_specs=[pl.BlockSpec((tk, LANES), lambda ph, ti: (ti, 0))],
        out_specs=[pl.BlockSpec((tk, LANES), lambda ph, ti: (ti * ph, 0)),
                   pl.BlockSpec((tk, LANES), lambda ph, ti: (ti * ph, 0)),
                   pl.BlockSpec((8, LANES), lambda ph, ti: (0, 0))],
        out_shape=[jax.ShapeDtypeStruct((t, LANES), jnp.int32),
                   jax.ShapeDtypeStruct((t, LANES), F32),
                   jax.ShapeDtypeStruct((8, LANES), F32)],
        scratch_shapes=[pltpu.VMEM((1, LANES), F32), pltpu.VMEM((1, LANES), F32), pltpu.VMEM((1, LANES), F32)],
        compiler_params=_cparams(("arbitrary", "arbitrary")),
        name="route",
    )(logits)


def _pad_fill_copies(b, nv_sm, zero_sc, xs_hbm, sem, blk, ns):
    nv = nv_sm[b]
    out = []
    off = b * blk + nv
    rest = blk - nv
    p = blk
    while p >= 1:
        cond = (rest & p) != 0
        out.append((cond, pltpu.make_async_copy(
            zero_sc.at[pl.ds(0, p * ns), :], xs_hbm.at[pl.ds(pl.multiple_of(off * ns, ns), p * ns), :], sem)))
        off = off + jnp.where(cond, p, 0)
        p //= 2
    return out


def _dispatch_kernel(dest_sm, nv_sm, h_hbm, xs_hbm, zero_sc, hbuf, in_sem, sem, fill_sem, *, tt, blk, nb, ns):
    i = pl.program_id(0)
    n = pl.num_programs(0)
    base = i * tt
    rows = tt * ns

    def fill(b, carry):
        for cond, cp in _pad_fill_copies(b, nv_sm, zero_sc, xs_hbm, fill_sem, blk, ns):
            pl.when(cond)(cp.start)
        return carry

    def fill_wait(b, carry):
        for cond, cp in _pad_fill_copies(b, nv_sm, zero_sc, xs_hbm, fill_sem, blk, ns):
            pl.when(cond)(cp.wait)
        return carry

    def load(step):
        slot = step % 3
        return pltpu.make_async_copy(h_hbm.at[pl.ds(pl.multiple_of(step * rows, rows), rows), :],
                                     hbuf.at[slot], in_sem.at[slot])

    @pl.when(i == 0)
    def _():
        zero_sc[...] = jnp.zeros_like(zero_sc)
        lax.fori_loop(0, nb, fill, 0)
        load(0).start()
        pl.when(n > 1)(lambda: load(1).start())

    def wait_step(step):
        for _ in range(TOP_K):
            pltpu.make_async_copy(hbuf.at[0], xs_hbm.at[pl.ds(0, rows), :], sem.at[step % 2]).wait()

    load(i).wait()
    src_tile = hbuf.at[i % 3]

    def body(t, carry):
        src = src_tile.at[pl.ds(pl.multiple_of(t * ns, ns), ns), :]
        for k in range(TOP_K):
            d = dest_sm[(base + t) * TOP_K + k]
            pltpu.make_async_copy(src, xs_hbm.at[pl.ds(pl.multiple_of(d * ns, ns), ns), :],
                                  sem.at[i % 2]).start()
        return carry

    lax.fori_loop(0, tt, body, 0)
    pl.when(i > 0)(lambda: wait_step(i - 1))
    pl.when(i + 2 < n)(lambda: load(i + 2).start())

    @pl.when(i == n - 1)
    def _():
        wait_step(i)
        lax.fori_loop(0, nb, fill_wait, 0)


def _dispatch(dest_flat, nvalid, h_rows, n_slots, tt, blk, ns):
    nb = n_slots // blk
    t = h_rows.shape[0] // ns
    return pl.pallas_call(
        functools.partial(_dispatch_kernel, tt=tt, blk=blk, nb=nb, ns=ns),
        grid_spec=pltpu.PrefetchScalarGridSpec(
            num_scalar_prefetch=2,
            grid=(t // tt,),
            in_specs=[pl.BlockSpec(memory_space=pl.ANY)],
            out_specs=pl.BlockSpec(memory_space=pl.ANY),
            scratch_shapes=[pltpu.VMEM((blk * ns, LANES), F32), pltpu.VMEM((3, tt * ns, LANES), F32),
                            pltpu.SemaphoreType.DMA((3,)), pltpu.SemaphoreType.DMA((2,)),
                            pltpu.SemaphoreType.DMA(())],
        ),
        out_shape=jax.ShapeDtypeStruct((n_slots * ns, LANES), F32),
        compiler_params=_cparams(("arbitrary",)),
        name="dispatch",
    )(dest_flat, nvalid, h_rows)


def _expert_changed(be, i, last):
    ii = jnp.minimum(i, last)
    prev = jnp.maximum(ii - 1, 0)
    return (i == 0) | (be[ii] != be[prev])


def _for_live_sub_blocks(active, n_valid, sub, o_ref, compute, rows_per_slot=1):
    n_live = jnp.where(active, (n_valid + sub - 1) // sub, 0)
    for count in range(MOE_SUB_BLOCKS + 1):
        @pl.when(n_live == count)
        def _(count=count):
            if count == MOE_SUB_BLOCKS:
                compute(slice(0, count * sub))
            else:
                for r in range(count):
                    compute(slice(r * sub, (r + 1) * sub))
            if count < MOE_SUB_BLOCKS:
                first = count * sub * rows_per_slot
                o_ref[first:, :] = jnp.zeros((o_ref.shape[0] - first, o_ref.shape[1]), o_ref.dtype)


def _stream_expert_weights(changed, first, prefetch, wait_cur, cast, start_next):
    @pl.when(changed)
    def _():
        pl.when(first)(lambda: start_next(True))
        wait_cur()
        cast()
        pl.when(prefetch)(lambda: start_next(False))


def _gateup_kernel(be, nv, nu, nxt, x_ref, w_hbm, bg_ref, bu_ref, o_ref, wbuf, wg_sc, wu_sc, sem):
    j = pl.program_id(0)
    i = pl.program_id(1)
    nj = pl.num_programs(0)
    th = wg_sc.shape[1]
    dff = w_hbm.shape[2] // 2
    last = nu[0] - 1
    active = i < nu[0]
    ii = jnp.minimum(i, last)

    def copies(e, jj):
        col = pl.multiple_of(jj * th, th)
        return (pltpu.make_async_copy(w_hbm.at[e, :, pl.ds(col, th)], wbuf.at[0], sem.at[0]),
                pltpu.make_async_copy(w_hbm.at[e, :, pl.ds(dff + col, th)], wbuf.at[1], sem.at[1]))

    nx = nxt[ii]
    same_pass = nx >= 0
    e_next = jnp.where(same_pass, be[jnp.maximum(nx, 0)], be[0])
    j_next = jnp.where(same_pass, j, j + 1)

    def start_next(current):
        for cp in (copies(be[ii], j) if current else copies(e_next, j_next)):
            cp.start()

    def wait_cur():
        for cp in copies(be[ii], j):
            cp.wait()

    def cast():
        def chunk(r, carry):
            rows = pl.ds(pl.multiple_of(r * CAST_ROWS, CAST_ROWS), CAST_ROWS)
            wg_sc[rows, :] = wbuf[0, rows, :].astype(BF16)
            wu_sc[rows, :] = wbuf[1, rows, :].astype(BF16)
            return carry

        lax.fori_loop(0, wg_sc.shape[0] // CAST_ROWS, chunk, 0)

    _stream_expert_weights(active & _expert_changed(be, i, last), (j == 0) & (i == 0),
                           same_pass | (j + 1 < nj), wait_cur, cast, start_next)

    sub = o_ref.shape[0] // MOE_SUB_BLOCKS

    def compute(rows):
        x = _load_token_rows(x_ref, rows.start, rows.stop - rows.start, wg_sc.shape[0] // LANES).astype(BF16)
        g = jnp.dot(x, wg_sc[...], preferred_element_type=F32) + bg_ref[0]
        up = jnp.dot(x, wu_sc[...], preferred_element_type=F32) + bu_ref[0]
        gate = jnp.minimum(g, SWIGLU_LIMIT)
        up = jnp.clip(up, -SWIGLU_LIMIT, SWIGLU_LIMIT)
        o_ref[rows, :] = ((up + 1.0) * gate * _sigmoid(SWIGLU_ALPHA * gate)).astype(o_ref.dtype)

    _for_live_sub_blocks(active, nv[i], sub, o_ref, compute)


def _gateup(block_e, nvalid, nused, nxt, xs, w_gu, b_gu3, tm, th):
    d = w_gu.shape[1]
    ns = d // LANES
    n_slots = xs.shape[0] // ns
    dff = w_gu.shape[2] // 2
    nj = dff // th
    nb = n_slots // tm

    def blk(i, nu):
        return jnp.minimum(i, nu[0] - 1)

    return pl.pallas_call(
        _gateup_kernel,
        grid_spec=pltpu.PrefetchScalarGridSpec(
            num_scalar_prefetch=4,
            grid=(nj, nb),
            in_specs=[pl.BlockSpec((tm * ns, LANES), lambda j, i, be, nv, nu, nx: (blk(i, nu), 0)),
                      pl.BlockSpec(memory_space=pl.ANY),
                      pl.BlockSpec((1, 1, th), lambda j, i, be, nv, nu, nx: (be[blk(i, nu)], 0, j)),
                      pl.BlockSpec((1, 1, th), lambda j, i, be, nv, nu, nx: (be[blk(i, nu)], 0, nj + j))],
            out_specs=pl.BlockSpec((tm, th), lambda j, i, be, nv, nu, nx: (i, j)),
            scratch_shapes=[pltpu.VMEM((2, d, th), F32), pltpu.VMEM((d, th), BF16), pltpu.VMEM((d, th), BF16),
                            pltpu.SemaphoreType.DMA((2,))],
        ),
        out_shape=jax.ShapeDtypeStruct((n_slots, dff), BF16),
        compiler_params=_cparams(("arbitrary", "arbitrary")),
        name="expert_gate_up",
    )(block_e, nvalid, nused, nxt, xs, w_gu, b_gu3, b_gu3)


def _down_kernel(be, nv, nu, nxt, a_ref, w_hbm, bd_ref, o_ref, wbuf, wd_sc, sem):
    i = pl.program_id(1)
    last = nu[0] - 1
    active = i < nu[0]
    ii = jnp.minimum(i, last)
    nx = nxt[ii]

    def copy(e):
        return pltpu.make_async_copy(w_hbm.at[e], wbuf, sem)

    def start_next(current):
        copy(be[ii] if current else be[jnp.maximum(nx, 0)]).start()

    def cast():
        def chunk(r, carry):
            rows = pl.ds(pl.multiple_of(r * CAST_ROWS, CAST_ROWS), CAST_ROWS)
            wd_sc[rows, :] = wbuf[rows, :].astype(BF16)
            return carry

        lax.fori_loop(0, wd_sc.shape[0] // CAST_ROWS, chunk, 0)

    _stream_expert_weights(active & _expert_changed(be, i, last), i == 0, nx >= 0,
                           lambda: copy(be[ii]).wait(), cast, start_next)

    sub = a_ref.shape[0] // MOE_SUB_BLOCKS
    ns = o_ref.shape[0] // a_ref.shape[0]

    def compute(rows):
        y = jnp.dot(a_ref[rows, :], wd_sc[...], preferred_element_type=F32) + bd_ref[0]
        _store_token_rows(o_ref, rows.start, y)

    _for_live_sub_blocks(active, nv[i], sub, o_ref, compute, rows_per_slot=ns)


def _down(block_e, nvalid, nused, nxt, act, w_d, b_d3, tm):
    n_slots, dff = act.shape
    d = w_d.shape[2]
    ns = d // LANES
    nb = n_slots // tm

    def blk(i, nu):
        return jnp.minimum(i, nu[0] - 1)

    return pl.pallas_call(
        _down_kernel,
        grid_spec=pltpu.PrefetchScalarGridSpec(
            num_scalar_prefetch=4,
            grid=(1, nb),
            in_specs=[pl.BlockSpec((tm, dff), lambda j, i, be, nv, nu, nx: (blk(i, nu), 0)),
                      pl.BlockSpec(memory_space=pl.ANY),
                      pl.BlockSpec((1, 1, d), lambda j, i, be, nv, nu, nx: (be[blk(i, nu)], 0, 0))],
            out_specs=pl.BlockSpec((tm * ns, LANES), lambda j, i, be, nv, nu, nx: (i, 0)),
            scratch_shapes=[pltpu.VMEM((dff, d), F32), pltpu.VMEM((dff, d), BF16), pltpu.SemaphoreType.DMA(())],
        ),
        out_shape=jax.ShapeDtypeStruct((n_slots * ns, LANES), F32),
        compiler_params=_cparams(("arbitrary", "arbitrary")),
        name="expert_down",
    )(block_e, nvalid, nused, nxt, act, w_d, b_d3)


def _combine_kernel(dest_sm, y_hbm, w_ref, x1_ref, gt_ref, g_ref, o_ref, buf, sem, *, tt, ns):
    i = pl.program_id(0)
    n = pl.num_programs(0)

    def gather(tile, slot):
        def body(t, carry):
            for k in range(TOP_K):
                d = dest_sm[(tile * tt + t) * TOP_K + k]
                pltpu.make_async_copy(y_hbm.at[pl.ds(pl.multiple_of(d * ns, ns), ns), :],
                                      buf.at[slot, pl.ds(pl.multiple_of((k * tt + t) * ns, ns), ns), :],
                                      sem.at[slot]).start()
            return carry

        lax.fori_loop(0, tt, body, 0)

    slot = i % 2

    @pl.when(i == 0)
    def _():
        gather(0, 0)

    @pl.when(i + 1 < n)
    def _():
        gather(i + 1, 1 - slot)

    pltpu.make_async_copy(y_hbm.at[pl.ds(0, TOP_K * tt * ns), :], buf.at[slot], sem.at[slot]).wait()
    w = w_ref[...]
    rows = buf.at[slot]
    f = None
    for k in range(TOP_K):
        yk = _load_token_rows(rows, k * tt, tt, ns) * w[:, k:k + 1]
        f = yk if f is None else f + yk
    nf = f * lax.rsqrt(jnp.mean(f * f, axis=-1, keepdims=True) + NORM_EPS) * g_ref[...]
    o_ref[...] = x1_ref[...] + gt_ref[0] * nf


def _combine(dest_flat, y_rows, wts, x1, mod3, gpost, seq, tt, gt_blk):
    t, d = x1.shape
    ns = d // LANES
    per_b = seq // tt
    return pl.pallas_call(
        functools.partial(_combine_kernel, tt=tt, ns=ns),
        grid_spec=pltpu.PrefetchScalarGridSpec(
            num_scalar_prefetch=1,
            grid=(t // tt,),
            in_specs=[pl.BlockSpec(memory_space=pl.ANY),
                      pl.BlockSpec((tt, LANES), lambda i, ds: (i, 0)),
                      pl.BlockSpec((tt, d), lambda i, ds: (i, 0)),
                      pl.BlockSpec((1, 1, d), lambda i, ds: (i // per_b, 0, gt_blk)),
                      pl.BlockSpec((1, d), lambda i, ds: (0, 0))],
            out_specs=pl.BlockSpec((tt, d), lambda i, ds: (i, 0)),
            scratch_shapes=[pltpu.VMEM((2, TOP_K * tt * ns, LANES), F32), pltpu.SemaphoreType.DMA((2,))],
        ),
        out_shape=jax.ShapeDtypeStruct((t, d), F32),
        compiler_params=_cparams(("arbitrary",)),
        name="combine",
    )(dest_flat, y_rows, wts, x1, mod3, gpost)


def _tile(n, pref):
    t = min(n, pref)
    while n % t:
        t //= 2
    return t


def _rope_tables(seq):
    half = ROT_DIM // 2
    inv = ROPE_THETA ** (-jnp.arange(0, ROT_DIM, 2, dtype=F32) / ROT_DIM)
    ang = jnp.arange(seq, dtype=F32)[:, None] * inv[None, :]
    cos, sin = jnp.cos(ang), jnp.sin(ang)
    ones = jnp.ones((seq, ATTN_DK - ROT_DIM), F32)
    zeros = jnp.zeros((seq, ATTN_DK - ROT_DIM), F32)
    zh = jnp.zeros((seq, half), F32)
    c64 = jnp.concatenate([cos, cos, ones], axis=1)
    s1_64 = jnp.concatenate([-sin, zh, zeros], axis=1)
    s2_64 = jnp.concatenate([zh, sin, zeros], axis=1)
    rep = LANES // ATTN_DK
    return tuple(jnp.tile(t, (1, rep)) for t in (c64, s1_64, s2_64))


def _layer(x, c_pad, l, p, moe_blk):
    bsz, seq, d = x.shape
    t = bsz * seq
    dh = d
    n_sheads = dh // SSD_HEADDIM
    gn = SSD_GROUPS * SSD_STATE
    n_aheads = d // ATTN_DV
    aw = n_aheads * ATTN_DV
    qkw = 2 * n_aheads * ATTN_DK
    n_experts = p["w_router"].shape[-1]

    mod = _ada(c_pad, p["w_ada"][l], p["b_ada"][l][None, :], _tile(6 * d, 1024))
    mod3 = mod[:bsz].reshape(bsz, 1, 6 * d)

    o = 0
    segs = {}
    for name, size in (("z", dh), ("xbc", dh + 2 * gn), ("dt", n_sheads), ("q", qkw), ("k", qkw),
                       ("v", aw), ("gs", d), ("ga", d)):
        segs[name] = (o, o + size)
        o += size
    scale = ATTN_DK ** -0.5 * math.log2(math.e)
    w_plain, w_rope, w_gate, w_dt = _regroup(jnp.swapaxes(p["w_in"][l], 0, 1), segs, scale)

    h = _prenorm(x, p["g_pre_mix"][l][None, :], mod3, 1, 0, _tile(seq, 512)).reshape(t, d)
    tm = _tile(seq, 1024)
    plain = _mm(h, w_plain, BF16, tm, _tile(w_plain.shape[1], 1024), name="proj_plain")
    qk = _mm(h, w_rope, BF16, tm, _tile(w_rope.shape[1], 1024), "rope", _rope_tables(seq), name="proj_rope")
    gates = _mm(h, w_gate, BF16, tm, _tile(w_gate.shape[1], 1024), "sigmoid", name="proj_gate")
    dt_raw = _mm(h, w_dt, F32, tm, LANES, name="proj_dt")

    pad_h = lambda v: jnp.pad(v, (0, LANES - n_sheads))[None, :]
    plain3 = plain.reshape(bsz, seq, plain.shape[1])
    y_ssd = _ssd(plain3, dt_raw.reshape(bsz, seq, LANES), p["conv_w"][l], p["conv_b"][l][None, :],
                 pad_h(p["dt_bias"][l]), pad_h(p["a_log"][l]),
                 jnp.repeat(p["d_skip"][l], SSD_HEADDIM)[None, :], p["ssd_norm_w"][l][None, :], dh, n_sheads)

    lam_init = 0.8 - 0.6 * math.exp(-0.3 * l)
    lam_rows = jnp.zeros((8, LANES), F32)
    for r, nm in enumerate(("lambda_q1", "lambda_k1", "lambda_q2", "lambda_k2")):
        lam_rows = lam_rows.at[r, :ATTN_DK].set(p[nm][l])
    v_blk0 = (dh + dh + 2 * gn) // LANES
    o_attn = _attention(qk.reshape(bsz, seq, 2 * qkw), plain3, lam_rows, p["subln_w"][l][None, :],
                        n_aheads, v_blk0, _tile(seq, ATTN_Q_TILE), lam_init)

    tm2 = _tile(seq, 512)
    merged = _merge(y_ssd.reshape(t, dh), o_attn.reshape(t, aw), p["w_br_ssd"][l].astype(BF16),
                    p["w_br_attn"][l].astype(BF16), gates, tm2, _tile(d, 1024))
    wr = jnp.pad(p["w_router"][l], ((0, 0), (0, LANES - n_experts)))
    wr_hi = wr.astype(BF16)
    wr_lo = (wr - wr_hi.astype(F32)).astype(BF16)
    br = jnp.pad(p["b_router"][l], (0, LANES - n_experts))[None, :]
    x1, h_rows, logits = _outproj(merged, p["w_out"][l].astype(BF16), x.reshape(t, d),
                                    p["g_post_mix"][l][None, :], mod3, p["g_pre_ffn"][l][None, :],
                                    wr_hi, wr_lo, br, seq, tm2, 2, 4, 3)

    dest, wts, cnt = _route(logits, n_experts, moe_blk, _tile(t, 512))
    counts = cnt[0, :n_experts].astype(jnp.int32)
    n_slots = t * TOP_K + n_experts * moe_blk
    nb = n_slots // moe_blk
    pblocks = (counts + moe_blk - 1) // moe_blk
    pend = jnp.cumsum(pblocks)
    nused = jnp.maximum(pend[-1], 1).astype(jnp.int32)
    bidx = jnp.arange(nb, dtype=jnp.int32)
    block_e = jnp.minimum(jnp.sum((pend[None, :] <= bidx[:, None]).astype(jnp.int32), axis=1), n_experts - 1)
    pstart = pend - pblocks
    nvalid = jnp.clip(counts[block_e] - (bidx - pstart[block_e]) * moe_blk, 0, moe_blk).astype(jnp.int32)
    dest_flat = dest[:, :TOP_K].reshape(-1)
    nused1 = nused.reshape(1)
    after = pend[block_e].astype(jnp.int32)
    nxt = jnp.where(after < nused, after, -1).astype(jnp.int32)

    xs = _dispatch(dest_flat, nvalid, h_rows, n_slots, _tile(t, 256), moe_blk, d // LANES)
    dff = p["w_down"].shape[2]
    act = _gateup(block_e, nvalid, nused1, nxt, xs, p["w_gate_up"][l], p["b_gate_up"][l][:, None, :],
                  moe_blk, _tile(dff, 1024))
    y_sorted = _down(block_e, nvalid, nused1, nxt, act, p["w_down"][l], p["b_down"][l][:, None, :], moe_blk)
    out = _combine(dest_flat, y_sorted, wts, x1, mod3, p["g_post_ffn"][l][None, :], seq, _tile(seq, 128), 5)
    return out.reshape(bsz, seq, d)


MOE_ROW_BLOCK = 512
MM_ROW_CHUNKS = 4
CAST_ROWS = 128
MOE_SUB_BLOCKS = 2


def kernel(x, c, w_ada, b_ada, g_pre_mix, g_post_mix, g_pre_ffn, g_post_ffn, w_in, conv_w, conv_b, dt_bias, a_log, d_skip, ssd_norm_w, lambda_q1, lambda_k1, lambda_q2, lambda_k2, subln_w, w_br_ssd, w_br_attn, w_out, w_router, b_router, w_gate_up, b_gate_up, w_down, b_down):
    p = dict(w_ada=w_ada, b_ada=b_ada, g_pre_mix=g_pre_mix, g_post_mix=g_post_mix, g_pre_ffn=g_pre_ffn,
             g_post_ffn=g_post_ffn, w_in=w_in, conv_w=conv_w, conv_b=conv_b, dt_bias=dt_bias, a_log=a_log,
             d_skip=d_skip, ssd_norm_w=ssd_norm_w, lambda_q1=lambda_q1, lambda_k1=lambda_k1,
             lambda_q2=lambda_q2, lambda_k2=lambda_k2, subln_w=subln_w, w_br_ssd=w_br_ssd,
             w_br_attn=w_br_attn, w_out=w_out, w_router=w_router, b_router=b_router, w_gate_up=w_gate_up,
             b_gate_up=b_gate_up, w_down=w_down, b_down=b_down)
    bsz = x.shape[0]
    c_pad = jnp.pad(c, ((0, (-bsz) % 8), (0, 0)))
    for l in range(w_ada.shape[0]):
        x = _layer(x, c_pad, l, p, min(MOE_ROW_BLOCK, x.shape[0] * x.shape[1]))
    return x
```

```python
import functools
import math

import jax
import jax.numpy as jnp
from jax import lax
from jax.experimental import pallas as pl
from jax.experimental.pallas import tpu as pltpu

F32 = jnp.float32
BF16 = jnp.bfloat16
HIGHEST = lax.Precision.HIGHEST

SSD_HEADDIM = 64
SSD_GROUPS = 4
SSD_STATE = 128
SSD_CONV = 4
SSD_CHUNK = 128
ATTN_DK = 64
ATTN_DV = 128
ATTN_KV_UNIT = 512
ATTN_Q_TILE = 512
ATTN_HEADS_PER_STEP = 2
ROT_DIM = ATTN_DK // 4
ROPE_THETA = 500000.0
TOP_K = 4
SWIGLU_LIMIT = 7.0
SWIGLU_ALPHA = 1.702
NORM_EPS = 1e-6
SUB_EPS = 1e-5
LANES = 128
NEG_BIG = -1e30

VMEM_LIMIT = 56 * 1024 * 1024


def _cparams(sem, vmem=VMEM_LIMIT):
    return pltpu.CompilerParams(dimension_semantics=sem, vmem_limit_bytes=vmem)


def _sigmoid(x):
    return 1.0 / (1.0 + jnp.exp(-x))


def _store_token_rows(ref, first_tok, val):
    n_tok, width = val.shape
    ns = width // LANES
    for s_ in range(ns):
        ref[pl.ds(first_tok * ns + s_, n_tok, stride=ns), :] = val[:, s_ * LANES:(s_ + 1) * LANES]


def _load_token_rows(ref, first_tok, n_tok, ns):
    return jnp.concatenate([ref[pl.ds(first_tok * ns + s_, n_tok, stride=ns), :] for s_ in range(ns)], axis=1)


def _ada_kernel(c_ref, w_ref, b_ref, o_ref):
    c = c_ref[...]
    sc = c * _sigmoid(c)
    o_ref[...] = jnp.dot(sc, w_ref[...], preferred_element_type=F32, precision=HIGHEST) + b_ref[...]


def _ada(c_pad, w, b, tn):
    rows, d = c_pad.shape
    n = w.shape[1]
    return pl.pallas_call(
        _ada_kernel,
        grid=(n // tn,),
        in_specs=[pl.BlockSpec((rows, d), lambda j: (0, 0)),
                  pl.BlockSpec((d, tn), lambda j: (0, j)),
                  pl.BlockSpec((1, tn), lambda j: (0, j))],
        out_specs=pl.BlockSpec((rows, tn), lambda j: (0, j)),
        out_shape=jax.ShapeDtypeStruct((rows, n), F32),
        compiler_params=_cparams(("arbitrary",)),
        name="ada",
    )(c_pad, w, b)


def _prenorm_kernel(x_ref, g_ref, sc_ref, sh_ref, o_ref):
    x = x_ref[0]
    y = x * lax.rsqrt(jnp.mean(x * x, axis=-1, keepdims=True) + NORM_EPS) * g_ref[...]
    o_ref[0] = (y * (1.0 + sc_ref[0]) + sh_ref[0]).astype(o_ref.dtype)


def _prenorm(x, g, mod3, sc_blk, sh_blk, ts):
    b, s, d = x.shape
    return pl.pallas_call(
        _prenorm_kernel,
        grid=(b, s // ts),
        in_specs=[pl.BlockSpec((1, ts, d), lambda bi, si: (bi, si, 0)),
                  pl.BlockSpec((1, d), lambda bi, si: (0, 0)),
                  pl.BlockSpec((1, 1, d), lambda bi, si: (bi, 0, sc_blk)),
                  pl.BlockSpec((1, 1, d), lambda bi, si: (bi, 0, sh_blk))],
        out_specs=pl.BlockSpec((1, ts, d), lambda bi, si: (bi, si, 0)),
        out_shape=jax.ShapeDtypeStruct((b, s, d), BF16),
        compiler_params=_cparams(("arbitrary", "arbitrary")),
        name="prenorm",
    )(x, g, mod3, mod3)


REGROUP_TILE = 512


def _regroup_kernel(src_sm, wt_hbm, plain_ref, rope_ref, gate_ref, dt_ref, buf, dt_buf, sem, dt_sem, *,
                    tiles, dt_rows, q_scale):
    t = pl.program_id(0)
    n = pl.num_programs(0)
    tr = REGROUP_TILE
    n_plain, n_rope, n_gate, n_q = tiles

    def load(step):
        row = pl.multiple_of(src_sm[step], 8)
        return pltpu.make_async_copy(wt_hbm.at[pl.ds(row, tr), :], buf.at[step % 2], sem.at[step % 2])

    dt_copy = pltpu.make_async_copy(wt_hbm.at[pl.ds(dt_rows[0], dt_rows[1]), :], dt_buf, dt_sem)

    @pl.when(t == 0)
    def _():
        load(0).start()
        dt_copy.start()

    pl.when(t + 1 < n)(lambda: load(t + 1).start())
    load(t).wait()
    cols = buf[t % 2].T

    @pl.when(t < n_plain)
    def _():
        plain_ref[...] = cols.astype(BF16)

    @pl.when((t >= n_plain) & (t < n_plain + n_rope))
    def _():
        rope_ref[...] = jnp.where(t < n_plain + n_q, cols * q_scale, cols).astype(BF16)

    @pl.when(t >= n_plain + n_rope)
    def _():
        gate_ref[...] = cols.astype(BF16)

    @pl.when(t == n - 1)
    def _():
        dt_copy.wait()
        dt_ref[...] = jnp.zeros_like(dt_ref)
        dt_ref[:, :dt_rows[1]] = dt_buf[...].T.astype(BF16)


def _regroup(w_in_t, segs, q_scale):
    n_in, d = w_in_t.shape
    tr = REGROUP_TILE
    starts = []
    counts = []
    for group in (("z", "xbc", "v"), ("q", "k"), ("gs", "ga")):
        c = 0
        for nm in group:
            lo, hi = segs[nm]
            assert (hi - lo) % tr == 0 and lo % 8 == 0
            starts += list(range(lo, hi, tr))
            c += (hi - lo) // tr
        counts.append(c)
    n_plain, n_rope, n_gate = counts
    n_q = (segs["q"][1] - segs["q"][0]) // tr
    dt_lo, dt_hi = segs["dt"]
    clamp = lambda t, first, cnt: jnp.clip(t - first, 0, cnt - 1)
    return pl.pallas_call(
        functools.partial(_regroup_kernel, tiles=(n_plain, n_rope, n_gate, n_q), dt_rows=(dt_lo, dt_hi - dt_lo),
                          q_scale=q_scale),
        grid_spec=pltpu.PrefetchScalarGridSpec(
            num_scalar_prefetch=1,
            grid=(len(starts),),
            in_specs=[pl.BlockSpec(memory_space=pl.ANY)],
            out_specs=[pl.BlockSpec((d, tr), lambda t, src: (0, clamp(t, 0, n_plain))),
                       pl.BlockSpec((d, tr), lambda t, src: (0, clamp(t, n_plain, n_rope))),
                       pl.BlockSpec((d, tr), lambda t, src: (0, clamp(t, n_plain + n_rope, n_gate))),
                       pl.BlockSpec((d, LANES), lambda t, src: (0, 0))],
            scratch_shapes=[pltpu.VMEM((2, tr, d), F32), pltpu.VMEM((dt_hi - dt_lo, d), F32),
                            pltpu.SemaphoreType.DMA((2,)), pltpu.SemaphoreType.DMA(())],
        ),
        out_shape=[jax.ShapeDtypeStruct((d, n_plain * tr), BF16), jax.ShapeDtypeStruct((d, n_rope * tr), BF16),
                   jax.ShapeDtypeStruct((d, n_gate * tr), BF16), jax.ShapeDtypeStruct((d, LANES), BF16)],
        compiler_params=_cparams(("arbitrary",)),
        name="regroup_w_in",
    )(jnp.asarray(starts, jnp.int32), w_in_t)


def _mm_kernel(a_ref, w_ref, o_ref):
    o_ref[...] = jnp.dot(a_ref[...], w_ref[...], preferred_element_type=F32).astype(o_ref.dtype)


def _row_chunks(n_rows):
    step = n_rows // MM_ROW_CHUNKS if n_rows % MM_ROW_CHUNKS == 0 and n_rows >= 64 * MM_ROW_CHUNKS else n_rows
    return [slice(r, r + step) for r in range(0, n_rows, step)]


def _mm_sigmoid_kernel(a_ref, w_ref, o_ref):
    for rows in _row_chunks(a_ref.shape[0]):
        acc = jnp.dot(a_ref[rows, :], w_ref[...], preferred_element_type=F32)
        o_ref[rows, :] = _sigmoid(acc).astype(o_ref.dtype)


def _mm_rope_kernel(a_ref, w_ref, c_ref, s1_ref, s2_ref, o_ref):
    for rows in _row_chunks(a_ref.shape[0]):
        acc = jnp.dot(a_ref[rows, :], w_ref[...], preferred_element_type=F32)
        c = c_ref[rows, :]
        s1 = s1_ref[rows, :]
        s2 = s2_ref[rows, :]
        for g in range(acc.shape[1] // LANES):
            blk = acc[:, g * LANES:(g + 1) * LANES]
            fwd = pltpu.roll(blk, LANES - ROT_DIM // 2, 1)
            bwd = pltpu.roll(blk, ROT_DIM // 2, 1)
            o_ref[rows, g * LANES:(g + 1) * LANES] = (blk * c + fwd * s1 + bwd * s2).astype(o_ref.dtype)


def _mm(a, w, out_dtype, tm, tn, epilogue="none", rope_tabs=None, name="mm"):
    m, k = a.shape
    n = w.shape[1]
    in_specs = [pl.BlockSpec((tm, k), lambda i, j: (i, 0)),
                pl.BlockSpec((k, tn), lambda i, j: (0, j))]
    args = [a, w]
    if epilogue == "rope":
        seq_blocks = rope_tabs[0].shape[0] // tm
        for t in rope_tabs:
            in_specs.append(pl.BlockSpec((tm, LANES), lambda i, j: (i % seq_blocks, 0)))
            args.append(t)
        kern = _mm_rope_kernel
    elif epilogue == "sigmoid":
        kern = _mm_sigmoid_kernel
    else:
        kern = _mm_kernel
    return pl.pallas_call(
        kern,
        grid=(m // tm, n // tn),
        in_specs=in_specs,
        out_specs=pl.BlockSpec((tm, tn), lambda i, j: (i, j)),
        out_shape=jax.ShapeDtypeStruct((m, n), out_dtype),
        compiler_params=_cparams(("arbitrary", "arbitrary")),
        name=name,
    )(*args)


def _ssd_kernel(z_ref, xs_ref, bm_ref, cm_ref, dt_ref, cwx_ref, cwb_ref, cwc_ref, cbx_ref, cbb_ref,
                cbc_ref, dtb_ref, alog_ref, dsk_ref, nw_ref, o_ref,
                px_sc, pb_sc, pc_sc, st_sc, y_sc, *, n_heads):
    c = pl.program_id(1)
    L = SSD_CHUNK
    N = SSD_STATE
    hpg = n_heads // SSD_GROUPS
    gw = hpg * SSD_HEADDIM

    @pl.when(c == 0)
    def _():
        px_sc[...] = jnp.zeros_like(px_sc)
        pb_sc[...] = jnp.zeros_like(pb_sc)
        pc_sc[...] = jnp.zeros_like(pc_sc)
        st_sc[...] = jnp.zeros_like(st_sc)

    def conv_silu(u, prev_sc, w_ref, b_ref):
        prev = prev_sc[...]
        row = lax.broadcasted_iota(jnp.int32, u.shape, 0)
        acc = u * w_ref[SSD_CONV - 1:SSD_CONV, :] + b_ref[...]
        for j in range(1, SSD_CONV):
            sh = jnp.where(row < j, pltpu.roll(prev, j, 0), pltpu.roll(u, j, 0))
            acc = acc + sh * w_ref[SSD_CONV - 1 - j:SSD_CONV - j, :]
        prev_sc[...] = u
        return acc * _sigmoid(acc)

    xs = conv_silu(xs_ref[0].astype(F32), px_sc, cwx_ref, cbx_ref)
    bm = conv_silu(bm_ref[0].astype(F32), pb_sc, cwb_ref, cbb_ref)
    cm = conv_silu(cm_ref[0].astype(F32), pc_sc, cwc_ref, cbc_ref)

    raw = dt_ref[0] + dtb_ref[...]
    dt = jnp.maximum(raw, 0.0) + jnp.log1p(jnp.exp(-jnp.abs(raw)))
    a = dt * (-jnp.exp(alog_ref[...]))
    r_i = lax.broadcasted_iota(jnp.int32, (L, L), 0)
    c_i = lax.broadcasted_iota(jnp.int32, (L, L), 1)
    causal = r_i >= c_i
    tri = jnp.where(causal, 1.0, 0.0).astype(F32)
    a_cs = jnp.dot(tri, a, preferred_element_type=F32, precision=HIGHEST)
    a_cs_t = a_cs.T

    lane = lax.broadcasted_iota(jnp.int32, (L, LANES), 1)
    first = lane < SSD_HEADDIM
    acs_tiles = []
    dt_tiles = []
    for j in range(n_heads // 2):
        h0, h1 = 2 * j, 2 * j + 1
        acs_tiles.append(jnp.where(first, a_cs[:, h0:h0 + 1], a_cs[:, h1:h1 + 1]))
        dt_tiles.append(jnp.where(first, dt[:, h0:h0 + 1], dt[:, h1:h1 + 1]))
    acs_e = jnp.concatenate(acs_tiles, axis=1)
    dt_e = jnp.concatenate(dt_tiles, axis=1)
    xdt = xs * dt_e
    ea = jnp.exp(acs_e)
    alast = acs_e[L - 1:L, :]
    xdec = (xdt * jnp.exp(alast - acs_e)).astype(BF16)
    ealast = jnp.exp(alast)

    for g in range(SSD_GROUPS):
        bg = bm[:, g * N:(g + 1) * N]
        cg = cm[:, g * N:(g + 1) * N].astype(BF16)
        scores = lax.dot_general(cg, bg.astype(BF16), (((1,), (1,)), ((), ())),
                                 preferred_element_type=F32)
        st_old = st_sc[:, g * gw:(g + 1) * gw]
        y_off = jnp.dot(cg, st_old.astype(BF16), preferred_element_type=F32) * ea[:, g * gw:(g + 1) * gw]
        st_sc[:, g * gw:(g + 1) * gw] = st_old * ealast[:, g * gw:(g + 1) * gw] + jnp.dot(
            bg.T.astype(BF16), xdec[:, g * gw:(g + 1) * gw], preferred_element_type=F32)
        for jj in range(hpg // 2):
            j = g * (hpg // 2) + jj
            h0, h1 = 2 * j, 2 * j + 1
            la = jnp.exp(jnp.where(causal, a_cs[:, h0:h0 + 1] - a_cs_t[h0:h0 + 1, :], NEG_BIG))
            lb = jnp.exp(jnp.where(causal, a_cs[:, h1:h1 + 1] - a_cs_t[h1:h1 + 1, :], NEG_BIG))
            mcat = jnp.concatenate([(scores * la).astype(BF16), (scores * lb).astype(BF16)], axis=1)
            xp = xdt[:, j * LANES:(j + 1) * LANES]
            xcat = jnp.concatenate([jnp.where(first, xp, 0.0).astype(BF16),
                                    jnp.where(first, 0.0, xp).astype(BF16)], axis=0)
            y_diag = jnp.dot(mcat, xcat, preferred_element_type=F32)
            lo = jj * LANES
            y_sc[:, j * LANES:(j + 1) * LANES] = (
                y_diag + y_off[:, lo:lo + LANES]
                + dsk_ref[:, j * LANES:(j + 1) * LANES] * xs[:, j * LANES:(j + 1) * LANES])

    z = z_ref[0].astype(F32)
    u = y_sc[...] * (z * _sigmoid(z))
    for g in range(SSD_GROUPS):
        ug = u[:, g * gw:(g + 1) * gw]
        ms = jnp.mean(ug * ug, axis=-1, keepdims=True)
        o_ref[0, :, g * gw:(g + 1) * gw] = (ug * lax.rsqrt(ms + SUB_EPS)
                                            * nw_ref[:, g * gw:(g + 1) * gw]).astype(o_ref.dtype)


def _ssd(plain3, dt3, conv_w, conv_b, dtb, alog, dsk_e, norm_w, dh, n_heads):
    b, s, _ = plain3.shape
    L = SSD_CHUNK
    gn = SSD_GROUPS * SSD_STATE
    nc = s // L
    xblk = 1
    bblk = (2 * dh) // gn
    cw_x, cw_b, cw_c = conv_w[:, :dh], conv_w[:, dh:dh + gn], conv_w[:, dh + gn:]
    cb_x, cb_b, cb_c = conv_b[:, :dh], conv_b[:, dh:dh + gn], conv_b[:, dh + gn:]
    full = lambda shape: pl.BlockSpec(shape, lambda bi, ci: (0, 0))
    return pl.pallas_call(
        functools.partial(_ssd_kernel, n_heads=n_heads),
        grid=(b, nc),
        in_specs=[pl.BlockSpec((1, L, dh), lambda bi, ci: (bi, ci, 0)),
                  pl.BlockSpec((1, L, dh), lambda bi, ci: (bi, ci, xblk)),
                  pl.BlockSpec((1, L, gn), lambda bi, ci: (bi, ci, bblk)),
                  pl.BlockSpec((1, L, gn), lambda bi, ci: (bi, ci, bblk + 1)),
                  pl.BlockSpec((1, L, LANES), lambda bi, ci: (bi, ci, 0)),
                  full((SSD_CONV, dh)), full((SSD_CONV, gn)), full((SSD_CONV, gn)),
                  full((1, dh)), full((1, gn)), full((1, gn)),
                  full((1, LANES)), full((1, LANES)), full((1, dh)), full((1, dh))],
        out_specs=pl.BlockSpec((1, L, dh), lambda bi, ci: (bi, ci, 0)),
        out_shape=jax.ShapeDtypeStruct((b, s, dh), BF16),
        scratch_shapes=[pltpu.VMEM((L, dh), F32), pltpu.VMEM((L, gn), F32), pltpu.VMEM((L, gn), F32),
                        pltpu.VMEM((SSD_STATE, dh), F32), pltpu.VMEM((L, dh), F32)],
        compiler_params=_cparams(("arbitrary", "arbitrary")),
        name="ssd",
    )(plain3, plain3, plain3, plain3, dt3, cw_x, cw_b, cw_c, cb_x, cb_b, cb_c, dtb, alog, dsk_e, norm_w)


def _attn_kernel(q_ref, k_ref, v_ref, bias_ref, lam_ref, sw_ref, o_ref, vt_sc, st_a, st_b, m_sc, l_sc, acc_sc, *,
                 tq, lam_init):
    qi = pl.program_id(2)
    tu = ATTN_KV_UNIT
    nh = ATTN_HEADS_PER_STEP
    n_all = k_ref.shape[1] // tu
    heads = [slice(hh * LANES, (hh + 1) * LANES) for hh in range(nh)]

    @pl.when(qi == 0)
    def _():
        def transpose_block(c, carry):
            start = pl.multiple_of(c * tu, tu)
            for hh in range(nh):
                vt_sc[hh, c] = v_ref[0, pl.ds(start, tu), heads[hh]].astype(F32).T.astype(BF16)
            return carry

        lax.fori_loop(0, n_all, transpose_block, 0)

    qts = []
    for hh in range(nh):
        qt = q_ref[0, :, heads[hh]].astype(F32).T
        row = lax.broadcasted_iota(jnp.int32, qt.shape, 0)
        qts.append((jnp.where(row < ATTN_DK, qt, 0.0).astype(BF16),
                    jnp.where(row < ATTN_DK, 0.0, qt).astype(BF16)))
    m_sc[...] = jnp.full_like(m_sc, NEG_BIG)
    l_sc[...] = jnp.zeros_like(l_sc)
    acc_sc[...] = jnp.zeros_like(acc_sc)

    def scores(u, st_ref):
        start = pl.multiple_of(u * tu, tu)
        for hh in range(nh):
            k = k_ref[0, pl.ds(start, tu), heads[hh]]
            for m in range(2):
                st_ref[2 * hh + m] = jnp.dot(k, qts[hh][m], preferred_element_type=F32)

    def update(u, st_ref, masked):
        for hh in range(nh):
            vt = vt_sc[hh, u]
            for m in range(2):
                c = 2 * hh + m
                st = st_ref[c]
                if masked:
                    st = st + bias_ref[...]
                m_prev = m_sc[c]
                m_new = jnp.maximum(m_prev, jnp.max(st, axis=0, keepdims=True))
                alpha = jnp.exp2(m_prev - m_new)
                pt = jnp.exp2(st - m_new)
                l_sc[c] = alpha * l_sc[c] + jnp.sum(pt, axis=0, keepdims=True)
                acc_sc[c] = alpha * acc_sc[c] + jnp.dot(vt, pt.astype(BF16), preferred_element_type=F32)
                m_sc[c] = m_new

    n_units = (qi * tq) // tu + 1
    n_loop = (n_units - 1) // 2
    scores(0, st_a)

    def two_units(j, carry):
        u = 2 * j
        scores(u + 1, st_b)
        update(u, st_a, False)
        scores(u + 2, st_a)
        update(u + 1, st_b, False)
        return carry

    lax.fori_loop(0, n_loop, two_units, 0)
    last = n_units - 1

    @pl.when(last == 2 * n_loop)
    def _():
        update(last, st_a, True)

    @pl.when(last != 2 * n_loop)
    def _():
        scores(last, st_b)
        update(last - 1, st_a, False)
        update(last, st_b, True)

    lv = lam_ref[...]
    lam = (jnp.exp(jnp.sum(lv[0:1] * lv[1:2], axis=-1, keepdims=True))
           - jnp.exp(jnp.sum(lv[2:3] * lv[3:4], axis=-1, keepdims=True)) + lam_init)
    for hh in range(nh):
        c0, c1 = 2 * hh, 2 * hh + 1
        ot = acc_sc[c0] * (1.0 / l_sc[c0]) - lam * (acc_sc[c1] * (1.0 / l_sc[c1]))
        ot = ot * lax.rsqrt(jnp.mean(ot * ot, axis=0, keepdims=True) + SUB_EPS)
        o_ref[0, :, heads[hh]] = (ot.T * sw_ref[...] * (1.0 - lam_init)).astype(o_ref.dtype)


def _attention(qk3, plain3, lam_rows, subln_w, n_heads, v_blk0, tq, lam_init):
    b, s, _ = qk3.shape
    tu = ATTN_KV_UNIT
    nh = ATTN_HEADS_PER_STEP
    hw = nh * LANES
    assert tq == tu, "the diagonal unit must coincide with the query tile"
    kpos = lax.broadcasted_iota(jnp.int32, (tu, tq), 0)
    qpos = lax.broadcasted_iota(jnp.int32, (tu, tq), 1)
    diag_bias = jnp.where(kpos <= qpos, 0.0, NEG_BIG).astype(F32)
    return pl.pallas_call(
        functools.partial(_attn_kernel, tq=tq, lam_init=lam_init),
        grid=(b, n_heads // nh, s // tq),
        in_specs=[pl.BlockSpec((1, tq, hw), lambda bi, hi, qi: (bi, qi, hi)),
                  pl.BlockSpec((1, s, hw), lambda bi, hi, qi: (bi, 0, n_heads // nh + hi)),
                  pl.BlockSpec((1, s, hw), lambda bi, hi, qi: (bi, 0, v_blk0 // nh + hi)),
                  pl.BlockSpec((tu, tq), lambda bi, hi, qi: (0, 0)),
                  pl.BlockSpec((8, LANES), lambda bi, hi, qi: (0, 0)),
                  pl.BlockSpec((1, LANES), lambda bi, hi, qi: (0, 0))],
        out_specs=pl.BlockSpec((1, tq, hw), lambda bi, hi, qi: (bi, qi, hi)),
        out_shape=jax.ShapeDtypeStruct((b, s, n_heads * ATTN_DV), BF16),
        scratch_shapes=[pltpu.VMEM((nh, s // tu, ATTN_DV, tu), BF16),
                        pltpu.VMEM((2 * nh, tu, tq), F32), pltpu.VMEM((2 * nh, tu, tq), F32),
                        pltpu.VMEM((2 * nh, 1, tq), F32), pltpu.VMEM((2 * nh, 1, tq), F32),
                        pltpu.VMEM((2 * nh, ATTN_DV, tq), F32)],
        compiler_params=_cparams(("arbitrary", "arbitrary", "arbitrary")),
        name="diff_attn",
    )(qk3, qk3, plain3, diag_bias, lam_rows, subln_w)


def _merge_kernel(y_ref, o_ref, ws_ref, wa_ref, gs_ref, ga_ref, out_ref):
    for rows in _row_chunks(y_ref.shape[0]):
        bs = jnp.dot(y_ref[rows, :], ws_ref[...], preferred_element_type=F32)
        ba = jnp.dot(o_ref[rows, :], wa_ref[...], preferred_element_type=F32)
        out_ref[rows, :] = (gs_ref[rows, :].astype(F32) * bs
                            + ga_ref[rows, :].astype(F32) * ba).astype(out_ref.dtype)


def _merge(y, o, ws, wa, gates, tm, tn):
    m, k = y.shape
    n = ws.shape[1]
    nj = n // tn
    return pl.pallas_call(
        _merge_kernel,
        grid=(m // tm, nj),
        in_specs=[pl.BlockSpec((tm, k), lambda i, j: (i, 0)),
                  pl.BlockSpec((tm, o.shape[1]), lambda i, j: (i, 0)),
                  pl.BlockSpec((k, tn), lambda i, j: (0, j)),
                  pl.BlockSpec((o.shape[1], tn), lambda i, j: (0, j)),
                  pl.BlockSpec((tm, tn), lambda i, j: (i, j)),
                  pl.BlockSpec((tm, tn), lambda i, j: (i, nj + j))],
        out_specs=pl.BlockSpec((tm, tn), lambda i, j: (i, j)),
        out_shape=jax.ShapeDtypeStruct((m, n), BF16),
        compiler_params=_cparams(("arbitrary", "arbitrary")),
        name="merge",
    )(y, o, ws, wa, gates, gates)


def _outproj_kernel(mg_ref, wo_ref, x_ref, gpost_ref, gt_ref, gpre_ref, sc_ref, sh_ref, wr_hi_ref,
                    wr_lo_ref, br_ref, x1_ref, hp_ref, lg_ref):
    mix = jnp.dot(mg_ref[...], wo_ref[...], preferred_element_type=F32)
    nm = mix * lax.rsqrt(jnp.mean(mix * mix, axis=-1, keepdims=True) + NORM_EPS) * gpost_ref[...]
    x1 = x_ref[...] + gt_ref[0] * nm
    x1_ref[...] = x1
    h2 = (x1 * lax.rsqrt(jnp.mean(x1 * x1, axis=-1, keepdims=True) + NORM_EPS) * gpre_ref[...]
          * (1.0 + sc_ref[0]) + sh_ref[0])
    h_hi = h2.astype(BF16)
    h_lo = (h2 - h_hi.astype(F32)).astype(BF16)
    wr_hi = wr_hi_ref[...]
    lg_ref[...] = (jnp.dot(h_hi, wr_hi, preferred_element_type=F32)
                   + jnp.dot(h_lo, wr_hi, preferred_element_type=F32)
                   + jnp.dot(h_hi, wr_lo_ref[...], preferred_element_type=F32) + br_ref[...])
    _store_token_rows(hp_ref, 0, h2)


def _outproj(merged, wo, x2, gpost, mod3, gpre, wr_hi, wr_lo, br, seq, tm, gt_blk, sc_blk, sh_blk):
    m, d = x2.shape
    per_b = seq // tm
    row = lambda i: (i, 0)
    const = lambda i: (0, 0)
    return pl.pallas_call(
        _outproj_kernel,
        grid=(m // tm,),
        in_specs=[pl.BlockSpec((tm, d), row),
                  pl.BlockSpec((d, d), const),
                  pl.BlockSpec((tm, d), row),
                  pl.BlockSpec((1, d), const),
                  pl.BlockSpec((1, 1, d), lambda i: (i // per_b, 0, gt_blk)),
                  pl.BlockSpec((1, d), const),
                  pl.BlockSpec((1, 1, d), lambda i: (i // per_b, 0, sc_blk)),
                  pl.BlockSpec((1, 1, d), lambda i: (i // per_b, 0, sh_blk)),
                  pl.BlockSpec((d, LANES), const),
                  pl.BlockSpec((d, LANES), const),
                  pl.BlockSpec((1, LANES), const)],
        out_specs=[pl.BlockSpec((tm, d), row),
                   pl.BlockSpec((tm * (d // LANES), LANES), row),
                   pl.BlockSpec((tm, LANES), row)],
        out_shape=[jax.ShapeDtypeStruct((m, d), F32),
                   jax.ShapeDtypeStruct((m * (d // LANES), LANES), F32),
                   jax.ShapeDtypeStruct((m, LANES), F32)],
        compiler_params=_cparams(("arbitrary",)),
        name="outproj",
    )(merged, wo, x2, gpost, mod3, gpre, mod3, mod3, wr_hi, wr_lo, br)


def _route_kernel(lg_ref, dest_ref, w_ref, cnt_ref, cnt_sc, pst_sc, run_sc, *, n_experts, blk):
    ph = pl.program_id(0)
    t = pl.program_id(1)
    tk = lg_ref.shape[0]
    lane = lax.broadcasted_iota(jnp.int32, (tk, LANES), 1)
    lg = jnp.where(lane < n_experts, lg_ref[...], -jnp.inf)
    vals = []
    hots = []
    for _ in range(TOP_K):
        mx = jnp.max(lg, axis=-1, keepdims=True)
        ix = jnp.min(jnp.where(lg == mx, lane, LANES), axis=-1, keepdims=True)
        hot = lane == ix
        lg = jnp.where(hot, -jnp.inf, lg)
        vals.append(mx)
        hots.append(hot)
    multi = jnp.zeros((tk, LANES), F32)
    for hot in hots:
        multi = multi + jnp.where(hot, 1.0, 0.0)
    colsum = jnp.sum(multi, axis=0, keepdims=True)

    @pl.when((ph == 0) & (t == 0))
    def _():
        cnt_sc[...] = jnp.zeros_like(cnt_sc)

    @pl.when(ph == 0)
    def _():
        cnt_sc[...] += colsum

    @pl.when((ph == 1) & (t == 0))
    def _():
        cnt = cnt_sc[...].astype(jnp.int32)
        padded = (((cnt + (blk - 1)) // blk) * blk).astype(F32)
        r_i = lax.broadcasted_iota(jnp.int32, (LANES, LANES), 0)
        c_i = lax.broadcasted_iota(jnp.int32, (LANES, LANES), 1)
        upper = jnp.where(r_i < c_i, 1.0, 0.0).astype(F32)
        pst_sc[...] = jnp.dot(jnp.broadcast_to(padded, (8, LANES)), upper,
                              preferred_element_type=F32, precision=HIGHEST)[0:1]
        run_sc[...] = jnp.zeros_like(run_sc)

    @pl.when(ph == 1)
    def _():
        r_i = lax.broadcasted_iota(jnp.int32, (tk, tk), 0)
        c_i = lax.broadcasted_iota(jnp.int32, (tk, tk), 1)
        strict = jnp.where(r_i > c_i, 1.0, 0.0).astype(BF16)
        before = jnp.dot(strict, multi.astype(BF16), preferred_element_type=F32)
        base = before + run_sc[...] + pst_sc[...]
        esum = jnp.zeros((tk, 1), F32)
        evals = []
        for r in range(TOP_K):
            e = jnp.exp(vals[r] - vals[0])
            evals.append(e)
            esum = esum + e
        dest = jnp.zeros((tk, LANES), jnp.int32)
        wts = jnp.zeros((tk, LANES), F32)
        for r in range(TOP_K):
            d_r = jnp.sum(jnp.where(hots[r], base, 0.0), axis=-1, keepdims=True).astype(jnp.int32)
            dest = jnp.where(lane == r, d_r, dest)
            wts = jnp.where(lane == r, evals[r] / esum, wts)
        dest_ref[...] = dest
        w_ref[...] = wts
        run_sc[...] += colsum
        cnt_ref[...] = jnp.broadcast_to(cnt_sc[...], cnt_ref.shape)


def _route(logits, n_experts, blk, tk):
    t = logits.shape[0]
    return pl.pallas_call(
        functools.partial(_route_kernel, n_experts=n_experts, blk=blk),
        grid=(2, t // tk),
        in_specs=[pl.BlockSpec((tk, LANES), lambda ph, ti: (ti, 0))],
        out_specs=[pl.BlockSpec((tk, LANES), lambda ph, ti: (ti * ph, 0)),
                   pl.BlockSpec((tk, LANES), lambda ph, ti: (ti * ph, 0)),
                   pl.BlockSpec((8, LANES), lambda ph, ti: (0, 0))],
        out_shape=[jax.ShapeDtypeStruct((t, LANES), jnp.int32),
                   jax.ShapeDtypeStruct((t, LANES), F32),
                   jax.ShapeDtypeStruct((8, LANES), F32)],
        scratch_shapes=[pltpu.VMEM((1, LANES), F32), pltpu.VMEM((1, LANES), F32), pltpu.VMEM((1, LANES), F32)],
        compiler_params=_cparams(("arbitrary", "arbitrary")),
        name="route",
    )(logits)


def _pad_fill_copies(b, nv_sm, zero_sc, xs_hbm, sem, blk, ns):
    nv = nv_sm[b]
    out = []
    off = b * blk + nv
    rest = blk - nv
    p = blk
    while p >= 1:
        cond = (rest & p) != 0
        out.append((cond, pltpu.make_async_copy(
            zero_sc.at[pl.ds(0, p * ns), :], xs_hbm.at[pl.ds(pl.multiple_of(off * ns, ns), p * ns), :], sem)))
        off = off + jnp.where(cond, p, 0)
        p //= 2
    return out


def _dispatch_kernel(dest_sm, nv_sm, h_hbm, xs_hbm, zero_sc, hbuf, in_sem, sem, fill_sem, *, tt, blk, nb, ns):
    i = pl.program_id(0)
    n = pl.num_programs(0)
    base = i * tt
    rows = tt * ns

    def fill(b, carry):
        for cond, cp in _pad_fill_copies(b, nv_sm, zero_sc, xs_hbm, fill_sem, blk, ns):
            pl.when(cond)(cp.start)
        return carry

    def fill_wait(b, carry):
        for cond, cp in _pad_fill_copies(b, nv_sm, zero_sc, xs_hbm, fill_sem, blk, ns):
            pl.when(cond)(cp.wait)
        return carry

    def load(step):
        slot = step % 3
        return pltpu.make_async_copy(h_hbm.at[pl.ds(pl.multiple_of(step * rows, rows), rows), :],
                                     hbuf.at[slot], in_sem.at[slot])

    @pl.when(i == 0)
    def _():
        zero_sc[...] = jnp.zeros_like(zero_sc)
        lax.fori_loop(0, nb, fill, 0)
        load(0).start()
        pl.when(n > 1)(lambda: load(1).start())

    def wait_step(step):
        for _ in range(TOP_K):
            pltpu.make_async_copy(hbuf.at[0], xs_hbm.at[pl.ds(0, rows), :], sem.at[step % 2]).wait()

    load(i).wait()
    src_tile = hbuf.at[i % 3]

    def body(t, carry):
        src = src_tile.at[pl.ds(pl.multiple_of(t * ns, ns), ns), :]
        for k in range(TOP_K):
            d = dest_sm[(base + t) * TOP_K + k]
            pltpu.make_async_copy(src, xs_hbm.at[pl.ds(pl.multiple_of(d * ns, ns), ns), :],
                                  sem.at[i % 2]).start()
        return carry

    lax.fori_loop(0, tt, body, 0)
    pl.when(i > 0)(lambda: wait_step(i - 1))
    pl.when(i + 2 < n)(lambda: load(i + 2).start())

    @pl.when(i == n - 1)
    def _():
        wait_step(i)
        lax.fori_loop(0, nb, fill_wait, 0)


def _dispatch(dest_flat, nvalid, h_rows, n_slots, tt, blk, ns):
    nb = n_slots // blk
    t = h_rows.shape[0] // ns
    return pl.pallas_call(
        functools.partial(_dispatch_kernel, tt=tt, blk=blk, nb=nb, ns=ns),
        grid_spec=pltpu.PrefetchScalarGridSpec(
            num_scalar_prefetch=2,
            grid=(t // tt,),
            in_specs=[pl.BlockSpec(memory_space=pl.ANY)],
            out_specs=pl.BlockSpec(memory_space=pl.ANY),
            scratch_shapes=[pltpu.VMEM((blk * ns, LANES), F32), pltpu.VMEM((3, tt * ns, LANES), F32),
                            pltpu.SemaphoreType.DMA((3,)), pltpu.SemaphoreType.DMA((2,)),
                            pltpu.SemaphoreType.DMA(())],
        ),
        out_shape=jax.ShapeDtypeStruct((n_slots * ns, LANES), F32),
        compiler_params=_cparams(("arbitrary",)),
        name="dispatch",
    )(dest_flat, nvalid, h_rows)


def _expert_changed(be, i, last):
    ii = jnp.minimum(i, last)
    prev = jnp.maximum(ii - 1, 0)
    return (i == 0) | (be[ii] != be[prev])


def _for_live_sub_blocks(active, n_valid, sub, o_ref, compute, rows_per_slot=1):
    n_live = jnp.where(active, (n_valid + sub - 1) // sub, 0)
    for count in range(MOE_SUB_BLOCKS + 1):
        @pl.when(n_live == count)
        def _(count=count):
            if count == MOE_SUB_BLOCKS:
                compute(slice(0, count * sub))
            else:
                for r in range(count):
                    compute(slice(r * sub, (r + 1) * sub))
            if count < MOE_SUB_BLOCKS:
                first = count * sub * rows_per_slot
                o_ref[first:, :] = jnp.zeros((o_ref.shape[0] - first, o_ref.shape[1]), o_ref.dtype)


def _stream_expert_weights(changed, first, prefetch, wait_cur, cast, start_next):
    @pl.when(changed)
    def _():
        pl.when(first)(lambda: start_next(True))
        wait_cur()
        cast()
        pl.when(prefetch)(lambda: start_next(False))


def _gateup_kernel(be, nv, nu, nxt, x_ref, w_hbm, bg_ref, bu_ref, o_ref, wbuf, wg_sc, wu_sc, sem):
    j = pl.program_id(0)
    i = pl.program_id(1)
    nj = pl.num_programs(0)
    th = wg_sc.shape[1]
    dff = w_hbm.shape[2] // 2
    last = nu[0] - 1
    active = i < nu[0]
    ii = jnp.minimum(i, last)

    def copies(e, jj):
        col = pl.multiple_of(jj * th, th)
        return (pltpu.make_async_copy(w_hbm.at[e, :, pl.ds(col, th)], wbuf.at[0], sem.at[0]),
                pltpu.make_async_copy(w_hbm.at[e, :, pl.ds(dff + col, th)], wbuf.at[1], sem.at[1]))

    nx = nxt[ii]
    same_pass = nx >= 0
    e_next = jnp.where(same_pass, be[jnp.maximum(nx, 0)], be[0])
    j_next = jnp.where(same_pass, j, j + 1)

    def start_next(current):
        for cp in (copies(be[ii], j) if current else copies(e_next, j_next)):
            cp.start()

    def wait_cur():
        for cp in copies(be[ii], j):
            cp.wait()

    def cast():
        def chunk(r, carry):
            rows = pl.ds(pl.multiple_of(r * CAST_ROWS, CAST_ROWS), CAST_ROWS)
            wg_sc[rows, :] = wbuf[0, rows, :].astype(BF16)
            wu_sc[rows, :] = wbuf[1, rows, :].astype(BF16)
            return carry

        lax.fori_loop(0, wg_sc.shape[0] // CAST_ROWS, chunk, 0)

    _stream_expert_weights(active & _expert_changed(be, i, last), (j == 0) & (i == 0),
                           same_pass | (j + 1 < nj), wait_cur, cast, start_next)

    sub = o_ref.shape[0] // MOE_SUB_BLOCKS

    def compute(rows):
        x = _load_token_rows(x_ref, rows.start, rows.stop - rows.start, wg_sc.shape[0] // LANES).astype(BF16)
        g = jnp.dot(x, wg_sc[...], preferred_element_type=F32) + bg_ref[0]
        up = jnp.dot(x, wu_sc[...], preferred_element_type=F32) + bu_ref[0]
        gate = jnp.minimum(g, SWIGLU_LIMIT)
        up = jnp.clip(up, -SWIGLU_LIMIT, SWIGLU_LIMIT)
        o_ref[rows, :] = ((up + 1.0) * gate * _sigmoid(SWIGLU_ALPHA * gate)).astype(o_ref.dtype)

    _for_live_sub_blocks(active, nv[i], sub, o_ref, compute)


def _gateup(block_e, nvalid, nused, nxt, xs, w_gu, b_gu3, tm, th):
    d = w_gu.shape[1]
    ns = d // LANES
    n_slots = xs.shape[0] // ns
    dff = w_gu.shape[2] // 2
    nj = dff // th
    nb = n_slots // tm

    def blk(i, nu):
        return jnp.minimum(i, nu[0] - 1)

    return pl.pallas_call(
        _gateup_kernel,
        grid_spec=pltpu.PrefetchScalarGridSpec(
            num_scalar_prefetch=4,
            grid=(nj, nb),
            in_specs=[pl.BlockSpec((tm * ns, LANES), lambda j, i, be, nv, nu, nx: (blk(i, nu), 0)),
                      pl.BlockSpec(memory_space=pl.ANY),
                      pl.BlockSpec((1, 1, th), lambda j, i, be, nv, nu, nx: (be[blk(i, nu)], 0, j)),
                      pl.BlockSpec((1, 1, th), lambda j, i, be, nv, nu, nx: (be[blk(i, nu)], 0, nj + j))],
            out_specs=pl.BlockSpec((tm, th), lambda j, i, be, nv, nu, nx: (i, j)),
            scratch_shapes=[pltpu.VMEM((2, d, th), F32), pltpu.VMEM((d, th), BF16), pltpu.VMEM((d, th), BF16),
                            pltpu.SemaphoreType.DMA((2,))],
        ),
        out_shape=jax.ShapeDtypeStruct((n_slots, dff), BF16),
        compiler_params=_cparams(("arbitrary", "arbitrary")),
        name="expert_gate_up",
    )(block_e, nvalid, nused, nxt, xs, w_gu, b_gu3, b_gu3)


def _down_kernel(be, nv, nu, nxt, a_ref, w_hbm, bd_ref, o_ref, wbuf, wd_sc, sem):
    i = pl.program_id(1)
    last = nu[0] - 1
    active = i < nu[0]
    ii = jnp.minimum(i, last)
    nx = nxt[ii]

    def copy(e):
        return pltpu.make_async_copy(w_hbm.at[e], wbuf, sem)

    def start_next(current):
        copy(be[ii] if current else be[jnp.maximum(nx, 0)]).start()

    def cast():
        def chunk(r, carry):
            rows = pl.ds(pl.multiple_of(r * CAST_ROWS, CAST_ROWS), CAST_ROWS)
            wd_sc[rows, :] = wbuf[rows, :].astype(BF16)
            return carry

        lax.fori_loop(0, wd_sc.shape[0] // CAST_ROWS, chunk, 0)

    _stream_expert_weights(active & _expert_changed(be, i, last), i == 0, nx >= 0,
                           lambda: copy(be[ii]).wait(), cast, start_next)

    sub = a_ref.shape[0] // MOE_SUB_BLOCKS
    ns = o_ref.shape[0] // a_ref.shape[0]

    def compute(rows):
        y = jnp.dot(a_ref[rows, :], wd_sc[...], preferred_element_type=F32) + bd_ref[0]
        _store_token_rows(o_ref, rows.start, y)

    _for_live_sub_blocks(active, nv[i], sub, o_ref, compute, rows_per_slot=ns)


def _down(block_e, nvalid, nused, nxt, act, w_d, b_d3, tm):
    n_slots, dff = act.shape
    d = w_d.shape[2]
    ns = d // LANES
    nb = n_slots // tm

    def blk(i, nu):
        return jnp.minimum(i, nu[0] - 1)

    return pl.pallas_call(
        _down_kernel,
        grid_spec=pltpu.PrefetchScalarGridSpec(
            num_scalar_prefetch=4,
            grid=(1, nb),
            in_specs=[pl.BlockSpec((tm, dff), lambda j, i, be, nv, nu, nx: (blk(i, nu), 0)),
                      pl.BlockSpec(memory_space=pl.ANY),
                      pl.BlockSpec((1, 1, d), lambda j, i, be, nv, nu, nx: (be[blk(i, nu)], 0, 0))],
            out_specs=pl.BlockSpec((tm * ns, LANES), lambda j, i, be, nv, nu, nx: (i, 0)),
            scratch_shapes=[pltpu.VMEM((dff, d), F32), pltpu.VMEM((dff, d), BF16), pltpu.SemaphoreType.DMA(())],
        ),
        out_shape=jax.ShapeDtypeStruct((n_slots * ns, LANES), F32),
        compiler_params=_cparams(("arbitrary", "arbitrary")),
        name="expert_down",
    )(block_e, nvalid, nused, nxt, act, w_d, b_d3)


def _combine_kernel(dest_sm, y_hbm, w_ref, x1_ref, gt_ref, g_ref, o_ref, buf, sem, *, tt, ns):
    i = pl.program_id(0)
    n = pl.num_programs(0)

    def gather(tile, slot):
        def body(t, carry):
            for k in range(TOP_K):
                d = dest_sm[(tile * tt + t) * TOP_K + k]
                pltpu.make_async_copy(y_hbm.at[pl.ds(pl.multiple_of(d * ns, ns), ns), :],
                                      buf.at[slot, pl.ds(pl.multiple_of((k * tt + t) * ns, ns), ns), :],
                                      sem.at[slot]).start()
            return carry

        lax.fori_loop(0, tt, body, 0)

    slot = i % 2

    @pl.when(i == 0)
    def _():
        gather(0, 0)

    @pl.when(i + 1 < n)
    def _():
        gather(i + 1, 1 - slot)

    pltpu.make_async_copy(y_hbm.at[pl.ds(0, TOP_K * tt * ns), :], buf.at[slot], sem.at[slot]).wait()
    w = w_ref[...]
    rows = buf.at[slot]
    f = None
    for k in range(TOP_K):
        yk = _load_token_rows(rows, k * tt, tt, ns) * w[:, k:k + 1]
        f = yk if f is None else f + yk
    nf = f * lax.rsqrt(jnp.mean(f * f, axis=-1, keepdims=True) + NORM_EPS) * g_ref[...]
    o_ref[...] = x1_ref[...] + gt_ref[0] * nf


def _combine(dest_flat, y_rows, wts, x1, mod3, gpost, seq, tt, gt_blk):
    t, d = x1.shape
    ns = d // LANES
    per_b = seq // tt
    return pl.pallas_call(
        functools.partial(_combine_kernel, tt=tt, ns=ns),
        grid_spec=pltpu.PrefetchScalarGridSpec(
            num_scalar_prefetch=1,
            grid=(t // tt,),
            in_specs=[pl.BlockSpec(memory_space=pl.ANY),
                      pl.BlockSpec((tt, LANES), lambda i, ds: (i, 0)),
                      pl.BlockSpec((tt, d), lambda i, ds: (i, 0)),
                      pl.BlockSpec((1, 1, d), lambda i, ds: (i // per_b, 0, gt_blk)),
                      pl.BlockSpec((1, d), lambda i, ds: (0, 0))],
            out_specs=pl.BlockSpec((tt, d), lambda i, ds: (i, 0)),
            scratch_shapes=[pltpu.VMEM((2, TOP_K * tt * ns, LANES), F32), pltpu.SemaphoreType.DMA((2,))],
        ),
        out_shape=jax.ShapeDtypeStruct((t, d), F32),
        compiler_params=_cparams(("arbitrary",)),
        name="combine",
    )(dest_flat, y_rows, wts, x1, mod3, gpost)


def _tile(n, pref):
    t = min(n, pref)
    while n % t:
        t //= 2
    return t


def _rope_tables(seq):
    half = ROT_DIM // 2
    inv = ROPE_THETA ** (-jnp.arange(0, ROT_DIM, 2, dtype=F32) / ROT_DIM)
    ang = jnp.arange(seq, dtype=F32)[:, None] * inv[None, :]
    cos, sin = jnp.cos(ang), jnp.sin(ang)
    ones = jnp.ones((seq, ATTN_DK - ROT_DIM), F32)
    zeros = jnp.zeros((seq, ATTN_DK - ROT_DIM), F32)
    zh = jnp.zeros((seq, half), F32)
    c64 = jnp.concatenate([cos, cos, ones], axis=1)
    s1_64 = jnp.concatenate([-sin, zh, zeros], axis=1)
    s2_64 = jnp.concatenate([zh, sin, zeros], axis=1)
    rep = LANES // ATTN_DK
    return tuple(jnp.tile(t, (1, rep)) for t in (c64, s1_64, s2_64))


def _layer(x, c_pad, l, p, moe_blk):
    bsz, seq, d = x.shape
    t = bsz * seq
    dh = d
    n_sheads = dh // SSD_HEADDIM
    gn = SSD_GROUPS * SSD_STATE
    n_aheads = d // ATTN_DV
    aw = n_aheads * ATTN_DV
    qkw = 2 * n_aheads * ATTN_DK
    n_experts = p["w_router"].shape[-1]

    mod = _ada(c_pad, p["w_ada"][l], p["b_ada"][l][None, :], _tile(6 * d, 1024))
    mod3 = mod[:bsz].reshape(bsz, 1, 6 * d)

    o = 0
    segs = {}
    for name, size in (("z", dh), ("xbc", dh + 2 * gn), ("dt", n_sheads), ("q", qkw), ("k", qkw),
                       ("v", aw), ("gs", d), ("ga", d)):
        segs[name] = (o, o + size)
        o += size
    scale = ATTN_DK ** -0.5 * math.log2(math.e)
    w_plain, w_rope, w_gate, w_dt = _regroup(jnp.swapaxes(p["w_in"][l], 0, 1), segs, scale)

    h = _prenorm(x, p["g_pre_mix"][l][None, :], mod3, 1, 0, _tile(seq, 512)).reshape(t, d)
    tm = _tile(seq, 1024)
    plain = _mm(h, w_plain, BF16, tm, _tile(w_plain.shape[1], 1024), name="proj_plain")
    qk = _mm(h, w_rope, BF16, tm, _tile(w_rope.shape[1], 1024), "rope", _rope_tables(seq), name="proj_rope")
    gates = _mm(h, w_gate, BF16, tm, _tile(w_gate.shape[1], 1024), "sigmoid", name="proj_gate")
    dt_raw = _mm(h, w_dt, F32, tm, LANES, name="proj_dt")

    pad_h = lambda v: jnp.pad(v, (0, LANES - n_sheads))[None, :]
    plain3 = plain.reshape(bsz, seq, plain.shape[1])
    y_ssd = _ssd(plain3, dt_raw.reshape(bsz, seq, LANES), p["conv_w"][l], p["conv_b"][l][None, :],
                 pad_h(p["dt_bias"][l]), pad_h(p["a_log"][l]),
                 jnp.repeat(p["d_skip"][l], SSD_HEADDIM)[None, :], p["ssd_norm_w"][l][None, :], dh, n_sheads)

    lam_init = 0.8 - 0.6 * math.exp(-0.3 * l)
    lam_rows = jnp.zeros((8, LANES), F32)
    for r, nm in enumerate(("lambda_q1", "lambda_k1", "lambda_q2", "lambda_k2")):
        lam_rows = lam_rows.at[r, :ATTN_DK].set(p[nm][l])
    v_blk0 = (dh + dh + 2 * gn) // LANES
    o_attn = _attention(qk.reshape(bsz, seq, 2 * qkw), plain3, lam_rows, p["subln_w"][l][None, :],
                        n_aheads, v_blk0, _tile(seq, ATTN_Q_TILE), lam_init)

    tm2 = _tile(seq, 512)
    merged = _merge(y_ssd.reshape(t, dh), o_attn.reshape(t, aw), p["w_br_ssd"][l].astype(BF16),
                    p["w_br_attn"][l].astype(BF16), gates, tm, _tile(d, 1024))
    wr = jnp.pad(p["w_router"][l], ((0, 0), (0, LANES - n_experts)))
    wr_hi = wr.astype(BF16)
    wr_lo = (wr - wr_hi.astype(F32)).astype(BF16)
    br = jnp.pad(p["b_router"][l], (0, LANES - n_experts))[None, :]
    x1, h_rows, logits = _outproj(merged, p["w_out"][l].astype(BF16), x.reshape(t, d),
                                    p["g_post_mix"][l][None, :], mod3, p["g_pre_ffn"][l][None, :],
                                    wr_hi, wr_lo, br, seq, tm2, 2, 4, 3)

    dest, wts, cnt = _route(logits, n_experts, moe_blk, _tile(t, 512))
    counts = cnt[0, :n_experts].astype(jnp.int32)
    n_slots = t * TOP_K + n_experts * moe_blk
    nb = n_slots // moe_blk
    pblocks = (counts + moe_blk - 1) // moe_blk
    pend = jnp.cumsum(pblocks)
    nused = jnp.maximum(pend[-1], 1).astype(jnp.int32)
    bidx = jnp.arange(nb, dtype=jnp.int32)
    block_e = jnp.minimum(jnp.sum((pend[None, :] <= bidx[:, None]).astype(jnp.int32), axis=1), n_experts - 1)
    pstart = pend - pblocks
    nvalid = jnp.clip(counts[block_e] - (bidx - pstart[block_e]) * moe_blk, 0, moe_blk).astype(jnp.int32)
    dest_flat = dest[:, :TOP_K].reshape(-1)
    nused1 = nused.reshape(1)
    after = pend[block_e].astype(jnp.int32)
    nxt = jnp.where(after < nused, after, -1).astype(jnp.int32)

    xs = _dispatch(dest_flat, nvalid, h_rows, n_slots, _tile(t, 256), moe_blk, d // LANES)
    dff = p["w_down"].shape[2]
    act = _gateup(block_e, nvalid, nused1, nxt, xs, p["w_gate_up"][l], p["b_gate_up"][l][:, None, :],
                  moe_blk, _tile(dff, 1024))
    y_sorted = _down(block_e, nvalid, nused1, nxt, act, p["w_down"][l], p["b_down"][l][:, None, :], moe_blk)
    out = _combine(dest_flat, y_sorted, wts, x1, mod3, p["g_post_ffn"][l][None, :], seq, _tile(seq, 128), 5)
    return out.reshape(bsz, seq, d)


MOE_ROW_BLOCK = 512
MM_ROW_CHUNKS = 2
CAST_ROWS = 128
MOE_SUB_BLOCKS = 2


def kernel(x, c, w_ada, b_ada, g_pre_mix, g_post_mix, g_pre_ffn, g_post_ffn, w_in, conv_w, conv_b, dt_bias, a_log, d_skip, ssd_norm_w, lambda_q1, lambda_k1, lambda_q2, lambda_k2, subln_w, w_br_ssd, w_br_attn, w_out, w_router, b_router, w_gate_up, b_gate_up, w_down, b_down):
    p = dict(w_ada=w_ada, b_ada=b_ada, g_pre_mix=g_pre_mix, g_post_mix=g_post_mix, g_pre_ffn=g_pre_ffn,
             g_post_ffn=g_post_ffn, w_in=w_in, conv_w=conv_w, conv_b=conv_b, dt_bias=dt_bias, a_log=a_log,
             d_skip=d_skip, ssd_norm_w=ssd_norm_w, lambda_q1=lambda_q1, lambda_k1=lambda_k1,
             lambda_q2=lambda_q2, lambda_k2=lambda_k2, subln_w=subln_w, w_br_ssd=w_br_ssd,
             w_br_attn=w_br_attn, w_out=w_out, w_router=w_router, b_router=b_router, w_gate_up=w_gate_up,
             b_gate_up=b_gate_up, w_down=w_down, b_down=b_down)
    bsz = x.shape[0]
    c_pad = jnp.pad(c, ((0, (-bsz) % 8), (0, 0)))
    for l in range(w_ada.shape[0]):
        x = _layer(x, c_pad, l, p, min(MOE_ROW_BLOCK, x.shape[0] * x.shape[1]))
    return x
```

```python
import functools
import math

import jax
import jax.numpy as jnp
from jax import lax
from jax.experimental import pallas as pl
from jax.experimental.pallas import tpu as pltpu

F32 = jnp.float32
BF16 = jnp.bfloat16
HIGHEST = lax.Precision.HIGHEST

SSD_HEADDIM = 64
SSD_GROUPS = 4
SSD_STATE = 128
SSD_CONV = 4
SSD_CHUNK = 128
ATTN_DK = 64
ATTN_DV = 128
ATTN_KV_UNIT = 512
ATTN_Q_TILE = 512
ATTN_HEADS_PER_STEP = 2
ROT_DIM = ATTN_DK // 4
ROPE_THETA = 500000.0
TOP_K = 4
SWIGLU_LIMIT = 7.0
SWIGLU_ALPHA = 1.702
NORM_EPS = 1e-6
SUB_EPS = 1e-5
LANES = 128
NEG_BIG = -1e30

VMEM_LIMIT = 56 * 1024 * 1024


def _cparams(sem, vmem=VMEM_LIMIT):
    return pltpu.CompilerParams(dimension_semantics=sem, vmem_limit_bytes=vmem)


def _sigmoid(x):
    return 1.0 / (1.0 + jnp.exp(-x))


def _store_token_rows(ref, first_tok, val):
    n_tok, width = val.shape
    ns = width // LANES
    for s_ in range(ns):
        ref[pl.ds(first_tok * ns + s_, n_tok, stride=ns), :] = val[:, s_ * LANES:(s_ + 1) * LANES]


def _load_token_rows(ref, first_tok, n_tok, ns):
    return jnp.concatenate([ref[pl.ds(first_tok * ns + s_, n_tok, stride=ns), :] for s_ in range(ns)], axis=1)


def _ada_kernel(c_ref, w_ref, b_ref, o_ref):
    c = c_ref[...]
    sc = c * _sigmoid(c)
    o_ref[...] = jnp.dot(sc, w_ref[...], preferred_element_type=F32, precision=HIGHEST) + b_ref[...]


def _ada(c_pad, w, b, tn):
    rows, d = c_pad.shape
    n = w.shape[1]
    return pl.pallas_call(
        _ada_kernel,
        grid=(n // tn,),
        in_specs=[pl.BlockSpec((rows, d), lambda j: (0, 0)),
                  pl.BlockSpec((d, tn), lambda j: (0, j)),
                  pl.BlockSpec((1, tn), lambda j: (0, j))],
        out_specs=pl.BlockSpec((rows, tn), lambda j: (0, j)),
        out_shape=jax.ShapeDtypeStruct((rows, n), F32),
        compiler_params=_cparams(("arbitrary",)),
        name="ada",
    )(c_pad, w, b)


def _prenorm_kernel(x_ref, g_ref, sc_ref, sh_ref, o_ref):
    x = x_ref[0]
    y = x * lax.rsqrt(jnp.mean(x * x, axis=-1, keepdims=True) + NORM_EPS) * g_ref[...]
    o_ref[0] = (y * (1.0 + sc_ref[0]) + sh_ref[0]).astype(o_ref.dtype)


def _prenorm(x, g, mod3, sc_blk, sh_blk, ts):
    b, s, d = x.shape
    return pl.pallas_call(
        _prenorm_kernel,
        grid=(b, s // ts),
        in_specs=[pl.BlockSpec((1, ts, d), lambda bi, si: (bi, si, 0)),
                  pl.BlockSpec((1, d), lambda bi, si: (0, 0)),
                  pl.BlockSpec((1, 1, d), lambda bi, si: (bi, 0, sc_blk)),
                  pl.BlockSpec((1, 1, d), lambda bi, si: (bi, 0, sh_blk))],
        out_specs=pl.BlockSpec((1, ts, d), lambda bi, si: (bi, si, 0)),
        out_shape=jax.ShapeDtypeStruct((b, s, d), BF16),
        compiler_params=_cparams(("arbitrary", "arbitrary")),
        name="prenorm",
    )(x, g, mod3, mod3)


REGROUP_TILE = 512


def _regroup_kernel(src_sm, wt_hbm, plain_ref, rope_ref, gate_ref, dt_ref, buf, dt_buf, sem, dt_sem, *,
                    tiles, dt_rows, q_scale):
    t = pl.program_id(0)
    n = pl.num_programs(0)
    tr = REGROUP_TILE
    n_plain, n_rope, n_gate, n_q = tiles

    def load(step):
        row = pl.multiple_of(src_sm[step], 8)
        return pltpu.make_async_copy(wt_hbm.at[pl.ds(row, tr), :], buf.at[step % 2], sem.at[step % 2])

    dt_copy = pltpu.make_async_copy(wt_hbm.at[pl.ds(dt_rows[0], dt_rows[1]), :], dt_buf, dt_sem)

    @pl.when(t == 0)
    def _():
        load(0).start()
        dt_copy.start()

    pl.when(t + 1 < n)(lambda: load(t + 1).start())
    load(t).wait()
    cols = buf[t % 2].T

    @pl.when(t < n_plain)
    def _():
        plain_ref[...] = cols.astype(BF16)

    @pl.when((t >= n_plain) & (t < n_plain + n_rope))
    def _():
        rope_ref[...] = jnp.where(t < n_plain + n_q, cols * q_scale, cols).astype(BF16)

    @pl.when(t >= n_plain + n_rope)
    def _():
        gate_ref[...] = cols.astype(BF16)

    @pl.when(t == n - 1)
    def _():
        dt_copy.wait()
        dt_ref[...] = jnp.zeros_like(dt_ref)
        dt_ref[:, :dt_rows[1]] = dt_buf[...].T.astype(BF16)


def _regroup(w_in_t, segs, q_scale):
    n_in, d = w_in_t.shape
    tr = REGROUP_TILE
    starts = []
    counts = []
    for group in (("z", "xbc", "v"), ("q", "k"), ("gs", "ga")):
        c = 0
        for nm in group:
            lo, hi = segs[nm]
            assert (hi - lo) % tr == 0 and lo % 8 == 0
            starts += list(range(lo, hi, tr))
            c += (hi - lo) // tr
        counts.append(c)
    n_plain, n_rope, n_gate = counts
    n_q = (segs["q"][1] - segs["q"][0]) // tr
    dt_lo, dt_hi = segs["dt"]
    clamp = lambda t, first, cnt: jnp.clip(t - first, 0, cnt - 1)
    return pl.pallas_call(
        functools.partial(_regroup_kernel, tiles=(n_plain, n_rope, n_gate, n_q), dt_rows=(dt_lo, dt_hi - dt_lo),
                          q_scale=q_scale),
        grid_spec=pltpu.PrefetchScalarGridSpec(
            num_scalar_prefetch=1,
            grid=(len(starts),),
            in_specs=[pl.BlockSpec(memory_space=pl.ANY)],
            out_specs=[pl.BlockSpec((d, tr), lambda t, src: (0, clamp(t, 0, n_plain))),
                       pl.BlockSpec((d, tr), lambda t, src: (0, clamp(t, n_plain, n_rope))),
                       pl.BlockSpec((d, tr), lambda t, src: (0, clamp(t, n_plain + n_rope, n_gate))),
                       pl.BlockSpec((d, LANES), lambda t, src: (0, 0))],
            scratch_shapes=[pltpu.VMEM((2, tr, d), F32), pltpu.VMEM((dt_hi - dt_lo, d), F32),
                            pltpu.SemaphoreType.DMA((2,)), pltpu.SemaphoreType.DMA(())],
        ),
        out_shape=[jax.ShapeDtypeStruct((d, n_plain * tr), BF16), jax.ShapeDtypeStruct((d, n_rope * tr), BF16),
                   jax.ShapeDtypeStruct((d, n_gate * tr), BF16), jax.ShapeDtypeStruct((d, LANES), BF16)],
        compiler_params=_cparams(("arbitrary",)),
        name="regroup_w_in",
    )(jnp.asarray(starts, jnp.int32), w_in_t)


def _mm_kernel(a_ref, w_ref, o_ref):
    o_ref[...] = jnp.dot(a_ref[...], w_ref[...], preferred_element_type=F32).astype(o_ref.dtype)


def _row_chunks(n_rows, chunks):
    step = n_rows // chunks if n_rows % chunks == 0 and n_rows >= 64 * chunks else n_rows
    return [slice(r, r + step) for r in range(0, n_rows, step)]


def _mm_sigmoid_kernel(a_ref, w_ref, o_ref):
    for rows in _row_chunks(a_ref.shape[0], EPILOGUE_ROW_CHUNKS):
        acc = jnp.dot(a_ref[rows, :], w_ref[...], preferred_element_type=F32)
        o_ref[rows, :] = _sigmoid(acc).astype(o_ref.dtype)


def _mm_rope_kernel(a_ref, w_ref, c_ref, s1_ref, s2_ref, o_ref):
    for rows in _row_chunks(a_ref.shape[0], EPILOGUE_ROW_CHUNKS):
        acc = jnp.dot(a_ref[rows, :], w_ref[...], preferred_element_type=F32)
        c = c_ref[rows, :]
        s1 = s1_ref[rows, :]
        s2 = s2_ref[rows, :]
        for g in range(acc.shape[1] // LANES):
            blk = acc[:, g * LANES:(g + 1) * LANES]
            fwd = pltpu.roll(blk, LANES - ROT_DIM // 2, 1)
            bwd = pltpu.roll(blk, ROT_DIM // 2, 1)
            o_ref[rows, g * LANES:(g + 1) * LANES] = (blk * c + fwd * s1 + bwd * s2).astype(o_ref.dtype)


def _mm(a, w, out_dtype, tm, tn, epilogue="none", rope_tabs=None, name="mm"):
    m, k = a.shape
    n = w.shape[1]
    in_specs = [pl.BlockSpec((tm, k), lambda i, j: (i, 0)),
                pl.BlockSpec((k, tn), lambda i, j: (0, j))]
    args = [a, w]
    if epilogue == "rope":
        seq_blocks = rope_tabs[0].shape[0] // tm
        for t in rope_tabs:
            in_specs.append(pl.BlockSpec((tm, LANES), lambda i, j: (i % seq_blocks, 0)))
            args.append(t)
        kern = _mm_rope_kernel
    elif epilogue == "sigmoid":
        kern = _mm_sigmoid_kernel
    else:
        kern = _mm_kernel
    return pl.pallas_call(
        kern,
        grid=(m // tm, n // tn),
        in_specs=in_specs,
        out_specs=pl.BlockSpec((tm, tn), lambda i, j: (i, j)),
        out_shape=jax.ShapeDtypeStruct((m, n), out_dtype),
        compiler_params=_cparams(("arbitrary", "arbitrary")),
        name=name,
    )(*args)


def _ssd_kernel(z_ref, xs_ref, bm_ref, cm_ref, dt_ref, cwx_ref, cwb_ref, cwc_ref, cbx_ref, cbb_ref,
                cbc_ref, dtb_ref, alog_ref, dsk_ref, nw_ref, o_ref,
                px_sc, pb_sc, pc_sc, st_sc, y_sc, *, n_heads):
    c = pl.program_id(1)
    L = SSD_CHUNK
    N = SSD_STATE
    hpg = n_heads // SSD_GROUPS
    gw = hpg * SSD_HEADDIM

    @pl.when(c == 0)
    def _():
        px_sc[...] = jnp.zeros_like(px_sc)
        pb_sc[...] = jnp.zeros_like(pb_sc)
        pc_sc[...] = jnp.zeros_like(pc_sc)
        st_sc[...] = jnp.zeros_like(st_sc)

    def conv_silu(u, prev_sc, w_ref, b_ref):
        prev = prev_sc[...]
        row = lax.broadcasted_iota(jnp.int32, u.shape, 0)
        acc = u * w_ref[SSD_CONV - 1:SSD_CONV, :] + b_ref[...]
        for j in range(1, SSD_CONV):
            sh = jnp.where(row < j, pltpu.roll(prev, j, 0), pltpu.roll(u, j, 0))
            acc = acc + sh * w_ref[SSD_CONV - 1 - j:SSD_CONV - j, :]
        prev_sc[...] = u
        return acc * _sigmoid(acc)

    xs = conv_silu(xs_ref[0].astype(F32), px_sc, cwx_ref, cbx_ref)
    bm = conv_silu(bm_ref[0].astype(F32), pb_sc, cwb_ref, cbb_ref)
    cm = conv_silu(cm_ref[0].astype(F32), pc_sc, cwc_ref, cbc_ref)

    raw = dt_ref[0] + dtb_ref[...]
    dt = jnp.maximum(raw, 0.0) + jnp.log1p(jnp.exp(-jnp.abs(raw)))
    a = dt * (-jnp.exp(alog_ref[...]))
    r_i = lax.broadcasted_iota(jnp.int32, (L, L), 0)
    c_i = lax.broadcasted_iota(jnp.int32, (L, L), 1)
    causal = r_i >= c_i
    tri = jnp.where(causal, 1.0, 0.0).astype(F32)
    a_cs = jnp.dot(tri, a, preferred_element_type=F32, precision=HIGHEST)
    a_cs_t = a_cs.T

    lane = lax.broadcasted_iota(jnp.int32, (L, LANES), 1)
    first = lane < SSD_HEADDIM
    acs_tiles = []
    dt_tiles = []
    for j in range(n_heads // 2):
        h0, h1 = 2 * j, 2 * j + 1
        acs_tiles.append(jnp.where(first, a_cs[:, h0:h0 + 1], a_cs[:, h1:h1 + 1]))
        dt_tiles.append(jnp.where(first, dt[:, h0:h0 + 1], dt[:, h1:h1 + 1]))
    acs_e = jnp.concatenate(acs_tiles, axis=1)
    dt_e = jnp.concatenate(dt_tiles, axis=1)
    xdt = xs * dt_e
    ea = jnp.exp(acs_e)
    alast = acs_e[L - 1:L, :]
    xdec = (xdt * jnp.exp(alast - acs_e)).astype(BF16)
    ealast = jnp.exp(alast)

    for g in range(SSD_GROUPS):
        bg = bm[:, g * N:(g + 1) * N]
        cg = cm[:, g * N:(g + 1) * N].astype(BF16)
        scores = lax.dot_general(cg, bg.astype(BF16), (((1,), (1,)), ((), ())),
                                 preferred_element_type=F32)
        st_old = st_sc[:, g * gw:(g + 1) * gw]
        y_off = jnp.dot(cg, st_old.astype(BF16), preferred_element_type=F32) * ea[:, g * gw:(g + 1) * gw]
        st_sc[:, g * gw:(g + 1) * gw] = st_old * ealast[:, g * gw:(g + 1) * gw] + jnp.dot(
            bg.T.astype(BF16), xdec[:, g * gw:(g + 1) * gw], preferred_element_type=F32)
        for jj in range(hpg // 2):
            j = g * (hpg // 2) + jj
            h0, h1 = 2 * j, 2 * j + 1
            la = jnp.exp(jnp.where(causal, a_cs[:, h0:h0 + 1] - a_cs_t[h0:h0 + 1, :], NEG_BIG))
            lb = jnp.exp(jnp.where(causal, a_cs[:, h1:h1 + 1] - a_cs_t[h1:h1 + 1, :], NEG_BIG))
            mcat = jnp.concatenate([(scores * la).astype(BF16), (scores * lb).astype(BF16)], axis=1)
            xp = xdt[:, j * LANES:(j + 1) * LANES]
            xcat = jnp.concatenate([jnp.where(first, xp, 0.0).astype(BF16),
                                    jnp.where(first, 0.0, xp).astype(BF16)], axis=0)
            y_diag = jnp.dot(mcat, xcat, preferred_element_type=F32)
            lo = jj * LANES
            y_sc[:, j * LANES:(j + 1) * LANES] = (
                y_diag + y_off[:, lo:lo + LANES]
                + dsk_ref[:, j * LANES:(j + 1) * LANES] * xs[:, j * LANES:(j + 1) * LANES])

    z = z_ref[0].astype(F32)
    u = y_sc[...] * (z * _sigmoid(z))
    for g in range(SSD_GROUPS):
        ug = u[:, g * gw:(g + 1) * gw]
        ms = jnp.mean(ug * ug, axis=-1, keepdims=True)
        o_ref[0, :, g * gw:(g + 1) * gw] = (ug * lax.rsqrt(ms + SUB_EPS)
                                            * nw_ref[:, g * gw:(g + 1) * gw]).astype(o_ref.dtype)


def _ssd(plain3, dt3, conv_w, conv_b, dtb, alog, dsk_e, norm_w, dh, n_heads):
    b, s, _ = plain3.shape
    L = SSD_CHUNK
    gn = SSD_GROUPS * SSD_STATE
    nc = s // L
    xblk = 1
    bblk = (2 * dh) // gn
    cw_x, cw_b, cw_c = conv_w[:, :dh], conv_w[:, dh:dh + gn], conv_w[:, dh + gn:]
    cb_x, cb_b, cb_c = conv_b[:, :dh], conv_b[:, dh:dh + gn], conv_b[:, dh + gn:]
    full = lambda shape: pl.BlockSpec(shape, lambda bi, ci: (0, 0))
    return pl.pallas_call(
        functools.partial(_ssd_kernel, n_heads=n_heads),
        grid=(b, nc),
        in_specs=[pl.BlockSpec((1, L, dh), lambda bi, ci: (bi, ci, 0)),
                  pl.BlockSpec((1, L, dh), lambda bi, ci: (bi, ci, xblk)),
                  pl.BlockSpec((1, L, gn), lambda bi, ci: (bi, ci, bblk)),
                  pl.BlockSpec((1, L, gn), lambda bi, ci: (bi, ci, bblk + 1)),
                  pl.BlockSpec((1, L, LANES), lambda bi, ci: (bi, ci, 0)),
                  full((SSD_CONV, dh)), full((SSD_CONV, gn)), full((SSD_CONV, gn)),
                  full((1, dh)), full((1, gn)), full((1, gn)),
                  full((1, LANES)), full((1, LANES)), full((1, dh)), full((1, dh))],
        out_specs=pl.BlockSpec((1, L, dh), lambda bi, ci: (bi, ci, 0)),
        out_shape=jax.ShapeDtypeStruct((b, s, dh), BF16),
        scratch_shapes=[pltpu.VMEM((L, dh), F32), pltpu.VMEM((L, gn), F32), pltpu.VMEM((L, gn), F32),
                        pltpu.VMEM((SSD_STATE, dh), F32), pltpu.VMEM((L, dh), F32)],
        compiler_params=_cparams(("arbitrary", "arbitrary")),
        name="ssd",
    )(plain3, plain3, plain3, plain3, dt3, cw_x, cw_b, cw_c, cb_x, cb_b, cb_c, dtb, alog, dsk_e, norm_w)


def _attn_kernel(q_ref, k_ref, v_ref, bias_ref, lam_ref, sw_ref, o_ref, vt_sc, st_a, st_b, m_sc, l_sc, acc_sc, *,
                 tq, lam_init):
    qi = pl.program_id(2)
    tu = ATTN_KV_UNIT
    nh = ATTN_HEADS_PER_STEP
    n_all = k_ref.shape[1] // tu
    heads = [slice(hh * LANES, (hh + 1) * LANES) for hh in range(nh)]

    @pl.when(qi == 0)
    def _():
        def transpose_block(c, carry):
            start = pl.multiple_of(c * tu, tu)
            for hh in range(nh):
                vt_sc[hh, c] = v_ref[0, pl.ds(start, tu), heads[hh]].astype(F32).T.astype(BF16)
            return carry

        lax.fori_loop(0, n_all, transpose_block, 0)

    qts = []
    for hh in range(nh):
        qt = q_ref[0, :, heads[hh]].astype(F32).T
        row = lax.broadcasted_iota(jnp.int32, qt.shape, 0)
        qts.append((jnp.where(row < ATTN_DK, qt, 0.0).astype(BF16),
                    jnp.where(row < ATTN_DK, 0.0, qt).astype(BF16)))
    m_sc[...] = jnp.full_like(m_sc, NEG_BIG)
    l_sc[...] = jnp.zeros_like(l_sc)
    acc_sc[...] = jnp.zeros_like(acc_sc)

    def scores(u, st_ref):
        start = pl.multiple_of(u * tu, tu)
        for hh in range(nh):
            k = k_ref[0, pl.ds(start, tu), heads[hh]]
            for m in range(2):
                st_ref[2 * hh + m] = jnp.dot(k, qts[hh][m], preferred_element_type=F32)

    def update(u, st_ref, masked):
        for hh in range(nh):
            vt = vt_sc[hh, u]
            for m in range(2):
                c = 2 * hh + m
                st = st_ref[c]
                if masked:
                    st = st + bias_ref[...]
                m_prev = m_sc[c]
                m_new = jnp.maximum(m_prev, jnp.max(st, axis=0, keepdims=True))
                alpha = jnp.exp2(m_prev - m_new)
                pt = jnp.exp2(st - m_new)
                l_sc[c] = alpha * l_sc[c] + jnp.sum(pt, axis=0, keepdims=True)
                acc_sc[c] = alpha * acc_sc[c] + jnp.dot(vt, pt.astype(BF16), preferred_element_type=F32)
                m_sc[c] = m_new

    n_units = (qi * tq) // tu + 1
    n_loop = (n_units - 1) // 2
    scores(0, st_a)

    def two_units(j, carry):
        u = 2 * j
        scores(u + 1, st_b)
        update(u, st_a, False)
        scores(u + 2, st_a)
        update(u + 1, st_b, False)
        return carry

    lax.fori_loop(0, n_loop, two_units, 0)
    last = n_units - 1

    @pl.when(last == 2 * n_loop)
    def _():
        update(last, st_a, True)

    @pl.when(last != 2 * n_loop)
    def _():
        scores(last, st_b)
        update(last - 1, st_a, False)
        update(last, st_b, True)

    lv = lam_ref[...]
    lam = (jnp.exp(jnp.sum(lv[0:1] * lv[1:2], axis=-1, keepdims=True))
           - jnp.exp(jnp.sum(lv[2:3] * lv[3:4], axis=-1, keepdims=True)) + lam_init)
    for hh in range(nh):
        c0, c1 = 2 * hh, 2 * hh + 1
        ot = acc_sc[c0] * (1.0 / l_sc[c0]) - lam * (acc_sc[c1] * (1.0 / l_sc[c1]))
        ot = ot * lax.rsqrt(jnp.mean(ot * ot, axis=0, keepdims=True) + SUB_EPS)
        o_ref[0, :, heads[hh]] = (ot.T * sw_ref[...] * (1.0 - lam_init)).astype(o_ref.dtype)


def _attention(qk3, plain3, lam_rows, subln_w, n_heads, v_blk0, tq, lam_init):
    b, s, _ = qk3.shape
    tu = ATTN_KV_UNIT
    nh = ATTN_HEADS_PER_STEP
    hw = nh * LANES
    assert tq == tu, "the diagonal unit must coincide with the query tile"
    kpos = lax.broadcasted_iota(jnp.int32, (tu, tq), 0)
    qpos = lax.broadcasted_iota(jnp.int32, (tu, tq), 1)
    diag_bias = jnp.where(kpos <= qpos, 0.0, NEG_BIG).astype(F32)
    return pl.pallas_call(
        functools.partial(_attn_kernel, tq=tq, lam_init=lam_init),
        grid=(b, n_heads // nh, s // tq),
        in_specs=[pl.BlockSpec((1, tq, hw), lambda bi, hi, qi: (bi, qi, hi)),
                  pl.BlockSpec((1, s, hw), lambda bi, hi, qi: (bi, 0, n_heads // nh + hi)),
                  pl.BlockSpec((1, s, hw), lambda bi, hi, qi: (bi, 0, v_blk0 // nh + hi)),
                  pl.BlockSpec((tu, tq), lambda bi, hi, qi: (0, 0)),
                  pl.BlockSpec((8, LANES), lambda bi, hi, qi: (0, 0)),
                  pl.BlockSpec((1, LANES), lambda bi, hi, qi: (0, 0))],
        out_specs=pl.BlockSpec((1, tq, hw), lambda bi, hi, qi: (bi, qi, hi)),
        out_shape=jax.ShapeDtypeStruct((b, s, n_heads * ATTN_DV), BF16),
        scratch_shapes=[pltpu.VMEM((nh, s // tu, ATTN_DV, tu), BF16),
                        pltpu.VMEM((2 * nh, tu, tq), F32), pltpu.VMEM((2 * nh, tu, tq), F32),
                        pltpu.VMEM((2 * nh, 1, tq), F32), pltpu.VMEM((2 * nh, 1, tq), F32),
                        pltpu.VMEM((2 * nh, ATTN_DV, tq), F32)],
        compiler_params=_cparams(("arbitrary", "arbitrary", "arbitrary")),
        name="diff_attn",
    )(qk3, qk3, plain3, diag_bias, lam_rows, subln_w)


def _merge_kernel(y_ref, o_ref, ws_ref, wa_ref, gs_ref, ga_ref, out_ref):
    for rows in _row_chunks(y_ref.shape[0], MERGE_ROW_CHUNKS):
        bs = jnp.dot(y_ref[rows, :], ws_ref[...], preferred_element_type=F32)
        ba = jnp.dot(o_ref[rows, :], wa_ref[...], preferred_element_type=F32)
        out_ref[rows, :] = (gs_ref[rows, :].astype(F32) * bs
                            + ga_ref[rows, :].astype(F32) * ba).astype(out_ref.dtype)


def _merge(y, o, ws, wa, gates, tm, tn):
    m, k = y.shape
    n = ws.shape[1]
    nj = n // tn
    return pl.pallas_call(
        _merge_kernel,
        grid=(m // tm, nj),
        in_specs=[pl.BlockSpec((tm, k), lambda i, j: (i, 0)),
                  pl.BlockSpec((tm, o.shape[1]), lambda i, j: (i, 0)),
                  pl.BlockSpec((k, tn), lambda i, j: (0, j)),
                  pl.BlockSpec((o.shape[1], tn), lambda i, j: (0, j)),
                  pl.BlockSpec((tm, tn), lambda i, j: (i, j)),
                  pl.BlockSpec((tm, tn), lambda i, j: (i, nj + j))],
        out_specs=pl.BlockSpec((tm, tn), lambda i, j: (i, j)),
        out_shape=jax.ShapeDtypeStruct((m, n), BF16),
        compiler_params=_cparams(("arbitrary", "arbitrary")),
        name="merge",
    )(y, o, ws, wa, gates, gates)


def _outproj_kernel(mg_ref, wo_ref, x_ref, gpost_ref, gt_ref, gpre_ref, sc_ref, sh_ref, wr_hi_ref,
                    wr_lo_ref, br_ref, x1_ref, hp_ref, lg_ref):
    mix = jnp.dot(mg_ref[...], wo_ref[...], preferred_element_type=F32)
    nm = mix * lax.rsqrt(jnp.mean(mix * mix, axis=-1, keepdims=True) + NORM_EPS) * gpost_ref[...]
    x1 = x_ref[...] + gt_ref[0] * nm
    x1_ref[...] = x1
    h2 = (x1 * lax.rsqrt(jnp.mean(x1 * x1, axis=-1, keepdims=True) + NORM_EPS) * gpre_ref[...]
          * (1.0 + sc_ref[0]) + sh_ref[0])
    h_hi = h2.astype(BF16)
    h_lo = (h2 - h_hi.astype(F32)).astype(BF16)
    wr_hi = wr_hi_ref[...]
    lg_ref[...] = (jnp.dot(h_hi, wr_hi, preferred_element_type=F32)
                   + jnp.dot(h_lo, wr_hi, preferred_element_type=F32)
                   + jnp.dot(h_hi, wr_lo_ref[...], preferred_element_type=F32) + br_ref[...])
    _store_token_rows(hp_ref, 0, h2)


def _outproj(merged, wo, x2, gpost, mod3, gpre, wr_hi, wr_lo, br, seq, tm, gt_blk, sc_blk, sh_blk):
    m, d = x2.shape
    per_b = seq // tm
    row = lambda i: (i, 0)
    const = lambda i: (0, 0)
    return pl.pallas_call(
        _outproj_kernel,
        grid=(m // tm,),
        in_specs=[pl.BlockSpec((tm, d), row),
                  pl.BlockSpec((d, d), const),
                  pl.BlockSpec((tm, d), row),
                  pl.BlockSpec((1, d), const),
                  pl.BlockSpec((1, 1, d), lambda i: (i // per_b, 0, gt_blk)),
                  pl.BlockSpec((1, d), const),
                  pl.BlockSpec((1, 1, d), lambda i: (i // per_b, 0, sc_blk)),
                  pl.BlockSpec((1, 1, d), lambda i: (i // per_b, 0, sh_blk)),
                  pl.BlockSpec((d, LANES), const),
                  pl.BlockSpec((d, LANES), const),
                  pl.BlockSpec((1, LANES), const)],
        out_specs=[pl.BlockSpec((tm, d), row),
                   pl.BlockSpec((tm * (d // LANES), LANES), row),
                   pl.BlockSpec((tm, LANES), row)],
        out_shape=[jax.ShapeDtypeStruct((m, d), F32),
                   jax.ShapeDtypeStruct((m * (d // LANES), LANES), F32),
                   jax.ShapeDtypeStruct((m, LANES), F32)],
        compiler_params=_cparams(("arbitrary",)),
        name="outproj",
    )(merged, wo, x2, gpost, mod3, gpre, mod3, mod3, wr_hi, wr_lo, br)


def _route_kernel(lg_ref, dest_ref, w_ref, cnt_ref, cnt_sc, pst_sc, run_sc, *, n_experts, blk):
    ph = pl.program_id(0)
    t = pl.program_id(1)
    tk = lg_ref.shape[0]
    lane = lax.broadcasted_iota(jnp.int32, (tk, LANES), 1)
    lg = jnp.where(lane < n_experts, lg_ref[...], -jnp.inf)
    vals = []
    hots = []
    for _ in range(TOP_K):
        mx = jnp.max(lg, axis=-1, keepdims=True)
        ix = jnp.min(jnp.where(lg == mx, lane, LANES), axis=-1, keepdims=True)
        hot = lane == ix
        lg = jnp.where(hot, -jnp.inf, lg)
        vals.append(mx)
        hots.append(hot)
    multi = jnp.zeros((tk, LANES), F32)
    for hot in hots:
        multi = multi + jnp.where(hot, 1.0, 0.0)
    colsum = jnp.sum(multi, axis=0, keepdims=True)

    @pl.when((ph == 0) & (t == 0))
    def _():
        cnt_sc[...] = jnp.zeros_like(cnt_sc)

    @pl.when(ph == 0)
    def _():
        cnt_sc[...] += colsum

    @pl.when((ph == 1) & (t == 0))
    def _():
        cnt = cnt_sc[...].astype(jnp.int32)
        padded = (((cnt + (blk - 1)) // blk) * blk).astype(F32)
        r_i = lax.broadcasted_iota(jnp.int32, (LANES, LANES), 0)
        c_i = lax.broadcasted_iota(jnp.int32, (LANES, LANES), 1)
        upper = jnp.where(r_i < c_i, 1.0, 0.0).astype(F32)
        pst_sc[...] = jnp.dot(jnp.broadcast_to(padded, (8, LANES)), upper,
                              preferred_element_type=F32, precision=HIGHEST)[0:1]
        run_sc[...] = jnp.zeros_like(run_sc)

    @pl.when(ph == 1)
    def _():
        r_i = lax.broadcasted_iota(jnp.int32, (tk, tk), 0)
        c_i = lax.broadcasted_iota(jnp.int32, (tk, tk), 1)
        strict = jnp.where(r_i > c_i, 1.0, 0.0).astype(BF16)
        before = jnp.dot(strict, multi.astype(BF16), preferred_element_type=F32)
        base = before + run_sc[...] + pst_sc[...]
        esum = jnp.zeros((tk, 1), F32)
        evals = []
        for r in range(TOP_K):
            e = jnp.exp(vals[r] - vals[0])
            evals.append(e)
            esum = esum + e
        dest = jnp.zeros((tk, LANES), jnp.int32)
        wts = jnp.zeros((tk, LANES), F32)
        for r in range(TOP_K):
            d_r = jnp.sum(jnp.where(hots[r], base, 0.0), axis=-1, keepdims=True).astype(jnp.int32)
            dest = jnp.where(lane == r, d_r, dest)
            wts = jnp.where(lane == r, evals[r] / esum, wts)
        dest_ref[...] = dest
        w_ref[...] = wts
        run_sc[...] += colsum
        cnt_ref[...] = jnp.broadcast_to(cnt_sc[...], cnt_ref.shape)


def _route(logits, n_experts, blk, tk):
    t = logits.shape[0]
    return pl.pallas_call(
        functools.partial(_route_kernel, n_experts=n_experts, blk=blk),
        grid=(2, t // tk),
        in_specs=[pl.BlockSpec((tk, LANES), lambda ph, ti: (ti, 0))],
        out_specs=[pl.BlockSpec((tk, LANES), lambda ph, ti: (ti * ph, 0)),
                   pl.BlockSpec((tk, LANES), lambda ph, ti: (ti * ph, 0)),
                   pl.BlockSpec((8, LANES), lambda ph, ti: (0, 0))],
        out_shape=[jax.ShapeDtypeStruct((t, LANES), jnp.int32),
                   jax.ShapeDtypeStruct((t, LANES), F32),
                   jax.ShapeDtypeStruct((8, LANES), F32)],
        scratch_shapes=[pltpu.VMEM((1, LANES), F32), pltpu.VMEM((1, LANES), F32), pltpu.VMEM((1, LANES), F32)],
        compiler_params=_cparams(("arbitrary", "arbitrary")),
        name="route",
    )(logits)


def _pad_fill_copies(b, nv_sm, zero_sc, xs_hbm, sem, blk, ns):
    nv = nv_sm[b]
    out = []
    off = b * blk + nv
    rest = blk - nv
    p = blk
    while p >= 1:
        cond = (rest & p) != 0
        out.append((cond, pltpu.make_async_copy(
            zero_sc.at[pl.ds(0, p * ns), :], xs_hbm.at[pl.ds(pl.multiple_of(off * ns, ns), p * ns), :], sem)))
        off = off + jnp.where(cond, p, 0)
        p //= 2
    return out


def _dispatch_kernel(dest_sm, nv_sm, h_hbm, xs_hbm, zero_sc, hbuf, in_sem, sem, fill_sem, *, tt, blk, nb, ns):
    i = pl.program_id(0)
    n = pl.num_programs(0)
    base = i * tt
    rows = tt * ns

    def fill(b, carry):
        for cond, cp in _pad_fill_copies(b, nv_sm, zero_sc, xs_hbm, fill_sem, blk, ns):
            pl.when(cond)(cp.start)
        return carry

    def fill_wait(b, carry):
        for cond, cp in _pad_fill_copies(b, nv_sm, zero_sc, xs_hbm, fill_sem, blk, ns):
            pl.when(cond)(cp.wait)
        return carry

    def load(step):
        slot = step % 3
        return pltpu.make_async_copy(h_hbm.at[pl.ds(pl.multiple_of(step * rows, rows), rows), :],
                                     hbuf.at[slot], in_sem.at[slot])

    @pl.when(i == 0)
    def _():
        zero_sc[...] = jnp.zeros_like(zero_sc)
        lax.fori_loop(0, nb, fill, 0)
        load(0).start()
        pl.when(n > 1)(lambda: load(1).start())

    def wait_step(step):
        for _ in range(TOP_K):
            pltpu.make_async_copy(hbuf.at[0], xs_hbm.at[pl.ds(0, rows), :], sem.at[step % 2]).wait()

    load(i).wait()
    src_tile = hbuf.at[i % 3]

    def body(t, carry):
        src = src_tile.at[pl.ds(pl.multiple_of(t * ns, ns), ns), :]
        for k in range(TOP_K):
            d = dest_sm[(base + t) * TOP_K + k]
            pltpu.make_async_copy(src, xs_hbm.at[pl.ds(pl.multiple_of(d * ns, ns), ns), :],
                                  sem.at[i % 2]).start()
        return carry

    lax.fori_loop(0, tt, body, 0)
    pl.when(i > 0)(lambda: wait_step(i - 1))
    pl.when(i + 2 < n)(lambda: load(i + 2).start())

    @pl.when(i == n - 1)
    def _():
        wait_step(i)
        lax.fori_loop(0, nb, fill_wait, 0)


def _dispatch(dest_flat, nvalid, h_rows, n_slots, tt, blk, ns):
    nb = n_slots // blk
    t = h_rows.shape[0] // ns
    return pl.pallas_call(
        functools.partial(_dispatch_kernel, tt=tt, blk=blk, nb=nb, ns=ns),
        grid_spec=pltpu.PrefetchScalarGridSpec(
            num_scalar_prefetch=2,
            grid=(t // tt,),
            in_specs=[pl.BlockSpec(memory_space=pl.ANY)],
            out_specs=pl.BlockSpec(memory_space=pl.ANY),
            scratch_shapes=[pltpu.VMEM((blk * ns, LANES), F32), pltpu.VMEM((3, tt * ns, LANES), F32),
                            pltpu.SemaphoreType.DMA((3,)), pltpu.SemaphoreType.DMA((2,)),
                            pltpu.SemaphoreType.DMA(())],
        ),
        out_shape=jax.ShapeDtypeStruct((n_slots * ns, LANES), F32),
        compiler_params=_cparams(("arbitrary",)),
        name="dispatch",
    )(dest_flat, nvalid, h_rows)


def _expert_changed(be, i, last):
    ii = jnp.minimum(i, last)
    prev = jnp.maximum(ii - 1, 0)
    return (i == 0) | (be[ii] != be[prev])


def _for_live_sub_blocks(active, n_valid, sub, o_ref, compute, rows_per_slot=1):
    n_live = jnp.where(active, (n_valid + sub - 1) // sub, 0)
    for count in range(MOE_SUB_BLOCKS + 1):
        @pl.when(n_live == count)
        def _(count=count):
            if count == MOE_SUB_BLOCKS:
                compute(slice(0, count * sub))
            else:
                for r in range(count):
                    compute(slice(r * sub, (r + 1) * sub))
            if count < MOE_SUB_BLOCKS:
                first = count * sub * rows_per_slot
                o_ref[first:, :] = jnp.zeros((o_ref.shape[0] - first, o_ref.shape[1]), o_ref.dtype)


def _stream_expert_weights(changed, first, prefetch, wait_cur, cast, start_next):
    @pl.when(changed)
    def _():
        pl.when(first)(lambda: start_next(True))
        wait_cur()
        cast()
        pl.when(prefetch)(lambda: start_next(False))


def _gateup_kernel(be, nv, nu, nxt, x_ref, w_hbm, bg_ref, bu_ref, o_ref, wbuf, wg_sc, wu_sc, sem):
    j = pl.program_id(0)
    i = pl.program_id(1)
    nj = pl.num_programs(0)
    th = wg_sc.shape[1]
    dff = w_hbm.shape[2] // 2
    last = nu[0] - 1
    active = i < nu[0]
    ii = jnp.minimum(i, last)

    def copies(e, jj):
        col = pl.multiple_of(jj * th, th)
        return (pltpu.make_async_copy(w_hbm.at[e, :, pl.ds(col, th)], wbuf.at[0], sem.at[0]),
                pltpu.make_async_copy(w_hbm.at[e, :, pl.ds(dff + col, th)], wbuf.at[1], sem.at[1]))

    nx = nxt[ii]
    same_pass = nx >= 0
    e_next = jnp.where(same_pass, be[jnp.maximum(nx, 0)], be[0])
    j_next = jnp.where(same_pass, j, j + 1)

    def start_next(current):
        for cp in (copies(be[ii], j) if current else copies(e_next, j_next)):
            cp.start()

    def wait_cur():
        for cp in copies(be[ii], j):
            cp.wait()

    def cast():
        def chunk(r, carry):
            rows = pl.ds(pl.multiple_of(r * CAST_ROWS, CAST_ROWS), CAST_ROWS)
            wg_sc[rows, :] = wbuf[0, rows, :].astype(BF16)
            wu_sc[rows, :] = wbuf[1, rows, :].astype(BF16)
            return carry

        lax.fori_loop(0, wg_sc.shape[0] // CAST_ROWS, chunk, 0)

    _stream_expert_weights(active & _expert_changed(be, i, last), (j == 0) & (i == 0),
                           same_pass | (j + 1 < nj), wait_cur, cast, start_next)

    sub = o_ref.shape[0] // MOE_SUB_BLOCKS

    def compute(rows):
        x = _load_token_rows(x_ref, rows.start, rows.stop - rows.start, wg_sc.shape[0] // LANES).astype(BF16)
        g = jnp.dot(x, wg_sc[...], preferred_element_type=F32) + bg_ref[0]
        up = jnp.dot(x, wu_sc[...], preferred_element_type=F32) + bu_ref[0]
        gate = jnp.minimum(g, SWIGLU_LIMIT)
        up = jnp.clip(up, -SWIGLU_LIMIT, SWIGLU_LIMIT)
        o_ref[rows, :] = ((up + 1.0) * gate * _sigmoid(SWIGLU_ALPHA * gate)).astype(o_ref.dtype)

    _for_live_sub_blocks(active, nv[i], sub, o_ref, compute)


def _gateup(block_e, nvalid, nused, nxt, xs, w_gu, b_gu3, tm, th):
    d = w_gu.shape[1]
    ns = d // LANES
    n_slots = xs.shape[0] // ns
    dff = w_gu.shape[2] // 2
    nj = dff // th
    nb = n_slots // tm

    def blk(i, nu):
        return jnp.minimum(i, nu[0] - 1)

    return pl.pallas_call(
        _gateup_kernel,
        grid_spec=pltpu.PrefetchScalarGridSpec(
            num_scalar_prefetch=4,
            grid=(nj, nb),
            in_specs=[pl.BlockSpec((tm * ns, LANES), lambda j, i, be, nv, nu, nx: (blk(i, nu), 0)),
                      pl.BlockSpec(memory_space=pl.ANY),
                      pl.BlockSpec((1, 1, th), lambda j, i, be, nv, nu, nx: (be[blk(i, nu)], 0, j)),
                      pl.BlockSpec((1, 1, th), lambda j, i, be, nv, nu, nx: (be[blk(i, nu)], 0, nj + j))],
            out_specs=pl.BlockSpec((tm, th), lambda j, i, be, nv, nu, nx: (i, j)),
            scratch_shapes=[pltpu.VMEM((2, d, th), F32), pltpu.VMEM((d, th), BF16), pltpu.VMEM((d, th), BF16),
                            pltpu.SemaphoreType.DMA((2,))],
        ),
        out_shape=jax.ShapeDtypeStruct((n_slots, dff), BF16),
        compiler_params=_cparams(("arbitrary", "arbitrary")),
        name="expert_gate_up",
    )(block_e, nvalid, nused, nxt, xs, w_gu, b_gu3, b_gu3)


def _down_kernel(be, nv, nu, nxt, a_ref, w_hbm, bd_ref, o_ref, wbuf, wd_sc, sem):
    i = pl.program_id(1)
    last = nu[0] - 1
    active = i < nu[0]
    ii = jnp.minimum(i, last)
    nx = nxt[ii]

    def copy(e):
        return pltpu.make_async_copy(w_hbm.at[e], wbuf, sem)

    def start_next(current):
        copy(be[ii] if current else be[jnp.maximum(nx, 0)]).start()

    def cast():
        def chunk(r, carry):
            rows = pl.ds(pl.multiple_of(r * CAST_ROWS, CAST_ROWS), CAST_ROWS)
            wd_sc[rows, :] = wbuf[rows, :].astype(BF16)
            return carry

        lax.fori_loop(0, wd_sc.shape[0] // CAST_ROWS, chunk, 0)

    _stream_expert_weights(active & _expert_changed(be, i, last), i == 0, nx >= 0,
                           lambda: copy(be[ii]).wait(), cast, start_next)

    sub = a_ref.shape[0] // MOE_SUB_BLOCKS
    ns = o_ref.shape[0] // a_ref.shape[0]

    def compute(rows):
        y = jnp.dot(a_ref[rows, :], wd_sc[...], preferred_element_type=F32) + bd_ref[0]
        _store_token_rows(o_ref, rows.start, y)

    _for_live_sub_blocks(active, nv[i], sub, o_ref, compute, rows_per_slot=ns)


def _down(block_e, nvalid, nused, nxt, act, w_d, b_d3, tm):
    n_slots, dff = act.shape
    d = w_d.shape[2]
    ns = d // LANES
    nb = n_slots // tm

    def blk(i, nu):
        return jnp.minimum(i, nu[0] - 1)

    return pl.pallas_call(
        _down_kernel,
        grid_spec=pltpu.PrefetchScalarGridSpec(
            num_scalar_prefetch=4,
            grid=(1, nb),
            in_specs=[pl.BlockSpec((tm, dff), lambda j, i, be, nv, nu, nx: (blk(i, nu), 0)),
                      pl.BlockSpec(memory_space=pl.ANY),
                      pl.BlockSpec((1, 1, d), lambda j, i, be, nv, nu, nx: (be[blk(i, nu)], 0, 0))],
            out_specs=pl.BlockSpec((tm * ns, LANES), lambda j, i, be, nv, nu, nx: (i, 0)),
            scratch_shapes=[pltpu.VMEM((dff, d), F32), pltpu.VMEM((dff, d), BF16), pltpu.SemaphoreType.DMA(())],
        ),
        out_shape=jax.ShapeDtypeStruct((n_slots * ns, LANES), F32),
        compiler_params=_cparams(("arbitrary", "arbitrary")),
        name="expert_down",
    )(block_e, nvalid, nused, nxt, act, w_d, b_d3)


def _combine_kernel(dest_sm, y_hbm, w_ref, x1_ref, gt_ref, g_ref, o_ref, buf, sem, *, tt, ns):
    i = pl.program_id(0)
    n = pl.num_programs(0)

    def gather(tile, slot):
        def body(t, carry):
            for k in range(TOP_K):
                d = dest_sm[(tile * tt + t) * TOP_K + k]
                pltpu.make_async_copy(y_hbm.at[pl.ds(pl.multiple_of(d * ns, ns), ns), :],
                                      buf.at[slot, pl.ds(pl.multiple_of((k * tt + t) * ns, ns), ns), :],
                                      sem.at[slot]).start()
            return carry

        lax.fori_loop(0, tt, body, 0)

    slot = i % 2

    @pl.when(i == 0)
    def _():
        gather(0, 0)

    @pl.when(i + 1 < n)
    def _():
        gather(i + 1, 1 - slot)

    pltpu.make_async_copy(y_hbm.at[pl.ds(0, TOP_K * tt * ns), :], buf.at[slot], sem.at[slot]).wait()
    w = w_ref[...]
    rows = buf.at[slot]
    f = None
    for k in range(TOP_K):
        yk = _load_token_rows(rows, k * tt, tt, ns) * w[:, k:k + 1]
        f = yk if f is None else f + yk
    nf = f * lax.rsqrt(jnp.mean(f * f, axis=-1, keepdims=True) + NORM_EPS) * g_ref[...]
    o_ref[...] = x1_ref[...] + gt_ref[0] * nf


def _combine(dest_flat, y_rows, wts, x1, mod3, gpost, seq, tt, gt_blk):
    t, d = x1.shape
    ns = d // LANES
    per_b = seq // tt
    return pl.pallas_call(
        functools.partial(_combine_kernel, tt=tt, ns=ns),
        grid_spec=pltpu.PrefetchScalarGridSpec(
            num_scalar_prefetch=1,
            grid=(t // tt,),
            in_specs=[pl.BlockSpec(memory_space=pl.ANY),
                      pl.BlockSpec((tt, LANES), lambda i, ds: (i, 0)),
                      pl.BlockSpec((tt, d), lambda i, ds: (i, 0)),
                      pl.BlockSpec((1, 1, d), lambda i, ds: (i // per_b, 0, gt_blk)),
                      pl.BlockSpec((1, d), lambda i, ds: (0, 0))],
            out_specs=pl.BlockSpec((tt, d), lambda i, ds: (i, 0)),
            scratch_shapes=[pltpu.VMEM((2, TOP_K * tt * ns, LANES), F32), pltpu.SemaphoreType.DMA((2,))],
        ),
        out_shape=jax.ShapeDtypeStruct((t, d), F32),
        compiler_params=_cparams(("arbitrary",)),
        name="combine",
    )(dest_flat, y_rows, wts, x1, mod3, gpost)


def _tile(n, pref):
    t = min(n, pref)
    while n % t:
        t //= 2
    return t


def _rope_tables(seq):
    half = ROT_DIM // 2
    inv = ROPE_THETA ** (-jnp.arange(0, ROT_DIM, 2, dtype=F32) / ROT_DIM)
    ang = jnp.arange(seq, dtype=F32)[:, None] * inv[None, :]
    cos, sin = jnp.cos(ang), jnp.sin(ang)
    ones = jnp.ones((seq, ATTN_DK - ROT_DIM), F32)
    zeros = jnp.zeros((seq, ATTN_DK - ROT_DIM), F32)
    zh = jnp.zeros((seq, half), F32)
    c64 = jnp.concatenate([cos, cos, ones], axis=1)
    s1_64 = jnp.concatenate([-sin, zh, zeros], axis=1)
    s2_64 = jnp.concatenate([zh, sin, zeros], axis=1)
    rep = LANES // ATTN_DK
    return tuple(jnp.tile(t, (1, rep)) for t in (c64, s1_64, s2_64))


def _layer(x, c_pad, l, p, moe_blk):
    bsz, seq, d = x.shape
    t = bsz * seq
    dh = d
    n_sheads = dh // SSD_HEADDIM
    gn = SSD_GROUPS * SSD_STATE
    n_aheads = d // ATTN_DV
    aw = n_aheads * ATTN_DV
    qkw = 2 * n_aheads * ATTN_DK
    n_experts = p["w_router"].shape[-1]

    mod = _ada(c_pad, p["w_ada"][l], p["b_ada"][l][None, :], _tile(6 * d, 1024))
    mod3 = mod[:bsz].reshape(bsz, 1, 6 * d)

    o = 0
    segs = {}
    for name, size in (("z", dh), ("xbc", dh + 2 * gn), ("dt", n_sheads), ("q", qkw), ("k", qkw),
                       ("v", aw), ("gs", d), ("ga", d)):
        segs[name] = (o, o + size)
        o += size
    scale = ATTN_DK ** -0.5 * math.log2(math.e)
    w_plain, w_rope, w_gate, w_dt = _regroup(jnp.swapaxes(p["w_in"][l], 0, 1), segs, scale)

    h = _prenorm(x, p["g_pre_mix"][l][None, :], mod3, 1, 0, _tile(seq, 1024)).reshape(t, d)
    tm = _tile(seq, 1024)
    plain = _mm(h, w_plain, BF16, tm, _tile(w_plain.shape[1], 1024), name="proj_plain")
    qk = _mm(h, w_rope, BF16, tm, _tile(w_rope.shape[1], 1024), "rope", _rope_tables(seq), name="proj_rope")
    gates = _mm(h, w_gate, BF16, tm, _tile(w_gate.shape[1], 1024), "sigmoid", name="proj_gate")
    dt_raw = _mm(h, w_dt, F32, tm, LANES, name="proj_dt")

    pad_h = lambda v: jnp.pad(v, (0, LANES - n_sheads))[None, :]
    plain3 = plain.reshape(bsz, seq, plain.shape[1])
    y_ssd = _ssd(plain3, dt_raw.reshape(bsz, seq, LANES), p["conv_w"][l], p["conv_b"][l][None, :],
                 pad_h(p["dt_bias"][l]), pad_h(p["a_log"][l]),
                 jnp.repeat(p["d_skip"][l], SSD_HEADDIM)[None, :], p["ssd_norm_w"][l][None, :], dh, n_sheads)

    lam_init = 0.8 - 0.6 * math.exp(-0.3 * l)
    lam_rows = jnp.zeros((8, LANES), F32)
    for r, nm in enumerate(("lambda_q1", "lambda_k1", "lambda_q2", "lambda_k2")):
        lam_rows = lam_rows.at[r, :ATTN_DK].set(p[nm][l])
    v_blk0 = (dh + dh + 2 * gn) // LANES
    o_attn = _attention(qk.reshape(bsz, seq, 2 * qkw), plain3, lam_rows, p["subln_w"][l][None, :],
                        n_aheads, v_blk0, _tile(seq, ATTN_Q_TILE), lam_init)

    tm2 = _tile(seq, 512)
    merged = _merge(y_ssd.reshape(t, dh), o_attn.reshape(t, aw), p["w_br_ssd"][l].astype(BF16),
                    p["w_br_attn"][l].astype(BF16), gates, tm, _tile(d, 1024))
    wr = jnp.pad(p["w_router"][l], ((0, 0), (0, LANES - n_experts)))
    wr_hi = wr.astype(BF16)
    wr_lo = (wr - wr_hi.astype(F32)).astype(BF16)
    br = jnp.pad(p["b_router"][l], (0, LANES - n_experts))[None, :]
    x1, h_rows, logits = _outproj(merged, p["w_out"][l].astype(BF16), x.reshape(t, d),
                                    p["g_post_mix"][l][None, :], mod3, p["g_pre_ffn"][l][None, :],
                                    wr_hi, wr_lo, br, seq, tm2, 2, 4, 3)

    dest, wts, cnt = _route(logits, n_experts, moe_blk, _tile(t, 1024))
    counts = cnt[0, :n_experts].astype(jnp.int32)
    n_slots = t * TOP_K + n_experts * moe_blk
    nb = n_slots // moe_blk
    pblocks = (counts + moe_blk - 1) // moe_blk
    pend = jnp.cumsum(pblocks)
    nused = jnp.maximum(pend[-1], 1).astype(jnp.int32)
    bidx = jnp.arange(nb, dtype=jnp.int32)
    block_e = jnp.minimum(jnp.sum((pend[None, :] <= bidx[:, None]).astype(jnp.int32), axis=1), n_experts - 1)
    pstart = pend - pblocks
    nvalid = jnp.clip(counts[block_e] - (bidx - pstart[block_e]) * moe_blk, 0, moe_blk).astype(jnp.int32)
    dest_flat = dest[:, :TOP_K].reshape(-1)
    nused1 = nused.reshape(1)
    after = pend[block_e].astype(jnp.int32)
    nxt = jnp.where(after < nused, after, -1).astype(jnp.int32)

    xs = _dispatch(dest_flat, nvalid, h_rows, n_slots, _tile(t, 512), moe_blk, d // LANES)
    dff = p["w_down"].shape[2]
    act = _gateup(block_e, nvalid, nused1, nxt, xs, p["w_gate_up"][l], p["b_gate_up"][l][:, None, :],
                  moe_blk, _tile(dff, 1024))
    y_sorted = _down(block_e, nvalid, nused1, nxt, act, p["w_down"][l], p["b_down"][l][:, None, :], moe_blk)
    out = _combine(dest_flat, y_sorted, wts, x1, mod3, p["g_post_ffn"][l][None, :], seq, _tile(seq, 256), 5)
    return out.reshape(bsz, seq, d)


MOE_ROW_BLOCK = 512
EPILOGUE_ROW_CHUNKS = 4
MERGE_ROW_CHUNKS = 2
CAST_ROWS = 128
MOE_SUB_BLOCKS = 2


def kernel(x, c, w_ada, b_ada, g_pre_mix, g_post_mix, g_pre_ffn, g_post_ffn, w_in, conv_w, conv_b, dt_bias, a_log, d_skip, ssd_norm_w, lambda_q1, lambda_k1, lambda_q2, lambda_k2, subln_w, w_br_ssd, w_br_attn, w_out, w_router, b_router, w_gate_up, b_gate_up, w_down, b_down):
    p = dict(w_ada=w_ada, b_ada=b_ada, g_pre_mix=g_pre_mix, g_post_mix=g_post_mix, g_pre_ffn=g_pre_ffn,
             g_post_ffn=g_post_ffn, w_in=w_in, conv_w=conv_w, conv_b=conv_b, dt_bias=dt_bias, a_log=a_log,
             d_skip=d_skip, ssd_norm_w=ssd_norm_w, lambda_q1=lambda_q1, lambda_k1=lambda_k1,
             lambda_q2=lambda_q2, lambda_k2=lambda_k2, subln_w=subln_w, w_br_ssd=w_br_ssd,
             w_br_attn=w_br_attn, w_out=w_out, w_router=w_router, b_router=b_router, w_gate_up=w_gate_up,
             b_gate_up=b_gate_up, w_down=w_down, b_down=b_down)
    bsz = x.shape[0]
    c_pad = jnp.pad(c, ((0, (-bsz) % 8), (0, 0)))
    for l in range(w_ada.shape[0]):
        x = _layer(x, c_pad, l, p, min(MOE_ROW_BLOCK, x.shape[0] * x.shape[1]))
    return x
```

```python
import functools
import math

import jax
import jax.numpy as jnp
from jax import lax
from jax.experimental import pallas as pl
from jax.experimental.pallas import tpu as pltpu

F32 = jnp.float32
BF16 = jnp.bfloat16
HIGHEST = lax.Precision.HIGHEST

SSD_HEADDIM = 64
SSD_GROUPS = 4
SSD_STATE = 128
SSD_CONV = 4
SSD_CHUNK = 128
ATTN_DK = 64
ATTN_DV = 128
ATTN_KV_UNIT = 512
ATTN_Q_TILE = 512
ATTN_HEADS_PER_STEP = 2
ROT_DIM = ATTN_DK // 4
ROPE_THETA = 500000.0
TOP_K = 4
SWIGLU_LIMIT = 7.0
SWIGLU_ALPHA = 1.702
NORM_EPS = 1e-6
SUB_EPS = 1e-5
LANES = 128
NEG_BIG = -1e30

VMEM_BYTES = 64 * 1024 * 1024
VMEM_LIMIT = VMEM_BYTES - 8 * 1024 * 1024
ADA_COL_TILE = 1024
PRENORM_ROW_TILE = 1024
PROJ_ROW_TILE = 1024
PROJ_COL_TILE = 1024
OUTPROJ_ROW_TILE = 512
ROUTE_ROW_TILE = 1024
DISPATCH_TOKENS = 512
COMBINE_TOKENS = 256
EXPERT_HIDDEN_TILE = 1024
REGROUP_TILE = 512
MOE_ROW_BLOCK = 512
MOE_SUB_BLOCKS = 2
EPILOGUE_ROW_CHUNKS = 4
MERGE_ROW_CHUNKS = 2
CAST_ROWS = 128


def _cparams(sem, vmem=VMEM_LIMIT):
    return pltpu.CompilerParams(dimension_semantics=sem, vmem_limit_bytes=vmem)


def _sigmoid(x):
    return 1.0 / (1.0 + jnp.exp(-x))


def _store_token_rows(ref, first_tok, val):
    n_tok, width = val.shape
    ns = width // LANES
    for s_ in range(ns):
        ref[pl.ds(first_tok * ns + s_, n_tok, stride=ns), :] = val[:, s_ * LANES:(s_ + 1) * LANES]


def _load_token_rows(ref, first_tok, n_tok, ns):
    return jnp.concatenate([ref[pl.ds(first_tok * ns + s_, n_tok, stride=ns), :] for s_ in range(ns)], axis=1)


def _ada_kernel(c_ref, w_ref, b_ref, o_ref):
    c = c_ref[...]
    sc = c * _sigmoid(c)
    o_ref[...] = jnp.dot(sc, w_ref[...], preferred_element_type=F32, precision=HIGHEST) + b_ref[...]


def _ada(c_pad, w, b, tn):
    rows, d = c_pad.shape
    n = w.shape[1]
    return pl.pallas_call(
        _ada_kernel,
        grid=(n // tn,),
        in_specs=[pl.BlockSpec((rows, d), lambda j: (0, 0)),
                  pl.BlockSpec((d, tn), lambda j: (0, j)),
                  pl.BlockSpec((1, tn), lambda j: (0, j))],
        out_specs=pl.BlockSpec((rows, tn), lambda j: (0, j)),
        out_shape=jax.ShapeDtypeStruct((rows, n), F32),
        compiler_params=_cparams(("arbitrary",)),
        name="ada",
    )(c_pad, w, b)


def _prenorm_kernel(x_ref, g_ref, sc_ref, sh_ref, o_ref):
    x = x_ref[0]
    y = x * lax.rsqrt(jnp.mean(x * x, axis=-1, keepdims=True) + NORM_EPS) * g_ref[...]
    o_ref[0] = (y * (1.0 + sc_ref[0]) + sh_ref[0]).astype(o_ref.dtype)


def _prenorm(x, g, mod3, sc_blk, sh_blk, ts):
    b, s, d = x.shape
    return pl.pallas_call(
        _prenorm_kernel,
        grid=(b, s // ts),
        in_specs=[pl.BlockSpec((1, ts, d), lambda bi, si: (bi, si, 0)),
                  pl.BlockSpec((1, d), lambda bi, si: (0, 0)),
                  pl.BlockSpec((1, 1, d), lambda bi, si: (bi, 0, sc_blk)),
                  pl.BlockSpec((1, 1, d), lambda bi, si: (bi, 0, sh_blk))],
        out_specs=pl.BlockSpec((1, ts, d), lambda bi, si: (bi, si, 0)),
        out_shape=jax.ShapeDtypeStruct((b, s, d), BF16),
        compiler_params=_cparams(("arbitrary", "arbitrary")),
        name="prenorm",
    )(x, g, mod3, mod3)


def _regroup_kernel(src_sm, wt_hbm, plain_ref, rope_ref, gate_ref, dt_ref, buf, dt_buf, sem, dt_sem, *,
                    tiles, dt_rows, q_scale):
    t = pl.program_id(0)
    n = pl.num_programs(0)
    tr = REGROUP_TILE
    n_plain, n_rope, n_gate, n_q = tiles

    def load(step):
        row = pl.multiple_of(src_sm[step], 8)
        return pltpu.make_async_copy(wt_hbm.at[pl.ds(row, tr), :], buf.at[step % 2], sem.at[step % 2])

    dt_copy = pltpu.make_async_copy(wt_hbm.at[pl.ds(dt_rows[0], dt_rows[1]), :], dt_buf, dt_sem)

    @pl.when(t == 0)
    def _():
        load(0).start()
        dt_copy.start()

    pl.when(t + 1 < n)(lambda: load(t + 1).start())
    load(t).wait()
    cols = buf[t % 2].T

    @pl.when(t < n_plain)
    def _():
        plain_ref[...] = cols.astype(BF16)

    @pl.when((t >= n_plain) & (t < n_plain + n_rope))
    def _():
        rope_ref[...] = jnp.where(t < n_plain + n_q, cols * q_scale, cols).astype(BF16)

    @pl.when(t >= n_plain + n_rope)
    def _():
        gate_ref[...] = cols.astype(BF16)

    @pl.when(t == n - 1)
    def _():
        dt_copy.wait()
        dt_ref[...] = jnp.zeros_like(dt_ref)
        dt_ref[:, :dt_rows[1]] = dt_buf[...].T.astype(BF16)


def _regroup(w_in_t, segs, q_scale):
    n_in, d = w_in_t.shape
    tr = REGROUP_TILE
    starts = []
    counts = []
    for group in (("z", "xbc", "v"), ("q", "k"), ("gs", "ga")):
        c = 0
        for nm in group:
            lo, hi = segs[nm]
            assert (hi - lo) % tr == 0 and lo % 8 == 0
            starts += list(range(lo, hi, tr))
            c += (hi - lo) // tr
        counts.append(c)
    n_plain, n_rope, n_gate = counts
    n_q = (segs["q"][1] - segs["q"][0]) // tr
    dt_lo, dt_hi = segs["dt"]
    clamp = lambda t, first, cnt: jnp.clip(t - first, 0, cnt - 1)
    return pl.pallas_call(
        functools.partial(_regroup_kernel, tiles=(n_plain, n_rope, n_gate, n_q), dt_rows=(dt_lo, dt_hi - dt_lo),
                          q_scale=q_scale),
        grid_spec=pltpu.PrefetchScalarGridSpec(
            num_scalar_prefetch=1,
            grid=(len(starts),),
            in_specs=[pl.BlockSpec(memory_space=pl.ANY)],
            out_specs=[pl.BlockSpec((d, tr), lambda t, src: (0, clamp(t, 0, n_plain))),
                       pl.BlockSpec((d, tr), lambda t, src: (0, clamp(t, n_plain, n_rope))),
                       pl.BlockSpec((d, tr), lambda t, src: (0, clamp(t, n_plain + n_rope, n_gate))),
                       pl.BlockSpec((d, LANES), lambda t, src: (0, 0))],
            scratch_shapes=[pltpu.VMEM((2, tr, d), F32), pltpu.VMEM((dt_hi - dt_lo, d), F32),
                            pltpu.SemaphoreType.DMA((2,)), pltpu.SemaphoreType.DMA(())],
        ),
        out_shape=[jax.ShapeDtypeStruct((d, n_plain * tr), BF16), jax.ShapeDtypeStruct((d, n_rope * tr), BF16),
                   jax.ShapeDtypeStruct((d, n_gate * tr), BF16), jax.ShapeDtypeStruct((d, LANES), BF16)],
        compiler_params=_cparams(("arbitrary",)),
        name="regroup_w_in",
    )(jnp.asarray(starts, jnp.int32), w_in_t)


def _mm_kernel(a_ref, w_ref, o_ref):
    o_ref[...] = jnp.dot(a_ref[...], w_ref[...], preferred_element_type=F32).astype(o_ref.dtype)


def _row_chunks(n_rows, chunks):
    step = n_rows // chunks if n_rows % chunks == 0 and n_rows >= 64 * chunks else n_rows
    return [slice(r, r + step) for r in range(0, n_rows, step)]


def _mm_sigmoid_kernel(a_ref, w_ref, o_ref):
    for rows in _row_chunks(a_ref.shape[0], EPILOGUE_ROW_CHUNKS):
        acc = jnp.dot(a_ref[rows, :], w_ref[...], preferred_element_type=F32)
        o_ref[rows, :] = _sigmoid(acc).astype(o_ref.dtype)


def _mm_rope_kernel(a_ref, w_ref, c_ref, s1_ref, s2_ref, o_ref):
    for rows in _row_chunks(a_ref.shape[0], EPILOGUE_ROW_CHUNKS):
        acc = jnp.dot(a_ref[rows, :], w_ref[...], preferred_element_type=F32)
        c = c_ref[rows, :]
        s1 = s1_ref[rows, :]
        s2 = s2_ref[rows, :]
        for g in range(acc.shape[1] // LANES):
            blk = acc[:, g * LANES:(g + 1) * LANES]
            fwd = pltpu.roll(blk, LANES - ROT_DIM // 2, 1)
            bwd = pltpu.roll(blk, ROT_DIM // 2, 1)
            o_ref[rows, g * LANES:(g + 1) * LANES] = (blk * c + fwd * s1 + bwd * s2).astype(o_ref.dtype)


def _mm(a, w, out_dtype, tm, tn, epilogue="none", rope_tabs=None, name="mm"):
    m, k = a.shape
    n = w.shape[1]
    in_specs = [pl.BlockSpec((tm, k), lambda i, j: (i, 0)),
                pl.BlockSpec((k, tn), lambda i, j: (0, j))]
    args = [a, w]
    if epilogue == "rope":
        seq_blocks = rope_tabs[0].shape[0] // tm
        for t in rope_tabs:
            in_specs.append(pl.BlockSpec((tm, LANES), lambda i, j: (i % seq_blocks, 0)))
            args.append(t)
        kern = _mm_rope_kernel
    elif epilogue == "sigmoid":
        kern = _mm_sigmoid_kernel
    else:
        kern = _mm_kernel
    return pl.pallas_call(
        kern,
        grid=(m // tm, n // tn),
        in_specs=in_specs,
        out_specs=pl.BlockSpec((tm, tn), lambda i, j: (i, j)),
        out_shape=jax.ShapeDtypeStruct((m, n), out_dtype),
        compiler_params=_cparams(("arbitrary", "arbitrary")),
        name=name,
    )(*args)


def _ssd_kernel(z_ref, xs_ref, bm_ref, cm_ref, dt_ref, cwx_ref, cwb_ref, cwc_ref, cbx_ref, cbb_ref,
                cbc_ref, dtb_ref, alog_ref, dsk_ref, nw_ref, o_ref,
                px_sc, pb_sc, pc_sc, st_sc, y_sc, *, n_heads):
    c = pl.program_id(1)
    L = SSD_CHUNK
    N = SSD_STATE
    hpg = n_heads // SSD_GROUPS
    gw = hpg * SSD_HEADDIM

    @pl.when(c == 0)
    def _():
        px_sc[...] = jnp.zeros_like(px_sc)
        pb_sc[...] = jnp.zeros_like(pb_sc)
        pc_sc[...] = jnp.zeros_like(pc_sc)
        st_sc[...] = jnp.zeros_like(st_sc)

    def conv_silu(u, prev_sc, w_ref, b_ref):
        prev = prev_sc[...]
        row = lax.broadcasted_iota(jnp.int32, u.shape, 0)
        acc = u * w_ref[SSD_CONV - 1:SSD_CONV, :] + b_ref[...]
        for j in range(1, SSD_CONV):
            sh = jnp.where(row < j, pltpu.roll(prev, j, 0), pltpu.roll(u, j, 0))
            acc = acc + sh * w_ref[SSD_CONV - 1 - j:SSD_CONV - j, :]
        prev_sc[...] = u
        return acc * _sigmoid(acc)

    xs = conv_silu(xs_ref[0].astype(F32), px_sc, cwx_ref, cbx_ref)
    bm = conv_silu(bm_ref[0].astype(F32), pb_sc, cwb_ref, cbb_ref)
    cm = conv_silu(cm_ref[0].astype(F32), pc_sc, cwc_ref, cbc_ref)

    raw = dt_ref[0] + dtb_ref[...]
    dt = jnp.maximum(raw, 0.0) + jnp.log1p(jnp.exp(-jnp.abs(raw)))
    a = dt * (-jnp.exp(alog_ref[...]))
    r_i = lax.broadcasted_iota(jnp.int32, (L, L), 0)
    c_i = lax.broadcasted_iota(jnp.int32, (L, L), 1)
    causal = r_i >= c_i
    tri = jnp.where(causal, 1.0, 0.0).astype(F32)
    a_cs = jnp.dot(tri, a, preferred_element_type=F32, precision=HIGHEST)
    a_cs_t = a_cs.T

    lane = lax.broadcasted_iota(jnp.int32, (L, LANES), 1)
    first = lane < SSD_HEADDIM
    acs_tiles = []
    dt_tiles = []
    for j in range(n_heads // 2):
        h0, h1 = 2 * j, 2 * j + 1
        acs_tiles.append(jnp.where(first, a_cs[:, h0:h0 + 1], a_cs[:, h1:h1 + 1]))
        dt_tiles.append(jnp.where(first, dt[:, h0:h0 + 1], dt[:, h1:h1 + 1]))
    acs_e = jnp.concatenate(acs_tiles, axis=1)
    dt_e = jnp.concatenate(dt_tiles, axis=1)
    xdt = xs * dt_e
    ea = jnp.exp(acs_e)
    alast = acs_e[L - 1:L, :]
    xdec = (xdt * jnp.exp(alast - acs_e)).astype(BF16)
    ealast = jnp.exp(alast)

    for g in range(SSD_GROUPS):
        bg = bm[:, g * N:(g + 1) * N]
        cg = cm[:, g * N:(g + 1) * N].astype(BF16)
        scores = lax.dot_general(cg, bg.astype(BF16), (((1,), (1,)), ((), ())),
                                 preferred_element_type=F32)
        st_old = st_sc[:, g * gw:(g + 1) * gw]
        y_off = jnp.dot(cg, st_old.astype(BF16), preferred_element_type=F32) * ea[:, g * gw:(g + 1) * gw]
        st_sc[:, g * gw:(g + 1) * gw] = st_old * ealast[:, g * gw:(g + 1) * gw] + jnp.dot(
            bg.T.astype(BF16), xdec[:, g * gw:(g + 1) * gw], preferred_element_type=F32)
        for jj in range(hpg // 2):
            j = g * (hpg // 2) + jj
            h0, h1 = 2 * j, 2 * j + 1
            la = jnp.exp(jnp.where(causal, a_cs[:, h0:h0 + 1] - a_cs_t[h0:h0 + 1, :], NEG_BIG))
            lb = jnp.exp(jnp.where(causal, a_cs[:, h1:h1 + 1] - a_cs_t[h1:h1 + 1, :], NEG_BIG))
            mcat = jnp.concatenate([(scores * la).astype(BF16), (scores * lb).astype(BF16)], axis=1)
            xp = xdt[:, j * LANES:(j + 1) * LANES]
            xcat = jnp.concatenate([jnp.where(first, xp, 0.0).astype(BF16),
                                    jnp.where(first, 0.0, xp).astype(BF16)], axis=0)
            y_diag = jnp.dot(mcat, xcat, preferred_element_type=F32)
            lo = jj * LANES
            y_sc[:, j * LANES:(j + 1) * LANES] = (
                y_diag + y_off[:, lo:lo + LANES]
                + dsk_ref[:, j * LANES:(j + 1) * LANES] * xs[:, j * LANES:(j + 1) * LANES])

    z = z_ref[0].astype(F32)
    u = y_sc[...] * (z * _sigmoid(z))
    for g in range(SSD_GROUPS):
        ug = u[:, g * gw:(g + 1) * gw]
        ms = jnp.mean(ug * ug, axis=-1, keepdims=True)
        o_ref[0, :, g * gw:(g + 1) * gw] = (ug * lax.rsqrt(ms + SUB_EPS)
                                            * nw_ref[:, g * gw:(g + 1) * gw]).astype(o_ref.dtype)


def _ssd(plain3, dt3, conv_w, conv_b, dtb, alog, dsk_e, norm_w, dh, n_heads):
    b, s, _ = plain3.shape
    L = SSD_CHUNK
    gn = SSD_GROUPS * SSD_STATE
    nc = s // L
    xblk = 1
    bblk = (2 * dh) // gn
    cw_x, cw_b, cw_c = conv_w[:, :dh], conv_w[:, dh:dh + gn], conv_w[:, dh + gn:]
    cb_x, cb_b, cb_c = conv_b[:, :dh], conv_b[:, dh:dh + gn], conv_b[:, dh + gn:]
    full = lambda shape: pl.BlockSpec(shape, lambda bi, ci: (0, 0))
    return pl.pallas_call(
        functools.partial(_ssd_kernel, n_heads=n_heads),
        grid=(b, nc),
        in_specs=[pl.BlockSpec((1, L, dh), lambda bi, ci: (bi, ci, 0)),
                  pl.BlockSpec((1, L, dh), lambda bi, ci: (bi, ci, xblk)),
                  pl.BlockSpec((1, L, gn), lambda bi, ci: (bi, ci, bblk)),
                  pl.BlockSpec((1, L, gn), lambda bi, ci: (bi, ci, bblk + 1)),
                  pl.BlockSpec((1, L, LANES), lambda bi, ci: (bi, ci, 0)),
                  full((SSD_CONV, dh)), full((SSD_CONV, gn)), full((SSD_CONV, gn)),
                  full((1, dh)), full((1, gn)), full((1, gn)),
                  full((1, LANES)), full((1, LANES)), full((1, dh)), full((1, dh))],
        out_specs=pl.BlockSpec((1, L, dh), lambda bi, ci: (bi, ci, 0)),
        out_shape=jax.ShapeDtypeStruct((b, s, dh), BF16),
        scratch_shapes=[pltpu.VMEM((L, dh), F32), pltpu.VMEM((L, gn), F32), pltpu.VMEM((L, gn), F32),
                        pltpu.VMEM((SSD_STATE, dh), F32), pltpu.VMEM((L, dh), F32)],
        compiler_params=_cparams(("arbitrary", "arbitrary")),
        name="ssd",
    )(plain3, plain3, plain3, plain3, dt3, cw_x, cw_b, cw_c, cb_x, cb_b, cb_c, dtb, alog, dsk_e, norm_w)


def _attn_kernel(q_ref, k_ref, v_ref, bias_ref, lam_ref, sw_ref, o_ref, vt_sc, st_a, st_b, m_sc, l_sc, acc_sc, *,
                 tq, lam_init):
    qi = pl.program_id(2)
    tu = ATTN_KV_UNIT
    nh = ATTN_HEADS_PER_STEP
    n_all = k_ref.shape[1] // tu
    heads = [slice(hh * LANES, (hh + 1) * LANES) for hh in range(nh)]

    @pl.when(qi == 0)
    def _():
        def transpose_block(c, carry):
            start = pl.multiple_of(c * tu, tu)
            for hh in range(nh):
                vt_sc[hh, c] = v_ref[0, pl.ds(start, tu), heads[hh]].astype(F32).T.astype(BF16)
            return carry

        lax.fori_loop(0, n_all, transpose_block, 0)

    qts = []
    for hh in range(nh):
        qt = q_ref[0, :, heads[hh]].astype(F32).T
        row = lax.broadcasted_iota(jnp.int32, qt.shape, 0)
        qts.append((jnp.where(row < ATTN_DK, qt, 0.0).astype(BF16),
                    jnp.where(row < ATTN_DK, 0.0, qt).astype(BF16)))
    m_sc[...] = jnp.full_like(m_sc, NEG_BIG)
    l_sc[...] = jnp.zeros_like(l_sc)
    acc_sc[...] = jnp.zeros_like(acc_sc)

    def scores(u, st_ref):
        start = pl.multiple_of(u * tu, tu)
        for hh in range(nh):
            k = k_ref[0, pl.ds(start, tu), heads[hh]]
            for m in range(2):
                st_ref[2 * hh + m] = jnp.dot(k, qts[hh][m], preferred_element_type=F32)

    def update(u, st_ref, masked):
        for hh in range(nh):
            vt = vt_sc[hh, u]
            for m in range(2):
                c = 2 * hh + m
                st = st_ref[c]
                if masked:
                    st = st + bias_ref[...]
                m_prev = m_sc[c]
                m_new = jnp.maximum(m_prev, jnp.max(st, axis=0, keepdims=True))
                alpha = jnp.exp2(m_prev - m_new)
                pt = jnp.exp2(st - m_new)
                l_sc[c] = alpha * l_sc[c] + jnp.sum(pt, axis=0, keepdims=True)
                acc_sc[c] = alpha * acc_sc[c] + jnp.dot(vt, pt.astype(BF16), preferred_element_type=F32)
                m_sc[c] = m_new

    n_units = (qi * tq) // tu + 1
    n_loop = (n_units - 1) // 2
    scores(0, st_a)

    def two_units(j, carry):
        u = 2 * j
        scores(u + 1, st_b)
        update(u, st_a, False)
        scores(u + 2, st_a)
        update(u + 1, st_b, False)
        return carry

    lax.fori_loop(0, n_loop, two_units, 0)
    last = n_units - 1

    @pl.when(last == 2 * n_loop)
    def _():
        update(last, st_a, True)

    @pl.when(last != 2 * n_loop)
    def _():
        scores(last, st_b)
        update(last - 1, st_a, False)
        update(last, st_b, True)

    lv = lam_ref[...]
    lam = (jnp.exp(jnp.sum(lv[0:1] * lv[1:2], axis=-1, keepdims=True))
           - jnp.exp(jnp.sum(lv[2:3] * lv[3:4], axis=-1, keepdims=True)) + lam_init)
    for hh in range(nh):
        c0, c1 = 2 * hh, 2 * hh + 1
        ot = acc_sc[c0] * (1.0 / l_sc[c0]) - lam * (acc_sc[c1] * (1.0 / l_sc[c1]))
        ot = ot * lax.rsqrt(jnp.mean(ot * ot, axis=0, keepdims=True) + SUB_EPS)
        o_ref[0, :, heads[hh]] = (ot.T * sw_ref[...] * (1.0 - lam_init)).astype(o_ref.dtype)


def _attention(qk3, plain3, lam_rows, subln_w, n_heads, v_blk0, tq, lam_init):
    b, s, _ = qk3.shape
    tu = ATTN_KV_UNIT
    nh = ATTN_HEADS_PER_STEP
    hw = nh * LANES
    assert tq == tu, "the diagonal unit must coincide with the query tile"
    kpos = lax.broadcasted_iota(jnp.int32, (tu, tq), 0)
    qpos = lax.broadcasted_iota(jnp.int32, (tu, tq), 1)
    diag_bias = jnp.where(kpos <= qpos, 0.0, NEG_BIG).astype(F32)
    return pl.pallas_call(
        functools.partial(_attn_kernel, tq=tq, lam_init=lam_init),
        grid=(b, n_heads // nh, s // tq),
        in_specs=[pl.BlockSpec((1, tq, hw), lambda bi, hi, qi: (bi, qi, hi)),
                  pl.BlockSpec((1, s, hw), lambda bi, hi, qi: (bi, 0, n_heads // nh + hi)),
                  pl.BlockSpec((1, s, hw), lambda bi, hi, qi: (bi, 0, v_blk0 // nh + hi)),
                  pl.BlockSpec((tu, tq), lambda bi, hi, qi: (0, 0)),
                  pl.BlockSpec((8, LANES), lambda bi, hi, qi: (0, 0)),
                  pl.BlockSpec((1, LANES), lambda bi, hi, qi: (0, 0))],
        out_specs=pl.BlockSpec((1, tq, hw), lambda bi, hi, qi: (bi, qi, hi)),
        out_shape=jax.ShapeDtypeStruct((b, s, n_heads * ATTN_DV), BF16),
        scratch_shapes=[pltpu.VMEM((nh, s // tu, ATTN_DV, tu), BF16),
                        pltpu.VMEM((2 * nh, tu, tq), F32), pltpu.VMEM((2 * nh, tu, tq), F32),
                        pltpu.VMEM((2 * nh, 1, tq), F32), pltpu.VMEM((2 * nh, 1, tq), F32),
                        pltpu.VMEM((2 * nh, ATTN_DV, tq), F32)],
        compiler_params=_cparams(("arbitrary", "arbitrary", "arbitrary")),
        name="diff_attn",
    )(qk3, qk3, plain3, diag_bias, lam_rows, subln_w)


def _merge_kernel(y_ref, o_ref, ws_ref, wa_ref, gs_ref, ga_ref, out_ref):
    for rows in _row_chunks(y_ref.shape[0], MERGE_ROW_CHUNKS):
        bs = jnp.dot(y_ref[rows, :], ws_ref[...], preferred_element_type=F32)
        ba = jnp.dot(o_ref[rows, :], wa_ref[...], preferred_element_type=F32)
        out_ref[rows, :] = (gs_ref[rows, :].astype(F32) * bs
                            + ga_ref[rows, :].astype(F32) * ba).astype(out_ref.dtype)


def _merge(y, o, ws, wa, gates, tm, tn):
    m, k = y.shape
    n = ws.shape[1]
    nj = n // tn
    return pl.pallas_call(
        _merge_kernel,
        grid=(m // tm, nj),
        in_specs=[pl.BlockSpec((tm, k), lambda i, j: (i, 0)),
                  pl.BlockSpec((tm, o.shape[1]), lambda i, j: (i, 0)),
                  pl.BlockSpec((k, tn), lambda i, j: (0, j)),
                  pl.BlockSpec((o.shape[1], tn), lambda i, j: (0, j)),
                  pl.BlockSpec((tm, tn), lambda i, j: (i, j)),
                  pl.BlockSpec((tm, tn), lambda i, j: (i, nj + j))],
        out_specs=pl.BlockSpec((tm, tn), lambda i, j: (i, j)),
        out_shape=jax.ShapeDtypeStruct((m, n), BF16),
        compiler_params=_cparams(("arbitrary", "arbitrary")),
        name="merge",
    )(y, o, ws, wa, gates, gates)


def _outproj_kernel(mg_ref, wo_ref, x_ref, gpost_ref, gt_ref, gpre_ref, sc_ref, sh_ref, wr_hi_ref,
                    wr_lo_ref, br_ref, x1_ref, hp_ref, lg_ref):
    mix = jnp.dot(mg_ref[...], wo_ref[...], preferred_element_type=F32)
    nm = mix * lax.rsqrt(jnp.mean(mix * mix, axis=-1, keepdims=True) + NORM_EPS) * gpost_ref[...]
    x1 = x_ref[...] + gt_ref[0] * nm
    x1_ref[...] = x1
    h2 = (x1 * lax.rsqrt(jnp.mean(x1 * x1, axis=-1, keepdims=True) + NORM_EPS) * gpre_ref[...]
          * (1.0 + sc_ref[0]) + sh_ref[0])
    h_hi = h2.astype(BF16)
    h_lo = (h2 - h_hi.astype(F32)).astype(BF16)
    wr_hi = wr_hi_ref[...]
    lg_ref[...] = (jnp.dot(h_hi, wr_hi, preferred_element_type=F32)
                   + jnp.dot(h_lo, wr_hi, preferred_element_type=F32)
                   + jnp.dot(h_hi, wr_lo_ref[...], preferred_element_type=F32) + br_ref[...])
    _store_token_rows(hp_ref, 0, h2)


def _outproj(merged, wo, x2, gpost, mod3, gpre, wr_hi, wr_lo, br, seq, tm, gt_blk, sc_blk, sh_blk):
    m, d = x2.shape
    per_b = seq // tm
    row = lambda i: (i, 0)
    const = lambda i: (0, 0)
    return pl.pallas_call(
        _outproj_kernel,
        grid=(m // tm,),
        in_specs=[pl.BlockSpec((tm, d), row),
                  pl.BlockSpec((d, d), const),
                  pl.BlockSpec((tm, d), row),
                  pl.BlockSpec((1, d), const),
                  pl.BlockSpec((1, 1, d), lambda i: (i // per_b, 0, gt_blk)),
                  pl.BlockSpec((1, d), const),
                  pl.BlockSpec((1, 1, d), lambda i: (i // per_b, 0, sc_blk)),
                  pl.BlockSpec((1, 1, d), lambda i: (i // per_b, 0, sh_blk)),
                  pl.BlockSpec((d, LANES), const),
                  pl.BlockSpec((d, LANES), const),
                  pl.BlockSpec((1, LANES), const)],
        out_specs=[pl.BlockSpec((tm, d), row),
                   pl.BlockSpec((tm * (d // LANES), LANES), row),
                   pl.BlockSpec((tm, LANES), row)],
        out_shape=[jax.ShapeDtypeStruct((m, d), F32),
                   jax.ShapeDtypeStruct((m * (d // LANES), LANES), F32),
                   jax.ShapeDtypeStruct((m, LANES), F32)],
        compiler_params=_cparams(("arbitrary",)),
        name="outproj",
    )(merged, wo, x2, gpost, mod3, gpre, mod3, mod3, wr_hi, wr_lo, br)


def _route_kernel(lg_ref, dest_ref, w_ref, cnt_ref, cnt_sc, pst_sc, run_sc, *, n_experts, blk):
    ph = pl.program_id(0)
    t = pl.program_id(1)
    tk = lg_ref.shape[0]
    lane = lax.broadcasted_iota(jnp.int32, (tk, LANES), 1)
    lg = jnp.where(lane < n_experts, lg_ref[...], -jnp.inf)
    vals = []
    hots = []
    for _ in range(TOP_K):
        mx = jnp.max(lg, axis=-1, keepdims=True)
        ix = jnp.min(jnp.where(lg == mx, lane, LANES), axis=-1, keepdims=True)
        hot = lane == ix
        lg = jnp.where(hot, -jnp.inf, lg)
        vals.append(mx)
        hots.append(hot)
    multi = jnp.zeros((tk, LANES), F32)
    for hot in hots:
        multi = multi + jnp.where(hot, 1.0, 0.0)
    colsum = jnp.sum(multi, axis=0, keepdims=True)

    @pl.when((ph == 0) & (t == 0))
    def _():
        cnt_sc[...] = jnp.zeros_like(cnt_sc)

    @pl.when(ph == 0)
    def _():
        cnt_sc[...] += colsum

    @pl.when((ph == 1) & (t == 0))
    def _():
        cnt = cnt_sc[...].astype(jnp.int32)
        padded = (((cnt + (blk - 1)) // blk) * blk).astype(F32)
        r_i = lax.broadcasted_iota(jnp.int32, (LANES, LANES), 0)
        c_i = lax.broadcasted_iota(jnp.int32, (LANES, LANES), 1)
        upper = jnp.where(r_i < c_i, 1.0, 0.0).astype(F32)
        pst_sc[...] = jnp.dot(jnp.broadcast_to(padded, (8, LANES)), upper,
                              preferred_element_type=F32, precision=HIGHEST)[0:1]
        run_sc[...] = jnp.zeros_like(run_sc)

    @pl.when(ph == 1)
    def _():
        r_i = lax.broadcasted_iota(jnp.int32, (tk, tk), 0)
        c_i = lax.broadcasted_iota(jnp.int32, (tk, tk), 1)
        strict = jnp.where(r_i > c_i, 1.0, 0.0).astype(BF16)
        before = jnp.dot(strict, multi.astype(BF16), preferred_element_type=F32)
        base = before + run_sc[...] + pst_sc[...]
        esum = jnp.zeros((tk, 1), F32)
        evals = []
        for r in range(TOP_K):
            e = jnp.exp(vals[r] - vals[0])
            evals.append(e)
            esum = esum + e
        dest = jnp.zeros((tk, LANES), jnp.int32)
        wts = jnp.zeros((tk, LANES), F32)
        for r in range(TOP_K):
            d_r = jnp.sum(jnp.where(hots[r], base, 0.0), axis=-1, keepdims=True).astype(jnp.int32)
            dest = jnp.where(lane == r, d_r, dest)
            wts = jnp.where(lane == r, evals[r] / esum, wts)
        dest_ref[...] = dest
        w_ref[...] = wts
        run_sc[...] += colsum
        cnt_ref[...] = jnp.broadcast_to(cnt_sc[...], cnt_ref.shape)


def _route(logits, n_experts, blk, tk):
    t = logits.shape[0]
    return pl.pallas_call(
        functools.partial(_route_kernel, n_experts=n_experts, blk=blk),
        grid=(2, t // tk),
        in_specs=[pl.BlockSpec((tk, LANES), lambda ph, ti: (ti, 0))],
        out_specs=[pl.BlockSpec((tk, LANES), lambda ph, ti: (ti * ph, 0)),
                   pl.BlockSpec((tk, LANES), lambda ph, ti: (ti * ph, 0)),
                   pl.BlockSpec((8, LANES), lambda ph, ti: (0, 0))],
        out_shape=[jax.ShapeDtypeStruct((t, LANES), jnp.int32),
                   jax.ShapeDtypeStruct((t, LANES), F32),
                   jax.ShapeDtypeStruct((8, LANES), F32)],
        scratch_shapes=[pltpu.VMEM((1, LANES), F32), pltpu.VMEM((1, LANES), F32), pltpu.VMEM((1, LANES), F32)],
        compiler_params=_cparams(("arbitrary", "arbitrary")),
        name="route",
    )(logits)


def _pad_fill_copies(b, nv_sm, zero_sc, xs_hbm, sem, blk, ns):
    nv = nv_sm[b]
    out = []
    off = b * blk + nv
    rest = blk - nv
    p = blk
    while p >= 1:
        cond = (rest & p) != 0
        out.append((cond, pltpu.make_async_copy(
            zero_sc.at[pl.ds(0, p * ns), :], xs_hbm.at[pl.ds(pl.multiple_of(off * ns, ns), p * ns), :], sem)))
        off = off + jnp.where(cond, p, 0)
        p //= 2
    return out


def _dispatch_kernel(dest_sm, nv_sm, h_hbm, xs_hbm, zero_sc, hbuf, in_sem, sem, fill_sem, *, tt, blk, nb, ns):
    i = pl.program_id(0)
    n = pl.num_programs(0)
    base = i * tt
    rows = tt * ns

    def fill(b, carry):
        for cond, cp in _pad_fill_copies(b, nv_sm, zero_sc, xs_hbm, fill_sem, blk, ns):
            pl.when(cond)(cp.start)
        return carry

    def fill_wait(b, carry):
        for cond, cp in _pad_fill_copies(b, nv_sm, zero_sc, xs_hbm, fill_sem, blk, ns):
            pl.when(cond)(cp.wait)
        return carry

    def load(step):
        slot = step % 3
        return pltpu.make_async_copy(h_hbm.at[pl.ds(pl.multiple_of(step * rows, rows), rows), :],
                                     hbuf.at[slot], in_sem.at[slot])

    @pl.when(i == 0)
    def _():
        zero_sc[...] = jnp.zeros_like(zero_sc)
        lax.fori_loop(0, nb, fill, 0)
        load(0).start()
        pl.when(n > 1)(lambda: load(1).start())

    def wait_step(step):
        for _ in range(TOP_K):
            pltpu.make_async_copy(hbuf.at[0], xs_hbm.at[pl.ds(0, rows), :], sem.at[step % 2]).wait()

    load(i).wait()
    src_tile = hbuf.at[i % 3]

    def body(t, carry):
        src = src_tile.at[pl.ds(pl.multiple_of(t * ns, ns), ns), :]
        for k in range(TOP_K):
            d = dest_sm[(base + t) * TOP_K + k]
            pltpu.make_async_copy(src, xs_hbm.at[pl.ds(pl.multiple_of(d * ns, ns), ns), :],
                                  sem.at[i % 2]).start()
        return carry

    lax.fori_loop(0, tt, body, 0)
    pl.when(i > 0)(lambda: wait_step(i - 1))
    pl.when(i + 2 < n)(lambda: load(i + 2).start())

    @pl.when(i == n - 1)
    def _():
        wait_step(i)
        lax.fori_loop(0, nb, fill_wait, 0)


def _dispatch(dest_flat, nvalid, h_rows, n_slots, tt, blk, ns):
    nb = n_slots // blk
    t = h_rows.shape[0] // ns
    return pl.pallas_call(
        functools.partial(_dispatch_kernel, tt=tt, blk=blk, nb=nb, ns=ns),
        grid_spec=pltpu.PrefetchScalarGridSpec(
            num_scalar_prefetch=2,
            grid=(t // tt,),
            in_specs=[pl.BlockSpec(memory_space=pl.ANY)],
            out_specs=pl.BlockSpec(memory_space=pl.ANY),
            scratch_shapes=[pltpu.VMEM((blk * ns, LANES), F32), pltpu.VMEM((3, tt * ns, LANES), F32),
                            pltpu.SemaphoreType.DMA((3,)), pltpu.SemaphoreType.DMA((2,)),
                            pltpu.SemaphoreType.DMA(())],
        ),
        out_shape=jax.ShapeDtypeStruct((n_slots * ns, LANES), F32),
        compiler_params=_cparams(("arbitrary",)),
        name="dispatch",
    )(dest_flat, nvalid, h_rows)


def _expert_changed(be, i, last):
    ii = jnp.minimum(i, last)
    prev = jnp.maximum(ii - 1, 0)
    return (i == 0) | (be[ii] != be[prev])


def _for_live_sub_blocks(active, n_valid, sub, o_ref, compute, rows_per_slot=1):
    n_live = jnp.where(active, (n_valid + sub - 1) // sub, 0)
    for count in range(MOE_SUB_BLOCKS + 1):
        @pl.when(n_live == count)
        def _(count=count):
            if count == MOE_SUB_BLOCKS:
                compute(slice(0, count * sub))
            else:
                for r in range(count):
                    compute(slice(r * sub, (r + 1) * sub))
            if count < MOE_SUB_BLOCKS:
                first = count * sub * rows_per_slot
                o_ref[first:, :] = jnp.zeros((o_ref.shape[0] - first, o_ref.shape[1]), o_ref.dtype)


def _stream_expert_weights(changed, first, prefetch, wait_cur, cast, start_next):
    @pl.when(changed)
    def _():
        pl.when(first)(lambda: start_next(True))
        wait_cur()
        cast()
        pl.when(prefetch)(lambda: start_next(False))


def _gateup_kernel(be, nv, nu, nxt, x_ref, w_hbm, bg_ref, bu_ref, o_ref, wbuf, wg_sc, wu_sc, sem):
    j = pl.program_id(0)
    i = pl.program_id(1)
    nj = pl.num_programs(0)
    th = wg_sc.shape[1]
    dff = w_hbm.shape[2] // 2
    last = nu[0] - 1
    active = i < nu[0]
    ii = jnp.minimum(i, last)

    def copies(e, jj):
        col = pl.multiple_of(jj * th, th)
        return (pltpu.make_async_copy(w_hbm.at[e, :, pl.ds(col, th)], wbuf.at[0], sem.at[0]),
                pltpu.make_async_copy(w_hbm.at[e, :, pl.ds(dff + col, th)], wbuf.at[1], sem.at[1]))

    nx = nxt[ii]
    same_pass = nx >= 0
    e_next = jnp.where(same_pass, be[jnp.maximum(nx, 0)], be[0])
    j_next = jnp.where(same_pass, j, j + 1)

    def start_next(current):
        for cp in (copies(be[ii], j) if current else copies(e_next, j_next)):
            cp.start()

    def wait_cur():
        for cp in copies(be[ii], j):
            cp.wait()

    def cast():
        def chunk(r, carry):
            rows = pl.ds(pl.multiple_of(r * CAST_ROWS, CAST_ROWS), CAST_ROWS)
            wg_sc[rows, :] = wbuf[0, rows, :].astype(BF16)
            wu_sc[rows, :] = wbuf[1, rows, :].astype(BF16)
            return carry

        lax.fori_loop(0, wg_sc.shape[0] // CAST_ROWS, chunk, 0)

    _stream_expert_weights(active & _expert_changed(be, i, last), (j == 0) & (i == 0),
                           same_pass | (j + 1 < nj), wait_cur, cast, start_next)

    sub = o_ref.shape[0] // MOE_SUB_BLOCKS

    def compute(rows):
        x = _load_token_rows(x_ref, rows.start, rows.stop - rows.start, wg_sc.shape[0] // LANES).astype(BF16)
        g = jnp.dot(x, wg_sc[...], preferred_element_type=F32) + bg_ref[0]
        up = jnp.dot(x, wu_sc[...], preferred_element_type=F32) + bu_ref[0]
        gate = jnp.minimum(g, SWIGLU_LIMIT)
        up = jnp.clip(up, -SWIGLU_LIMIT, SWIGLU_LIMIT)
        o_ref[rows, :] = ((up + 1.0) * gate * _sigmoid(SWIGLU_ALPHA * gate)).astype(o_ref.dtype)

    _for_live_sub_blocks(active, nv[i], sub, o_ref, compute)


def _gateup(block_e, nvalid, nused, nxt, xs, w_gu, b_gu3, tm, th):
    d = w_gu.shape[1]
    ns = d // LANES
    n_slots = xs.shape[0] // ns
    dff = w_gu.shape[2] // 2
    nj = dff // th
    nb = n_slots // tm

    def blk(i, nu):
        return jnp.minimum(i, nu[0] - 1)

    return pl.pallas_call(
        _gateup_kernel,
        grid_spec=pltpu.PrefetchScalarGridSpec(
            num_scalar_prefetch=4,
            grid=(nj, nb),
            in_specs=[pl.BlockSpec((tm * ns, LANES), lambda j, i, be, nv, nu, nx: (blk(i, nu), 0)),
                      pl.BlockSpec(memory_space=pl.ANY),
                      pl.BlockSpec((1, 1, th), lambda j, i, be, nv, nu, nx: (be[blk(i, nu)], 0, j)),
                      pl.BlockSpec((1, 1, th), lambda j, i, be, nv, nu, nx: (be[blk(i, nu)], 0, nj + j))],
            out_specs=pl.BlockSpec((tm, th), lambda j, i, be, nv, nu, nx: (i, j)),
            scratch_shapes=[pltpu.VMEM((2, d, th), F32), pltpu.VMEM((d, th), BF16), pltpu.VMEM((d, th), BF16),
                            pltpu.SemaphoreType.DMA((2,))],
        ),
        out_shape=jax.ShapeDtypeStruct((n_slots, dff), BF16),
        compiler_params=_cparams(("arbitrary", "arbitrary")),
        name="expert_gate_up",
    )(block_e, nvalid, nused, nxt, xs, w_gu, b_gu3, b_gu3)


def _down_kernel(be, nv, nu, nxt, a_ref, w_hbm, bd_ref, o_ref, wbuf, wd_sc, sem):
    i = pl.program_id(1)
    last = nu[0] - 1
    active = i < nu[0]
    ii = jnp.minimum(i, last)
    nx = nxt[ii]

    def copy(e):
        return pltpu.make_async_copy(w_hbm.at[e], wbuf, sem)

    def start_next(current):
        copy(be[ii] if current else be[jnp.maximum(nx, 0)]).start()

    def cast():
        def chunk(r, carry):
            rows = pl.ds(pl.multiple_of(r * CAST_ROWS, CAST_ROWS), CAST_ROWS)
            wd_sc[rows, :] = wbuf[rows, :].astype(BF16)
            return carry

        lax.fori_loop(0, wd_sc.shape[0] // CAST_ROWS, chunk, 0)

    _stream_expert_weights(active & _expert_changed(be, i, last), i == 0, nx >= 0,
                           lambda: copy(be[ii]).wait(), cast, start_next)

    sub = a_ref.shape[0] // MOE_SUB_BLOCKS
    ns = o_ref.shape[0] // a_ref.shape[0]

    def compute(rows):
        y = jnp.dot(a_ref[rows, :], wd_sc[...], preferred_element_type=F32) + bd_ref[0]
        _store_token_rows(o_ref, rows.start, y)

    _for_live_sub_blocks(active, nv[i], sub, o_ref, compute, rows_per_slot=ns)


def _down(block_e, nvalid, nused, nxt, act, w_d, b_d3, tm):
    n_slots, dff = act.shape
    d = w_d.shape[2]
    ns = d // LANES
    nb = n_slots // tm

    def blk(i, nu):
        return jnp.minimum(i, nu[0] - 1)

    return pl.pallas_call(
        _down_kernel,
        grid_spec=pltpu.PrefetchScalarGridSpec(
            num_scalar_prefetch=4,
            grid=(1, nb),
            in_specs=[pl.BlockSpec((tm, dff), lambda j, i, be, nv, nu, nx: (blk(i, nu), 0)),
                      pl.BlockSpec(memory_space=pl.ANY),
                      pl.BlockSpec((1, 1, d), lambda j, i, be, nv, nu, nx: (be[blk(i, nu)], 0, 0))],
            out_specs=pl.BlockSpec((tm * ns, LANES), lambda j, i, be, nv, nu, nx: (i, 0)),
            scratch_shapes=[pltpu.VMEM((dff, d), F32), pltpu.VMEM((dff, d), BF16), pltpu.SemaphoreType.DMA(())],
        ),
        out_shape=jax.ShapeDtypeStruct((n_slots * ns, LANES), F32),
        compiler_params=_cparams(("arbitrary", "arbitrary")),
        name="expert_down",
    )(block_e, nvalid, nused, nxt, act, w_d, b_d3)


def _combine_kernel(dest_sm, y_hbm, w_ref, x1_ref, gt_ref, g_ref, o_ref, buf, sem, *, tt, ns):
    i = pl.program_id(0)
    n = pl.num_programs(0)

    def gather(tile, slot):
        def body(t, carry):
            for k in range(TOP_K):
                d = dest_sm[(tile * tt + t) * TOP_K + k]
                pltpu.make_async_copy(y_hbm.at[pl.ds(pl.multiple_of(d * ns, ns), ns), :],
                                      buf.at[slot, pl.ds(pl.multiple_of((k * tt + t) * ns, ns), ns), :],
                                      sem.at[slot]).start()
            return carry

        lax.fori_loop(0, tt, body, 0)

    slot = i % 2

    @pl.when(i == 0)
    def _():
        gather(0, 0)

    @pl.when(i + 1 < n)
    def _():
        gather(i + 1, 1 - slot)

    pltpu.make_async_copy(y_hbm.at[pl.ds(0, TOP_K * tt * ns), :], buf.at[slot], sem.at[slot]).wait()
    w = w_ref[...]
    rows = buf.at[slot]
    f = None
    for k in range(TOP_K):
        yk = _load_token_rows(rows, k * tt, tt, ns) * w[:, k:k + 1]
        f = yk if f is None else f + yk
    nf = f * lax.rsqrt(jnp.mean(f * f, axis=-1, keepdims=True) + NORM_EPS) * g_ref[...]
    o_ref[...] = x1_ref[...] + gt_ref[0] * nf


def _combine(dest_flat, y_rows, wts, x1, mod3, gpost, seq, tt, gt_blk):
    t, d = x1.shape
    ns = d // LANES
    per_b = seq // tt
    return pl.pallas_call(
        functools.partial(_combine_kernel, tt=tt, ns=ns),
        grid_spec=pltpu.PrefetchScalarGridSpec(
            num_scalar_prefetch=1,
            grid=(t // tt,),
            in_specs=[pl.BlockSpec(memory_space=pl.ANY),
                      pl.BlockSpec((tt, LANES), lambda i, ds: (i, 0)),
                      pl.BlockSpec((tt, d), lambda i, ds: (i, 0)),
                      pl.BlockSpec((1, 1, d), lambda i, ds: (i // per_b, 0, gt_blk)),
                      pl.BlockSpec((1, d), lambda i, ds: (0, 0))],
            out_specs=pl.BlockSpec((tt, d), lambda i, ds: (i, 0)),
            scratch_shapes=[pltpu.VMEM((2, TOP_K * tt * ns, LANES), F32), pltpu.SemaphoreType.DMA((2,))],
        ),
        out_shape=jax.ShapeDtypeStruct((t, d), F32),
        compiler_params=_cparams(("arbitrary",)),
        name="combine",
    )(dest_flat, y_rows, wts, x1, mod3, gpost)


def _tile(n, pref):
    t = min(n, pref)
    while n % t:
        t //= 2
    return t


def _rope_tables(seq):
    half = ROT_DIM // 2
    inv = ROPE_THETA ** (-jnp.arange(0, ROT_DIM, 2, dtype=F32) / ROT_DIM)
    ang = jnp.arange(seq, dtype=F32)[:, None] * inv[None, :]
    cos, sin = jnp.cos(ang), jnp.sin(ang)
    ones = jnp.ones((seq, ATTN_DK - ROT_DIM), F32)
    zeros = jnp.zeros((seq, ATTN_DK - ROT_DIM), F32)
    zh = jnp.zeros((seq, half), F32)
    c64 = jnp.concatenate([cos, cos, ones], axis=1)
    s1_64 = jnp.concatenate([-sin, zh, zeros], axis=1)
    s2_64 = jnp.concatenate([zh, sin, zeros], axis=1)
    rep = LANES // ATTN_DK
    return tuple(jnp.tile(t, (1, rep)) for t in (c64, s1_64, s2_64))


def _layer(x, c_pad, l, p, moe_blk):
    bsz, seq, d = x.shape
    t = bsz * seq
    dh = d
    n_sheads = dh // SSD_HEADDIM
    gn = SSD_GROUPS * SSD_STATE
    n_aheads = d // ATTN_DV
    aw = n_aheads * ATTN_DV
    qkw = 2 * n_aheads * ATTN_DK
    n_experts = p["w_router"].shape[-1]

    mod = _ada(c_pad, p["w_ada"][l], p["b_ada"][l][None, :], _tile(6 * d, ADA_COL_TILE))
    mod3 = mod[:bsz].reshape(bsz, 1, 6 * d)
    sh_m, sc_m, gt_m, sh_f, sc_f, gt_f = range(6)

    o = 0
    segs = {}
    for name, size in (("z", dh), ("xbc", dh + 2 * gn), ("dt", n_sheads), ("q", qkw), ("k", qkw),
                       ("v", aw), ("gs", d), ("ga", d)):
        segs[name] = (o, o + size)
        o += size
    scale = ATTN_DK ** -0.5 * math.log2(math.e)
    w_plain, w_rope, w_gate, w_dt = _regroup(jnp.swapaxes(p["w_in"][l], 0, 1), segs, scale)

    h = _prenorm(x, p["g_pre_mix"][l][None, :], mod3, sc_m, sh_m, _tile(seq, PRENORM_ROW_TILE)).reshape(t, d)
    tm = _tile(seq, PROJ_ROW_TILE)
    tn = lambda w: _tile(w.shape[1], PROJ_COL_TILE)
    plain = _mm(h, w_plain, BF16, tm, tn(w_plain), name="proj_plain")
    qk = _mm(h, w_rope, BF16, tm, tn(w_rope), "rope", _rope_tables(seq), name="proj_rope")
    gates = _mm(h, w_gate, BF16, tm, tn(w_gate), "sigmoid", name="proj_gate")
    dt_raw = _mm(h, w_dt, F32, tm, LANES, name="proj_dt")

    pad_h = lambda v: jnp.pad(v, (0, LANES - n_sheads))[None, :]
    plain3 = plain.reshape(bsz, seq, plain.shape[1])
    y_ssd = _ssd(plain3, dt_raw.reshape(bsz, seq, LANES), p["conv_w"][l], p["conv_b"][l][None, :],
                 pad_h(p["dt_bias"][l]), pad_h(p["a_log"][l]),
                 jnp.repeat(p["d_skip"][l], SSD_HEADDIM)[None, :], p["ssd_norm_w"][l][None, :], dh, n_sheads)

    lam_init = 0.8 - 0.6 * math.exp(-0.3 * l)
    lam_rows = jnp.zeros((8, LANES), F32)
    for r, nm in enumerate(("lambda_q1", "lambda_k1", "lambda_q2", "lambda_k2")):
        lam_rows = lam_rows.at[r, :ATTN_DK].set(p[nm][l])
    v_blk0 = (dh + dh + 2 * gn) // LANES
    o_attn = _attention(qk.reshape(bsz, seq, 2 * qkw), plain3, lam_rows, p["subln_w"][l][None, :],
                        n_aheads, v_blk0, _tile(seq, ATTN_Q_TILE), lam_init)

    tm2 = _tile(seq, OUTPROJ_ROW_TILE)
    merged = _merge(y_ssd.reshape(t, dh), o_attn.reshape(t, aw), p["w_br_ssd"][l].astype(BF16),
                    p["w_br_attn"][l].astype(BF16), gates, tm, _tile(d, PROJ_COL_TILE))
    wr = jnp.pad(p["w_router"][l], ((0, 0), (0, LANES - n_experts)))
    wr_hi = wr.astype(BF16)
    wr_lo = (wr - wr_hi.astype(F32)).astype(BF16)
    br = jnp.pad(p["b_router"][l], (0, LANES - n_experts))[None, :]
    x1, h_rows, logits = _outproj(merged, p["w_out"][l].astype(BF16), x.reshape(t, d),
                                    p["g_post_mix"][l][None, :], mod3, p["g_pre_ffn"][l][None, :],
                                    wr_hi, wr_lo, br, seq, tm2, gt_m, sc_f, sh_f)

    dest, wts, cnt = _route(logits, n_experts, moe_blk, _tile(t, ROUTE_ROW_TILE))
    counts = cnt[0, :n_experts].astype(jnp.int32)
    n_slots = t * TOP_K + n_experts * moe_blk
    nb = n_slots // moe_blk
    pblocks = (counts + moe_blk - 1) // moe_blk
    pend = jnp.cumsum(pblocks)
    nused = jnp.maximum(pend[-1], 1).astype(jnp.int32)
    bidx = jnp.arange(nb, dtype=jnp.int32)
    block_e = jnp.minimum(jnp.sum((pend[None, :] <= bidx[:, None]).astype(jnp.int32), axis=1), n_experts - 1)
    pstart = pend - pblocks
    nvalid = jnp.clip(counts[block_e] - (bidx - pstart[block_e]) * moe_blk, 0, moe_blk).astype(jnp.int32)
    dest_flat = dest[:, :TOP_K].reshape(-1)
    nused1 = nused.reshape(1)
    after = pend[block_e].astype(jnp.int32)
    nxt = jnp.where(after < nused, after, -1).astype(jnp.int32)

    xs = _dispatch(dest_flat, nvalid, h_rows, n_slots, _tile(t, DISPATCH_TOKENS), moe_blk, d // LANES)
    dff = p["w_down"].shape[2]
    act = _gateup(block_e, nvalid, nused1, nxt, xs, p["w_gate_up"][l], p["b_gate_up"][l][:, None, :],
                  moe_blk, _tile(dff, EXPERT_HIDDEN_TILE))
    y_sorted = _down(block_e, nvalid, nused1, nxt, act, p["w_down"][l], p["b_down"][l][:, None, :], moe_blk)
    out = _combine(dest_flat, y_sorted, wts, x1, mod3, p["g_post_ffn"][l][None, :], seq, _tile(seq, COMBINE_TOKENS), gt_f)
    return out.reshape(bsz, seq, d)


def kernel(x, c, w_ada, b_ada, g_pre_mix, g_post_mix, g_pre_ffn, g_post_ffn, w_in, conv_w, conv_b, dt_bias, a_log, d_skip, ssd_norm_w, lambda_q1, lambda_k1, lambda_q2, lambda_k2, subln_w, w_br_ssd, w_br_attn, w_out, w_router, b_router, w_gate_up, b_gate_up, w_down, b_down):
    p = dict(w_ada=w_ada, b_ada=b_ada, g_pre_mix=g_pre_mix, g_post_mix=g_post_mix, g_pre_ffn=g_pre_ffn,
             g_post_ffn=g_post_ffn, w_in=w_in, conv_w=conv_w, conv_b=conv_b, dt_bias=dt_bias, a_log=a_log,
             d_skip=d_skip, ssd_norm_w=ssd_norm_w, lambda_q1=lambda_q1, lambda_k1=lambda_k1,
             lambda_q2=lambda_q2, lambda_k2=lambda_k2, subln_w=subln_w, w_br_ssd=w_br_ssd,
             w_br_attn=w_br_attn, w_out=w_out, w_router=w_router, b_router=b_router, w_gate_up=w_gate_up,
             b_gate_up=b_gate_up, w_down=w_down, b_down=b_down)
    bsz = x.shape[0]
    c_pad = jnp.pad(c, ((0, (-bsz) % 8), (0, 0)))
    for l in range(w_ada.shape[0]):
        x = _layer(x, c_pad, l, p, min(MOE_ROW_BLOCK, x.shape[0] * x.shape[1]))
    return x
```

```python
import functools
import math

import jax
import jax.numpy as jnp
from jax import lax
from jax.experimental import pallas as pl
from jax.experimental.pallas import tpu as pltpu

F32 = jnp.float32
BF16 = jnp.bfloat16
HIGHEST = lax.Precision.HIGHEST

SSD_HEADDIM = 64
SSD_GROUPS = 4
SSD_STATE = 128
SSD_CONV = 4
SSD_CHUNK = 128
ATTN_DK = 64
ATTN_DV = 128
ATTN_KV_UNIT = 512
ATTN_Q_TILE = 512
ATTN_HEADS_PER_STEP = 4
ROT_DIM = ATTN_DK // 4
ROPE_THETA = 500000.0
TOP_K = 4
SWIGLU_LIMIT = 7.0
SWIGLU_ALPHA = 1.702
NORM_EPS = 1e-6
SUB_EPS = 1e-5
LANES = 128
NEG_BIG = -1e30

VMEM_BYTES = 64 * 1024 * 1024
VMEM_LIMIT = VMEM_BYTES - 8 * 1024 * 1024
ADA_COL_TILE = 1024
PRENORM_ROW_TILE = 1024
PROJ_ROW_TILE = 1024
PROJ_COL_TILE = 1024
OUTPROJ_ROW_TILE = 512
ROUTE_ROW_TILE = 1024
DISPATCH_TOKENS = 512
COMBINE_TOKENS = 256
EXPERT_HIDDEN_TILE = 1024
REGROUP_TILE = 512
MOE_ROW_BLOCK = 512
MOE_SUB_BLOCKS = 2
EPILOGUE_ROW_CHUNKS = 4
MERGE_ROW_CHUNKS = 2
CAST_ROWS = 128


def _cparams(sem, vmem=VMEM_LIMIT):
    return pltpu.CompilerParams(dimension_semantics=sem, vmem_limit_bytes=vmem)


def _sigmoid(x):
    return 1.0 / (1.0 + jnp.exp(-x))


def _store_token_rows(ref, first_tok, val):
    n_tok, width = val.shape
    ns = width // LANES
    for s_ in range(ns):
        ref[pl.ds(first_tok * ns + s_, n_tok, stride=ns), :] = val[:, s_ * LANES:(s_ + 1) * LANES]


def _load_token_rows(ref, first_tok, n_tok, ns):
    return jnp.concatenate([ref[pl.ds(first_tok * ns + s_, n_tok, stride=ns), :] for s_ in range(ns)], axis=1)


def _ada_kernel(c_ref, w_ref, b_ref, o_ref):
    c = c_ref[...]
    sc = c * _sigmoid(c)
    o_ref[...] = jnp.dot(sc, w_ref[...], preferred_element_type=F32, precision=HIGHEST) + b_ref[...]


def _ada(c_pad, w, b, tn):
    rows, d = c_pad.shape
    n = w.shape[1]
    return pl.pallas_call(
        _ada_kernel,
        grid=(n // tn,),
        in_specs=[pl.BlockSpec((rows, d), lambda j: (0, 0)),
                  pl.BlockSpec((d, tn), lambda j: (0, j)),
                  pl.BlockSpec((1, tn), lambda j: (0, j))],
        out_specs=pl.BlockSpec((rows, tn), lambda j: (0, j)),
        out_shape=jax.ShapeDtypeStruct((rows, n), F32),
        compiler_params=_cparams(("arbitrary",)),
        name="ada",
    )(c_pad, w, b)


def _prenorm_kernel(x_ref, g_ref, sc_ref, sh_ref, o_ref):
    x = x_ref[0]
    y = x * lax.rsqrt(jnp.mean(x * x, axis=-1, keepdims=True) + NORM_EPS) * g_ref[...]
    o_ref[0] = (y * (1.0 + sc_ref[0]) + sh_ref[0]).astype(o_ref.dtype)


def _prenorm(x, g, mod3, sc_blk, sh_blk, ts):
    b, s, d = x.shape
    return pl.pallas_call(
        _prenorm_kernel,
        grid=(b, s // ts),
        in_specs=[pl.BlockSpec((1, ts, d), lambda bi, si: (bi, si, 0)),
                  pl.BlockSpec((1, d), lambda bi, si: (0, 0)),
                  pl.BlockSpec((1, 1, d), lambda bi, si: (bi, 0, sc_blk)),
                  pl.BlockSpec((1, 1, d), lambda bi, si: (bi, 0, sh_blk))],
        out_specs=pl.BlockSpec((1, ts, d), lambda bi, si: (bi, si, 0)),
        out_shape=jax.ShapeDtypeStruct((b, s, d), BF16),
        compiler_params=_cparams(("arbitrary", "arbitrary")),
        name="prenorm",
    )(x, g, mod3, mod3)


def _regroup_kernel(src_sm, wt_hbm, plain_ref, rope_ref, gate_ref, dt_ref, buf, dt_buf, sem, dt_sem, *,
                    tiles, dt_rows, q_scale):
    t = pl.program_id(0)
    n = pl.num_programs(0)
    tr = REGROUP_TILE
    n_plain, n_rope, n_gate, n_q = tiles

    def load(step):
        row = pl.multiple_of(src_sm[step], 8)
        return pltpu.make_async_copy(wt_hbm.at[pl.ds(row, tr), :], buf.at[step % 2], sem.at[step % 2])

    dt_copy = pltpu.make_async_copy(wt_hbm.at[pl.ds(dt_rows[0], dt_rows[1]), :], dt_buf, dt_sem)

    @pl.when(t == 0)
    def _():
        load(0).start()
        dt_copy.start()

    pl.when(t + 1 < n)(lambda: load(t + 1).start())
    load(t).wait()
    cols = buf[t % 2].T

    @pl.when(t < n_plain)
    def _():
        plain_ref[...] = cols.astype(BF16)

    @pl.when((t >= n_plain) & (t < n_plain + n_rope))
    def _():
        rope_ref[...] = jnp.where(t < n_plain + n_q, cols * q_scale, cols).astype(BF16)

    @pl.when(t >= n_plain + n_rope)
    def _():
        gate_ref[...] = cols.astype(BF16)

    @pl.when(t == n - 1)
    def _():
        dt_copy.wait()
        dt_ref[...] = jnp.zeros_like(dt_ref)
        dt_ref[:, :dt_rows[1]] = dt_buf[...].T.astype(BF16)


def _regroup(w_in_t, segs, q_scale):
    n_in, d = w_in_t.shape
    tr = REGROUP_TILE
    starts = []
    counts = []
    for group in (("z", "xbc", "v"), ("q", "k"), ("gs", "ga")):
        c = 0
        for nm in group:
            lo, hi = segs[nm]
            assert (hi - lo) % tr == 0 and lo % 8 == 0
            starts += list(range(lo, hi, tr))
            c += (hi - lo) // tr
        counts.append(c)
    n_plain, n_rope, n_gate = counts
    n_q = (segs["q"][1] - segs["q"][0]) // tr
    dt_lo, dt_hi = segs["dt"]
    clamp = lambda t, first, cnt: jnp.clip(t - first, 0, cnt - 1)
    return pl.pallas_call(
        functools.partial(_regroup_kernel, tiles=(n_plain, n_rope, n_gate, n_q), dt_rows=(dt_lo, dt_hi - dt_lo),
                          q_scale=q_scale),
        grid_spec=pltpu.PrefetchScalarGridSpec(
            num_scalar_prefetch=1,
            grid=(len(starts),),
            in_specs=[pl.BlockSpec(memory_space=pl.ANY)],
            out_specs=[pl.BlockSpec((d, tr), lambda t, src: (0, clamp(t, 0, n_plain))),
                       pl.BlockSpec((d, tr), lambda t, src: (0, clamp(t, n_plain, n_rope))),
                       pl.BlockSpec((d, tr), lambda t, src: (0, clamp(t, n_plain + n_rope, n_gate))),
                       pl.BlockSpec((d, LANES), lambda t, src: (0, 0))],
            scratch_shapes=[pltpu.VMEM((2, tr, d), F32), pltpu.VMEM((dt_hi - dt_lo, d), F32),
                            pltpu.SemaphoreType.DMA((2,)), pltpu.SemaphoreType.DMA(())],
        ),
        out_shape=[jax.ShapeDtypeStruct((d, n_plain * tr), BF16), jax.ShapeDtypeStruct((d, n_rope * tr), BF16),
                   jax.ShapeDtypeStruct((d, n_gate * tr), BF16), jax.ShapeDtypeStruct((d, LANES), BF16)],
        compiler_params=_cparams(("arbitrary",)),
        name="regroup_w_in",
    )(jnp.asarray(starts, jnp.int32), w_in_t)


def _mm_kernel(a_ref, w_ref, o_ref):
    o_ref[...] = jnp.dot(a_ref[...], w_ref[...], preferred_element_type=F32).astype(o_ref.dtype)


def _row_chunks(n_rows, chunks):
    step = n_rows // chunks if n_rows % chunks == 0 and n_rows >= 64 * chunks else n_rows
    return [slice(r, r + step) for r in range(0, n_rows, step)]


def _mm_sigmoid_kernel(a_ref, w_ref, o_ref):
    for rows in _row_chunks(a_ref.shape[0], EPILOGUE_ROW_CHUNKS):
        acc = jnp.dot(a_ref[rows, :], w_ref[...], preferred_element_type=F32)
        o_ref[rows, :] = _sigmoid(acc).astype(o_ref.dtype)


def _mm_rope_kernel(a_ref, w_ref, c_ref, s1_ref, s2_ref, o_ref):
    for rows in _row_chunks(a_ref.shape[0], EPILOGUE_ROW_CHUNKS):
        acc = jnp.dot(a_ref[rows, :], w_ref[...], preferred_element_type=F32)
        c = c_ref[rows, :]
        s1 = s1_ref[rows, :]
        s2 = s2_ref[rows, :]
        for g in range(acc.shape[1] // LANES):
            blk = acc[:, g * LANES:(g + 1) * LANES]
            fwd = pltpu.roll(blk, LANES - ROT_DIM // 2, 1)
            bwd = pltpu.roll(blk, ROT_DIM // 2, 1)
            o_ref[rows, g * LANES:(g + 1) * LANES] = (blk * c + fwd * s1 + bwd * s2).astype(o_ref.dtype)


def _mm(a, w, out_dtype, tm, tn, epilogue="none", rope_tabs=None, name="mm"):
    m, k = a.shape
    n = w.shape[1]
    in_specs = [pl.BlockSpec((tm, k), lambda i, j: (i, 0)),
                pl.BlockSpec((k, tn), lambda i, j: (0, j))]
    args = [a, w]
    if epilogue == "rope":
        seq_blocks = rope_tabs[0].shape[0] // tm
        for t in rope_tabs:
            in_specs.append(pl.BlockSpec((tm, LANES), lambda i, j: (i % seq_blocks, 0)))
            args.append(t)
        kern = _mm_rope_kernel
    elif epilogue == "sigmoid":
        kern = _mm_sigmoid_kernel
    else:
        kern = _mm_kernel
    return pl.pallas_call(
        kern,
        grid=(m // tm, n // tn),
        in_specs=in_specs,
        out_specs=pl.BlockSpec((tm, tn), lambda i, j: (i, j)),
        out_shape=jax.ShapeDtypeStruct((m, n), out_dtype),
        compiler_params=_cparams(("arbitrary", "arbitrary")),
        name=name,
    )(*args)


def _ssd_kernel(z_ref, xs_ref, bm_ref, cm_ref, dt_ref, cwx_ref, cwb_ref, cwc_ref, cbx_ref, cbb_ref,
                cbc_ref, dtb_ref, alog_ref, dsk_ref, nw_ref, o_ref,
                px_sc, pb_sc, pc_sc, st_sc, y_sc, *, n_heads):
    c = pl.program_id(1)
    L = SSD_CHUNK
    N = SSD_STATE
    hpg = n_heads // SSD_GROUPS
    gw = hpg * SSD_HEADDIM

    @pl.when(c == 0)
    def _():
        px_sc[...] = jnp.zeros_like(px_sc)
        pb_sc[...] = jnp.zeros_like(pb_sc)
        pc_sc[...] = jnp.zeros_like(pc_sc)
        st_sc[...] = jnp.zeros_like(st_sc)

    def conv_silu(u, prev_sc, w_ref, b_ref):
        prev = prev_sc[...]
        row = lax.broadcasted_iota(jnp.int32, u.shape, 0)
        acc = u * w_ref[SSD_CONV - 1:SSD_CONV, :] + b_ref[...]
        for j in range(1, SSD_CONV):
            sh = jnp.where(row < j, pltpu.roll(prev, j, 0), pltpu.roll(u, j, 0))
            acc = acc + sh * w_ref[SSD_CONV - 1 - j:SSD_CONV - j, :]
        prev_sc[...] = u
        return acc * _sigmoid(acc)

    xs = conv_silu(xs_ref[0].astype(F32), px_sc, cwx_ref, cbx_ref)
    bm = conv_silu(bm_ref[0].astype(F32), pb_sc, cwb_ref, cbb_ref)
    cm = conv_silu(cm_ref[0].astype(F32), pc_sc, cwc_ref, cbc_ref)

    raw = dt_ref[0] + dtb_ref[...]
    dt = jnp.maximum(raw, 0.0) + jnp.log1p(jnp.exp(-jnp.abs(raw)))
    a = dt * (-jnp.exp(alog_ref[...]))
    r_i = lax.broadcasted_iota(jnp.int32, (L, L), 0)
    c_i = lax.broadcasted_iota(jnp.int32, (L, L), 1)
    causal = r_i >= c_i
    tri = jnp.where(causal, 1.0, 0.0).astype(F32)
    a_cs = jnp.dot(tri, a, preferred_element_type=F32, precision=HIGHEST)
    a_cs_t = a_cs.T

    lane = lax.broadcasted_iota(jnp.int32, (L, LANES), 1)
    first = lane < SSD_HEADDIM
    acs_tiles = []
    dt_tiles = []
    for j in range(n_heads // 2):
        h0, h1 = 2 * j, 2 * j + 1
        acs_tiles.append(jnp.where(first, a_cs[:, h0:h0 + 1], a_cs[:, h1:h1 + 1]))
        dt_tiles.append(jnp.where(first, dt[:, h0:h0 + 1], dt[:, h1:h1 + 1]))
    acs_e = jnp.concatenate(acs_tiles, axis=1)
    dt_e = jnp.concatenate(dt_tiles, axis=1)
    xdt = xs * dt_e
    ea = jnp.exp(acs_e)
    alast = acs_e[L - 1:L, :]
    xdec = (xdt * jnp.exp(alast - acs_e)).astype(BF16)
    ealast = jnp.exp(alast)

    for g in range(SSD_GROUPS):
        bg = bm[:, g * N:(g + 1) * N]
        cg = cm[:, g * N:(g + 1) * N].astype(BF16)
        scores = lax.dot_general(cg, bg.astype(BF16), (((1,), (1,)), ((), ())),
                                 preferred_element_type=F32)
        st_old = st_sc[:, g * gw:(g + 1) * gw]
        y_off = jnp.dot(cg, st_old.astype(BF16), preferred_element_type=F32) * ea[:, g * gw:(g + 1) * gw]
        st_sc[:, g * gw:(g + 1) * gw] = st_old * ealast[:, g * gw:(g + 1) * gw] + jnp.dot(
            bg.T.astype(BF16), xdec[:, g * gw:(g + 1) * gw], preferred_element_type=F32)
        for jj in range(hpg // 2):
            j = g * (hpg // 2) + jj
            h0, h1 = 2 * j, 2 * j + 1
            la = jnp.exp(jnp.where(causal, a_cs[:, h0:h0 + 1] - a_cs_t[h0:h0 + 1, :], NEG_BIG))
            lb = jnp.exp(jnp.where(causal, a_cs[:, h1:h1 + 1] - a_cs_t[h1:h1 + 1, :], NEG_BIG))
            mcat = jnp.concatenate([(scores * la).astype(BF16), (scores * lb).astype(BF16)], axis=1)
            xp = xdt[:, j * LANES:(j + 1) * LANES]
            xcat = jnp.concatenate([jnp.where(first, xp, 0.0).astype(BF16),
                                    jnp.where(first, 0.0, xp).astype(BF16)], axis=0)
            y_diag = jnp.dot(mcat, xcat, preferred_element_type=F32)
            lo = jj * LANES
            y_sc[:, j * LANES:(j + 1) * LANES] = (
                y_diag + y_off[:, lo:lo + LANES]
                + dsk_ref[:, j * LANES:(j + 1) * LANES] * xs[:, j * LANES:(j + 1) * LANES])

    z = z_ref[0].astype(F32)
    u = y_sc[...] * (z * _sigmoid(z))
    for g in range(SSD_GROUPS):
        ug = u[:, g * gw:(g + 1) * gw]
        ms = jnp.mean(ug * ug, axis=-1, keepdims=True)
        o_ref[0, :, g * gw:(g + 1) * gw] = (ug * lax.rsqrt(ms + SUB_EPS)
                                            * nw_ref[:, g * gw:(g + 1) * gw]).astype(o_ref.dtype)


def _ssd(plain3, dt3, conv_w, conv_b, dtb, alog, dsk_e, norm_w, dh, n_heads):
    b, s, _ = plain3.shape
    L = SSD_CHUNK
    gn = SSD_GROUPS * SSD_STATE
    nc = s // L
    xblk = 1
    bblk = (2 * dh) // gn
    cw_x, cw_b, cw_c = conv_w[:, :dh], conv_w[:, dh:dh + gn], conv_w[:, dh + gn:]
    cb_x, cb_b, cb_c = conv_b[:, :dh], conv_b[:, dh:dh + gn], conv_b[:, dh + gn:]
    full = lambda shape: pl.BlockSpec(shape, lambda bi, ci: (0, 0))
    return pl.pallas_call(
        functools.partial(_ssd_kernel, n_heads=n_heads),
        grid=(b, nc),
        in_specs=[pl.BlockSpec((1, L, dh), lambda bi, ci: (bi, ci, 0)),
                  pl.BlockSpec((1, L, dh), lambda bi, ci: (bi, ci, xblk)),
                  pl.BlockSpec((1, L, gn), lambda bi, ci: (bi, ci, bblk)),
                  pl.BlockSpec((1, L, gn), lambda bi, ci: (bi, ci, bblk + 1)),
                  pl.BlockSpec((1, L, LANES), lambda bi, ci: (bi, ci, 0)),
                  full((SSD_CONV, dh)), full((SSD_CONV, gn)), full((SSD_CONV, gn)),
                  full((1, dh)), full((1, gn)), full((1, gn)),
                  full((1, LANES)), full((1, LANES)), full((1, dh)), full((1, dh))],
        out_specs=pl.BlockSpec((1, L, dh), lambda bi, ci: (bi, ci, 0)),
        out_shape=jax.ShapeDtypeStruct((b, s, dh), BF16),
        scratch_shapes=[pltpu.VMEM((L, dh), F32), pltpu.VMEM((L, gn), F32), pltpu.VMEM((L, gn), F32),
                        pltpu.VMEM((SSD_STATE, dh), F32), pltpu.VMEM((L, dh), F32)],
        compiler_params=_cparams(("arbitrary", "arbitrary")),
        name="ssd",
    )(plain3, plain3, plain3, plain3, dt3, cw_x, cw_b, cw_c, cb_x, cb_b, cb_c, dtb, alog, dsk_e, norm_w)


def _attn_kernel(q_ref, k_ref, v_ref, bias_ref, lam_ref, sw_ref, o_ref, vt_sc, st_a, st_b, m_sc, l_sc, acc_sc, *,
                 tq, lam_init):
    qi = pl.program_id(2)
    tu = ATTN_KV_UNIT
    nh = ATTN_HEADS_PER_STEP
    n_all = k_ref.shape[1] // tu
    heads = [slice(hh * LANES, (hh + 1) * LANES) for hh in range(nh)]

    @pl.when(qi == 0)
    def _():
        def transpose_block(c, carry):
            start = pl.multiple_of(c * tu, tu)
            for hh in range(nh):
                vt_sc[hh, c] = v_ref[0, pl.ds(start, tu), heads[hh]].astype(F32).T.astype(BF16)
            return carry

        lax.fori_loop(0, n_all, transpose_block, 0)

    qts = []
    for hh in range(nh):
        qt = q_ref[0, :, heads[hh]].astype(F32).T
        row = lax.broadcasted_iota(jnp.int32, qt.shape, 0)
        qts.append((jnp.where(row < ATTN_DK, qt, 0.0).astype(BF16),
                    jnp.where(row < ATTN_DK, 0.0, qt).astype(BF16)))
    m_sc[...] = jnp.full_like(m_sc, NEG_BIG)
    l_sc[...] = jnp.zeros_like(l_sc)
    acc_sc[...] = jnp.zeros_like(acc_sc)

    def scores(u, st_ref):
        start = pl.multiple_of(u * tu, tu)
        for hh in range(nh):
            k = k_ref[0, pl.ds(start, tu), heads[hh]]
            for m in range(2):
                st_ref[2 * hh + m] = jnp.dot(k, qts[hh][m], preferred_element_type=F32)

    def update(u, st_ref, masked):
        for hh in range(nh):
            vt = vt_sc[hh, u]
            for m in range(2):
                c = 2 * hh + m
                st = st_ref[c]
                if masked:
                    st = st + bias_ref[...]
                m_prev = m_sc[c]
                m_new = jnp.maximum(m_prev, jnp.max(st, axis=0, keepdims=True))
                alpha = jnp.exp2(m_prev - m_new)
                pt = jnp.exp2(st - m_new)
                l_sc[c] = alpha * l_sc[c] + jnp.sum(pt, axis=0, keepdims=True)
                acc_sc[c] = alpha * acc_sc[c] + jnp.dot(vt, pt.astype(BF16), preferred_element_type=F32)
                m_sc[c] = m_new

    n_units = (qi * tq) // tu + 1
    n_loop = (n_units - 1) // 2
    scores(0, st_a)

    def two_units(j, carry):
        u = 2 * j
        scores(u + 1, st_b)
        update(u, st_a, False)
        scores(u + 2, st_a)
        update(u + 1, st_b, False)
        return carry

    lax.fori_loop(0, n_loop, two_units, 0)
    last = n_units - 1

    @pl.when(last == 2 * n_loop)
    def _():
        update(last, st_a, True)

    @pl.when(last != 2 * n_loop)
    def _():
        scores(last, st_b)
        update(last - 1, st_a, False)
        update(last, st_b, True)

    lv = lam_ref[...]
    lam = (jnp.exp(jnp.sum(lv[0:1] * lv[1:2], axis=-1, keepdims=True))
           - jnp.exp(jnp.sum(lv[2:3] * lv[3:4], axis=-1, keepdims=True)) + lam_init)
    for hh in range(nh):
        c0, c1 = 2 * hh, 2 * hh + 1
        ot = acc_sc[c0] * (1.0 / l_sc[c0]) - lam * (acc_sc[c1] * (1.0 / l_sc[c1]))
        ot = ot * lax.rsqrt(jnp.mean(ot * ot, axis=0, keepdims=True) + SUB_EPS)
        o_ref[0, :, heads[hh]] = (ot.T * sw_ref[...] * (1.0 - lam_init)).astype(o_ref.dtype)


def _attention(qk3, plain3, lam_rows, subln_w, n_heads, v_blk0, tq, lam_init):
    b, s, _ = qk3.shape
    tu = ATTN_KV_UNIT
    nh = ATTN_HEADS_PER_STEP
    hw = nh * LANES
    assert tq == tu, "the diagonal unit must coincide with the query tile"
    kpos = lax.broadcasted_iota(jnp.int32, (tu, tq), 0)
    qpos = lax.broadcasted_iota(jnp.int32, (tu, tq), 1)
    diag_bias = jnp.where(kpos <= qpos, 0.0, NEG_BIG).astype(F32)
    return pl.pallas_call(
        functools.partial(_attn_kernel, tq=tq, lam_init=lam_init),
        grid=(b, n_heads // nh, s // tq),
        in_specs=[pl.BlockSpec((1, tq, hw), lambda bi, hi, qi: (bi, qi, hi)),
                  pl.BlockSpec((1, s, hw), lambda bi, hi, qi: (bi, 0, n_heads // nh + hi)),
                  pl.BlockSpec((1, s, hw), lambda bi, hi, qi: (bi, 0, v_blk0 // nh + hi)),
                  pl.BlockSpec((tu, tq), lambda bi, hi, qi: (0, 0)),
                  pl.BlockSpec((8, LANES), lambda bi, hi, qi: (0, 0)),
                  pl.BlockSpec((1, LANES), lambda bi, hi, qi: (0, 0))],
        out_specs=pl.BlockSpec((1, tq, hw), lambda bi, hi, qi: (bi, qi, hi)),
        out_shape=jax.ShapeDtypeStruct((b, s, n_heads * ATTN_DV), BF16),
        scratch_shapes=[pltpu.VMEM((nh, s // tu, ATTN_DV, tu), BF16),
                        pltpu.VMEM((2 * nh, tu, tq), F32), pltpu.VMEM((2 * nh, tu, tq), F32),
                        pltpu.VMEM((2 * nh, 1, tq), F32), pltpu.VMEM((2 * nh, 1, tq), F32),
                        pltpu.VMEM((2 * nh, ATTN_DV, tq), F32)],
        compiler_params=_cparams(("arbitrary", "arbitrary", "arbitrary")),
        name="diff_attn",
    )(qk3, qk3, plain3, diag_bias, lam_rows, subln_w)


def _merge_kernel(y_ref, o_ref, ws_ref, wa_ref, gs_ref, ga_ref, out_ref):
    for rows in _row_chunks(y_ref.shape[0], MERGE_ROW_CHUNKS):
        bs = jnp.dot(y_ref[rows, :], ws_ref[...], preferred_element_type=F32)
        ba = jnp.dot(o_ref[rows, :], wa_ref[...], preferred_element_type=F32)
        out_ref[rows, :] = (gs_ref[rows, :].astype(F32) * bs
                            + ga_ref[rows, :].astype(F32) * ba).astype(out_ref.dtype)


def _merge(y, o, ws, wa, gates, tm, tn):
    m, k = y.shape
    n = ws.shape[1]
    nj = n // tn
    return pl.pallas_call(
        _merge_kernel,
        grid=(m // tm, nj),
        in_specs=[pl.BlockSpec((tm, k), lambda i, j: (i, 0)),
                  pl.BlockSpec((tm, o.shape[1]), lambda i, j: (i, 0)),
                  pl.BlockSpec((k, tn), lambda i, j: (0, j)),
                  pl.BlockSpec((o.shape[1], tn), lambda i, j: (0, j)),
                  pl.BlockSpec((tm, tn), lambda i, j: (i, j)),
                  pl.BlockSpec((tm, tn), lambda i, j: (i, nj + j))],
        out_specs=pl.BlockSpec((tm, tn), lambda i, j: (i, j)),
        out_shape=jax.ShapeDtypeStruct((m, n), BF16),
        compiler_params=_cparams(("arbitrary", "arbitrary")),
        name="merge",
    )(y, o, ws, wa, gates, gates)


def _outproj_kernel(mg_ref, wo_ref, x_ref, gpost_ref, gt_ref, gpre_ref, sc_ref, sh_ref, wr_hi_ref,
                    wr_lo_ref, br_ref, x1_ref, hp_ref, lg_ref):
    mix = jnp.dot(mg_ref[...], wo_ref[...], preferred_element_type=F32)
    nm = mix * lax.rsqrt(jnp.mean(mix * mix, axis=-1, keepdims=True) + NORM_EPS) * gpost_ref[...]
    x1 = x_ref[...] + gt_ref[0] * nm
    x1_ref[...] = x1
    h2 = (x1 * lax.rsqrt(jnp.mean(x1 * x1, axis=-1, keepdims=True) + NORM_EPS) * gpre_ref[...]
          * (1.0 + sc_ref[0]) + sh_ref[0])
    h_hi = h2.astype(BF16)
    h_lo = (h2 - h_hi.astype(F32)).astype(BF16)
    wr_hi = wr_hi_ref[...]
    lg_ref[...] = (jnp.dot(h_hi, wr_hi, preferred_element_type=F32)
                   + jnp.dot(h_lo, wr_hi, preferred_element_type=F32)
                   + jnp.dot(h_hi, wr_lo_ref[...], preferred_element_type=F32) + br_ref[...])
    _store_token_rows(hp_ref, 0, h2)


def _outproj(merged, wo, x2, gpost, mod3, gpre, wr_hi, wr_lo, br, seq, tm, gt_blk, sc_blk, sh_blk):
    m, d = x2.shape
    per_b = seq // tm
    row = lambda i: (i, 0)
    const = lambda i: (0, 0)
    return pl.pallas_call(
        _outproj_kernel,
        grid=(m // tm,),
        in_specs=[pl.BlockSpec((tm, d), row),
                  pl.BlockSpec((d, d), const),
                  pl.BlockSpec((tm, d), row),
                  pl.BlockSpec((1, d), const),
                  pl.BlockSpec((1, 1, d), lambda i: (i // per_b, 0, gt_blk)),
                  pl.BlockSpec((1, d), const),
                  pl.BlockSpec((1, 1, d), lambda i: (i // per_b, 0, sc_blk)),
                  pl.BlockSpec((1, 1, d), lambda i: (i // per_b, 0, sh_blk)),
                  pl.BlockSpec((d, LANES), const),
                  pl.BlockSpec((d, LANES), const),
                  pl.BlockSpec((1, LANES), const)],
        out_specs=[pl.BlockSpec((tm, d), row),
                   pl.BlockSpec((tm * (d // LANES), LANES), row),
                   pl.BlockSpec((tm, LANES), row)],
        out_shape=[jax.ShapeDtypeStruct((m, d), F32),
                   jax.ShapeDtypeStruct((m * (d // LANES), LANES), F32),
                   jax.ShapeDtypeStruct((m, LANES), F32)],
        compiler_params=_cparams(("arbitrary",)),
        name="outproj",
    )(merged, wo, x2, gpost, mod3, gpre, mod3, mod3, wr_hi, wr_lo, br)


def _route_kernel(lg_ref, dest_ref, w_ref, cnt_ref, cnt_sc, pst_sc, run_sc, *, n_experts, blk):
    ph = pl.program_id(0)
    t = pl.program_id(1)
    tk = lg_ref.shape[0]
    lane = lax.broadcasted_iota(jnp.int32, (tk, LANES), 1)
    lg = jnp.where(lane < n_experts, lg_ref[...], -jnp.inf)
    vals = []
    hots = []
    for _ in range(TOP_K):
        mx = jnp.max(lg, axis=-1, keepdims=True)
        ix = jnp.min(jnp.where(lg == mx, lane, LANES), axis=-1, keepdims=True)
        hot = lane == ix
        lg = jnp.where(hot, -jnp.inf, lg)
        vals.append(mx)
        hots.append(hot)
    multi = jnp.zeros((tk, LANES), F32)
    for hot in hots:
        multi = multi + jnp.where(hot, 1.0, 0.0)
    colsum = jnp.sum(multi, axis=0, keepdims=True)

    @pl.when((ph == 0) & (t == 0))
    def _():
        cnt_sc[...] = jnp.zeros_like(cnt_sc)

    @pl.when(ph == 0)
    def _():
        cnt_sc[...] += colsum

    @pl.when((ph == 1) & (t == 0))
    def _():
        cnt = cnt_sc[...].astype(jnp.int32)
        padded = (((cnt + (blk - 1)) // blk) * blk).astype(F32)
        r_i = lax.broadcasted_iota(jnp.int32, (LANES, LANES), 0)
        c_i = lax.broadcasted_iota(jnp.int32, (LANES, LANES), 1)
        upper = jnp.where(r_i < c_i, 1.0, 0.0).astype(F32)
        pst_sc[...] = jnp.dot(jnp.broadcast_to(padded, (8, LANES)), upper,
                              preferred_element_type=F32, precision=HIGHEST)[0:1]
        run_sc[...] = jnp.zeros_like(run_sc)

    @pl.when(ph == 1)
    def _():
        r_i = lax.broadcasted_iota(jnp.int32, (tk, tk), 0)
        c_i = lax.broadcasted_iota(jnp.int32, (tk, tk), 1)
        strict = jnp.where(r_i > c_i, 1.0, 0.0).astype(BF16)
        before = jnp.dot(strict, multi.astype(BF16), preferred_element_type=F32)
        base = before + run_sc[...] + pst_sc[...]
        esum = jnp.zeros((tk, 1), F32)
        evals = []
        for r in range(TOP_K):
            e = jnp.exp(vals[r] - vals[0])
            evals.append(e)
            esum = esum + e
        dest = jnp.zeros((tk, LANES), jnp.int32)
        wts = jnp.zeros((tk, LANES), F32)
        for r in range(TOP_K):
            d_r = jnp.sum(jnp.where(hots[r], base, 0.0), axis=-1, keepdims=True).astype(jnp.int32)
            dest = jnp.where(lane == r, d_r, dest)
            wts = jnp.where(lane == r, evals[r] / esum, wts)
        dest_ref[...] = dest
        w_ref[...] = wts
        run_sc[...] += colsum
        cnt_ref[...] = jnp.broadcast_to(cnt_sc[...], cnt_ref.shape)


def _route(logits, n_experts, blk, tk):
    t = logits.shape[0]
    return pl.pallas_call(
        functools.partial(_route_kernel, n_experts=n_experts, blk=blk),
        grid=(2, t // tk),
        in_specs=[pl.BlockSpec((tk, LANES), lambda ph, ti: (ti, 0))],
        out_specs=[pl.BlockSpec((tk, LANES), lambda ph, ti: (ti * ph, 0)),
                   pl.BlockSpec((tk, LANES), lambda ph, ti: (ti * ph, 0)),
                   pl.BlockSpec((8, LANES), lambda ph, ti: (0, 0))],
        out_shape=[jax.ShapeDtypeStruct((t, LANES), jnp.int32),
                   jax.ShapeDtypeStruct((t, LANES), F32),
                   jax.ShapeDtypeStruct((8, LANES), F32)],
        scratch_shapes=[pltpu.VMEM((1, LANES), F32), pltpu.VMEM((1, LANES), F32), pltpu.VMEM((1, LANES), F32)],
        compiler_params=_cparams(("arbitrary", "arbitrary")),
        name="route",
    )(logits)


def _pad_fill_copies(b, nv_sm, zero_sc, xs_hbm, sem, blk, ns):
    nv = nv_sm[b]
    out = []
    off = b * blk + nv
    rest = blk - nv
    p = blk
    while p >= 1:
        cond = (rest & p) != 0
        out.append((cond, pltpu.make_async_copy(
            zero_sc.at[pl.ds(0, p * ns), :], xs_hbm.at[pl.ds(pl.multiple_of(off * ns, ns), p * ns), :], sem)))
        off = off + jnp.where(cond, p, 0)
        p //= 2
    return out


def _dispatch_kernel(dest_sm, nv_sm, h_hbm, xs_hbm, zero_sc, hbuf, in_sem, sem, fill_sem, *, tt, blk, nb, ns):
    i = pl.program_id(0)
    n = pl.num_programs(0)
    base = i * tt
    rows = tt * ns

    def fill(b, carry):
        for cond, cp in _pad_fill_copies(b, nv_sm, zero_sc, xs_hbm, fill_sem, blk, ns):
            pl.when(cond)(cp.start)
        return carry

    def fill_wait(b, carry):
        for cond, cp in _pad_fill_copies(b, nv_sm, zero_sc, xs_hbm, fill_sem, blk, ns):
            pl.when(cond)(cp.wait)
        return carry

    def load(step):
        slot = step % 3
        return pltpu.make_async_copy(h_hbm.at[pl.ds(pl.multiple_of(step * rows, rows), rows), :],
                                     hbuf.at[slot], in_sem.at[slot])

    @pl.when(i == 0)
    def _():
        zero_sc[...] = jnp.zeros_like(zero_sc)
        lax.fori_loop(0, nb, fill, 0)
        load(0).start()
        pl.when(n > 1)(lambda: load(1).start())

    def wait_step(step):
        for _ in range(TOP_K):
            pltpu.make_async_copy(hbuf.at[0], xs_hbm.at[pl.ds(0, rows), :], sem.at[step % 2]).wait()

    load(i).wait()
    src_tile = hbuf.at[i % 3]

    def body(t, carry):
        src = src_tile.at[pl.ds(pl.multiple_of(t * ns, ns), ns), :]
        for k in range(TOP_K):
            d = dest_sm[(base + t) * TOP_K + k]
            pltpu.make_async_copy(src, xs_hbm.at[pl.ds(pl.multiple_of(d * ns, ns), ns), :],
                                  sem.at[i % 2]).start()
        return carry

    lax.fori_loop(0, tt, body, 0)
    pl.when(i > 0)(lambda: wait_step(i - 1))
    pl.when(i + 2 < n)(lambda: load(i + 2).start())

    @pl.when(i == n - 1)
    def _():
        wait_step(i)
        lax.fori_loop(0, nb, fill_wait, 0)


def _dispatch(dest_flat, nvalid, h_rows, n_slots, tt, blk, ns):
    nb = n_slots // blk
    t = h_rows.shape[0] // ns
    return pl.pallas_call(
        functools.partial(_dispatch_kernel, tt=tt, blk=blk, nb=nb, ns=ns),
        grid_spec=pltpu.PrefetchScalarGridSpec(
            num_scalar_prefetch=2,
            grid=(t // tt,),
            in_specs=[pl.BlockSpec(memory_space=pl.ANY)],
            out_specs=pl.BlockSpec(memory_space=pl.ANY),
            scratch_shapes=[pltpu.VMEM((blk * ns, LANES), F32), pltpu.VMEM((3, tt * ns, LANES), F32),
                            pltpu.SemaphoreType.DMA((3,)), pltpu.SemaphoreType.DMA((2,)),
                            pltpu.SemaphoreType.DMA(())],
        ),
        out_shape=jax.ShapeDtypeStruct((n_slots * ns, LANES), F32),
        compiler_params=_cparams(("arbitrary",)),
        name="dispatch",
    )(dest_flat, nvalid, h_rows)


def _expert_changed(be, i, last):
    ii = jnp.minimum(i, last)
    prev = jnp.maximum(ii - 1, 0)
    return (i == 0) | (be[ii] != be[prev])


def _for_live_sub_blocks(active, n_valid, sub, o_ref, compute, rows_per_slot=1):
    n_live = jnp.where(active, (n_valid + sub - 1) // sub, 0)
    for count in range(MOE_SUB_BLOCKS + 1):
        @pl.when(n_live == count)
        def _(count=count):
            if count == MOE_SUB_BLOCKS:
                compute(slice(0, count * sub))
            else:
                for r in range(count):
                    compute(slice(r * sub, (r + 1) * sub))
            if count < MOE_SUB_BLOCKS:
                first = count * sub * rows_per_slot
                o_ref[first:, :] = jnp.zeros((o_ref.shape[0] - first, o_ref.shape[1]), o_ref.dtype)


def _stream_expert_weights(changed, first, prefetch, wait_cur, cast, start_next):
    @pl.when(changed)
    def _():
        pl.when(first)(lambda: start_next(True))
        wait_cur()
        cast()
        pl.when(prefetch)(lambda: start_next(False))


def _gateup_kernel(be, nv, nu, nxt, x_ref, w_hbm, bg_ref, bu_ref, o_ref, wbuf, wg_sc, wu_sc, sem):
    j = pl.program_id(0)
    i = pl.program_id(1)
    nj = pl.num_programs(0)
    th = wg_sc.shape[1]
    dff = w_hbm.shape[2] // 2
    last = nu[0] - 1
    active = i < nu[0]
    ii = jnp.minimum(i, last)

    def copies(e, jj):
        col = pl.multiple_of(jj * th, th)
        return (pltpu.make_async_copy(w_hbm.at[e, :, pl.ds(col, th)], wbuf.at[0], sem.at[0]),
                pltpu.make_async_copy(w_hbm.at[e, :, pl.ds(dff + col, th)], wbuf.at[1], sem.at[1]))

    nx = nxt[ii]
    same_pass = nx >= 0
    e_next = jnp.where(same_pass, be[jnp.maximum(nx, 0)], be[0])
    j_next = jnp.where(same_pass, j, j + 1)

    def start_next(current):
        for cp in (copies(be[ii], j) if current else copies(e_next, j_next)):
            cp.start()

    def wait_cur():
        for cp in copies(be[ii], j):
            cp.wait()

    def cast():
        def chunk(r, carry):
            rows = pl.ds(pl.multiple_of(r * CAST_ROWS, CAST_ROWS), CAST_ROWS)
            wg_sc[rows, :] = wbuf[0, rows, :].astype(BF16)
            wu_sc[rows, :] = wbuf[1, rows, :].astype(BF16)
            return carry

        lax.fori_loop(0, wg_sc.shape[0] // CAST_ROWS, chunk, 0)

    _stream_expert_weights(active & _expert_changed(be, i, last), (j == 0) & (i == 0),
                           same_pass | (j + 1 < nj), wait_cur, cast, start_next)

    sub = o_ref.shape[0] // MOE_SUB_BLOCKS

    def compute(rows):
        x = _load_token_rows(x_ref, rows.start, rows.stop - rows.start, wg_sc.shape[0] // LANES).astype(BF16)
        g = jnp.dot(x, wg_sc[...], preferred_element_type=F32) + bg_ref[0]
        up = jnp.dot(x, wu_sc[...], preferred_element_type=F32) + bu_ref[0]
        gate = jnp.minimum(g, SWIGLU_LIMIT)
        up = jnp.clip(up, -SWIGLU_LIMIT, SWIGLU_LIMIT)
        o_ref[rows, :] = ((up + 1.0) * gate * _sigmoid(SWIGLU_ALPHA * gate)).astype(o_ref.dtype)

    _for_live_sub_blocks(active, nv[i], sub, o_ref, compute)


def _gateup(block_e, nvalid, nused, nxt, xs, w_gu, b_gu3, tm, th):
    d = w_gu.shape[1]
    ns = d // LANES
    n_slots = xs.shape[0] // ns
    dff = w_gu.shape[2] // 2
    nj = dff // th
    nb = n_slots // tm

    def blk(i, nu):
        return jnp.minimum(i, nu[0] - 1)

    return pl.pallas_call(
        _gateup_kernel,
        grid_spec=pltpu.PrefetchScalarGridSpec(
            num_scalar_prefetch=4,
            grid=(nj, nb),
            in_specs=[pl.BlockSpec((tm * ns, LANES), lambda j, i, be, nv, nu, nx: (blk(i, nu), 0)),
                      pl.BlockSpec(memory_space=pl.ANY),
                      pl.BlockSpec((1, 1, th), lambda j, i, be, nv, nu, nx: (be[blk(i, nu)], 0, j)),
                      pl.BlockSpec((1, 1, th), lambda j, i, be, nv, nu, nx: (be[blk(i, nu)], 0, nj + j))],
            out_specs=pl.BlockSpec((tm, th), lambda j, i, be, nv, nu, nx: (i, j)),
            scratch_shapes=[pltpu.VMEM((2, d, th), F32), pltpu.VMEM((d, th), BF16), pltpu.VMEM((d, th), BF16),
                            pltpu.SemaphoreType.DMA((2,))],
        ),
        out_shape=jax.ShapeDtypeStruct((n_slots, dff), BF16),
        compiler_params=_cparams(("arbitrary", "arbitrary")),
        name="expert_gate_up",
    )(block_e, nvalid, nused, nxt, xs, w_gu, b_gu3, b_gu3)


def _down_kernel(be, nv, nu, nxt, a_ref, w_hbm, bd_ref, o_ref, wbuf, wd_sc, sem):
    i = pl.program_id(1)
    last = nu[0] - 1
    active = i < nu[0]
    ii = jnp.minimum(i, last)
    nx = nxt[ii]

    def copy(e):
        return pltpu.make_async_copy(w_hbm.at[e], wbuf, sem)

    def start_next(current):
        copy(be[ii] if current else be[jnp.maximum(nx, 0)]).start()

    def cast():
        def chunk(r, carry):
            rows = pl.ds(pl.multiple_of(r * CAST_ROWS, CAST_ROWS), CAST_ROWS)
            wd_sc[rows, :] = wbuf[rows, :].astype(BF16)
            return carry

        lax.fori_loop(0, wd_sc.shape[0] // CAST_ROWS, chunk, 0)

    _stream_expert_weights(active & _expert_changed(be, i, last), i == 0, nx >= 0,
                           lambda: copy(be[ii]).wait(), cast, start_next)

    sub = a_ref.shape[0] // MOE_SUB_BLOCKS
    ns = o_ref.shape[0] // a_ref.shape[0]

    def compute(rows):
        y = jnp.dot(a_ref[rows, :], wd_sc[...], preferred_element_type=F32) + bd_ref[0]
        _store_token_rows(o_ref, rows.start, y)

    _for_live_sub_blocks(active, nv[i], sub, o_ref, compute, rows_per_slot=ns)


def _down(block_e, nvalid, nused, nxt, act, w_d, b_d3, tm):
    n_slots, dff = act.shape
    d = w_d.shape[2]
    ns = d // LANES
    nb = n_slots // tm

    def blk(i, nu):
        return jnp.minimum(i, nu[0] - 1)

    return pl.pallas_call(
        _down_kernel,
        grid_spec=pltpu.PrefetchScalarGridSpec(
            num_scalar_prefetch=4,
            grid=(1, nb),
            in_specs=[pl.BlockSpec((tm, dff), lambda j, i, be, nv, nu, nx: (blk(i, nu), 0)),
                      pl.BlockSpec(memory_space=pl.ANY),
                      pl.BlockSpec((1, 1, d), lambda j, i, be, nv, nu, nx: (be[blk(i, nu)], 0, 0))],
            out_specs=pl.BlockSpec((tm * ns, LANES), lambda j, i, be, nv, nu, nx: (i, 0)),
            scratch_shapes=[pltpu.VMEM((dff, d), F32), pltpu.VMEM((dff, d), BF16), pltpu.SemaphoreType.DMA(())],
        ),
        out_shape=jax.ShapeDtypeStruct((n_slots * ns, LANES), F32),
        compiler_params=_cparams(("arbitrary", "arbitrary")),
        name="expert_down",
    )(block_e, nvalid, nused, nxt, act, w_d, b_d3)


def _combine_kernel(dest_sm, y_hbm, w_ref, x1_ref, gt_ref, g_ref, o_ref, buf, sem, *, tt, ns):
    i = pl.program_id(0)
    n = pl.num_programs(0)

    def gather(tile, slot):
        def body(t, carry):
            for k in range(TOP_K):
                d = dest_sm[(tile * tt + t) * TOP_K + k]
                pltpu.make_async_copy(y_hbm.at[pl.ds(pl.multiple_of(d * ns, ns), ns), :],
                                      buf.at[slot, pl.ds(pl.multiple_of((k * tt + t) * ns, ns), ns), :],
                                      sem.at[slot]).start()
            return carry

        lax.fori_loop(0, tt, body, 0)

    slot = i % 2

    @pl.when(i == 0)
    def _():
        gather(0, 0)

    @pl.when(i + 1 < n)
    def _():
        gather(i + 1, 1 - slot)

    pltpu.make_async_copy(y_hbm.at[pl.ds(0, TOP_K * tt * ns), :], buf.at[slot], sem.at[slot]).wait()
    w = w_ref[...]
    rows = buf.at[slot]
    f = None
    for k in range(TOP_K):
        yk = _load_token_rows(rows, k * tt, tt, ns) * w[:, k:k + 1]
        f = yk if f is None else f + yk
    nf = f * lax.rsqrt(jnp.mean(f * f, axis=-1, keepdims=True) + NORM_EPS) * g_ref[...]
    o_ref[...] = x1_ref[...] + gt_ref[0] * nf


def _combine(dest_flat, y_rows, wts, x1, mod3, gpost, seq, tt, gt_blk):
    t, d = x1.shape
    ns = d // LANES
    per_b = seq // tt
    return pl.pallas_call(
        functools.partial(_combine_kernel, tt=tt, ns=ns),
        grid_spec=pltpu.PrefetchScalarGridSpec(
            num_scalar_prefetch=1,
            grid=(t // tt,),
            in_specs=[pl.BlockSpec(memory_space=pl.ANY),
                      pl.BlockSpec((tt, LANES), lambda i, ds: (i, 0)),
                      pl.BlockSpec((tt, d), lambda i, ds: (i, 0)),
                      pl.BlockSpec((1, 1, d), lambda i, ds: (i // per_b, 0, gt_blk)),
                      pl.BlockSpec((1, d), lambda i, ds: (0, 0))],
            out_specs=pl.BlockSpec((tt, d), lambda i, ds: (i, 0)),
            scratch_shapes=[pltpu.VMEM((2, TOP_K * tt * ns, LANES), F32), pltpu.SemaphoreType.DMA((2,))],
        ),
        out_shape=jax.ShapeDtypeStruct((t, d), F32),
        compiler_params=_cparams(("arbitrary",)),
        name="combine",
    )(dest_flat, y_rows, wts, x1, mod3, gpost)


def _tile(n, pref):
    t = min(n, pref)
    while n % t:
        t //= 2
    return t


def _rope_tables(seq):
    half = ROT_DIM // 2
    inv = ROPE_THETA ** (-jnp.arange(0, ROT_DIM, 2, dtype=F32) / ROT_DIM)
    ang = jnp.arange(seq, dtype=F32)[:, None] * inv[None, :]
    cos, sin = jnp.cos(ang), jnp.sin(ang)
    ones = jnp.ones((seq, ATTN_DK - ROT_DIM), F32)
    zeros = jnp.zeros((seq, ATTN_DK - ROT_DIM), F32)
    zh = jnp.zeros((seq, half), F32)
    c64 = jnp.concatenate([cos, cos, ones], axis=1)
    s1_64 = jnp.concatenate([-sin, zh, zeros], axis=1)
    s2_64 = jnp.concatenate([zh, sin, zeros], axis=1)
    rep = LANES // ATTN_DK
    return tuple(jnp.tile(t, (1, rep)) for t in (c64, s1_64, s2_64))


def _layer(x, c_pad, l, p, moe_blk):
    bsz, seq, d = x.shape
    t = bsz * seq
    dh = d
    n_sheads = dh // SSD_HEADDIM
    gn = SSD_GROUPS * SSD_STATE
    n_aheads = d // ATTN_DV
    aw = n_aheads * ATTN_DV
    qkw = 2 * n_aheads * ATTN_DK
    n_experts = p["w_router"].shape[-1]

    mod = _ada(c_pad, p["w_ada"][l], p["b_ada"][l][None, :], _tile(6 * d, ADA_COL_TILE))
    mod3 = mod[:bsz].reshape(bsz, 1, 6 * d)
    sh_m, sc_m, gt_m, sh_f, sc_f, gt_f = range(6)

    o = 0
    segs = {}
    for name, size in (("z", dh), ("xbc", dh + 2 * gn), ("dt", n_sheads), ("q", qkw), ("k", qkw),
                       ("v", aw), ("gs", d), ("ga", d)):
        segs[name] = (o, o + size)
        o += size
    scale = ATTN_DK ** -0.5 * math.log2(math.e)
    w_plain, w_rope, w_gate, w_dt = _regroup(jnp.swapaxes(p["w_in"][l], 0, 1), segs, scale)

    h = _prenorm(x, p["g_pre_mix"][l][None, :], mod3, sc_m, sh_m, _tile(seq, PRENORM_ROW_TILE)).reshape(t, d)
    tm = _tile(seq, PROJ_ROW_TILE)
    tn = lambda w: _tile(w.shape[1], PROJ_COL_TILE)
    plain = _mm(h, w_plain, BF16, tm, tn(w_plain), name="proj_plain")
    qk = _mm(h, w_rope, BF16, tm, tn(w_rope), "rope", _rope_tables(seq), name="proj_rope")
    gates = _mm(h, w_gate, BF16, tm, tn(w_gate), "sigmoid", name="proj_gate")
    dt_raw = _mm(h, w_dt, F32, tm, LANES, name="proj_dt")

    pad_h = lambda v: jnp.pad(v, (0, LANES - n_sheads))[None, :]
    plain3 = plain.reshape(bsz, seq, plain.shape[1])
    y_ssd = _ssd(plain3, dt_raw.reshape(bsz, seq, LANES), p["conv_w"][l], p["conv_b"][l][None, :],
                 pad_h(p["dt_bias"][l]), pad_h(p["a_log"][l]),
                 jnp.repeat(p["d_skip"][l], SSD_HEADDIM)[None, :], p["ssd_norm_w"][l][None, :], dh, n_sheads)

    lam_init = 0.8 - 0.6 * math.exp(-0.3 * l)
    lam_rows = jnp.zeros((8, LANES), F32)
    for r, nm in enumerate(("lambda_q1", "lambda_k1", "lambda_q2", "lambda_k2")):
        lam_rows = lam_rows.at[r, :ATTN_DK].set(p[nm][l])
    v_blk0 = (dh + dh + 2 * gn) // LANES
    o_attn = _attention(qk.reshape(bsz, seq, 2 * qkw), plain3, lam_rows, p["subln_w"][l][None, :],
                        n_aheads, v_blk0, _tile(seq, ATTN_Q_TILE), lam_init)

    tm2 = _tile(seq, OUTPROJ_ROW_TILE)
    merged = _merge(y_ssd.reshape(t, dh), o_attn.reshape(t, aw), p["w_br_ssd"][l].astype(BF16),
                    p["w_br_attn"][l].astype(BF16), gates, tm, _tile(d, PROJ_COL_TILE))
    wr = jnp.pad(p["w_router"][l], ((0, 0), (0, LANES - n_experts)))
    wr_hi = wr.astype(BF16)
    wr_lo = (wr - wr_hi.astype(F32)).astype(BF16)
    br = jnp.pad(p["b_router"][l], (0, LANES - n_experts))[None, :]
    x1, h_rows, logits = _outproj(merged, p["w_out"][l].astype(BF16), x.reshape(t, d),
                                    p["g_post_mix"][l][None, :], mod3, p["g_pre_ffn"][l][None, :],
                                    wr_hi, wr_lo, br, seq, tm2, gt_m, sc_f, sh_f)

    dest, wts, cnt = _route(logits, n_experts, moe_blk, _tile(t, ROUTE_ROW_TILE))
    counts = cnt[0, :n_experts].astype(jnp.int32)
    n_slots = t * TOP_K + n_experts * moe_blk
    nb = n_slots // moe_blk
    pblocks = (counts + moe_blk - 1) // moe_blk
    pend = jnp.cumsum(pblocks)
    nused = jnp.maximum(pend[-1], 1).astype(jnp.int32)
    bidx = jnp.arange(nb, dtype=jnp.int32)
    block_e = jnp.minimum(jnp.sum((pend[None, :] <= bidx[:, None]).astype(jnp.int32), axis=1), n_experts - 1)
    pstart = pend - pblocks
    nvalid = jnp.clip(counts[block_e] - (bidx - pstart[block_e]) * moe_blk, 0, moe_blk).astype(jnp.int32)
    dest_flat = dest[:, :TOP_K].reshape(-1)
    nused1 = nused.reshape(1)
    after = pend[block_e].astype(jnp.int32)
    nxt = jnp.where(after < nused, after, -1).astype(jnp.int32)

    xs = _dispatch(dest_flat, nvalid, h_rows, n_slots, _tile(t, DISPATCH_TOKENS), moe_blk, d // LANES)
    dff = p["w_down"].shape[2]
    act = _gateup(block_e, nvalid, nused1, nxt, xs, p["w_gate_up"][l], p["b_gate_up"][l][:, None, :],
                  moe_blk, _tile(dff, EXPERT_HIDDEN_TILE))
    y_sorted = _down(block_e, nvalid, nused1, nxt, act, p["w_down"][l], p["b_down"][l][:, None, :], moe_blk)
    out = _combine(dest_flat, y_sorted, wts, x1, mod3, p["g_post_ffn"][l][None, :], seq, _tile(seq, COMBINE_TOKENS), gt_f)
    return out.reshape(bsz, seq, d)


def kernel(x, c, w_ada, b_ada, g_pre_mix, g_post_mix, g_pre_ffn, g_post_ffn, w_in, conv_w, conv_b, dt_bias, a_log, d_skip, ssd_norm_w, lambda_q1, lambda_k1, lambda_q2, lambda_k2, subln_w, w_br_ssd, w_br_attn, w_out, w_router, b_router, w_gate_up, b_gate_up, w_down, b_down):
    p = dict(w_ada=w_ada, b_ada=b_ada, g_pre_mix=g_pre_mix, g_post_mix=g_post_mix, g_pre_ffn=g_pre_ffn,
             g_post_ffn=g_post_ffn, w_in=w_in, conv_w=conv_w, conv_b=conv_b, dt_bias=dt_bias, a_log=a_log,
             d_skip=d_skip, ssd_norm_w=ssd_norm_w, lambda_q1=lambda_q1, lambda_k1=lambda_k1,
             lambda_q2=lambda_q2, lambda_k2=lambda_k2, subln_w=subln_w, w_br_ssd=w_br_ssd,
             w_br_attn=w_br_attn, w_out=w_out, w_router=w_router, b_router=b_router, w_gate_up=w_gate_up,
             b_gate_up=b_gate_up, w_down=w_down, b_down=b_down)
    bsz = x.shape[0]
    c_pad = jnp.pad(c, ((0, (-bsz) % 8), (0, 0)))
    for l in range(w_ada.shape[0]):
        x = _layer(x, c_pad, l, p, min(MOE_ROW_BLOCK, x.shape[0] * x.shape[1]))
    return x
```

```python
import functools
import math

import jax
import jax.numpy as jnp
from jax import lax
from jax.experimental import pallas as pl
from jax.experimental.pallas import tpu as pltpu

F32 = jnp.float32
BF16 = jnp.bfloat16
HIGHEST = lax.Precision.HIGHEST

SSD_HEADDIM = 64
SSD_GROUPS = 4
SSD_STATE = 128
SSD_CONV = 4
SSD_CHUNK = 128
ATTN_DK = 64
ATTN_DV = 128
ATTN_KV_UNIT = 512
ATTN_Q_TILE = 512
ATTN_HEADS_PER_STEP = 4
ROT_DIM = ATTN_DK // 4
ROPE_THETA = 500000.0
TOP_K = 4
SWIGLU_LIMIT = 7.0
SWIGLU_ALPHA = 1.702
NORM_EPS = 1e-6
SUB_EPS = 1e-5
LANES = 128
NEG_BIG = -1e30

VMEM_BYTES = 64 * 1024 * 1024
VMEM_LIMIT = VMEM_BYTES - 8 * 1024 * 1024
ADA_COL_TILE = 1024
PRENORM_ROW_TILE = 1024
PROJ_ROW_TILE = 2048
MERGE_ROW_TILE = 1024
PROJ_COL_TILE = 1024
OUTPROJ_ROW_TILE = 512
ROUTE_ROW_TILE = 1024
DISPATCH_TOKENS = 512
COMBINE_TOKENS = 256
EXPERT_HIDDEN_TILE = 1024
REGROUP_TILE = 512
MOE_ROW_BLOCK = 512
MOE_SUB_BLOCKS = 2
EPILOGUE_ROW_CHUNKS = 4
MERGE_ROW_CHUNKS = 2
CAST_ROWS = 128


def _cparams(sem, vmem=VMEM_LIMIT):
    return pltpu.CompilerParams(dimension_semantics=sem, vmem_limit_bytes=vmem)


def _sigmoid(x):
    return 1.0 / (1.0 + jnp.exp(-x))


def _store_token_rows(ref, first_tok, val):
    n_tok, width = val.shape
    ns = width // LANES
    for s_ in range(ns):
        ref[pl.ds(first_tok * ns + s_, n_tok, stride=ns), :] = val[:, s_ * LANES:(s_ + 1) * LANES]


def _load_token_rows(ref, first_tok, n_tok, ns):
    return jnp.concatenate([ref[pl.ds(first_tok * ns + s_, n_tok, stride=ns), :] for s_ in range(ns)], axis=1)


def _ada_kernel(c_ref, w_ref, b_ref, o_ref):
    c = c_ref[...]
    sc = c * _sigmoid(c)
    o_ref[...] = jnp.dot(sc, w_ref[...], preferred_element_type=F32, precision=HIGHEST) + b_ref[...]


def _ada(c_pad, w, b, tn):
    rows, d = c_pad.shape
    n = w.shape[1]
    return pl.pallas_call(
        _ada_kernel,
        grid=(n // tn,),
        in_specs=[pl.BlockSpec((rows, d), lambda j: (0, 0)),
                  pl.BlockSpec((d, tn), lambda j: (0, j)),
                  pl.BlockSpec((1, tn), lambda j: (0, j))],
        out_specs=pl.BlockSpec((rows, tn), lambda j: (0, j)),
        out_shape=jax.ShapeDtypeStruct((rows, n), F32),
        compiler_params=_cparams(("arbitrary",)),
        name="ada",
    )(c_pad, w, b)


def _prenorm_kernel(x_ref, g_ref, sc_ref, sh_ref, o_ref):
    x = x_ref[0]
    y = x * lax.rsqrt(jnp.mean(x * x, axis=-1, keepdims=True) + NORM_EPS) * g_ref[...]
    o_ref[0] = (y * (1.0 + sc_ref[0]) + sh_ref[0]).astype(o_ref.dtype)


def _prenorm(x, g, mod3, sc_blk, sh_blk, ts):
    b, s, d = x.shape
    return pl.pallas_call(
        _prenorm_kernel,
        grid=(b, s // ts),
        in_specs=[pl.BlockSpec((1, ts, d), lambda bi, si: (bi, si, 0)),
                  pl.BlockSpec((1, d), lambda bi, si: (0, 0)),
                  pl.BlockSpec((1, 1, d), lambda bi, si: (bi, 0, sc_blk)),
                  pl.BlockSpec((1, 1, d), lambda bi, si: (bi, 0, sh_blk))],
        out_specs=pl.BlockSpec((1, ts, d), lambda bi, si: (bi, si, 0)),
        out_shape=jax.ShapeDtypeStruct((b, s, d), BF16),
        compiler_params=_cparams(("arbitrary", "arbitrary")),
        name="prenorm",
    )(x, g, mod3, mod3)


def _regroup_kernel(src_sm, wt_hbm, plain_ref, rope_ref, gate_ref, dt_ref, buf, dt_buf, sem, dt_sem, *,
                    tiles, dt_rows, q_scale):
    t = pl.program_id(0)
    n = pl.num_programs(0)
    tr = REGROUP_TILE
    n_plain, n_rope, n_gate, n_q = tiles

    def load(step):
        row = pl.multiple_of(src_sm[step], 8)
        return pltpu.make_async_copy(wt_hbm.at[pl.ds(row, tr), :], buf.at[step % 2], sem.at[step % 2])

    dt_copy = pltpu.make_async_copy(wt_hbm.at[pl.ds(dt_rows[0], dt_rows[1]), :], dt_buf, dt_sem)

    @pl.when(t == 0)
    def _():
        load(0).start()
        dt_copy.start()

    pl.when(t + 1 < n)(lambda: load(t + 1).start())
    load(t).wait()
    cols = buf[t % 2].T

    @pl.when(t < n_plain)
    def _():
        plain_ref[...] = cols.astype(BF16)

    @pl.when((t >= n_plain) & (t < n_plain + n_rope))
    def _():
        rope_ref[...] = jnp.where(t < n_plain + n_q, cols * q_scale, cols).astype(BF16)

    @pl.when(t >= n_plain + n_rope)
    def _():
        gate_ref[...] = cols.astype(BF16)

    @pl.when(t == n - 1)
    def _():
        dt_copy.wait()
        dt_ref[...] = jnp.zeros_like(dt_ref)
        dt_ref[:, :dt_rows[1]] = dt_buf[...].T.astype(BF16)


def _regroup(w_in_t, segs, q_scale):
    n_in, d = w_in_t.shape
    tr = REGROUP_TILE
    starts = []
    counts = []
    for group in (("z", "xbc", "v"), ("q", "k"), ("gs", "ga")):
        c = 0
        for nm in group:
            lo, hi = segs[nm]
            assert (hi - lo) % tr == 0 and lo % 8 == 0
            starts += list(range(lo, hi, tr))
            c += (hi - lo) // tr
        counts.append(c)
    n_plain, n_rope, n_gate = counts
    n_q = (segs["q"][1] - segs["q"][0]) // tr
    dt_lo, dt_hi = segs["dt"]
    clamp = lambda t, first, cnt: jnp.clip(t - first, 0, cnt - 1)
    return pl.pallas_call(
        functools.partial(_regroup_kernel, tiles=(n_plain, n_rope, n_gate, n_q), dt_rows=(dt_lo, dt_hi - dt_lo),
                          q_scale=q_scale),
        grid_spec=pltpu.PrefetchScalarGridSpec(
            num_scalar_prefetch=1,
            grid=(len(starts),),
            in_specs=[pl.BlockSpec(memory_space=pl.ANY)],
            out_specs=[pl.BlockSpec((d, tr), lambda t, src: (0, clamp(t, 0, n_plain))),
                       pl.BlockSpec((d, tr), lambda t, src: (0, clamp(t, n_plain, n_rope))),
                       pl.BlockSpec((d, tr), lambda t, src: (0, clamp(t, n_plain + n_rope, n_gate))),
                       pl.BlockSpec((d, LANES), lambda t, src: (0, 0))],
            scratch_shapes=[pltpu.VMEM((2, tr, d), F32), pltpu.VMEM((dt_hi - dt_lo, d), F32),
                            pltpu.SemaphoreType.DMA((2,)), pltpu.SemaphoreType.DMA(())],
        ),
        out_shape=[jax.ShapeDtypeStruct((d, n_plain * tr), BF16), jax.ShapeDtypeStruct((d, n_rope * tr), BF16),
                   jax.ShapeDtypeStruct((d, n_gate * tr), BF16), jax.ShapeDtypeStruct((d, LANES), BF16)],
        compiler_params=_cparams(("arbitrary",)),
        name="regroup_w_in",
    )(jnp.asarray(starts, jnp.int32), w_in_t)


def _mm_kernel(a_ref, w_ref, o_ref):
    o_ref[...] = jnp.dot(a_ref[...], w_ref[...], preferred_element_type=F32).astype(o_ref.dtype)


def _row_chunks(n_rows, chunks):
    step = n_rows // chunks if n_rows % chunks == 0 and n_rows >= 64 * chunks else n_rows
    return [slice(r, r + step) for r in range(0, n_rows, step)]


def _mm_sigmoid_kernel(a_ref, w_ref, o_ref):
    for rows in _row_chunks(a_ref.shape[0], EPILOGUE_ROW_CHUNKS):
        acc = jnp.dot(a_ref[rows, :], w_ref[...], preferred_element_type=F32)
        o_ref[rows, :] = _sigmoid(acc).astype(o_ref.dtype)


def _mm_rope_kernel(a_ref, w_ref, c_ref, s1_ref, s2_ref, o_ref):
    for rows in _row_chunks(a_ref.shape[0], EPILOGUE_ROW_CHUNKS):
        acc = jnp.dot(a_ref[rows, :], w_ref[...], preferred_element_type=F32)
        c = c_ref[rows, :]
        s1 = s1_ref[rows, :]
        s2 = s2_ref[rows, :]
        for g in range(acc.shape[1] // LANES):
            blk = acc[:, g * LANES:(g + 1) * LANES]
            fwd = pltpu.roll(blk, LANES - ROT_DIM // 2, 1)
            bwd = pltpu.roll(blk, ROT_DIM // 2, 1)
            o_ref[rows, g * LANES:(g + 1) * LANES] = (blk * c + fwd * s1 + bwd * s2).astype(o_ref.dtype)


def _mm(a, w, out_dtype, tm, tn, epilogue="none", rope_tabs=None, name="mm"):
    m, k = a.shape
    n = w.shape[1]
    in_specs = [pl.BlockSpec((tm, k), lambda i, j: (i, 0)),
                pl.BlockSpec((k, tn), lambda i, j: (0, j))]
    args = [a, w]
    if epilogue == "rope":
        seq_blocks = rope_tabs[0].shape[0] // tm
        for t in rope_tabs:
            in_specs.append(pl.BlockSpec((tm, LANES), lambda i, j: (i % seq_blocks, 0)))
            args.append(t)
        kern = _mm_rope_kernel
    elif epilogue == "sigmoid":
        kern = _mm_sigmoid_kernel
    else:
        kern = _mm_kernel
    return pl.pallas_call(
        kern,
        grid=(m // tm, n // tn),
        in_specs=in_specs,
        out_specs=pl.BlockSpec((tm, tn), lambda i, j: (i, j)),
        out_shape=jax.ShapeDtypeStruct((m, n), out_dtype),
        compiler_params=_cparams(("arbitrary", "arbitrary")),
        name=name,
    )(*args)


def _ssd_kernel(z_ref, xs_ref, bm_ref, cm_ref, dt_ref, cwx_ref, cwb_ref, cwc_ref, cbx_ref, cbb_ref,
                cbc_ref, dtb_ref, alog_ref, dsk_ref, nw_ref, o_ref,
                px_sc, pb_sc, pc_sc, st_sc, y_sc, *, n_heads):
    c = pl.program_id(1)
    L = SSD_CHUNK
    N = SSD_STATE
    hpg = n_heads // SSD_GROUPS
    gw = hpg * SSD_HEADDIM

    @pl.when(c == 0)
    def _():
        px_sc[...] = jnp.zeros_like(px_sc)
        pb_sc[...] = jnp.zeros_like(pb_sc)
        pc_sc[...] = jnp.zeros_like(pc_sc)
        st_sc[...] = jnp.zeros_like(st_sc)

    def conv_silu(u, prev_sc, w_ref, b_ref):
        prev = prev_sc[...]
        row = lax.broadcasted_iota(jnp.int32, u.shape, 0)
        acc = u * w_ref[SSD_CONV - 1:SSD_CONV, :] + b_ref[...]
        for j in range(1, SSD_CONV):
            sh = jnp.where(row < j, pltpu.roll(prev, j, 0), pltpu.roll(u, j, 0))
            acc = acc + sh * w_ref[SSD_CONV - 1 - j:SSD_CONV - j, :]
        prev_sc[...] = u
        return acc * _sigmoid(acc)

    xs = conv_silu(xs_ref[0].astype(F32), px_sc, cwx_ref, cbx_ref)
    bm = conv_silu(bm_ref[0].astype(F32), pb_sc, cwb_ref, cbb_ref)
    cm = conv_silu(cm_ref[0].astype(F32), pc_sc, cwc_ref, cbc_ref)

    raw = dt_ref[0] + dtb_ref[...]
    dt = jnp.maximum(raw, 0.0) + jnp.log1p(jnp.exp(-jnp.abs(raw)))
    a = dt * (-jnp.exp(alog_ref[...]))
    r_i = lax.broadcasted_iota(jnp.int32, (L, L), 0)
    c_i = lax.broadcasted_iota(jnp.int32, (L, L), 1)
    causal = r_i >= c_i
    tri = jnp.where(causal, 1.0, 0.0).astype(F32)
    a_cs = jnp.dot(tri, a, preferred_element_type=F32, precision=HIGHEST)
    a_cs_t = a_cs.T

    lane = lax.broadcasted_iota(jnp.int32, (L, LANES), 1)
    first = lane < SSD_HEADDIM
    acs_tiles = []
    dt_tiles = []
    for j in range(n_heads // 2):
        h0, h1 = 2 * j, 2 * j + 1
        acs_tiles.append(jnp.where(first, a_cs[:, h0:h0 + 1], a_cs[:, h1:h1 + 1]))
        dt_tiles.append(jnp.where(first, dt[:, h0:h0 + 1], dt[:, h1:h1 + 1]))
    acs_e = jnp.concatenate(acs_tiles, axis=1)
    dt_e = jnp.concatenate(dt_tiles, axis=1)
    xdt = xs * dt_e
    ea = jnp.exp(acs_e)
    alast = acs_e[L - 1:L, :]
    xdec = (xdt * jnp.exp(alast - acs_e)).astype(BF16)
    ealast = jnp.exp(alast)

    for g in range(SSD_GROUPS):
        bg = bm[:, g * N:(g + 1) * N]
        cg = cm[:, g * N:(g + 1) * N].astype(BF16)
        scores = lax.dot_general(cg, bg.astype(BF16), (((1,), (1,)), ((), ())),
                                 preferred_element_type=F32)
        st_old = st_sc[:, g * gw:(g + 1) * gw]
        y_off = jnp.dot(cg, st_old.astype(BF16), preferred_element_type=F32) * ea[:, g * gw:(g + 1) * gw]
        st_sc[:, g * gw:(g + 1) * gw] = st_old * ealast[:, g * gw:(g + 1) * gw] + jnp.dot(
            bg.T.astype(BF16), xdec[:, g * gw:(g + 1) * gw], preferred_element_type=F32)
        for jj in range(hpg // 2):
            j = g * (hpg // 2) + jj
            h0, h1 = 2 * j, 2 * j + 1
            la = jnp.exp(jnp.where(causal, a_cs[:, h0:h0 + 1] - a_cs_t[h0:h0 + 1, :], NEG_BIG))
            lb = jnp.exp(jnp.where(causal, a_cs[:, h1:h1 + 1] - a_cs_t[h1:h1 + 1, :], NEG_BIG))
            mcat = jnp.concatenate([(scores * la).astype(BF16), (scores * lb).astype(BF16)], axis=1)
            xp = xdt[:, j * LANES:(j + 1) * LANES]
            xcat = jnp.concatenate([jnp.where(first, xp, 0.0).astype(BF16),
                                    jnp.where(first, 0.0, xp).astype(BF16)], axis=0)
            y_diag = jnp.dot(mcat, xcat, preferred_element_type=F32)
            lo = jj * LANES
            y_sc[:, j * LANES:(j + 1) * LANES] = (
                y_diag + y_off[:, lo:lo + LANES]
                + dsk_ref[:, j * LANES:(j + 1) * LANES] * xs[:, j * LANES:(j + 1) * LANES])

    z = z_ref[0].astype(F32)
    u = y_sc[...] * (z * _sigmoid(z))
    for g in range(SSD_GROUPS):
        ug = u[:, g * gw:(g + 1) * gw]
        ms = jnp.mean(ug * ug, axis=-1, keepdims=True)
        o_ref[0, :, g * gw:(g + 1) * gw] = (ug * lax.rsqrt(ms + SUB_EPS)
                                            * nw_ref[:, g * gw:(g + 1) * gw]).astype(o_ref.dtype)


def _ssd(plain3, dt3, conv_w, conv_b, dtb, alog, dsk_e, norm_w, dh, n_heads):
    b, s, _ = plain3.shape
    L = SSD_CHUNK
    gn = SSD_GROUPS * SSD_STATE
    nc = s // L
    xblk = 1
    bblk = (2 * dh) // gn
    cw_x, cw_b, cw_c = conv_w[:, :dh], conv_w[:, dh:dh + gn], conv_w[:, dh + gn:]
    cb_x, cb_b, cb_c = conv_b[:, :dh], conv_b[:, dh:dh + gn], conv_b[:, dh + gn:]
    full = lambda shape: pl.BlockSpec(shape, lambda bi, ci: (0, 0))
    return pl.pallas_call(
        functools.partial(_ssd_kernel, n_heads=n_heads),
        grid=(b, nc),
        in_specs=[pl.BlockSpec((1, L, dh), lambda bi, ci: (bi, ci, 0)),
                  pl.BlockSpec((1, L, dh), lambda bi, ci: (bi, ci, xblk)),
                  pl.BlockSpec((1, L, gn), lambda bi, ci: (bi, ci, bblk)),
                  pl.BlockSpec((1, L, gn), lambda bi, ci: (bi, ci, bblk + 1)),
                  pl.BlockSpec((1, L, LANES), lambda bi, ci: (bi, ci, 0)),
                  full((SSD_CONV, dh)), full((SSD_CONV, gn)), full((SSD_CONV, gn)),
                  full((1, dh)), full((1, gn)), full((1, gn)),
                  full((1, LANES)), full((1, LANES)), full((1, dh)), full((1, dh))],
        out_specs=pl.BlockSpec((1, L, dh), lambda bi, ci: (bi, ci, 0)),
        out_shape=jax.ShapeDtypeStruct((b, s, dh), BF16),
        scratch_shapes=[pltpu.VMEM((L, dh), F32), pltpu.VMEM((L, gn), F32), pltpu.VMEM((L, gn), F32),
                        pltpu.VMEM((SSD_STATE, dh), F32), pltpu.VMEM((L, dh), F32)],
        compiler_params=_cparams(("arbitrary", "arbitrary")),
        name="ssd",
    )(plain3, plain3, plain3, plain3, dt3, cw_x, cw_b, cw_c, cb_x, cb_b, cb_c, dtb, alog, dsk_e, norm_w)


def _attn_kernel(q_ref, k_ref, v_ref, bias_ref, lam_ref, sw_ref, o_ref, vt_sc, st_a, st_b, m_sc, l_sc, acc_sc, *,
                 tq, lam_init):
    qi = pl.program_id(2)
    tu = ATTN_KV_UNIT
    nh = ATTN_HEADS_PER_STEP
    n_all = k_ref.shape[1] // tu
    heads = [slice(hh * LANES, (hh + 1) * LANES) for hh in range(nh)]

    @pl.when(qi == 0)
    def _():
        def transpose_block(c, carry):
            start = pl.multiple_of(c * tu, tu)
            for hh in range(nh):
                vt_sc[hh, c] = v_ref[0, pl.ds(start, tu), heads[hh]].astype(F32).T.astype(BF16)
            return carry

        lax.fori_loop(0, n_all, transpose_block, 0)

    qts = []
    for hh in range(nh):
        qt = q_ref[0, :, heads[hh]].astype(F32).T
        row = lax.broadcasted_iota(jnp.int32, qt.shape, 0)
        qts.append((jnp.where(row < ATTN_DK, qt, 0.0).astype(BF16),
                    jnp.where(row < ATTN_DK, 0.0, qt).astype(BF16)))
    m_sc[...] = jnp.full_like(m_sc, NEG_BIG)
    l_sc[...] = jnp.zeros_like(l_sc)
    acc_sc[...] = jnp.zeros_like(acc_sc)

    def scores(u, st_ref):
        start = pl.multiple_of(u * tu, tu)
        for hh in range(nh):
            k = k_ref[0, pl.ds(start, tu), heads[hh]]
            for m in range(2):
                st_ref[2 * hh + m] = jnp.dot(k, qts[hh][m], preferred_element_type=F32)

    def update(u, st_ref, masked):
        for hh in range(nh):
            vt = vt_sc[hh, u]
            for m in range(2):
                c = 2 * hh + m
                st = st_ref[c]
                if masked:
                    st = st + bias_ref[...]
                m_prev = m_sc[c]
                m_new = jnp.maximum(m_prev, jnp.max(st, axis=0, keepdims=True))
                alpha = jnp.exp2(m_prev - m_new)
                pt = jnp.exp2(st - m_new)
                l_sc[c] = alpha * l_sc[c] + jnp.sum(pt, axis=0, keepdims=True)
                acc_sc[c] = alpha * acc_sc[c] + jnp.dot(vt, pt.astype(BF16), preferred_element_type=F32)
                m_sc[c] = m_new

    n_units = (qi * tq) // tu + 1
    n_loop = (n_units - 1) // 2
    scores(0, st_a)

    def two_units(j, carry):
        u = 2 * j
        scores(u + 1, st_b)
        update(u, st_a, False)
        scores(u + 2, st_a)
        update(u + 1, st_b, False)
        return carry

    lax.fori_loop(0, n_loop, two_units, 0)
    last = n_units - 1

    @pl.when(last == 2 * n_loop)
    def _():
        update(last, st_a, True)

    @pl.when(last != 2 * n_loop)
    def _():
        scores(last, st_b)
        update(last - 1, st_a, False)
        update(last, st_b, True)

    lv = lam_ref[...]
    lam = (jnp.exp(jnp.sum(lv[0:1] * lv[1:2], axis=-1, keepdims=True))
           - jnp.exp(jnp.sum(lv[2:3] * lv[3:4], axis=-1, keepdims=True)) + lam_init)
    for hh in range(nh):
        c0, c1 = 2 * hh, 2 * hh + 1
        ot = acc_sc[c0] * (1.0 / l_sc[c0]) - lam * (acc_sc[c1] * (1.0 / l_sc[c1]))
        ot = ot * lax.rsqrt(jnp.mean(ot * ot, axis=0, keepdims=True) + SUB_EPS)
        o_ref[0, :, heads[hh]] = (ot.T * sw_ref[...] * (1.0 - lam_init)).astype(o_ref.dtype)


def _attention(qk3, plain3, lam_rows, subln_w, n_heads, v_blk0, tq, lam_init):
    b, s, _ = qk3.shape
    tu = ATTN_KV_UNIT
    nh = ATTN_HEADS_PER_STEP
    hw = nh * LANES
    assert tq == tu, "the diagonal unit must coincide with the query tile"
    kpos = lax.broadcasted_iota(jnp.int32, (tu, tq), 0)
    qpos = lax.broadcasted_iota(jnp.int32, (tu, tq), 1)
    diag_bias = jnp.where(kpos <= qpos, 0.0, NEG_BIG).astype(F32)
    return pl.pallas_call(
        functools.partial(_attn_kernel, tq=tq, lam_init=lam_init),
        grid=(b, n_heads // nh, s // tq),
        in_specs=[pl.BlockSpec((1, tq, hw), lambda bi, hi, qi: (bi, qi, hi)),
                  pl.BlockSpec((1, s, hw), lambda bi, hi, qi: (bi, 0, n_heads // nh + hi)),
                  pl.BlockSpec((1, s, hw), lambda bi, hi, qi: (bi, 0, v_blk0 // nh + hi)),
                  pl.BlockSpec((tu, tq), lambda bi, hi, qi: (0, 0)),
                  pl.BlockSpec((8, LANES), lambda bi, hi, qi: (0, 0)),
                  pl.BlockSpec((1, LANES), lambda bi, hi, qi: (0, 0))],
        out_specs=pl.BlockSpec((1, tq, hw), lambda bi, hi, qi: (bi, qi, hi)),
        out_shape=jax.ShapeDtypeStruct((b, s, n_heads * ATTN_DV), BF16),
        scratch_shapes=[pltpu.VMEM((nh, s // tu, ATTN_DV, tu), BF16),
                        pltpu.VMEM((2 * nh, tu, tq), F32), pltpu.VMEM((2 * nh, tu, tq), F32),
                        pltpu.VMEM((2 * nh, 1, tq), F32), pltpu.VMEM((2 * nh, 1, tq), F32),
                        pltpu.VMEM((2 * nh, ATTN_DV, tq), F32)],
        compiler_params=_cparams(("arbitrary", "arbitrary", "arbitrary")),
        name="diff_attn",
    )(qk3, qk3, plain3, diag_bias, lam_rows, subln_w)


def _merge_kernel(y_ref, o_ref, ws_ref, wa_ref, gs_ref, ga_ref, out_ref):
    for rows in _row_chunks(y_ref.shape[0], MERGE_ROW_CHUNKS):
        bs = jnp.dot(y_ref[rows, :], ws_ref[...], preferred_element_type=F32)
        ba = jnp.dot(o_ref[rows, :], wa_ref[...], preferred_element_type=F32)
        out_ref[rows, :] = (gs_ref[rows, :].astype(F32) * bs
                            + ga_ref[rows, :].astype(F32) * ba).astype(out_ref.dtype)


def _merge(y, o, ws, wa, gates, tm, tn):
    m, k = y.shape
    n = ws.shape[1]
    nj = n // tn
    return pl.pallas_call(
        _merge_kernel,
        grid=(m // tm, nj),
        in_specs=[pl.BlockSpec((tm, k), lambda i, j: (i, 0)),
                  pl.BlockSpec((tm, o.shape[1]), lambda i, j: (i, 0)),
                  pl.BlockSpec((k, tn), lambda i, j: (0, j)),
                  pl.BlockSpec((o.shape[1], tn), lambda i, j: (0, j)),
                  pl.BlockSpec((tm, tn), lambda i, j: (i, j)),
                  pl.BlockSpec((tm, tn), lambda i, j: (i, nj + j))],
        out_specs=pl.BlockSpec((tm, tn), lambda i, j: (i, j)),
        out_shape=jax.ShapeDtypeStruct((m, n), BF16),
        compiler_params=_cparams(("arbitrary", "arbitrary")),
        name="merge",
    )(y, o, ws, wa, gates, gates)


def _outproj_kernel(mg_ref, wo_ref, x_ref, gpost_ref, gt_ref, gpre_ref, sc_ref, sh_ref, wr_hi_ref,
                    wr_lo_ref, br_ref, x1_ref, hp_ref, lg_ref):
    mix = jnp.dot(mg_ref[...], wo_ref[...], preferred_element_type=F32)
    nm = mix * lax.rsqrt(jnp.mean(mix * mix, axis=-1, keepdims=True) + NORM_EPS) * gpost_ref[...]
    x1 = x_ref[...] + gt_ref[0] * nm
    x1_ref[...] = x1
    h2 = (x1 * lax.rsqrt(jnp.mean(x1 * x1, axis=-1, keepdims=True) + NORM_EPS) * gpre_ref[...]
          * (1.0 + sc_ref[0]) + sh_ref[0])
    h_hi = h2.astype(BF16)
    h_lo = (h2 - h_hi.astype(F32)).astype(BF16)
    wr_hi = wr_hi_ref[...]
    lg_ref[...] = (jnp.dot(h_hi, wr_hi, preferred_element_type=F32)
                   + jnp.dot(h_lo, wr_hi, preferred_element_type=F32)
                   + jnp.dot(h_hi, wr_lo_ref[...], preferred_element_type=F32) + br_ref[...])
    _store_token_rows(hp_ref, 0, h2)


def _outproj(merged, wo, x2, gpost, mod3, gpre, wr_hi, wr_lo, br, seq, tm, gt_blk, sc_blk, sh_blk):
    m, d = x2.shape
    per_b = seq // tm
    row = lambda i: (i, 0)
    const = lambda i: (0, 0)
    return pl.pallas_call(
        _outproj_kernel,
        grid=(m // tm,),
        in_specs=[pl.BlockSpec((tm, d), row),
                  pl.BlockSpec((d, d), const),
                  pl.BlockSpec((tm, d), row),
                  pl.BlockSpec((1, d), const),
                  pl.BlockSpec((1, 1, d), lambda i: (i // per_b, 0, gt_blk)),
                  pl.BlockSpec((1, d), const),
                  pl.BlockSpec((1, 1, d), lambda i: (i // per_b, 0, sc_blk)),
                  pl.BlockSpec((1, 1, d), lambda i: (i // per_b, 0, sh_blk)),
                  pl.BlockSpec((d, LANES), const),
                  pl.BlockSpec((d, LANES), const),
                  pl.BlockSpec((1, LANES), const)],
        out_specs=[pl.BlockSpec((tm, d), row),
                   pl.BlockSpec((tm * (d // LANES), LANES), row),
                   pl.BlockSpec((tm, LANES), row)],
        out_shape=[jax.ShapeDtypeStruct((m, d), F32),
                   jax.ShapeDtypeStruct((m * (d // LANES), LANES), F32),
                   jax.ShapeDtypeStruct((m, LANES), F32)],
        compiler_params=_cparams(("arbitrary",)),
        name="outproj",
    )(merged, wo, x2, gpost, mod3, gpre, mod3, mod3, wr_hi, wr_lo, br)


def _route_kernel(lg_ref, dest_ref, w_ref, cnt_ref, cnt_sc, pst_sc, run_sc, *, n_experts, blk):
    ph = pl.program_id(0)
    t = pl.program_id(1)
    tk = lg_ref.shape[0]
    lane = lax.broadcasted_iota(jnp.int32, (tk, LANES), 1)
    lg = jnp.where(lane < n_experts, lg_ref[...], -jnp.inf)
    vals = []
    hots = []
    for _ in range(TOP_K):
        mx = jnp.max(lg, axis=-1, keepdims=True)
        ix = jnp.min(jnp.where(lg == mx, lane, LANES), axis=-1, keepdims=True)
        hot = lane == ix
        lg = jnp.where(hot, -jnp.inf, lg)
        vals.append(mx)
        hots.append(hot)
    multi = jnp.zeros((tk, LANES), F32)
    for hot in hots:
        multi = multi + jnp.where(hot, 1.0, 0.0)
    colsum = jnp.sum(multi, axis=0, keepdims=True)

    @pl.when((ph == 0) & (t == 0))
    def _():
        cnt_sc[...] = jnp.zeros_like(cnt_sc)

    @pl.when(ph == 0)
    def _():
        cnt_sc[...] += colsum

    @pl.when((ph == 1) & (t == 0))
    def _():
        cnt = cnt_sc[...].astype(jnp.int32)
        padded = (((cnt + (blk - 1)) // blk) * blk).astype(F32)
        r_i = lax.broadcasted_iota(jnp.int32, (LANES, LANES), 0)
        c_i = lax.broadcasted_iota(jnp.int32, (LANES, LANES), 1)
        upper = jnp.where(r_i < c_i, 1.0, 0.0).astype(F32)
        pst_sc[...] = jnp.dot(jnp.broadcast_to(padded, (8, LANES)), upper,
                              preferred_element_type=F32, precision=HIGHEST)[0:1]
        run_sc[...] = jnp.zeros_like(run_sc)

    @pl.when(ph == 1)
    def _():
        r_i = lax.broadcasted_iota(jnp.int32, (tk, tk), 0)
        c_i = lax.broadcasted_iota(jnp.int32, (tk, tk), 1)
        strict = jnp.where(r_i > c_i, 1.0, 0.0).astype(BF16)
        before = jnp.dot(strict, multi.astype(BF16), preferred_element_type=F32)
        base = before + run_sc[...] + pst_sc[...]
        esum = jnp.zeros((tk, 1), F32)
        evals = []
        for r in range(TOP_K):
            e = jnp.exp(vals[r] - vals[0])
            evals.append(e)
            esum = esum + e
        dest = jnp.zeros((tk, LANES), jnp.int32)
        wts = jnp.zeros((tk, LANES), F32)
        for r in range(TOP_K):
            d_r = jnp.sum(jnp.where(hots[r], base, 0.0), axis=-1, keepdims=True).astype(jnp.int32)
            dest = jnp.where(lane == r, d_r, dest)
            wts = jnp.where(lane == r, evals[r] / esum, wts)
        dest_ref[...] = dest
        w_ref[...] = wts
        run_sc[...] += colsum
        cnt_ref[...] = jnp.broadcast_to(cnt_sc[...], cnt_ref.shape)


def _route(logits, n_experts, blk, tk):
    t = logits.shape[0]
    return pl.pallas_call(
        functools.partial(_route_kernel, n_experts=n_experts, blk=blk),
        grid=(2, t // tk),
        in_specs=[pl.BlockSpec((tk, LANES), lambda ph, ti: (ti, 0))],
        out_specs=[pl.BlockSpec((tk, LANES), lambda ph, ti: (ti * ph, 0)),
                   pl.BlockSpec((tk, LANES), lambda ph, ti: (ti * ph, 0)),
                   pl.BlockSpec((8, LANES), lambda ph, ti: (0, 0))],
        out_shape=[jax.ShapeDtypeStruct((t, LANES), jnp.int32),
                   jax.ShapeDtypeStruct((t, LANES), F32),
                   jax.ShapeDtypeStruct((8, LANES), F32)],
        scratch_shapes=[pltpu.VMEM((1, LANES), F32), pltpu.VMEM((1, LANES), F32), pltpu.VMEM((1, LANES), F32)],
        compiler_params=_cparams(("arbitrary", "arbitrary")),
        name="route",
    )(logits)


def _pad_fill_copies(b, nv_sm, zero_sc, xs_hbm, sem, blk, ns):
    nv = nv_sm[b]
    out = []
    off = b * blk + nv
    rest = blk - nv
    p = blk
    while p >= 1:
        cond = (rest & p) != 0
        out.append((cond, pltpu.make_async_copy(
            zero_sc.at[pl.ds(0, p * ns), :], xs_hbm.at[pl.ds(pl.multiple_of(off * ns, ns), p * ns), :], sem)))
        off = off + jnp.where(cond, p, 0)
        p //= 2
    return out


def _dispatch_kernel(dest_sm, nv_sm, h_hbm, xs_hbm, zero_sc, hbuf, in_sem, sem, fill_sem, *, tt, blk, nb, ns):
    i = pl.program_id(0)
    n = pl.num_programs(0)
    base = i * tt
    rows = tt * ns

    def fill(b, carry):
        for cond, cp in _pad_fill_copies(b, nv_sm, zero_sc, xs_hbm, fill_sem, blk, ns):
            pl.when(cond)(cp.start)
        return carry

    def fill_wait(b, carry):
        for cond, cp in _pad_fill_copies(b, nv_sm, zero_sc, xs_hbm, fill_sem, blk, ns):
            pl.when(cond)(cp.wait)
        return carry

    def load(step):
        slot = step % 3
        return pltpu.make_async_copy(h_hbm.at[pl.ds(pl.multiple_of(step * rows, rows), rows), :],
                                     hbuf.at[slot], in_sem.at[slot])

    @pl.when(i == 0)
    def _():
        zero_sc[...] = jnp.zeros_like(zero_sc)
        lax.fori_loop(0, nb, fill, 0)
        load(0).start()
        pl.when(n > 1)(lambda: load(1).start())

    def wait_step(step):
        for _ in range(TOP_K):
            pltpu.make_async_copy(hbuf.at[0], xs_hbm.at[pl.ds(0, rows), :], sem.at[step % 2]).wait()

    load(i).wait()
    src_tile = hbuf.at[i % 3]

    def body(t, carry):
        src = src_tile.at[pl.ds(pl.multiple_of(t * ns, ns), ns), :]
        for k in range(TOP_K):
            d = dest_sm[(base + t) * TOP_K + k]
            pltpu.make_async_copy(src, xs_hbm.at[pl.ds(pl.multiple_of(d * ns, ns), ns), :],
                                  sem.at[i % 2]).start()
        return carry

    lax.fori_loop(0, tt, body, 0)
    pl.when(i > 0)(lambda: wait_step(i - 1))
    pl.when(i + 2 < n)(lambda: load(i + 2).start())

    @pl.when(i == n - 1)
    def _():
        wait_step(i)
        lax.fori_loop(0, nb, fill_wait, 0)


def _dispatch(dest_flat, nvalid, h_rows, n_slots, tt, blk, ns):
    nb = n_slots // blk
    t = h_rows.shape[0] // ns
    return pl.pallas_call(
        functools.partial(_dispatch_kernel, tt=tt, blk=blk, nb=nb, ns=ns),
        grid_spec=pltpu.PrefetchScalarGridSpec(
            num_scalar_prefetch=2,
            grid=(t // tt,),
            in_specs=[pl.BlockSpec(memory_space=pl.ANY)],
            out_specs=pl.BlockSpec(memory_space=pl.ANY),
            scratch_shapes=[pltpu.VMEM((blk * ns, LANES), F32), pltpu.VMEM((3, tt * ns, LANES), F32),
                            pltpu.SemaphoreType.DMA((3,)), pltpu.SemaphoreType.DMA((2,)),
                            pltpu.SemaphoreType.DMA(())],
        ),
        out_shape=jax.ShapeDtypeStruct((n_slots * ns, LANES), F32),
        compiler_params=_cparams(("arbitrary",)),
        name="dispatch",
    )(dest_flat, nvalid, h_rows)


def _expert_changed(be, i, last):
    ii = jnp.minimum(i, last)
    prev = jnp.maximum(ii - 1, 0)
    return (i == 0) | (be[ii] != be[prev])


def _for_live_sub_blocks(active, n_valid, sub, o_ref, compute, rows_per_slot=1):
    n_live = jnp.where(active, (n_valid + sub - 1) // sub, 0)
    for count in range(MOE_SUB_BLOCKS + 1):
        @pl.when(n_live == count)
        def _(count=count):
            if count == MOE_SUB_BLOCKS:
                compute(slice(0, count * sub))
            else:
                for r in range(count):
                    compute(slice(r * sub, (r + 1) * sub))
            if count < MOE_SUB_BLOCKS:
                first = count * sub * rows_per_slot
                o_ref[first:, :] = jnp.zeros((o_ref.shape[0] - first, o_ref.shape[1]), o_ref.dtype)


def _stream_expert_weights(changed, first, prefetch, wait_cur, cast, start_next):
    @pl.when(changed)
    def _():
        pl.when(first)(lambda: start_next(True))
        wait_cur()
        cast()
        pl.when(prefetch)(lambda: start_next(False))


def _gateup_kernel(be, nv, nu, nxt, x_ref, w_hbm, bg_ref, bu_ref, o_ref, wbuf, wg_sc, wu_sc, sem):
    j = pl.program_id(0)
    i = pl.program_id(1)
    nj = pl.num_programs(0)
    th = wg_sc.shape[1]
    dff = w_hbm.shape[2] // 2
    last = nu[0] - 1
    active = i < nu[0]
    ii = jnp.minimum(i, last)

    def copies(e, jj):
        col = pl.multiple_of(jj * th, th)
        return (pltpu.make_async_copy(w_hbm.at[e, :, pl.ds(col, th)], wbuf.at[0], sem.at[0]),
                pltpu.make_async_copy(w_hbm.at[e, :, pl.ds(dff + col, th)], wbuf.at[1], sem.at[1]))

    nx = nxt[ii]
    same_pass = nx >= 0
    e_next = jnp.where(same_pass, be[jnp.maximum(nx, 0)], be[0])
    j_next = jnp.where(same_pass, j, j + 1)

    def start_next(current):
        for cp in (copies(be[ii], j) if current else copies(e_next, j_next)):
            cp.start()

    def wait_cur():
        for cp in copies(be[ii], j):
            cp.wait()

    def cast():
        def chunk(r, carry):
            rows = pl.ds(pl.multiple_of(r * CAST_ROWS, CAST_ROWS), CAST_ROWS)
            wg_sc[rows, :] = wbuf[0, rows, :].astype(BF16)
            wu_sc[rows, :] = wbuf[1, rows, :].astype(BF16)
            return carry

        lax.fori_loop(0, wg_sc.shape[0] // CAST_ROWS, chunk, 0)

    _stream_expert_weights(active & _expert_changed(be, i, last), (j == 0) & (i == 0),
                           same_pass | (j + 1 < nj), wait_cur, cast, start_next)

    sub = o_ref.shape[0] // MOE_SUB_BLOCKS

    def compute(rows):
        x = _load_token_rows(x_ref, rows.start, rows.stop - rows.start, wg_sc.shape[0] // LANES).astype(BF16)
        g = jnp.dot(x, wg_sc[...], preferred_element_type=F32) + bg_ref[0]
        up = jnp.dot(x, wu_sc[...], preferred_element_type=F32) + bu_ref[0]
        gate = jnp.minimum(g, SWIGLU_LIMIT)
        up = jnp.clip(up, -SWIGLU_LIMIT, SWIGLU_LIMIT)
        o_ref[rows, :] = ((up + 1.0) * gate * _sigmoid(SWIGLU_ALPHA * gate)).astype(o_ref.dtype)

    _for_live_sub_blocks(active, nv[i], sub, o_ref, compute)


def _gateup(block_e, nvalid, nused, nxt, xs, w_gu, b_gu3, tm, th):
    d = w_gu.shape[1]
    ns = d // LANES
    n_slots = xs.shape[0] // ns
    dff = w_gu.shape[2] // 2
    nj = dff // th
    nb = n_slots // tm

    def blk(i, nu):
        return jnp.minimum(i, nu[0] - 1)

    return pl.pallas_call(
        _gateup_kernel,
        grid_spec=pltpu.PrefetchScalarGridSpec(
            num_scalar_prefetch=4,
            grid=(nj, nb),
            in_specs=[pl.BlockSpec((tm * ns, LANES), lambda j, i, be, nv, nu, nx: (blk(i, nu), 0)),
                      pl.BlockSpec(memory_space=pl.ANY),
                      pl.BlockSpec((1, 1, th), lambda j, i, be, nv, nu, nx: (be[blk(i, nu)], 0, j)),
                      pl.BlockSpec((1, 1, th), lambda j, i, be, nv, nu, nx: (be[blk(i, nu)], 0, nj + j))],
            out_specs=pl.BlockSpec((tm, th), lambda j, i, be, nv, nu, nx: (i, j)),
            scratch_shapes=[pltpu.VMEM((2, d, th), F32), pltpu.VMEM((d, th), BF16), pltpu.VMEM((d, th), BF16),
                            pltpu.SemaphoreType.DMA((2,))],
        ),
        out_shape=jax.ShapeDtypeStruct((n_slots, dff), BF16),
        compiler_params=_cparams(("arbitrary", "arbitrary")),
        name="expert_gate_up",
    )(block_e, nvalid, nused, nxt, xs, w_gu, b_gu3, b_gu3)


def _down_kernel(be, nv, nu, nxt, a_ref, w_hbm, bd_ref, o_ref, wbuf, wd_sc, sem):
    i = pl.program_id(1)
    last = nu[0] - 1
    active = i < nu[0]
    ii = jnp.minimum(i, last)
    nx = nxt[ii]

    def copy(e):
        return pltpu.make_async_copy(w_hbm.at[e], wbuf, sem)

    def start_next(current):
        copy(be[ii] if current else be[jnp.maximum(nx, 0)]).start()

    def cast():
        def chunk(r, carry):
            rows = pl.ds(pl.multiple_of(r * CAST_ROWS, CAST_ROWS), CAST_ROWS)
            wd_sc[rows, :] = wbuf[rows, :].astype(BF16)
            return carry

        lax.fori_loop(0, wd_sc.shape[0] // CAST_ROWS, chunk, 0)

    _stream_expert_weights(active & _expert_changed(be, i, last), i == 0, nx >= 0,
                           lambda: copy(be[ii]).wait(), cast, start_next)

    sub = a_ref.shape[0] // MOE_SUB_BLOCKS
    ns = o_ref.shape[0] // a_ref.shape[0]

    def compute(rows):
        y = jnp.dot(a_ref[rows, :], wd_sc[...], preferred_element_type=F32) + bd_ref[0]
        _store_token_rows(o_ref, rows.start, y)

    _for_live_sub_blocks(active, nv[i], sub, o_ref, compute, rows_per_slot=ns)


def _down(block_e, nvalid, nused, nxt, act, w_d, b_d3, tm):
    n_slots, dff = act.shape
    d = w_d.shape[2]
    ns = d // LANES
    nb = n_slots // tm

    def blk(i, nu):
        return jnp.minimum(i, nu[0] - 1)

    return pl.pallas_call(
        _down_kernel,
        grid_spec=pltpu.PrefetchScalarGridSpec(
            num_scalar_prefetch=4,
            grid=(1, nb),
            in_specs=[pl.BlockSpec((tm, dff), lambda j, i, be, nv, nu, nx: (blk(i, nu), 0)),
                      pl.BlockSpec(memory_space=pl.ANY),
                      pl.BlockSpec((1, 1, d), lambda j, i, be, nv, nu, nx: (be[blk(i, nu)], 0, 0))],
            out_specs=pl.BlockSpec((tm * ns, LANES), lambda j, i, be, nv, nu, nx: (i, 0)),
            scratch_shapes=[pltpu.VMEM((dff, d), F32), pltpu.VMEM((dff, d), BF16), pltpu.SemaphoreType.DMA(())],
        ),
        out_shape=jax.ShapeDtypeStruct((n_slots * ns, LANES), F32),
        compiler_params=_cparams(("arbitrary", "arbitrary")),
        name="expert_down",
    )(block_e, nvalid, nused, nxt, act, w_d, b_d3)


def _combine_kernel(dest_sm, y_hbm, w_ref, x1_ref, gt_ref, g_ref, o_ref, buf, sem, *, tt, ns):
    i = pl.program_id(0)
    n = pl.num_programs(0)

    def gather(tile, slot):
        def body(t, carry):
            for k in range(TOP_K):
                d = dest_sm[(tile * tt + t) * TOP_K + k]
                pltpu.make_async_copy(y_hbm.at[pl.ds(pl.multiple_of(d * ns, ns), ns), :],
                                      buf.at[slot, pl.ds(pl.multiple_of((k * tt + t) * ns, ns), ns), :],
                                      sem.at[slot]).start()
            return carry

        lax.fori_loop(0, tt, body, 0)

    slot = i % 2

    @pl.when(i == 0)
    def _():
        gather(0, 0)

    @pl.when(i + 1 < n)
    def _():
        gather(i + 1, 1 - slot)

    pltpu.make_async_copy(y_hbm.at[pl.ds(0, TOP_K * tt * ns), :], buf.at[slot], sem.at[slot]).wait()
    w = w_ref[...]
    rows = buf.at[slot]
    f = None
    for k in range(TOP_K):
        yk = _load_token_rows(rows, k * tt, tt, ns) * w[:, k:k + 1]
        f = yk if f is None else f + yk
    nf = f * lax.rsqrt(jnp.mean(f * f, axis=-1, keepdims=True) + NORM_EPS) * g_ref[...]
    o_ref[...] = x1_ref[...] + gt_ref[0] * nf


def _combine(dest_flat, y_rows, wts, x1, mod3, gpost, seq, tt, gt_blk):
    t, d = x1.shape
    ns = d // LANES
    per_b = seq // tt
    return pl.pallas_call(
        functools.partial(_combine_kernel, tt=tt, ns=ns),
        grid_spec=pltpu.PrefetchScalarGridSpec(
            num_scalar_prefetch=1,
            grid=(t // tt,),
            in_specs=[pl.BlockSpec(memory_space=pl.ANY),
                      pl.BlockSpec((tt, LANES), lambda i, ds: (i, 0)),
                      pl.BlockSpec((tt, d), lambda i, ds: (i, 0)),
                      pl.BlockSpec((1, 1, d), lambda i, ds: (i // per_b, 0, gt_blk)),
                      pl.BlockSpec((1, d), lambda i, ds: (0, 0))],
            out_specs=pl.BlockSpec((tt, d), lambda i, ds: (i, 0)),
            scratch_shapes=[pltpu.VMEM((2, TOP_K * tt * ns, LANES), F32), pltpu.SemaphoreType.DMA((2,))],
        ),
        out_shape=jax.ShapeDtypeStruct((t, d), F32),
        compiler_params=_cparams(("arbitrary",)),
        name="combine",
    )(dest_flat, y_rows, wts, x1, mod3, gpost)


def _tile(n, pref):
    t = min(n, pref)
    while n % t:
        t //= 2
    return t


def _rope_tables(seq):
    half = ROT_DIM // 2
    inv = ROPE_THETA ** (-jnp.arange(0, ROT_DIM, 2, dtype=F32) / ROT_DIM)
    ang = jnp.arange(seq, dtype=F32)[:, None] * inv[None, :]
    cos, sin = jnp.cos(ang), jnp.sin(ang)
    ones = jnp.ones((seq, ATTN_DK - ROT_DIM), F32)
    zeros = jnp.zeros((seq, ATTN_DK - ROT_DIM), F32)
    zh = jnp.zeros((seq, half), F32)
    c64 = jnp.concatenate([cos, cos, ones], axis=1)
    s1_64 = jnp.concatenate([-sin, zh, zeros], axis=1)
    s2_64 = jnp.concatenate([zh, sin, zeros], axis=1)
    rep = LANES // ATTN_DK
    return tuple(jnp.tile(t, (1, rep)) for t in (c64, s1_64, s2_64))


def _layer(x, c_pad, l, p, moe_blk):
    bsz, seq, d = x.shape
    t = bsz * seq
    dh = d
    n_sheads = dh // SSD_HEADDIM
    gn = SSD_GROUPS * SSD_STATE
    n_aheads = d // ATTN_DV
    aw = n_aheads * ATTN_DV
    qkw = 2 * n_aheads * ATTN_DK
    n_experts = p["w_router"].shape[-1]

    mod = _ada(c_pad, p["w_ada"][l], p["b_ada"][l][None, :], _tile(6 * d, ADA_COL_TILE))
    mod3 = mod[:bsz].reshape(bsz, 1, 6 * d)
    sh_m, sc_m, gt_m, sh_f, sc_f, gt_f = range(6)

    o = 0
    segs = {}
    for name, size in (("z", dh), ("xbc", dh + 2 * gn), ("dt", n_sheads), ("q", qkw), ("k", qkw),
                       ("v", aw), ("gs", d), ("ga", d)):
        segs[name] = (o, o + size)
        o += size
    scale = ATTN_DK ** -0.5 * math.log2(math.e)
    w_plain, w_rope, w_gate, w_dt = _regroup(jnp.swapaxes(p["w_in"][l], 0, 1), segs, scale)

    h = _prenorm(x, p["g_pre_mix"][l][None, :], mod3, sc_m, sh_m, _tile(seq, PRENORM_ROW_TILE)).reshape(t, d)
    tm = _tile(seq, PROJ_ROW_TILE)
    tn = lambda w: _tile(w.shape[1], PROJ_COL_TILE)
    plain = _mm(h, w_plain, BF16, tm, tn(w_plain), name="proj_plain")
    qk = _mm(h, w_rope, BF16, tm, tn(w_rope), "rope", _rope_tables(seq), name="proj_rope")
    gates = _mm(h, w_gate, BF16, tm, tn(w_gate), "sigmoid", name="proj_gate")
    dt_raw = _mm(h, w_dt, F32, tm, LANES, name="proj_dt")

    pad_h = lambda v: jnp.pad(v, (0, LANES - n_sheads))[None, :]
    plain3 = plain.reshape(bsz, seq, plain.shape[1])
    y_ssd = _ssd(plain3, dt_raw.reshape(bsz, seq, LANES), p["conv_w"][l], p["conv_b"][l][None, :],
                 pad_h(p["dt_bias"][l]), pad_h(p["a_log"][l]),
                 jnp.repeat(p["d_skip"][l], SSD_HEADDIM)[None, :], p["ssd_norm_w"][l][None, :], dh, n_sheads)

    lam_init = 0.8 - 0.6 * math.exp(-0.3 * l)
    lam_rows = jnp.zeros((8, LANES), F32)
    for r, nm in enumerate(("lambda_q1", "lambda_k1", "lambda_q2", "lambda_k2")):
        lam_rows = lam_rows.at[r, :ATTN_DK].set(p[nm][l])
    v_blk0 = (dh + dh + 2 * gn) // LANES
    o_attn = _attention(qk.reshape(bsz, seq, 2 * qkw), plain3, lam_rows, p["subln_w"][l][None, :],
                        n_aheads, v_blk0, _tile(seq, ATTN_Q_TILE), lam_init)

    tm2 = _tile(seq, OUTPROJ_ROW_TILE)
    merged = _merge(y_ssd.reshape(t, dh), o_attn.reshape(t, aw), p["w_br_ssd"][l].astype(BF16),
                    p["w_br_attn"][l].astype(BF16), gates, _tile(seq, MERGE_ROW_TILE), _tile(d, PROJ_COL_TILE))
    wr = jnp.pad(p["w_router"][l], ((0, 0), (0, LANES - n_experts)))
    wr_hi = wr.astype(BF16)
    wr_lo = (wr - wr_hi.astype(F32)).astype(BF16)
    br = jnp.pad(p["b_router"][l], (0, LANES - n_experts))[None, :]
    x1, h_rows, logits = _outproj(merged, p["w_out"][l].astype(BF16), x.reshape(t, d),
                                    p["g_post_mix"][l][None, :], mod3, p["g_pre_ffn"][l][None, :],
                                    wr_hi, wr_lo, br, seq, tm2, gt_m, sc_f, sh_f)

    dest, wts, cnt = _route(logits, n_experts, moe_blk, _tile(t, ROUTE_ROW_TILE))
    counts = cnt[0, :n_experts].astype(jnp.int32)
    n_slots = t * TOP_K + n_experts * moe_blk
    nb = n_slots // moe_blk
    pblocks = (counts + moe_blk - 1) // moe_blk
    pend = jnp.cumsum(pblocks)
    nused = jnp.maximum(pend[-1], 1).astype(jnp.int32)
    bidx = jnp.arange(nb, dtype=jnp.int32)
    block_e = jnp.minimum(jnp.sum((pend[None, :] <= bidx[:, None]).astype(jnp.int32), axis=1), n_experts - 1)
    pstart = pend - pblocks
    nvalid = jnp.clip(counts[block_e] - (bidx - pstart[block_e]) * moe_blk, 0, moe_blk).astype(jnp.int32)
    dest_flat = dest[:, :TOP_K].reshape(-1)
    nused1 = nused.reshape(1)
    after = pend[block_e].astype(jnp.int32)
    nxt = jnp.where(after < nused, after, -1).astype(jnp.int32)

    xs = _dispatch(dest_flat, nvalid, h_rows, n_slots, _tile(t, DISPATCH_TOKENS), moe_blk, d // LANES)
    dff = p["w_down"].shape[2]
    act = _gateup(block_e, nvalid, nused1, nxt, xs, p["w_gate_up"][l], p["b_gate_up"][l][:, None, :],
                  moe_blk, _tile(dff, EXPERT_HIDDEN_TILE))
    y_sorted = _down(block_e, nvalid, nused1, nxt, act, p["w_down"][l], p["b_down"][l][:, None, :], moe_blk)
    out = _combine(dest_flat, y_sorted, wts, x1, mod3, p["g_post_ffn"][l][None, :], seq, _tile(seq, COMBINE_TOKENS), gt_f)
    return out.reshape(bsz, seq, d)


def kernel(x, c, w_ada, b_ada, g_pre_mix, g_post_mix, g_pre_ffn, g_post_ffn, w_in, conv_w, conv_b, dt_bias, a_log, d_skip, ssd_norm_w, lambda_q1, lambda_k1, lambda_q2, lambda_k2, subln_w, w_br_ssd, w_br_attn, w_out, w_router, b_router, w_gate_up, b_gate_up, w_down, b_down):
    p = dict(w_ada=w_ada, b_ada=b_ada, g_pre_mix=g_pre_mix, g_post_mix=g_post_mix, g_pre_ffn=g_pre_ffn,
             g_post_ffn=g_post_ffn, w_in=w_in, conv_w=conv_w, conv_b=conv_b, dt_bias=dt_bias, a_log=a_log,
             d_skip=d_skip, ssd_norm_w=ssd_norm_w, lambda_q1=lambda_q1, lambda_k1=lambda_k1,
             lambda_q2=lambda_q2, lambda_k2=lambda_k2, subln_w=subln_w, w_br_ssd=w_br_ssd,
             w_br_attn=w_br_attn, w_out=w_out, w_router=w_router, b_router=b_router, w_gate_up=w_gate_up,
             b_gate_up=b_gate_up, w_down=w_down, b_down=b_down)
    bsz = x.shape[0]
    c_pad = jnp.pad(c, ((0, (-bsz) % 8), (0, 0)))
    for l in range(w_ada.shape[0]):
        x = _layer(x, c_pad, l, p, min(MOE_ROW_BLOCK, x.shape[0] * x.shape[1]))
    return x
```

```python
import functools
import math

import jax
import jax.numpy as jnp
from jax import lax
from jax.experimental import pallas as pl
from jax.experimental.pallas import tpu as pltpu

F32 = jnp.float32
BF16 = jnp.bfloat16
HIGHEST = lax.Precision.HIGHEST

SSD_HEADDIM = 64
SSD_GROUPS = 4
SSD_STATE = 128
SSD_CONV = 4
SSD_CHUNK = 128
ATTN_DK = 64
ATTN_DV = 128
ATTN_KV_UNIT = 512
ATTN_Q_TILE = 512
ATTN_HEADS_PER_STEP = 4
ROT_DIM = ATTN_DK // 4
ROPE_THETA = 500000.0
TOP_K = 4
SWIGLU_LIMIT = 7.0
SWIGLU_ALPHA = 1.702
NORM_EPS = 1e-6
SUB_EPS = 1e-5
LANES = 128
NEG_BIG = -1e30

VMEM_BYTES = 64 * 1024 * 1024
VMEM_LIMIT = VMEM_BYTES - 8 * 1024 * 1024
ADA_COL_TILE = 1024
PRENORM_ROW_TILE = 1024
PROJ_ROW_TILE = 2048
MERGE_ROW_TILE = 1024
PROJ_COL_TILE = 1024
OUTPROJ_ROW_TILE = 512
ROUTE_ROW_TILE = 1024
DISPATCH_TOKENS = 512
COMBINE_TOKENS = 256
EXPERT_HIDDEN_TILE = 1024
REGROUP_TILE = 512
MOE_ROW_BLOCK = 1024
MOE_SUB_BLOCKS = 4
EPILOGUE_ROW_CHUNKS = 4
MERGE_ROW_CHUNKS = 2
CAST_ROWS = 128


def _cparams(sem, vmem=VMEM_LIMIT):
    return pltpu.CompilerParams(dimension_semantics=sem, vmem_limit_bytes=vmem)


def _sigmoid(x):
    return 1.0 / (1.0 + jnp.exp(-x))


def _store_token_rows(ref, first_tok, val):
    n_tok, width = val.shape
    ns = width // LANES
    for s_ in range(ns):
        ref[pl.ds(first_tok * ns + s_, n_tok, stride=ns), :] = val[:, s_ * LANES:(s_ + 1) * LANES]


def _load_token_rows(ref, first_tok, n_tok, ns):
    return jnp.concatenate([ref[pl.ds(first_tok * ns + s_, n_tok, stride=ns), :] for s_ in range(ns)], axis=1)


def _ada_kernel(c_ref, w_ref, b_ref, o_ref):
    c = c_ref[...]
    sc = c * _sigmoid(c)
    o_ref[...] = jnp.dot(sc, w_ref[...], preferred_element_type=F32, precision=HIGHEST) + b_ref[...]


def _ada(c_pad, w, b, tn):
    rows, d = c_pad.shape
    n = w.shape[1]
    return pl.pallas_call(
        _ada_kernel,
        grid=(n // tn,),
        in_specs=[pl.BlockSpec((rows, d), lambda j: (0, 0)),
                  pl.BlockSpec((d, tn), lambda j: (0, j)),
                  pl.BlockSpec((1, tn), lambda j: (0, j))],
        out_specs=pl.BlockSpec((rows, tn), lambda j: (0, j)),
        out_shape=jax.ShapeDtypeStruct((rows, n), F32),
        compiler_params=_cparams(("arbitrary",)),
        name="ada",
    )(c_pad, w, b)


def _prenorm_kernel(x_ref, g_ref, sc_ref, sh_ref, o_ref):
    x = x_ref[0]
    y = x * lax.rsqrt(jnp.mean(x * x, axis=-1, keepdims=True) + NORM_EPS) * g_ref[...]
    o_ref[0] = (y * (1.0 + sc_ref[0]) + sh_ref[0]).astype(o_ref.dtype)


def _prenorm(x, g, mod3, sc_blk, sh_blk, ts):
    b, s, d = x.shape
    return pl.pallas_call(
        _prenorm_kernel,
        grid=(b, s // ts),
        in_specs=[pl.BlockSpec((1, ts, d), lambda bi, si: (bi, si, 0)),
                  pl.BlockSpec((1, d), lambda bi, si: (0, 0)),
                  pl.BlockSpec((1, 1, d), lambda bi, si: (bi, 0, sc_blk)),
                  pl.BlockSpec((1, 1, d), lambda bi, si: (bi, 0, sh_blk))],
        out_specs=pl.BlockSpec((1, ts, d), lambda bi, si: (bi, si, 0)),
        out_shape=jax.ShapeDtypeStruct((b, s, d), BF16),
        compiler_params=_cparams(("arbitrary", "arbitrary")),
        name="prenorm",
    )(x, g, mod3, mod3)


def _regroup_kernel(src_sm, wt_hbm, plain_ref, rope_ref, gate_ref, dt_ref, buf, dt_buf, sem, dt_sem, *,
                    tiles, dt_rows, q_scale):
    t = pl.program_id(0)
    n = pl.num_programs(0)
    tr = REGROUP_TILE
    n_plain, n_rope, n_gate, n_q = tiles

    def load(step):
        row = pl.multiple_of(src_sm[step], 8)
        return pltpu.make_async_copy(wt_hbm.at[pl.ds(row, tr), :], buf.at[step % 2], sem.at[step % 2])

    dt_copy = pltpu.make_async_copy(wt_hbm.at[pl.ds(dt_rows[0], dt_rows[1]), :], dt_buf, dt_sem)

    @pl.when(t == 0)
    def _():
        load(0).start()
        dt_copy.start()

    pl.when(t + 1 < n)(lambda: load(t + 1).start())
    load(t).wait()
    cols = buf[t % 2].T

    @pl.when(t < n_plain)
    def _():
        plain_ref[...] = cols.astype(BF16)

    @pl.when((t >= n_plain) & (t < n_plain + n_rope))
    def _():
        rope_ref[...] = jnp.where(t < n_plain + n_q, cols * q_scale, cols).astype(BF16)

    @pl.when(t >= n_plain + n_rope)
    def _():
        gate_ref[...] = cols.astype(BF16)

    @pl.when(t == n - 1)
    def _():
        dt_copy.wait()
        dt_ref[...] = jnp.zeros_like(dt_ref)
        dt_ref[:, :dt_rows[1]] = dt_buf[...].T.astype(BF16)


def _regroup(w_in_t, segs, q_scale):
    n_in, d = w_in_t.shape
    tr = REGROUP_TILE
    starts = []
    counts = []
    for group in (("z", "xbc", "v"), ("q", "k"), ("gs", "ga")):
        c = 0
        for nm in group:
            lo, hi = segs[nm]
            assert (hi - lo) % tr == 0 and lo % 8 == 0
            starts += list(range(lo, hi, tr))
            c += (hi - lo) // tr
        counts.append(c)
    n_plain, n_rope, n_gate = counts
    n_q = (segs["q"][1] - segs["q"][0]) // tr
    dt_lo, dt_hi = segs["dt"]
    clamp = lambda t, first, cnt: jnp.clip(t - first, 0, cnt - 1)
    return pl.pallas_call(
        functools.partial(_regroup_kernel, tiles=(n_plain, n_rope, n_gate, n_q), dt_rows=(dt_lo, dt_hi - dt_lo),
                          q_scale=q_scale),
        grid_spec=pltpu.PrefetchScalarGridSpec(
            num_scalar_prefetch=1,
            grid=(len(starts),),
            in_specs=[pl.BlockSpec(memory_space=pl.ANY)],
            out_specs=[pl.BlockSpec((d, tr), lambda t, src: (0, clamp(t, 0, n_plain))),
                       pl.BlockSpec((d, tr), lambda t, src: (0, clamp(t, n_plain, n_rope))),
                       pl.BlockSpec((d, tr), lambda t, src: (0, clamp(t, n_plain + n_rope, n_gate))),
                       pl.BlockSpec((d, LANES), lambda t, src: (0, 0))],
            scratch_shapes=[pltpu.VMEM((2, tr, d), F32), pltpu.VMEM((dt_hi - dt_lo, d), F32),
                            pltpu.SemaphoreType.DMA((2,)), pltpu.SemaphoreType.DMA(())],
        ),
        out_shape=[jax.ShapeDtypeStruct((d, n_plain * tr), BF16), jax.ShapeDtypeStruct((d, n_rope * tr), BF16),
                   jax.ShapeDtypeStruct((d, n_gate * tr), BF16), jax.ShapeDtypeStruct((d, LANES), BF16)],
        compiler_params=_cparams(("arbitrary",)),
        name="regroup_w_in",
    )(jnp.asarray(starts, jnp.int32), w_in_t)


def _mm_kernel(a_ref, w_ref, o_ref):
    o_ref[...] = jnp.dot(a_ref[...], w_ref[...], preferred_element_type=F32).astype(o_ref.dtype)


def _row_chunks(n_rows, chunks):
    step = n_rows // chunks if n_rows % chunks == 0 and n_rows >= 64 * chunks else n_rows
    return [slice(r, r + step) for r in range(0, n_rows, step)]


def _mm_sigmoid_kernel(a_ref, w_ref, o_ref):
    for rows in _row_chunks(a_ref.shape[0], EPILOGUE_ROW_CHUNKS):
        acc = jnp.dot(a_ref[rows, :], w_ref[...], preferred_element_type=F32)
        o_ref[rows, :] = _sigmoid(acc).astype(o_ref.dtype)


def _mm_rope_kernel(a_ref, w_ref, c_ref, s1_ref, s2_ref, o_ref):
    for rows in _row_chunks(a_ref.shape[0], EPILOGUE_ROW_CHUNKS):
        acc = jnp.dot(a_ref[rows, :], w_ref[...], preferred_element_type=F32)
        c = c_ref[rows, :]
        s1 = s1_ref[rows, :]
        s2 = s2_ref[rows, :]
        for g in range(acc.shape[1] // LANES):
            blk = acc[:, g * LANES:(g + 1) * LANES]
            fwd = pltpu.roll(blk, LANES - ROT_DIM // 2, 1)
            bwd = pltpu.roll(blk, ROT_DIM // 2, 1)
            o_ref[rows, g * LANES:(g + 1) * LANES] = (blk * c + fwd * s1 + bwd * s2).astype(o_ref.dtype)


def _mm(a, w, out_dtype, tm, tn, epilogue="none", rope_tabs=None, name="mm"):
    m, k = a.shape
    n = w.shape[1]
    in_specs = [pl.BlockSpec((tm, k), lambda i, j: (i, 0)),
                pl.BlockSpec((k, tn), lambda i, j: (0, j))]
    args = [a, w]
    if epilogue == "rope":
        seq_blocks = rope_tabs[0].shape[0] // tm
        for t in rope_tabs:
            in_specs.append(pl.BlockSpec((tm, LANES), lambda i, j: (i % seq_blocks, 0)))
            args.append(t)
        kern = _mm_rope_kernel
    elif epilogue == "sigmoid":
        kern = _mm_sigmoid_kernel
    else:
        kern = _mm_kernel
    return pl.pallas_call(
        kern,
        grid=(m // tm, n // tn),
        in_specs=in_specs,
        out_specs=pl.BlockSpec((tm, tn), lambda i, j: (i, j)),
        out_shape=jax.ShapeDtypeStruct((m, n), out_dtype),
        compiler_params=_cparams(("arbitrary", "arbitrary")),
        name=name,
    )(*args)


def _ssd_kernel(z_ref, xs_ref, bm_ref, cm_ref, dt_ref, cwx_ref, cwb_ref, cwc_ref, cbx_ref, cbb_ref,
                cbc_ref, dtb_ref, alog_ref, dsk_ref, nw_ref, o_ref,
                px_sc, pb_sc, pc_sc, st_sc, y_sc, *, n_heads):
    c = pl.program_id(1)
    L = SSD_CHUNK
    N = SSD_STATE
    hpg = n_heads // SSD_GROUPS
    gw = hpg * SSD_HEADDIM

    @pl.when(c == 0)
    def _():
        px_sc[...] = jnp.zeros_like(px_sc)
        pb_sc[...] = jnp.zeros_like(pb_sc)
        pc_sc[...] = jnp.zeros_like(pc_sc)
        st_sc[...] = jnp.zeros_like(st_sc)

    def conv_silu(u, prev_sc, w_ref, b_ref):
        prev = prev_sc[...]
        row = lax.broadcasted_iota(jnp.int32, u.shape, 0)
        acc = u * w_ref[SSD_CONV - 1:SSD_CONV, :] + b_ref[...]
        for j in range(1, SSD_CONV):
            sh = jnp.where(row < j, pltpu.roll(prev, j, 0), pltpu.roll(u, j, 0))
            acc = acc + sh * w_ref[SSD_CONV - 1 - j:SSD_CONV - j, :]
        prev_sc[...] = u
        return acc * _sigmoid(acc)

    xs = conv_silu(xs_ref[0].astype(F32), px_sc, cwx_ref, cbx_ref)
    bm = conv_silu(bm_ref[0].astype(F32), pb_sc, cwb_ref, cbb_ref)
    cm = conv_silu(cm_ref[0].astype(F32), pc_sc, cwc_ref, cbc_ref)

    raw = dt_ref[0] + dtb_ref[...]
    dt = jnp.maximum(raw, 0.0) + jnp.log1p(jnp.exp(-jnp.abs(raw)))
    a = dt * (-jnp.exp(alog_ref[...]))
    r_i = lax.broadcasted_iota(jnp.int32, (L, L), 0)
    c_i = lax.broadcasted_iota(jnp.int32, (L, L), 1)
    causal = r_i >= c_i
    tri = jnp.where(causal, 1.0, 0.0).astype(F32)
    a_cs = jnp.dot(tri, a, preferred_element_type=F32, precision=HIGHEST)
    a_cs_t = a_cs.T

    lane = lax.broadcasted_iota(jnp.int32, (L, LANES), 1)
    first = lane < SSD_HEADDIM
    acs_tiles = []
    dt_tiles = []
    for j in range(n_heads // 2):
        h0, h1 = 2 * j, 2 * j + 1
        acs_tiles.append(jnp.where(first, a_cs[:, h0:h0 + 1], a_cs[:, h1:h1 + 1]))
        dt_tiles.append(jnp.where(first, dt[:, h0:h0 + 1], dt[:, h1:h1 + 1]))
    acs_e = jnp.concatenate(acs_tiles, axis=1)
    dt_e = jnp.concatenate(dt_tiles, axis=1)
    xdt = xs * dt_e
    ea = jnp.exp(acs_e)
    alast = acs_e[L - 1:L, :]
    xdec = (xdt * jnp.exp(alast - acs_e)).astype(BF16)
    ealast = jnp.exp(alast)

    for g in range(SSD_GROUPS):
        bg = bm[:, g * N:(g + 1) * N]
        cg = cm[:, g * N:(g + 1) * N].astype(BF16)
        scores = lax.dot_general(cg, bg.astype(BF16), (((1,), (1,)), ((), ())),
                                 preferred_element_type=F32)
        st_old = st_sc[:, g * gw:(g + 1) * gw]
        y_off = jnp.dot(cg, st_old.astype(BF16), preferred_element_type=F32) * ea[:, g * gw:(g + 1) * gw]
        st_sc[:, g * gw:(g + 1) * gw] = st_old * ealast[:, g * gw:(g + 1) * gw] + jnp.dot(
            bg.T.astype(BF16), xdec[:, g * gw:(g + 1) * gw], preferred_element_type=F32)
        for jj in range(hpg // 2):
            j = g * (hpg // 2) + jj
            h0, h1 = 2 * j, 2 * j + 1
            la = jnp.exp(jnp.where(causal, a_cs[:, h0:h0 + 1] - a_cs_t[h0:h0 + 1, :], NEG_BIG))
            lb = jnp.exp(jnp.where(causal, a_cs[:, h1:h1 + 1] - a_cs_t[h1:h1 + 1, :], NEG_BIG))
            mcat = jnp.concatenate([(scores * la).astype(BF16), (scores * lb).astype(BF16)], axis=1)
            xp = xdt[:, j * LANES:(j + 1) * LANES]
            xcat = jnp.concatenate([jnp.where(first, xp, 0.0).astype(BF16),
                                    jnp.where(first, 0.0, xp).astype(BF16)], axis=0)
            y_diag = jnp.dot(mcat, xcat, preferred_element_type=F32)
            lo = jj * LANES
            y_sc[:, j * LANES:(j + 1) * LANES] = (
                y_diag + y_off[:, lo:lo + LANES]
                + dsk_ref[:, j * LANES:(j + 1) * LANES] * xs[:, j * LANES:(j + 1) * LANES])

    z = z_ref[0].astype(F32)
    u = y_sc[...] * (z * _sigmoid(z))
    for g in range(SSD_GROUPS):
        ug = u[:, g * gw:(g + 1) * gw]
        ms = jnp.mean(ug * ug, axis=-1, keepdims=True)
        o_ref[0, :, g * gw:(g + 1) * gw] = (ug * lax.rsqrt(ms + SUB_EPS)
                                            * nw_ref[:, g * gw:(g + 1) * gw]).astype(o_ref.dtype)


def _ssd(plain3, dt3, conv_w, conv_b, dtb, alog, dsk_e, norm_w, dh, n_heads):
    b, s, _ = plain3.shape
    L = SSD_CHUNK
    gn = SSD_GROUPS * SSD_STATE
    nc = s // L
    xblk = 1
    bblk = (2 * dh) // gn
    cw_x, cw_b, cw_c = conv_w[:, :dh], conv_w[:, dh:dh + gn], conv_w[:, dh + gn:]
    cb_x, cb_b, cb_c = conv_b[:, :dh], conv_b[:, dh:dh + gn], conv_b[:, dh + gn:]
    full = lambda shape: pl.BlockSpec(shape, lambda bi, ci: (0, 0))
    return pl.pallas_call(
        functools.partial(_ssd_kernel, n_heads=n_heads),
        grid=(b, nc),
        in_specs=[pl.BlockSpec((1, L, dh), lambda bi, ci: (bi, ci, 0)),
                  pl.BlockSpec((1, L, dh), lambda bi, ci: (bi, ci, xblk)),
                  pl.BlockSpec((1, L, gn), lambda bi, ci: (bi, ci, bblk)),
                  pl.BlockSpec((1, L, gn), lambda bi, ci: (bi, ci, bblk + 1)),
                  pl.BlockSpec((1, L, LANES), lambda bi, ci: (bi, ci, 0)),
                  full((SSD_CONV, dh)), full((SSD_CONV, gn)), full((SSD_CONV, gn)),
                  full((1, dh)), full((1, gn)), full((1, gn)),
                  full((1, LANES)), full((1, LANES)), full((1, dh)), full((1, dh))],
        out_specs=pl.BlockSpec((1, L, dh), lambda bi, ci: (bi, ci, 0)),
        out_shape=jax.ShapeDtypeStruct((b, s, dh), BF16),
        scratch_shapes=[pltpu.VMEM((L, dh), F32), pltpu.VMEM((L, gn), F32), pltpu.VMEM((L, gn), F32),
                        pltpu.VMEM((SSD_STATE, dh), F32), pltpu.VMEM((L, dh), F32)],
        compiler_params=_cparams(("arbitrary", "arbitrary")),
        name="ssd",
    )(plain3, plain3, plain3, plain3, dt3, cw_x, cw_b, cw_c, cb_x, cb_b, cb_c, dtb, alog, dsk_e, norm_w)


def _attn_kernel(q_ref, k_ref, v_ref, bias_ref, lam_ref, sw_ref, o_ref, vt_sc, st_a, st_b, m_sc, l_sc, acc_sc, *,
                 tq, lam_init):
    qi = pl.program_id(2)
    tu = ATTN_KV_UNIT
    nh = ATTN_HEADS_PER_STEP
    n_all = k_ref.shape[1] // tu
    heads = [slice(hh * LANES, (hh + 1) * LANES) for hh in range(nh)]

    @pl.when(qi == 0)
    def _():
        def transpose_block(c, carry):
            start = pl.multiple_of(c * tu, tu)
            for hh in range(nh):
                vt_sc[hh, c] = v_ref[0, pl.ds(start, tu), heads[hh]].astype(F32).T.astype(BF16)
            return carry

        lax.fori_loop(0, n_all, transpose_block, 0)

    qts = []
    for hh in range(nh):
        qt = q_ref[0, :, heads[hh]].astype(F32).T
        row = lax.broadcasted_iota(jnp.int32, qt.shape, 0)
        qts.append((jnp.where(row < ATTN_DK, qt, 0.0).astype(BF16),
                    jnp.where(row < ATTN_DK, 0.0, qt).astype(BF16)))
    m_sc[...] = jnp.full_like(m_sc, NEG_BIG)
    l_sc[...] = jnp.zeros_like(l_sc)
    acc_sc[...] = jnp.zeros_like(acc_sc)

    def scores(u, st_ref):
        start = pl.multiple_of(u * tu, tu)
        for hh in range(nh):
            k = k_ref[0, pl.ds(start, tu), heads[hh]]
            for m in range(2):
                st_ref[2 * hh + m] = jnp.dot(k, qts[hh][m], preferred_element_type=F32)

    def update(u, st_ref, masked):
        for hh in range(nh):
            vt = vt_sc[hh, u]
            for m in range(2):
                c = 2 * hh + m
                st = st_ref[c]
                if masked:
                    st = st + bias_ref[...]
                m_prev = m_sc[c]
                m_new = jnp.maximum(m_prev, jnp.max(st, axis=0, keepdims=True))
                alpha = jnp.exp2(m_prev - m_new)
                pt = jnp.exp2(st - m_new)
                l_sc[c] = alpha * l_sc[c] + jnp.sum(pt, axis=0, keepdims=True)
                acc_sc[c] = alpha * acc_sc[c] + jnp.dot(vt, pt.astype(BF16), preferred_element_type=F32)
                m_sc[c] = m_new

    n_units = (qi * tq) // tu + 1
    n_loop = (n_units - 1) // 2
    scores(0, st_a)

    def two_units(j, carry):
        u = 2 * j
        scores(u + 1, st_b)
        update(u, st_a, False)
        scores(u + 2, st_a)
        update(u + 1, st_b, False)
        return carry

    lax.fori_loop(0, n_loop, two_units, 0)
    last = n_units - 1

    @pl.when(last == 2 * n_loop)
    def _():
        update(last, st_a, True)

    @pl.when(last != 2 * n_loop)
    def _():
        scores(last, st_b)
        update(last - 1, st_a, False)
        update(last, st_b, True)

    lv = lam_ref[...]
    lam = (jnp.exp(jnp.sum(lv[0:1] * lv[1:2], axis=-1, keepdims=True))
           - jnp.exp(jnp.sum(lv[2:3] * lv[3:4], axis=-1, keepdims=True)) + lam_init)
    for hh in range(nh):
        c0, c1 = 2 * hh, 2 * hh + 1
        ot = acc_sc[c0] * (1.0 / l_sc[c0]) - lam * (acc_sc[c1] * (1.0 / l_sc[c1]))
        ot = ot * lax.rsqrt(jnp.mean(ot * ot, axis=0, keepdims=True) + SUB_EPS)
        o_ref[0, :, heads[hh]] = (ot.T * sw_ref[...] * (1.0 - lam_init)).astype(o_ref.dtype)


def _attention(qk3, plain3, lam_rows, subln_w, n_heads, v_blk0, tq, lam_init):
    b, s, _ = qk3.shape
    tu = ATTN_KV_UNIT
    nh = ATTN_HEADS_PER_STEP
    hw = nh * LANES
    assert tq == tu, "the diagonal unit must coincide with the query tile"
    kpos = lax.broadcasted_iota(jnp.int32, (tu, tq), 0)
    qpos = lax.broadcasted_iota(jnp.int32, (tu, tq), 1)
    diag_bias = jnp.where(kpos <= qpos, 0.0, NEG_BIG).astype(F32)
    return pl.pallas_call(
        functools.partial(_attn_kernel, tq=tq, lam_init=lam_init),
        grid=(b, n_heads // nh, s // tq),
        in_specs=[pl.BlockSpec((1, tq, hw), lambda bi, hi, qi: (bi, qi, hi)),
                  pl.BlockSpec((1, s, hw), lambda bi, hi, qi: (bi, 0, n_heads // nh + hi)),
                  pl.BlockSpec((1, s, hw), lambda bi, hi, qi: (bi, 0, v_blk0 // nh + hi)),
                  pl.BlockSpec((tu, tq), lambda bi, hi, qi: (0, 0)),
                  pl.BlockSpec((8, LANES), lambda bi, hi, qi: (0, 0)),
                  pl.BlockSpec((1, LANES), lambda bi, hi, qi: (0, 0))],
        out_specs=pl.BlockSpec((1, tq, hw), lambda bi, hi, qi: (bi, qi, hi)),
        out_shape=jax.ShapeDtypeStruct((b, s, n_heads * ATTN_DV), BF16),
        scratch_shapes=[pltpu.VMEM((nh, s // tu, ATTN_DV, tu), BF16),
                        pltpu.VMEM((2 * nh, tu, tq), F32), pltpu.VMEM((2 * nh, tu, tq), F32),
                        pltpu.VMEM((2 * nh, 1, tq), F32), pltpu.VMEM((2 * nh, 1, tq), F32),
                        pltpu.VMEM((2 * nh, ATTN_DV, tq), F32)],
        compiler_params=_cparams(("arbitrary", "arbitrary", "arbitrary")),
        name="diff_attn",
    )(qk3, qk3, plain3, diag_bias, lam_rows, subln_w)


def _merge_kernel(y_ref, o_ref, ws_ref, wa_ref, gs_ref, ga_ref, out_ref):
    for rows in _row_chunks(y_ref.shape[0], MERGE_ROW_CHUNKS):
        bs = jnp.dot(y_ref[rows, :], ws_ref[...], preferred_element_type=F32)
        ba = jnp.dot(o_ref[rows, :], wa_ref[...], preferred_element_type=F32)
        out_ref[rows, :] = (gs_ref[rows, :].astype(F32) * bs
                            + ga_ref[rows, :].astype(F32) * ba).astype(out_ref.dtype)


def _merge(y, o, ws, wa, gates, tm, tn):
    m, k = y.shape
    n = ws.shape[1]
    nj = n // tn
    return pl.pallas_call(
        _merge_kernel,
        grid=(m // tm, nj),
        in_specs=[pl.BlockSpec((tm, k), lambda i, j: (i, 0)),
                  pl.BlockSpec((tm, o.shape[1]), lambda i, j: (i, 0)),
                  pl.BlockSpec((k, tn), lambda i, j: (0, j)),
                  pl.BlockSpec((o.shape[1], tn), lambda i, j: (0, j)),
                  pl.BlockSpec((tm, tn), lambda i, j: (i, j)),
                  pl.BlockSpec((tm, tn), lambda i, j: (i, nj + j))],
        out_specs=pl.BlockSpec((tm, tn), lambda i, j: (i, j)),
        out_shape=jax.ShapeDtypeStruct((m, n), BF16),
        compiler_params=_cparams(("arbitrary", "arbitrary")),
        name="merge",
    )(y, o, ws, wa, gates, gates)


def _outproj_kernel(mg_ref, wo_ref, x_ref, gpost_ref, gt_ref, gpre_ref, sc_ref, sh_ref, wr_hi_ref,
                    wr_lo_ref, br_ref, x1_ref, hp_ref, lg_ref):
    mix = jnp.dot(mg_ref[...], wo_ref[...], preferred_element_type=F32)
    nm = mix * lax.rsqrt(jnp.mean(mix * mix, axis=-1, keepdims=True) + NORM_EPS) * gpost_ref[...]
    x1 = x_ref[...] + gt_ref[0] * nm
    x1_ref[...] = x1
    h2 = (x1 * lax.rsqrt(jnp.mean(x1 * x1, axis=-1, keepdims=True) + NORM_EPS) * gpre_ref[...]
          * (1.0 + sc_ref[0]) + sh_ref[0])
    h_hi = h2.astype(BF16)
    h_lo = (h2 - h_hi.astype(F32)).astype(BF16)
    wr_hi = wr_hi_ref[...]
    lg_ref[...] = (jnp.dot(h_hi, wr_hi, preferred_element_type=F32)
                   + jnp.dot(h_lo, wr_hi, preferred_element_type=F32)
                   + jnp.dot(h_hi, wr_lo_ref[...], preferred_element_type=F32) + br_ref[...])
    _store_token_rows(hp_ref, 0, h2)


def _outproj(merged, wo, x2, gpost, mod3, gpre, wr_hi, wr_lo, br, seq, tm, gt_blk, sc_blk, sh_blk):
    m, d = x2.shape
    per_b = seq // tm
    row = lambda i: (i, 0)
    const = lambda i: (0, 0)
    return pl.pallas_call(
        _outproj_kernel,
        grid=(m // tm,),
        in_specs=[pl.BlockSpec((tm, d), row),
                  pl.BlockSpec((d, d), const),
                  pl.BlockSpec((tm, d), row),
                  pl.BlockSpec((1, d), const),
                  pl.BlockSpec((1, 1, d), lambda i: (i // per_b, 0, gt_blk)),
                  pl.BlockSpec((1, d), const),
                  pl.BlockSpec((1, 1, d), lambda i: (i // per_b, 0, sc_blk)),
                  pl.BlockSpec((1, 1, d), lambda i: (i // per_b, 0, sh_blk)),
                  pl.BlockSpec((d, LANES), const),
                  pl.BlockSpec((d, LANES), const),
                  pl.BlockSpec((1, LANES), const)],
        out_specs=[pl.BlockSpec((tm, d), row),
                   pl.BlockSpec((tm * (d // LANES), LANES), row),
                   pl.BlockSpec((tm, LANES), row)],
        out_shape=[jax.ShapeDtypeStruct((m, d), F32),
                   jax.ShapeDtypeStruct((m * (d // LANES), LANES), F32),
                   jax.ShapeDtypeStruct((m, LANES), F32)],
        compiler_params=_cparams(("arbitrary",)),
        name="outproj",
    )(merged, wo, x2, gpost, mod3, gpre, mod3, mod3, wr_hi, wr_lo, br)


def _route_kernel(lg_ref, dest_ref, w_ref, cnt_ref, cnt_sc, pst_sc, run_sc, *, n_experts, blk):
    ph = pl.program_id(0)
    t = pl.program_id(1)
    tk = lg_ref.shape[0]
    lane = lax.broadcasted_iota(jnp.int32, (tk, LANES), 1)
    lg = jnp.where(lane < n_experts, lg_ref[...], -jnp.inf)
    vals = []
    hots = []
    for _ in range(TOP_K):
        mx = jnp.max(lg, axis=-1, keepdims=True)
        ix = jnp.min(jnp.where(lg == mx, lane, LANES), axis=-1, keepdims=True)
        hot = lane == ix
        lg = jnp.where(hot, -jnp.inf, lg)
        vals.append(mx)
        hots.append(hot)
    multi = jnp.zeros((tk, LANES), F32)
    for hot in hots:
        multi = multi + jnp.where(hot, 1.0, 0.0)
    colsum = jnp.sum(multi, axis=0, keepdims=True)

    @pl.when((ph == 0) & (t == 0))
    def _():
        cnt_sc[...] = jnp.zeros_like(cnt_sc)

    @pl.when(ph == 0)
    def _():
        cnt_sc[...] += colsum

    @pl.when((ph == 1) & (t == 0))
    def _():
        cnt = cnt_sc[...].astype(jnp.int32)
        padded = (((cnt + (blk - 1)) // blk) * blk).astype(F32)
        r_i = lax.broadcasted_iota(jnp.int32, (LANES, LANES), 0)
        c_i = lax.broadcasted_iota(jnp.int32, (LANES, LANES), 1)
        upper = jnp.where(r_i < c_i, 1.0, 0.0).astype(F32)
        pst_sc[...] = jnp.dot(jnp.broadcast_to(padded, (8, LANES)), upper,
                              preferred_element_type=F32, precision=HIGHEST)[0:1]
        run_sc[...] = jnp.zeros_like(run_sc)

    @pl.when(ph == 1)
    def _():
        r_i = lax.broadcasted_iota(jnp.int32, (tk, tk), 0)
        c_i = lax.broadcasted_iota(jnp.int32, (tk, tk), 1)
        strict = jnp.where(r_i > c_i, 1.0, 0.0).astype(BF16)
        before = jnp.dot(strict, multi.astype(BF16), preferred_element_type=F32)
        base = before + run_sc[...] + pst_sc[...]
        esum = jnp.zeros((tk, 1), F32)
        evals = []
        for r in range(TOP_K):
            e = jnp.exp(vals[r] - vals[0])
            evals.append(e)
            esum = esum + e
        dest = jnp.zeros((tk, LANES), jnp.int32)
        wts = jnp.zeros((tk, LANES), F32)
        for r in range(TOP_K):
            d_r = jnp.sum(jnp.where(hots[r], base, 0.0), axis=-1, keepdims=True).astype(jnp.int32)
            dest = jnp.where(lane == r, d_r, dest)
            wts = jnp.where(lane == r, evals[r] / esum, wts)
        dest_ref[...] = dest
        w_ref[...] = wts
        run_sc[...] += colsum
        cnt_ref[...] = jnp.broadcast_to(cnt_sc[...], cnt_ref.shape)


def _route(logits, n_experts, blk, tk):
    t = logits.shape[0]
    return pl.pallas_call(
        functools.partial(_route_kernel, n_experts=n_experts, blk=blk),
        grid=(2, t // tk),
        in_specs=[pl.BlockSpec((tk, LANES), lambda ph, ti: (ti, 0))],
        out_specs=[pl.BlockSpec((tk, LANES), lambda ph, ti: (ti * ph, 0)),
                   pl.BlockSpec((tk, LANES), lambda ph, ti: (ti * ph, 0)),
                   pl.BlockSpec((8, LANES), lambda ph, ti: (0, 0))],
        out_shape=[jax.ShapeDtypeStruct((t, LANES), jnp.int32),
                   jax.ShapeDtypeStruct((t, LANES), F32),
                   jax.ShapeDtypeStruct((8, LANES), F32)],
        scratch_shapes=[pltpu.VMEM((1, LANES), F32), pltpu.VMEM((1, LANES), F32), pltpu.VMEM((1, LANES), F32)],
        compiler_params=_cparams(("arbitrary", "arbitrary")),
        name="route",
    )(logits)


def _pad_fill_copies(b, nv_sm, zero_sc, xs_hbm, sem, blk, ns):
    nv = nv_sm[b]
    out = []
    off = b * blk + nv
    rest = blk - nv
    p = blk
    while p >= 1:
        cond = (rest & p) != 0
        out.append((cond, pltpu.make_async_copy(
            zero_sc.at[pl.ds(0, p * ns), :], xs_hbm.at[pl.ds(pl.multiple_of(off * ns, ns), p * ns), :], sem)))
        off = off + jnp.where(cond, p, 0)
        p //= 2
    return out


def _dispatch_kernel(dest_sm, nv_sm, h_hbm, xs_hbm, zero_sc, hbuf, in_sem, sem, fill_sem, *, tt, blk, nb, ns):
    i = pl.program_id(0)
    n = pl.num_programs(0)
    base = i * tt
    rows = tt * ns

    def fill(b, carry):
        for cond, cp in _pad_fill_copies(b, nv_sm, zero_sc, xs_hbm, fill_sem, blk, ns):
            pl.when(cond)(cp.start)
        return carry

    def fill_wait(b, carry):
        for cond, cp in _pad_fill_copies(b, nv_sm, zero_sc, xs_hbm, fill_sem, blk, ns):
            pl.when(cond)(cp.wait)
        return carry

    def load(step):
        slot = step % 3
        return pltpu.make_async_copy(h_hbm.at[pl.ds(pl.multiple_of(step * rows, rows), rows), :],
                                     hbuf.at[slot], in_sem.at[slot])

    @pl.when(i == 0)
    def _():
        zero_sc[...] = jnp.zeros_like(zero_sc)
        lax.fori_loop(0, nb, fill, 0)
        load(0).start()
        pl.when(n > 1)(lambda: load(1).start())

    def wait_step(step):
        for _ in range(TOP_K):
            pltpu.make_async_copy(hbuf.at[0], xs_hbm.at[pl.ds(0, rows), :], sem.at[step % 2]).wait()

    load(i).wait()
    src_tile = hbuf.at[i % 3]

    def body(t, carry):
        src = src_tile.at[pl.ds(pl.multiple_of(t * ns, ns), ns), :]
        for k in range(TOP_K):
            d = dest_sm[(base + t) * TOP_K + k]
            pltpu.make_async_copy(src, xs_hbm.at[pl.ds(pl.multiple_of(d * ns, ns), ns), :],
                                  sem.at[i % 2]).start()
        return carry

    lax.fori_loop(0, tt, body, 0)
    pl.when(i > 0)(lambda: wait_step(i - 1))
    pl.when(i + 2 < n)(lambda: load(i + 2).start())

    @pl.when(i == n - 1)
    def _():
        wait_step(i)
        lax.fori_loop(0, nb, fill_wait, 0)


def _dispatch(dest_flat, nvalid, h_rows, n_slots, tt, blk, ns):
    nb = n_slots // blk
    t = h_rows.shape[0] // ns
    return pl.pallas_call(
        functools.partial(_dispatch_kernel, tt=tt, blk=blk, nb=nb, ns=ns),
        grid_spec=pltpu.PrefetchScalarGridSpec(
            num_scalar_prefetch=2,
            grid=(t // tt,),
            in_specs=[pl.BlockSpec(memory_space=pl.ANY)],
            out_specs=pl.BlockSpec(memory_space=pl.ANY),
            scratch_shapes=[pltpu.VMEM((blk * ns, LANES), F32), pltpu.VMEM((3, tt * ns, LANES), F32),
                            pltpu.SemaphoreType.DMA((3,)), pltpu.SemaphoreType.DMA((2,)),
                            pltpu.SemaphoreType.DMA(())],
        ),
        out_shape=jax.ShapeDtypeStruct((n_slots * ns, LANES), F32),
        compiler_params=_cparams(("arbitrary",)),
        name="dispatch",
    )(dest_flat, nvalid, h_rows)


def _expert_changed(be, i, last):
    ii = jnp.minimum(i, last)
    prev = jnp.maximum(ii - 1, 0)
    return (i == 0) | (be[ii] != be[prev])


def _for_live_sub_blocks(active, n_valid, sub, o_ref, compute, rows_per_slot=1):
    n_live = jnp.where(active, (n_valid + sub - 1) // sub, 0)
    for count in range(MOE_SUB_BLOCKS + 1):
        @pl.when(n_live == count)
        def _(count=count):
            if count == MOE_SUB_BLOCKS:
                compute(slice(0, count * sub))
            else:
                for r in range(count):
                    compute(slice(r * sub, (r + 1) * sub))
            if count < MOE_SUB_BLOCKS:
                first = count * sub * rows_per_slot
                o_ref[first:, :] = jnp.zeros((o_ref.shape[0] - first, o_ref.shape[1]), o_ref.dtype)


def _stream_expert_weights(changed, first, prefetch, wait_cur, cast, start_next):
    @pl.when(changed)
    def _():
        pl.when(first)(lambda: start_next(True))
        wait_cur()
        cast()
        pl.when(prefetch)(lambda: start_next(False))


def _gateup_kernel(be, nv, nu, nxt, x_ref, w_hbm, bg_ref, bu_ref, o_ref, wbuf, wg_sc, wu_sc, sem):
    j = pl.program_id(0)
    i = pl.program_id(1)
    nj = pl.num_programs(0)
    th = wg_sc.shape[1]
    dff = w_hbm.shape[2] // 2
    last = nu[0] - 1
    active = i < nu[0]
    ii = jnp.minimum(i, last)

    def copies(e, jj):
        col = pl.multiple_of(jj * th, th)
        return (pltpu.make_async_copy(w_hbm.at[e, :, pl.ds(col, th)], wbuf.at[0], sem.at[0]),
                pltpu.make_async_copy(w_hbm.at[e, :, pl.ds(dff + col, th)], wbuf.at[1], sem.at[1]))

    nx = nxt[ii]
    same_pass = nx >= 0
    e_next = jnp.where(same_pass, be[jnp.maximum(nx, 0)], be[0])
    j_next = jnp.where(same_pass, j, j + 1)

    def start_next(current):
        for cp in (copies(be[ii], j) if current else copies(e_next, j_next)):
            cp.start()

    def wait_cur():
        for cp in copies(be[ii], j):
            cp.wait()

    def cast():
        def chunk(r, carry):
            rows = pl.ds(pl.multiple_of(r * CAST_ROWS, CAST_ROWS), CAST_ROWS)
            wg_sc[rows, :] = wbuf[0, rows, :].astype(BF16)
            wu_sc[rows, :] = wbuf[1, rows, :].astype(BF16)
            return carry

        lax.fori_loop(0, wg_sc.shape[0] // CAST_ROWS, chunk, 0)

    _stream_expert_weights(active & _expert_changed(be, i, last), (j == 0) & (i == 0),
                           same_pass | (j + 1 < nj), wait_cur, cast, start_next)

    sub = o_ref.shape[0] // MOE_SUB_BLOCKS

    def compute(rows):
        x = _load_token_rows(x_ref, rows.start, rows.stop - rows.start, wg_sc.shape[0] // LANES).astype(BF16)
        g = jnp.dot(x, wg_sc[...], preferred_element_type=F32) + bg_ref[0]
        up = jnp.dot(x, wu_sc[...], preferred_element_type=F32) + bu_ref[0]
        gate = jnp.minimum(g, SWIGLU_LIMIT)
        up = jnp.clip(up, -SWIGLU_LIMIT, SWIGLU_LIMIT)
        o_ref[rows, :] = ((up + 1.0) * gate * _sigmoid(SWIGLU_ALPHA * gate)).astype(o_ref.dtype)

    _for_live_sub_blocks(active, nv[i], sub, o_ref, compute)


def _gateup(block_e, nvalid, nused, nxt, xs, w_gu, b_gu3, tm, th):
    d = w_gu.shape[1]
    ns = d // LANES
    n_slots = xs.shape[0] // ns
    dff = w_gu.shape[2] // 2
    nj = dff // th
    nb = n_slots // tm

    def blk(i, nu):
        return jnp.minimum(i, nu[0] - 1)

    return pl.pallas_call(
        _gateup_kernel,
        grid_spec=pltpu.PrefetchScalarGridSpec(
            num_scalar_prefetch=4,
            grid=(nj, nb),
            in_specs=[pl.BlockSpec((tm * ns, LANES), lambda j, i, be, nv, nu, nx: (blk(i, nu), 0)),
                      pl.BlockSpec(memory_space=pl.ANY),
                      pl.BlockSpec((1, 1, th), lambda j, i, be, nv, nu, nx: (be[blk(i, nu)], 0, j)),
                      pl.BlockSpec((1, 1, th), lambda j, i, be, nv, nu, nx: (be[blk(i, nu)], 0, nj + j))],
            out_specs=pl.BlockSpec((tm, th), lambda j, i, be, nv, nu, nx: (i, j)),
            scratch_shapes=[pltpu.VMEM((2, d, th), F32), pltpu.VMEM((d, th), BF16), pltpu.VMEM((d, th), BF16),
                            pltpu.SemaphoreType.DMA((2,))],
        ),
        out_shape=jax.ShapeDtypeStruct((n_slots, dff), BF16),
        compiler_params=_cparams(("arbitrary", "arbitrary")),
        name="expert_gate_up",
    )(block_e, nvalid, nused, nxt, xs, w_gu, b_gu3, b_gu3)


def _down_kernel(be, nv, nu, nxt, a_ref, w_hbm, bd_ref, o_ref, wbuf, wd_sc, sem):
    i = pl.program_id(1)
    last = nu[0] - 1
    active = i < nu[0]
    ii = jnp.minimum(i, last)
    nx = nxt[ii]

    def copy(e):
        return pltpu.make_async_copy(w_hbm.at[e], wbuf, sem)

    def start_next(current):
        copy(be[ii] if current else be[jnp.maximum(nx, 0)]).start()

    def cast():
        def chunk(r, carry):
            rows = pl.ds(pl.multiple_of(r * CAST_ROWS, CAST_ROWS), CAST_ROWS)
            wd_sc[rows, :] = wbuf[rows, :].astype(BF16)
            return carry

        lax.fori_loop(0, wd_sc.shape[0] // CAST_ROWS, chunk, 0)

    _stream_expert_weights(active & _expert_changed(be, i, last), i == 0, nx >= 0,
                           lambda: copy(be[ii]).wait(), cast, start_next)

    sub = a_ref.shape[0] // MOE_SUB_BLOCKS
    ns = o_ref.shape[0] // a_ref.shape[0]

    def compute(rows):
        y = jnp.dot(a_ref[rows, :], wd_sc[...], preferred_element_type=F32) + bd_ref[0]
        _store_token_rows(o_ref, rows.start, y)

    _for_live_sub_blocks(active, nv[i], sub, o_ref, compute, rows_per_slot=ns)


def _down(block_e, nvalid, nused, nxt, act, w_d, b_d3, tm):
    n_slots, dff = act.shape
    d = w_d.shape[2]
    ns = d // LANES
    nb = n_slots // tm

    def blk(i, nu):
        return jnp.minimum(i, nu[0] - 1)

    return pl.pallas_call(
        _down_kernel,
        grid_spec=pltpu.PrefetchScalarGridSpec(
            num_scalar_prefetch=4,
            grid=(1, nb),
            in_specs=[pl.BlockSpec((tm, dff), lambda j, i, be, nv, nu, nx: (blk(i, nu), 0)),
                      pl.BlockSpec(memory_space=pl.ANY),
                      pl.BlockSpec((1, 1, d), lambda j, i, be, nv, nu, nx: (be[blk(i, nu)], 0, 0))],
            out_specs=pl.BlockSpec((tm * ns, LANES), lambda j, i, be, nv, nu, nx: (i, 0)),
            scratch_shapes=[pltpu.VMEM((dff, d), F32), pltpu.VMEM((dff, d), BF16), pltpu.SemaphoreType.DMA(())],
        ),
        out_shape=jax.ShapeDtypeStruct((n_slots * ns, LANES), F32),
        compiler_params=_cparams(("arbitrary", "arbitrary")),
        name="expert_down",
    )(block_e, nvalid, nused, nxt, act, w_d, b_d3)


def _combine_kernel(dest_sm, y_hbm, w_ref, x1_ref, gt_ref, g_ref, o_ref, buf, sem, *, tt, ns):
    i = pl.program_id(0)
    n = pl.num_programs(0)

    def gather(tile, slot):
        def body(t, carry):
            for k in range(TOP_K):
                d = dest_sm[(tile * tt + t) * TOP_K + k]
                pltpu.make_async_copy(y_hbm.at[pl.ds(pl.multiple_of(d * ns, ns), ns), :],
                                      buf.at[slot, pl.ds(pl.multiple_of((k * tt + t) * ns, ns), ns), :],
                                      sem.at[slot]).start()
            return carry

        lax.fori_loop(0, tt, body, 0)

    slot = i % 2

    @pl.when(i == 0)
    def _():
        gather(0, 0)

    @pl.when(i + 1 < n)
    def _():
        gather(i + 1, 1 - slot)

    pltpu.make_async_copy(y_hbm.at[pl.ds(0, TOP_K * tt * ns), :], buf.at[slot], sem.at[slot]).wait()
    w = w_ref[...]
    rows = buf.at[slot]
    f = None
    for k in range(TOP_K):
        yk = _load_token_rows(rows, k * tt, tt, ns) * w[:, k:k + 1]
        f = yk if f is None else f + yk
    nf = f * lax.rsqrt(jnp.mean(f * f, axis=-1, keepdims=True) + NORM_EPS) * g_ref[...]
    o_ref[...] = x1_ref[...] + gt_ref[0] * nf


def _combine(dest_flat, y_rows, wts, x1, mod3, gpost, seq, tt, gt_blk):
    t, d = x1.shape
    ns = d // LANES
    per_b = seq // tt
    return pl.pallas_call(
        functools.partial(_combine_kernel, tt=tt, ns=ns),
        grid_spec=pltpu.PrefetchScalarGridSpec(
            num_scalar_prefetch=1,
            grid=(t // tt,),
            in_specs=[pl.BlockSpec(memory_space=pl.ANY),
                      pl.BlockSpec((tt, LANES), lambda i, ds: (i, 0)),
                      pl.BlockSpec((tt, d), lambda i, ds: (i, 0)),
                      pl.BlockSpec((1, 1, d), lambda i, ds: (i // per_b, 0, gt_blk)),
                      pl.BlockSpec((1, d), lambda i, ds: (0, 0))],
            out_specs=pl.BlockSpec((tt, d), lambda i, ds: (i, 0)),
            scratch_shapes=[pltpu.VMEM((2, TOP_K * tt * ns, LANES), F32), pltpu.SemaphoreType.DMA((2,))],
        ),
        out_shape=jax.ShapeDtypeStruct((t, d), F32),
        compiler_params=_cparams(("arbitrary",)),
        name="combine",
    )(dest_flat, y_rows, wts, x1, mod3, gpost)


def _tile(n, pref):
    t = min(n, pref)
    while n % t:
        t //= 2
    return t


def _rope_tables(seq):
    half = ROT_DIM // 2
    inv = ROPE_THETA ** (-jnp.arange(0, ROT_DIM, 2, dtype=F32) / ROT_DIM)
    ang = jnp.arange(seq, dtype=F32)[:, None] * inv[None, :]
    cos, sin = jnp.cos(ang), jnp.sin(ang)
    ones = jnp.ones((seq, ATTN_DK - ROT_DIM), F32)
    zeros = jnp.zeros((seq, ATTN_DK - ROT_DIM), F32)
    zh = jnp.zeros((seq, half), F32)
    c64 = jnp.concatenate([cos, cos, ones], axis=1)
    s1_64 = jnp.concatenate([-sin, zh, zeros], axis=1)
    s2_64 = jnp.concatenate([zh, sin, zeros], axis=1)
    rep = LANES // ATTN_DK
    return tuple(jnp.tile(t, (1, rep)) for t in (c64, s1_64, s2_64))


def _layer(x, c_pad, l, p, moe_blk):
    bsz, seq, d = x.shape
    t = bsz * seq
    dh = d
    n_sheads = dh // SSD_HEADDIM
    gn = SSD_GROUPS * SSD_STATE
    n_aheads = d // ATTN_DV
    aw = n_aheads * ATTN_DV
    qkw = 2 * n_aheads * ATTN_DK
    n_experts = p["w_router"].shape[-1]

    mod = _ada(c_pad, p["w_ada"][l], p["b_ada"][l][None, :], _tile(6 * d, ADA_COL_TILE))
    mod3 = mod[:bsz].reshape(bsz, 1, 6 * d)
    sh_m, sc_m, gt_m, sh_f, sc_f, gt_f = range(6)

    o = 0
    segs = {}
    for name, size in (("z", dh), ("xbc", dh + 2 * gn), ("dt", n_sheads), ("q", qkw), ("k", qkw),
                       ("v", aw), ("gs", d), ("ga", d)):
        segs[name] = (o, o + size)
        o += size
    scale = ATTN_DK ** -0.5 * math.log2(math.e)
    w_plain, w_rope, w_gate, w_dt = _regroup(jnp.swapaxes(p["w_in"][l], 0, 1), segs, scale)

    h = _prenorm(x, p["g_pre_mix"][l][None, :], mod3, sc_m, sh_m, _tile(seq, PRENORM_ROW_TILE)).reshape(t, d)
    tm = _tile(seq, PROJ_ROW_TILE)
    tn = lambda w: _tile(w.shape[1], PROJ_COL_TILE)
    plain = _mm(h, w_plain, BF16, tm, tn(w_plain), name="proj_plain")
    qk = _mm(h, w_rope, BF16, tm, tn(w_rope), "rope", _rope_tables(seq), name="proj_rope")
    gates = _mm(h, w_gate, BF16, tm, tn(w_gate), "sigmoid", name="proj_gate")
    dt_raw = _mm(h, w_dt, F32, tm, LANES, name="proj_dt")

    pad_h = lambda v: jnp.pad(v, (0, LANES - n_sheads))[None, :]
    plain3 = plain.reshape(bsz, seq, plain.shape[1])
    y_ssd = _ssd(plain3, dt_raw.reshape(bsz, seq, LANES), p["conv_w"][l], p["conv_b"][l][None, :],
                 pad_h(p["dt_bias"][l]), pad_h(p["a_log"][l]),
                 jnp.repeat(p["d_skip"][l], SSD_HEADDIM)[None, :], p["ssd_norm_w"][l][None, :], dh, n_sheads)

    lam_init = 0.8 - 0.6 * math.exp(-0.3 * l)
    lam_rows = jnp.zeros((8, LANES), F32)
    for r, nm in enumerate(("lambda_q1", "lambda_k1", "lambda_q2", "lambda_k2")):
        lam_rows = lam_rows.at[r, :ATTN_DK].set(p[nm][l])
    v_blk0 = (dh + dh + 2 * gn) // LANES
    o_attn = _attention(qk.reshape(bsz, seq, 2 * qkw), plain3, lam_rows, p["subln_w"][l][None, :],
                        n_aheads, v_blk0, _tile(seq, ATTN_Q_TILE), lam_init)

    tm2 = _tile(seq, OUTPROJ_ROW_TILE)
    merged = _merge(y_ssd.reshape(t, dh), o_attn.reshape(t, aw), p["w_br_ssd"][l].astype(BF16),
                    p["w_br_attn"][l].astype(BF16), gates, _tile(seq, MERGE_ROW_TILE), _tile(d, PROJ_COL_TILE))
    wr = jnp.pad(p["w_router"][l], ((0, 0), (0, LANES - n_experts)))
    wr_hi = wr.astype(BF16)
    wr_lo = (wr - wr_hi.astype(F32)).astype(BF16)
    br = jnp.pad(p["b_router"][l], (0, LANES - n_experts))[None, :]
    x1, h_rows, logits = _outproj(merged, p["w_out"][l].astype(BF16), x.reshape(t, d),
                                    p["g_post_mix"][l][None, :], mod3, p["g_pre_ffn"][l][None, :],
                                    wr_hi, wr_lo, br, seq, tm2, gt_m, sc_f, sh_f)

    dest, wts, cnt = _route(logits, n_experts, moe_blk, _tile(t, ROUTE_ROW_TILE))
    counts = cnt[0, :n_experts].astype(jnp.int32)
    n_slots = t * TOP_K + n_experts * moe_blk
    nb = n_slots // moe_blk
    pblocks = (counts + moe_blk - 1) // moe_blk
    pend = jnp.cumsum(pblocks)
    nused = jnp.maximum(pend[-1], 1).astype(jnp.int32)
    bidx = jnp.arange(nb, dtype=jnp.int32)
    block_e = jnp.minimum(jnp.sum((pend[None, :] <= bidx[:, None]).astype(jnp.int32), axis=1), n_experts - 1)
    pstart = pend - pblocks
    nvalid = jnp.clip(counts[block_e] - (bidx - pstart[block_e]) * moe_blk, 0, moe_blk).astype(jnp.int32)
    dest_flat = dest[:, :TOP_K].reshape(-1)
    nused1 = nused.reshape(1)
    after = pend[block_e].astype(jnp.int32)
    nxt = jnp.where(after < nused, after, -1).astype(jnp.int32)

    xs = _dispatch(dest_flat, nvalid, h_rows, n_slots, _tile(t, DISPATCH_TOKENS), moe_blk, d // LANES)
    dff = p["w_down"].shape[2]
    act = _gateup(block_e, nvalid, nused1, nxt, xs, p["w_gate_up"][l], p["b_gate_up"][l][:, None, :],
                  moe_blk, _tile(dff, EXPERT_HIDDEN_TILE))
    y_sorted = _down(block_e, nvalid, nused1, nxt, act, p["w_down"][l], p["b_down"][l][:, None, :], moe_blk)
    out = _combine(dest_flat, y_sorted, wts, x1, mod3, p["g_post_ffn"][l][None, :], seq, _tile(seq, COMBINE_TOKENS), gt_f)
    return out.reshape(bsz, seq, d)


def kernel(x, c, w_ada, b_ada, g_pre_mix, g_post_mix, g_pre_ffn, g_post_ffn, w_in, conv_w, conv_b, dt_bias, a_log, d_skip, ssd_norm_w, lambda_q1, lambda_k1, lambda_q2, lambda_k2, subln_w, w_br_ssd, w_br_attn, w_out, w_router, b_router, w_gate_up, b_gate_up, w_down, b_down):
    p = dict(w_ada=w_ada, b_ada=b_ada, g_pre_mix=g_pre_mix, g_post_mix=g_post_mix, g_pre_ffn=g_pre_ffn,
             g_post_ffn=g_post_ffn, w_in=w_in, conv_w=conv_w, conv_b=conv_b, dt_bias=dt_bias, a_log=a_log,
             d_skip=d_skip, ssd_norm_w=ssd_norm_w, lambda_q1=lambda_q1, lambda_k1=lambda_k1,
             lambda_q2=lambda_q2, lambda_k2=lambda_k2, subln_w=subln_w, w_br_ssd=w_br_ssd,
             w_br_attn=w_br_attn, w_out=w_out, w_router=w_router, b_router=b_router, w_gate_up=w_gate_up,
             b_gate_up=b_gate_up, w_down=w_down, b_down=b_down)
    bsz = x.shape[0]
    c_pad = jnp.pad(c, ((0, (-bsz) % 8), (0, 0)))
    for l in range(w_ada.shape[0]):
        x = _layer(x, c_pad, l, p, min(MOE_ROW_BLOCK, x.shape[0] * x.shape[1]))
    return x
```
